```python
import math
import numpy as np
import jax
import jax.numpy as jnp
from jax import lax

D_MODEL = 1024
BATCH = 2
SEQ = 8192
DEPTH = 2

CTX_LEN = 256
GRID_W = 64
N_EVEN = (DEPTH + 1) // 2
N_ODD = DEPTH // 2

SSD_HEADS = 16
SSD_HEAD_DIM = 64
SSD_WIDTH = SSD_HEADS * SSD_HEAD_DIM
SSD_GROUPS = 2
SSD_STATE = 128
SSD_CHUNK = 128
CONV_K = 5
CONV_CH = SSD_WIDTH + 2 * SSD_GROUPS * SSD_STATE

GLA_HEADS = 4
GLA_KEY_DIM = 128
GLA_VAL_DIM = 256
GLA_K = GLA_HEADS * GLA_KEY_DIM
GLA_V = GLA_HEADS * GLA_VAL_DIM
GLA_RANK = 16
GLA_GATE_NORMALIZER = 16.0
GLA_CHUNK = 64

E_SPLITS = (SSD_WIDTH, CONV_CH, 2 * SSD_HEADS, GLA_K, GLA_K, GLA_V, GLA_V, 2 * GLA_RANK)
E_IN = SSD_WIDTH + CONV_CH + 2 * SSD_HEADS + 2 * GLA_K + 2 * GLA_V + 2 * GLA_RANK
E_MIX = SSD_WIDTH + GLA_V

ATT_HEADS = 16
ATT_KV_HEADS = 4
ATT_GROUP = ATT_HEADS // ATT_KV_HEADS
ATT_HEAD_DIM = 128
ATT_W = ATT_HEADS * ATT_HEAD_DIM
ATT_KV_W = ATT_KV_HEADS * ATT_HEAD_DIM
WINDOW = 128
ATT_BLOCK = 128
ROPE_BASE = 10000.0
ROPE_FREQS = ATT_HEAD_DIM // 4
O_IN = 2 * ATT_KV_W + 2 * ATT_W
NORM_EPS = 1e-6

kernel_name = 'hybrid_ssd_gla_swa_ctxprefix'


def rms_norm(t, g):
    tf = t.astype(jnp.float32)
    y = tf * lax.rsqrt(jnp.mean(tf * tf, axis=-1, keepdims=True) + NORM_EPS)
    return (y * g.astype(jnp.float32)).astype(t.dtype)


def split_cols(t, sizes):
    idx = np.cumsum(np.array(sizes))[:-1].tolist()
    return jnp.split(t, idx, axis=-1)


def _flip(t):
    return jnp.flip(t, axis=1)


def adaln(cvec, w, b):
    mod = (jnp.dot(jax.nn.silu(cvec), w) + b)[:, None, :]
    return jnp.split(mod, 3, axis=-1)


def centered_dwconv(t, w, b):
    half = (w.shape[0] - 1) // 2
    out = lax.conv_general_dilated(t, w.astype(t.dtype)[:, None, :], window_strides=(1,), padding=[(half, half)],
                                   dimension_numbers=('NWC', 'WIO', 'NWC'), feature_group_count=t.shape[-1])
    return out + b.astype(t.dtype)


def ssd_scan(xs, dt, a, bm, cm, init):
    bsz, seq, _, hd = xs.shape
    T = SSD_CHUNK
    nc = seq // T
    G, R, N = SSD_GROUPS, SSD_HEADS // SSD_GROUPS, SSD_STATE
    xdt = (xs * dt[..., None]).reshape(bsz, nc, T, G, R, hd)
    cs = jnp.cumsum((dt * a).reshape(bsz, nc, T, G, R), axis=2)
    bm = bm.reshape(bsz, nc, T, G, N)
    cm = cm.reshape(bsz, nc, T, G, N)
    incl = jnp.tril(jnp.ones((T, T), bool))[:, :, None, None]
    seg = jnp.exp(jnp.where(incl, cs[:, :, :, None] - cs[:, :, None, :], -jnp.inf))
    cb = jnp.einsum('bcign,bcjgn->bcijg', cm, bm)
    y_diag = jnp.einsum('bcijg,bcijgr,bcjgrp->bcigrp', cb, seg, xdt)
    u = jnp.einsum('bctgn,bctgr,bctgrp->bcgrpn', bm, jnp.exp(cs[:, :, -1:] - cs), xdt)
    chunk_decay = jnp.exp(cs[:, :, -1])

    def step(state, inp):
        d, uc = inp
        return d[..., None, None] * state + uc, state

    final, starts = lax.scan(step, init, (jnp.moveaxis(chunk_decay, 1, 0), jnp.moveaxis(u, 1, 0)))
    starts = jnp.moveaxis(starts, 0, 1)
    y_off = jnp.einsum('bcign,bcigr,bcgrpn->bcigrp', cm, jnp.exp(cs), starts)
    return (y_diag + y_off).reshape(bsz, seq, SSD_HEADS, hd), final


def ssd_mixer(z, xbc, dt_raw, conv_w, conv_b, dt_bias, a_log, d_skip, norm_g, init):
    bsz, seq, _ = z.shape
    xbc = jax.nn.silu(centered_dwconv(xbc, conv_w, conv_b))
    xs, bm, cm = split_cols(xbc, (SSD_WIDTH, SSD_GROUPS * SSD_STATE, SSD_GROUPS * SSD_STATE))
    xs = xs.reshape(bsz, seq, SSD_HEADS, SSD_HEAD_DIM)
    bm = bm.reshape(bsz, seq, SSD_GROUPS, SSD_STATE)
    cm = cm.reshape(bsz, seq, SSD_GROUPS, SSD_STATE)
    dt = jax.nn.softplus(dt_raw.reshape(bsz, seq, 2, SSD_HEADS) + dt_bias.astype(jnp.float32))
    a = -jnp.exp(a_log.astype(jnp.float32))
    y_f, fin_f = ssd_scan(xs, dt[:, :, 0], a[0], bm, cm, init[0])
    y_b, fin_b = ssd_scan(_flip(xs), _flip(dt[:, :, 1]), a[1], _flip(bm), _flip(cm), init[1])
    y = y_f + _flip(y_b) + d_skip.astype(jnp.float32)[:, None] * xs
    gsz = SSD_WIDTH // SSD_GROUPS
    y = y.reshape(bsz, seq, SSD_GROUPS, gsz) * jax.nn.silu(z).reshape(bsz, seq, SSD_GROUPS, gsz)
    y = rms_norm(y, norm_g.reshape(SSD_GROUPS, gsz))
    return y.reshape(bsz, seq, SSD_WIDTH), (fin_f, fin_b)


def gla_scan(q, k, v, lg, init):
    bsz, seq, nh, dk = q.shape
    dv = v.shape[-1]
    T = GLA_CHUNK
    nc = seq // T
    q = q.reshape(bsz, nc, T, nh, dk) * (dk ** -0.5)
    k = k.reshape(bsz, nc, T, nh, dk)
    v = v.reshape(bsz, nc, T, nh, dv)
    cs = jnp.cumsum(lg.reshape(bsz, nc, T, nh, dk), axis=2)
    q_dec = q * jnp.exp(cs)
    att = jnp.einsum('bcihd,bcjhd->bchij', q_dec, k * jnp.exp(-cs))
    att = jnp.where(jnp.tril(jnp.ones((T, T), bool)), att, 0.0)
    o_intra = jnp.einsum('bchij,bcjhe->bcihe', att, v)
    u = jnp.einsum('bcthd,bcthe->bchde', k * jnp.exp(cs[:, :, -1:] - cs), v)
    chunk_decay = jnp.exp(cs[:, :, -1])

    def step(state, inp):
        d, uc = inp
        return d[..., None] * state + uc, state

    final, starts = lax.scan(step, init, (jnp.moveaxis(chunk_decay, 1, 0), jnp.moveaxis(u, 1, 0)))
    starts = jnp.moveaxis(starts, 0, 1)
    o_inter = jnp.einsum('bcthd,bchde->bcthe', q_dec, starts)
    return (o_intra + o_inter).reshape(bsz, seq, nh, dv), final


def gla_mixer(q, k, v, g, g_lr, gate_w, gate_b, norm_g, init):
    bsz, seq, _ = q.shape
    lr = g_lr.reshape(bsz, seq, 2, GLA_RANK)
    logits = jnp.einsum('bldr,drk->bldk', lr, gate_w.astype(jnp.float32)) + gate_b.astype(jnp.float32)
    lg = (jax.nn.log_sigmoid(logits) / GLA_GATE_NORMALIZER).reshape(bsz, seq, 2, GLA_HEADS, GLA_KEY_DIM)
    qh = q.reshape(bsz, seq, GLA_HEADS, GLA_KEY_DIM)
    kh = k.reshape(bsz, seq, GLA_HEADS, GLA_KEY_DIM)
    vh = v.reshape(bsz, seq, GLA_HEADS, GLA_VAL_DIM)
    o_f, fin_f = gla_scan(qh, kh, vh, lg[:, :, 0], init[0])
    o_b, fin_b = gla_scan(_flip(qh), _flip(kh), _flip(vh), _flip(lg[:, :, 1]), init[1])
    o = rms_norm(o_f + _flip(o_b), norm_g.reshape(GLA_HEADS, GLA_VAL_DIM)).reshape(bsz, seq, GLA_V)
    return o * jax.nn.silu(g), (fin_f, fin_b)


def even_layer(x, xc, c, c_ctx, norm_g, mod_w, mod_b, w_in, conv_w, conv_b, dt_bias, a_log, d_skip,
               ssd_norm, gla_gate_w, gla_gate_b, gla_norm, w_out, update_ctx):
    f32 = jnp.float32

    def branch_inputs(stream, cvec):
        shift, scale, gate = adaln(cvec, mod_w, mod_b)
        h = rms_norm(stream, norm_g) * (1 + scale) + shift
        return [p.astype(f32) for p in split_cols(jnp.dot(h, w_in), E_SPLITS)], gate

    def mixers(parts, ssd_init, gla_init):
        z, xbc, dt_raw, q, k, v, g, g_lr = parts
        y_ssd, ssd_fin = ssd_mixer(z, xbc, dt_raw, conv_w, conv_b, dt_bias, a_log, d_skip, ssd_norm, ssd_init)
        y_gla, gla_fin = gla_mixer(q, k, v, g, g_lr, gla_gate_w, gla_gate_b, gla_norm, gla_init)
        return jnp.concatenate([y_ssd, y_gla], axis=-1), ssd_fin, gla_fin

    bsz = x.shape[0]
    ssd0 = jnp.zeros((bsz, SSD_GROUPS, SSD_HEADS // SSD_GROUPS, SSD_HEAD_DIM, SSD_STATE), f32)
    gla0 = jnp.zeros((bsz, GLA_HEADS, GLA_KEY_DIM, GLA_VAL_DIM), f32)
    ctx_parts, ctx_gate = branch_inputs(xc, c_ctx[None])
    y_c, ssd_fin, gla_fin = mixers(ctx_parts, (ssd0, ssd0), (gla0, gla0))
    lat_parts, lat_gate = branch_inputs(x, c)
    y, _, _ = mixers(lat_parts, ssd_fin, gla_fin)
    x = x + (lat_gate * jnp.dot(y, w_out)).astype(x.dtype)
    if update_ctx:
        xc = xc + (ctx_gate * jnp.dot(y_c, w_out)).astype(xc.dtype)
    return x, xc


def axial_rope_tables(seq):
    rows = seq // GRID_W
    row = jnp.repeat(jnp.arange(rows, dtype=jnp.float32), GRID_W)
    col = jnp.tile(jnp.arange(GRID_W, dtype=jnp.float32), rows)
    inv = 1.0 / (ROPE_BASE ** (jnp.arange(ROPE_FREQS, dtype=jnp.float32) / ROPE_FREQS))
    ang = jnp.stack([row[:, None] * inv, col[:, None] * inv], axis=1)
    return jnp.cos(ang), jnp.sin(ang)


def apply_axial_rope(t, cos, sin):
    bsz, seq, nh, dh = t.shape
    t = t.reshape(bsz, seq, nh, 2, 2, ROPE_FREQS)
    t1, t2 = t[..., 0, :], t[..., 1, :]
    cs = cos[None, :, None]
    sn = sin[None, :, None]
    return jnp.stack([t1 * cs - t2 * sn, t2 * cs + t1 * sn], axis=-2).reshape(bsz, seq, nh, dh)


def _heads(t, n):
    return t.reshape(t.shape[0], t.shape[1], n, ATT_HEAD_DIM)


def band_blocks(t, nb):
    tp = jnp.pad(t, ((0, 0), (ATT_BLOCK, ATT_BLOCK), (0, 0), (0, 0)))
    tp = tp.reshape(t.shape[0], nb + 2, ATT_BLOCK, t.shape[2], t.shape[3])
    return jnp.concatenate([tp[:, :-2], tp[:, 1:-1], tp[:, 2:]], axis=2)


def band_mask(nb, seq):
    qpos = jnp.arange(nb)[:, None] * ATT_BLOCK + jnp.arange(ATT_BLOCK)[None]
    kpos = (jnp.arange(nb)[:, None] - 1) * ATT_BLOCK + jnp.arange(3 * ATT_BLOCK)[None]
    inside = (kpos >= 0) & (kpos < seq)
    return (jnp.abs(qpos[:, :, None] - kpos[:, None, :]) <= WINDOW) & inside[:, None, :]


def attend_with_sink(qb, key_sets, sink):
    scores = []
    for k, _, mask in key_sets:
        spec = 'bnqhgd,bnkhd->bnhgqk' if k.ndim == 5 else 'bnqhgd,bkhd->bnhgqk'
        s = jnp.einsum(spec, qb, k)
        if mask is not None:
            s = jnp.where(mask[None, :, None, None], s, -jnp.inf)
        scores.append(s)
    sink_l = sink.reshape(ATT_KV_HEADS, ATT_GROUP)[None, None, :, :, None, None]
    m = sink_l
    for s in scores:
        m = jnp.maximum(m, jnp.max(s, axis=-1, keepdims=True))
    denom = jnp.exp(sink_l - m)[..., 0]
    out = 0.0
    for s, (_, v, _) in zip(scores, key_sets):
        p = jnp.exp(s - m)
        denom = denom + jnp.sum(p, axis=-1)
        spec = 'bnhgqk,bnkhd->bnqhgd' if v.ndim == 5 else 'bnhgqk,bkhd->bnqhgd'
        out = out + jnp.einsum(spec, p, v)
    return out / jnp.moveaxis(denom, -1, 2)[..., None]


def odd_layer(x, xc, c, c_ctx, norm_g, mod_w, mod_b, w_in, q_norm, k_norm, sink, w_out, update_ctx):
    f32 = jnp.float32
    bsz, seq, _ = x.shape
    n_ctx = xc.shape[1]
    scale = ATT_HEAD_DIM ** -0.5
    sink = sink.astype(f32)
    c_shift, c_scale, c_gate = adaln(c_ctx[None], mod_w, mod_b)
    hc = rms_norm(xc, norm_g) * (1 + c_scale) + c_shift
    k_c, v_c = split_cols(jnp.dot(hc, w_in[:, :2 * ATT_KV_W]), (ATT_KV_W, ATT_KV_W))
    k_c = rms_norm(_heads(k_c, ATT_KV_HEADS), k_norm).astype(f32)
    v_c = _heads(v_c, ATT_KV_HEADS).astype(f32)
    shift, mscale, gate = adaln(c, mod_w, mod_b)
    h = rms_norm(x, norm_g) * (1 + mscale) + shift
    k, v, q, g = split_cols(jnp.dot(h, w_in), (ATT_KV_W, ATT_KV_W, ATT_W, ATT_W))
    cos, sin = axial_rope_tables(seq)
    q = apply_axial_rope(rms_norm(_heads(q, ATT_HEADS), q_norm).astype(f32), cos, sin)
    k = apply_axial_rope(rms_norm(_heads(k, ATT_KV_HEADS), k_norm).astype(f32), cos, sin)
    v = _heads(v, ATT_KV_HEADS).astype(f32)
    nb = seq // ATT_BLOCK
    qb = (q * scale).reshape(bsz, nb, ATT_BLOCK, ATT_KV_HEADS, ATT_GROUP, ATT_HEAD_DIM)
    o = attend_with_sink(qb, [(band_blocks(k, nb), band_blocks(v, nb), band_mask(nb, seq)), (k_c, v_c, None)], sink)
    o = o.reshape(bsz, seq, ATT_W) * jax.nn.silu(g.astype(f32))
    x = x + (gate * jnp.dot(o, w_out)).astype(x.dtype)
    if update_ctx:
        q_c, g_c = split_cols(jnp.dot(hc, w_in[:, 2 * ATT_KV_W:]), (ATT_W, ATT_W))
        q_c = rms_norm(_heads(q_c, ATT_HEADS), q_norm).astype(f32) * scale
        o_c = attend_with_sink(q_c.reshape(bsz, 1, n_ctx, ATT_KV_HEADS, ATT_GROUP, ATT_HEAD_DIM), [(k_c, v_c, None)], sink)
        o_c = o_c.reshape(bsz, n_ctx, ATT_W) * jax.nn.silu(g_c.astype(f32))
        xc = xc + (c_gate * jnp.dot(o_c, w_out)).astype(xc.dtype)
    return x, xc


def setup_inputs(seed: int = 0) -> dict:
    key = jax.random.key(seed)
    keys = list(jax.random.split(key, 32))
    f32 = jnp.float32

    def normal(shape, s):
        return jax.random.normal(keys.pop(), shape, f32) * s

    def gain(shape):
        return 1.0 + normal(shape, 0.02)

    E, O, D = N_EVEN, N_ODD, D_MODEL
    dt0 = jnp.exp(jax.random.uniform(keys.pop(), (E, 2, SSD_HEADS), f32, math.log(1e-3), math.log(1e-1)))
    a_log = jnp.log(jax.random.uniform(keys.pop(), (E, 2, SSD_HEADS), f32, 1.0, 16.0))
    return {
        'x': normal((BATCH, SEQ, D), 1.0),
        'c': normal((BATCH, D), 1.0),
        'ctx': normal((BATCH, CTX_LEN, D), 1.0),
        'c_ctx': normal((D,), 1.0),
        'e_norm': gain((E, D)),
        'e_mod_w': normal((E, D, 3 * D), 0.5 * D ** -0.5),
        'e_mod_b': normal((E, 3 * D), 0.02),
        'e_w_in': normal((E, D, E_IN), D ** -0.5),
        'e_conv_w': normal((E, CONV_K, CONV_CH), CONV_K ** -0.5),
        'e_conv_b': normal((E, CONV_CH), 0.02),
        'e_dt_bias': dt0 + jnp.log(-jnp.expm1(-dt0)),
        'e_a_log': a_log,
        'e_d_skip': gain((E, SSD_HEADS)),
        'e_ssd_norm': gain((E, SSD_WIDTH)),
        'e_gla_gate_w': normal((E, 2, GLA_RANK, GLA_K), GLA_RANK ** -0.5),
        'e_gla_gate_b': normal((E, 2, GLA_K), 0.1),
        'e_gla_norm': gain((E, GLA_V)),
        'e_w_out': normal((E, E_MIX, D), E_MIX ** -0.5),
        'o_norm': gain((O, D)),
        'o_mod_w': normal((O, D, 3 * D), 0.5 * D ** -0.5),
        'o_mod_b': normal((O, 3 * D), 0.02),
        'o_w_in': normal((O, D, O_IN), D ** -0.5),
        'o_q_norm': gain((O, ATT_HEAD_DIM)),
        'o_k_norm': gain((O, ATT_HEAD_DIM)),
        'o_sink': normal((O, ATT_HEADS), 0.5),
        'o_w_out': normal((O, ATT_W, D), ATT_W ** -0.5),
    }


def reference(x, c, ctx, c_ctx, e_norm, e_mod_w, e_mod_b, e_w_in, e_conv_w, e_conv_b, e_dt_bias, e_a_log,
              e_d_skip, e_ssd_norm, e_gla_gate_w, e_gla_gate_b, e_gla_norm, e_w_out, o_norm, o_mod_w, o_mod_b,
              o_w_in, o_q_norm, o_k_norm, o_sink, o_w_out):
    xc = ctx
    for i in range(DEPTH):
        update_ctx = i < DEPTH - 1
        j = i // 2
        if i % 2 == 0:
            x, xc = even_layer(x, xc, c, c_ctx, e_norm[j], e_mod_w[j], e_mod_b[j], e_w_in[j], e_conv_w[j],
                               e_conv_b[j], e_dt_bias[j], e_a_log[j], e_d_skip[j], e_ssd_norm[j],
                               e_gla_gate_w[j], e_gla_gate_b[j], e_gla_norm[j], e_w_out[j], update_ctx)
        else:
            x, xc = odd_layer(x, xc, c, c_ctx, o_norm[j], o_mod_w[j], o_mod_b[j], o_w_in[j], o_q_norm[j],
                              o_k_norm[j], o_sink[j], o_w_out[j], update_ctx)
    return x
```

```python
import functools

import jax
import jax.numpy as jnp
import numpy as np
from jax import lax
from jax.experimental import pallas as pl
from jax.experimental.pallas import tpu as pltpu

F32 = jnp.float32
BF16 = jnp.bfloat16

GRID_W = 64
SSD_HEADS = 16
SSD_HEAD_DIM = 64
SSD_WIDTH = SSD_HEADS * SSD_HEAD_DIM
SSD_GROUPS = 2
SSD_STATE = 128
SSD_CHUNK = 128
CONV_K = 5
CONV_CH = SSD_WIDTH + 2 * SSD_GROUPS * SSD_STATE
GLA_HEADS = 4
GLA_KEY_DIM = 128
GLA_VAL_DIM = 256
GLA_K = GLA_HEADS * GLA_KEY_DIM
GLA_V = GLA_HEADS * GLA_VAL_DIM
GLA_RANK = 16
GLA_GATE_NORMALIZER = 16.0
GLA_CHUNK = 64
ATT_HEADS = 16
ATT_KV_HEADS = 4
ATT_GROUP = ATT_HEADS // ATT_KV_HEADS
ATT_HEAD_DIM = 128
ATT_W = ATT_HEADS * ATT_HEAD_DIM
ATT_KV_W = ATT_KV_HEADS * ATT_HEAD_DIM
WINDOW = 128
ATT_BLOCK = 128
ROPE_BASE = 10000.0
ROPE_FREQS = ATT_HEAD_DIM // 4
NORM_EPS = 1e-6

LANES = 128
SUBLANES = 8
VMEM_LIMIT_BYTES = 56 * 1024 * 1024

SSD_GROUP_W = SSD_WIDTH // SSD_GROUPS
SMALL_W = LANES
HALO = SUBLANES


def _dot(a, b):
    return jnp.dot(a.astype(BF16), b.astype(BF16), preferred_element_type=F32)


def _dot_nt(a, b):
    return lax.dot_general(a.astype(BF16), b.astype(BF16), (((1,), (1,)), ((), ())),
                           preferred_element_type=F32)


def _dot_tn(a, b):
    return lax.dot_general(a.astype(BF16), b.astype(BF16), (((0,), (0,)), ((), ())),
                           preferred_element_type=F32)


def _split(v):
    hi = v.astype(BF16)
    lo = (v - hi.astype(F32)).astype(BF16)
    return hi, lo


def _dot_split_lhs(v, m):
    hi, lo = _split(v)
    return (jnp.dot(hi, m, preferred_element_type=F32) + jnp.dot(lo, m, preferred_element_type=F32))


def _dot_split_rhs(m, v):
    hi, lo = _split(v)
    return (jnp.dot(m, hi, preferred_element_type=F32) + jnp.dot(m, lo, preferred_element_type=F32))


def _dot3(a, b):
    ah, al = _split(a)
    bh, bl = _split(b)
    return (jnp.dot(ah, bh, preferred_element_type=F32) + jnp.dot(al, bh, preferred_element_type=F32)
            + jnp.dot(ah, bl, preferred_element_type=F32))


def _silu(v):
    return v * jax.nn.sigmoid(v)


def _tri(n):
    row = lax.broadcasted_iota(jnp.int32, (n, n), 0)
    col = lax.broadcasted_iota(jnp.int32, (n, n), 1)
    return row >= col, col >= row


def _params(*sem):
    return pltpu.CompilerParams(dimension_semantics=sem, vmem_limit_bytes=VMEM_LIMIT_BYTES)


def _adaln_kernel(c_ref, w_ref, b_ref, o_ref):
    o_ref[...] = _dot3(_silu(c_ref[...]), w_ref[...]) + b_ref[...]


def _adaln(cvecs, w, b):
    rows, d = cvecs.shape
    n = w.shape[1]
    tn = 1024
    return pl.pallas_call(
        _adaln_kernel,
        grid=(n // tn,),
        in_specs=[pl.BlockSpec((rows, d), lambda j: (0, 0)),
                  pl.BlockSpec((d, tn), lambda j: (0, j)),
                  pl.BlockSpec((1, tn), lambda j: (0, j))],
        out_specs=pl.BlockSpec((rows, tn), lambda j: (0, j)),
        out_shape=jax.ShapeDtypeStruct((rows, n), F32),
        compiler_params=_params("parallel"),
        name="adaln",
    )(cvecs, w, b.reshape(1, n))


def _modulated_norm(x, g, sc, sh):
    r = lax.rsqrt(jnp.mean(x * x, axis=-1, keepdims=True) + NORM_EPS)
    return ((x * r) * g) * (1.0 + sc) + sh


def _store_cols(h, w_ref, off, ref):
    n = ref.shape[-1]
    for c0 in range(0, n, 512):
        c1 = min(n, c0 + 512)
        ref[:, c0:c1] = jnp.dot(h, w_ref[:, off + c0:off + c1],
                                preferred_element_type=F32).astype(ref.dtype)
    return off + n


def _proj_even_kernel(x_ref, g_ref, sc_ref, sh_ref, w_ref, *out_refs):
    h = _modulated_norm(x_ref[...], g_ref[...], sc_ref[...], sh_ref[...]).astype(BF16)
    off = 0
    for ref in out_refs:
        off = _store_cols(h, w_ref, off, ref)


def _head_norm(t, gain):
    r = lax.rsqrt(jnp.mean(t * t, axis=-1, keepdims=True) + NORM_EPS)
    return (t * r) * gain


def _rope(t, cos, sin_signed, first_half):
    partner = jnp.where(first_half, pltpu.roll(t, LANES - ROPE_FREQS, 1), pltpu.roll(t, ROPE_FREQS, 1))
    return t * cos + partner * sin_signed


def _proj_odd_kernel(x_ref, g_ref, sc_ref, sh_ref, w_ref, cos_ref, sin_ref, qn_ref, kn_ref,
                     k_ref, v_ref, *qg_refs):
    h = _modulated_norm(x_ref[...], g_ref[...], sc_ref[...], sh_ref[...]).astype(BF16)
    cos = cos_ref[...]
    sin = sin_ref[...]
    lane = lax.broadcasted_iota(jnp.int32, cos.shape, 1)
    first_half = (lane % (2 * ROPE_FREQS)) < ROPE_FREQS
    scale = ATT_HEAD_DIM ** -0.5

    def head(col, gain):
        t = jnp.dot(h, w_ref[:, col:col + ATT_HEAD_DIM], preferred_element_type=F32)
        return _rope(_head_norm(t, gain), cos, sin, first_half)

    for j in range(ATT_KV_HEADS):
        c0 = j * ATT_HEAD_DIM
        k_ref[:, c0:c0 + ATT_HEAD_DIM] = head(c0, kn_ref[...]).astype(k_ref.dtype)
    _store_cols(h, w_ref, ATT_KV_W, v_ref)
    if qg_refs:
        q_ref, gate_ref = qg_refs
        for j in range(ATT_HEADS):
            c0 = j * ATT_HEAD_DIM
            q_ref[:, c0:c0 + ATT_HEAD_DIM] = (head(2 * ATT_KV_W + c0, qn_ref[...]) * scale).astype(q_ref.dtype)
        _store_cols(h, w_ref, 2 * ATT_KV_W + ATT_W, gate_ref)


def _row_tile(length, want):
    return min(length, want)


def _proj_even(x, norm_g, scale, shift, w_bf16, widths, tm):
    bsz, length, d = x.shape
    tm = _row_tile(length, tm)
    n = w_bf16.shape[1]
    row = lambda b, i: (b, i, 0)
    mod = lambda b, i: (b, 0, 0)
    return pl.pallas_call(
        _proj_even_kernel,
        grid=(bsz, length // tm),
        in_specs=[pl.BlockSpec((None, tm, d), row),
                  pl.BlockSpec((1, d), lambda b, i: (0, 0)),
                  pl.BlockSpec((None, 1, d), mod),
                  pl.BlockSpec((None, 1, d), mod),
                  pl.BlockSpec((d, n), lambda b, i: (0, 0))],
        out_specs=[pl.BlockSpec((None, tm, wd), row) for wd in widths],
        out_shape=[jax.ShapeDtypeStruct((bsz, length, wd), F32) for wd in widths],
        compiler_params=_params("parallel", "parallel"),
        name="proj_even",
    )(x, norm_g.reshape(1, d), scale, shift, w_bf16)


def _proj_odd(x, norm_g, scale, shift, w_bf16, cos, sin, q_norm, k_norm, with_queries, tm):
    bsz, length, d = x.shape
    tm = _row_tile(length, tm)
    n = w_bf16.shape[1]
    row = lambda b, i: (b, i, 0)
    mod = lambda b, i: (b, 0, 0)
    const = lambda b, i: (0, 0)
    widths = [(ATT_KV_W, BF16), (ATT_KV_W, BF16)]
    if with_queries:
        widths += [(ATT_W, BF16), (ATT_W, F32)]
    return pl.pallas_call(
        _proj_odd_kernel,
        grid=(bsz, length // tm),
        in_specs=[pl.BlockSpec((None, tm, d), row),
                  pl.BlockSpec((1, d), const),
                  pl.BlockSpec((None, 1, d), mod),
                  pl.BlockSpec((None, 1, d), mod),
                  pl.BlockSpec((d, n), const),
                  pl.BlockSpec((tm, ATT_HEAD_DIM), lambda b, i: (i, 0)),
                  pl.BlockSpec((tm, ATT_HEAD_DIM), lambda b, i: (i, 0)),
                  pl.BlockSpec((1, ATT_HEAD_DIM), const),
                  pl.BlockSpec((1, ATT_HEAD_DIM), const)],
        out_specs=[pl.BlockSpec((None, tm, wd), row) for wd, _ in widths],
        out_shape=[jax.ShapeDtypeStruct((bsz, length, wd), dt) for wd, dt in widths],
        compiler_params=_params("parallel", "parallel"),
        name="proj_odd_q" if with_queries else "proj_odd_kv",
    )(x, norm_g.reshape(1, d), scale, shift, w_bf16, cos, sin,
      q_norm.reshape(1, ATT_HEAD_DIM), k_norm.reshape(1, ATT_HEAD_DIM))


def _conv_silu(win_s, prev_ref, cur_ref, next_ref, has_prev, has_next, cw_ref, cb_ref, out_s):
    t = cur_ref.shape[0]
    half = (CONV_K - 1) // 2
    win_s[0:HALO, :] = jnp.where(has_prev, prev_ref[...], 0.0)
    win_s[HALO:HALO + t, :] = cur_ref[...]
    win_s[HALO + t:2 * HALO + t, :] = jnp.where(has_next, next_ref[...], 0.0)
    for c0 in range(0, CONV_CH, 2 * LANES):
        cs = slice(c0, c0 + 2 * LANES)
        acc = cb_ref[:, cs] + cw_ref[0:1, cs] * win_s[HALO - half:HALO - half + t, cs]
        for tap in range(1, CONV_K):
            r0 = HALO - half + tap
            acc = acc + cw_ref[tap:tap + 1, cs] * win_s[r0:r0 + t, cs]
        out_s[:, cs] = _silu(acc)


def _ssd_kernel(fp_ref, fc_ref, fn_ref, fsm_ref, bp_ref, bc_ref, bn_ref, bsm_ref,
                cw_ref, cb_ref, bias_row_ref, bias_col_ref, alog_row_ref, alog_col_ref, dskip_ref,
                ef_ref, eb_ref, sf0_ref, sb0_ref,
                ya_ref, yb_ref, sff_ref, sbf_ref,
                sf_s, sb_s, win_s, xf_s, xb_s):
    i = pl.program_id(1)
    nc = pl.num_programs(1)
    t = SSD_CHUNK
    nh = SSD_HEADS
    xoff_b = SSD_WIDTH
    xoff_c = SSD_WIDTH + SSD_GROUPS * SSD_STATE

    @pl.when(i == 0)
    def _():
        sf_s[...] = sf0_ref[...]
        sb_s[...] = sb0_ref[...]

    lower, upper = _tri(t)
    ltri = jnp.where(lower, 1.0, 0.0).astype(BF16)
    utri = jnp.where(upper, 1.0, 0.0).astype(BF16)
    a_row = -jnp.exp(alog_row_ref[...])
    a_col = -jnp.exp(alog_col_ref[...])
    ef = ef_ref[...]
    eb = eb_ref[...]
    neg_inf = float("-inf")

    _conv_silu(win_s, fp_ref, fc_ref, fn_ref, i > 0, i < nc - 1, cw_ref, cb_ref, xf_s)
    sm = fsm_ref[...]
    dt = jax.nn.softplus(sm + bias_row_ref[...])
    dta = dt * a_row
    cs = _dot_split_rhs(ltri, dta)
    rc = _dot_split_rhs(utri, dta)
    dt_t = jax.nn.softplus(sm.T[0:2 * nh, :] + bias_col_ref[...])
    dta_t = dt_t * a_col
    cs_t = _dot_split_lhs(dta_t, utri)
    rc_t = _dot_split_lhs(dta_t, ltri)
    dec_f = _dot_split_lhs(jnp.exp(cs), ef)
    wx_f = xf_s[:, 0:SSD_WIDTH] * _dot_split_lhs(dt * jnp.exp(cs[t - 1:t, :] - cs), ef)
    lane = lax.broadcasted_iota(jnp.int32, (t, LANES), 1)
    left = lane < SSD_HEAD_DIM

    def decay_matrix(h, cbg):
        a = cs[:, h:h + 1] - cs_t[h:h + 1, :]
        lf = jnp.exp(jnp.where(lower, a, neg_inf))
        b = rc[:, nh + h:nh + h + 1] - rc_t[nh + h:nh + h + 1, :]
        ub = jnp.exp(jnp.where(upper, b, neg_inf))
        return cbg * (lf * dt_t[h:h + 1, :] + ub * dt_t[nh + h:nh + h + 1, :])

    heads_per_group = nh // SSD_GROUPS
    for g in range(SSD_GROUPS):
        cg = xf_s[:, xoff_c + g * SSD_STATE:xoff_c + (g + 1) * SSD_STATE]
        bg = xf_s[:, xoff_b + g * SSD_STATE:xoff_b + (g + 1) * SSD_STATE]
        gs = slice(g * SSD_GROUP_W, (g + 1) * SSD_GROUP_W)
        cbg = _dot_nt(cg, bg)
        y_off = _dot(cg, sf_s[g]) * dec_f[:, gs]
        for pair in range(heads_per_group // 2):
            h0 = g * heads_per_group + 2 * pair
            p0 = (h0 // 2) * LANES
            xp = xf_s[:, p0:p0 + LANES]
            y = (_dot(decay_matrix(h0, cbg), jnp.where(left, xp, 0.0))
                 + _dot(decay_matrix(h0 + 1, cbg), jnp.where(left, 0.0, xp)))
            ya_ref[:, p0:p0 + LANES] = (y + y_off[:, pair * LANES:(pair + 1) * LANES]
                                        + dskip_ref[:, p0:p0 + LANES] * xp)
        sf_s[g] = dec_f[t - 1:t, gs] * sf_s[g] + _dot(bg.T, wx_f[:, gs])

    _conv_silu(win_s, bp_ref, bc_ref, bn_ref, i < nc - 1, i > 0, cw_ref, cb_ref, xb_s)
    dt2 = jax.nn.softplus(bsm_ref[...] + bias_row_ref[...])
    rc2 = _dot_split_rhs(utri, dt2 * a_row)
    dec_b = _dot_split_lhs(jnp.exp(rc2), eb)
    wx_b = xb_s[:, 0:SSD_WIDTH] * _dot_split_lhs(dt2 * jnp.exp(rc2[0:1, :] - rc2), eb)
    for g in range(SSD_GROUPS):
        cg = xb_s[:, xoff_c + g * SSD_STATE:xoff_c + (g + 1) * SSD_STATE]
        bg = xb_s[:, xoff_b + g * SSD_STATE:xoff_b + (g + 1) * SSD_STATE]
        gs = slice(g * SSD_GROUP_W, (g + 1) * SSD_GROUP_W)
        yb_ref[:, gs] = _dot(cg, sb_s[g]) * dec_b[:, gs]
        sb_s[g] = dec_b[0:1, gs] * sb_s[g] + _dot(bg.T, wx_b[:, gs])

    @pl.when(i == nc - 1)
    def _():
        sff_ref[...] = sf_s[...]
        sbf_ref[...] = sb_s[...]


def _ssd(xbc, small, conv_w, conv_b, bias_row, bias_col, alog_row, alog_col, dskip_row, ef, eb, sf0, sb0):
    bsz, length, _ = xbc.shape
    t = SSD_CHUNK
    nc = length // t
    per = t // HALO
    last_halo = length // HALO - 1
    state_shape = (SSD_GROUPS, SSD_STATE, SSD_GROUP_W)

    def cur_f(b, i): return (b, i, 0)
    def prev_f(b, i): return (b, jnp.maximum(i * per - 1, 0), 0)
    def next_f(b, i): return (b, jnp.minimum((i + 1) * per, last_halo), 0)
    def cur_b(b, i): return (b, nc - 1 - i, 0)
    def prev_b(b, i): return (b, jnp.maximum((nc - 1 - i) * per - 1, 0), 0)
    def next_b(b, i): return (b, jnp.minimum((nc - i) * per, last_halo), 0)
    const2 = lambda b, i: (0, 0)
    state = lambda b, i: (b, 0, 0, 0)

    halo_spec = lambda f: pl.BlockSpec((None, HALO, CONV_CH), f)
    chunk_spec = lambda f, w: pl.BlockSpec((None, t, w), f)
    state_spec = pl.BlockSpec((None,) + state_shape, state)
    return pl.pallas_call(
        _ssd_kernel,
        grid=(bsz, nc),
        in_specs=[halo_spec(prev_f), chunk_spec(cur_f, CONV_CH), halo_spec(next_f), chunk_spec(cur_f, SMALL_W),
                  halo_spec(prev_b), chunk_spec(cur_b, CONV_CH), halo_spec(next_b), chunk_spec(cur_b, SMALL_W),
                  pl.BlockSpec((CONV_K, CONV_CH), const2),
                  pl.BlockSpec((1, CONV_CH), const2),
                  pl.BlockSpec((1, SMALL_W), const2),
                  pl.BlockSpec((2 * SSD_HEADS, 1), const2),
                  pl.BlockSpec((1, SMALL_W), const2),
                  pl.BlockSpec((2 * SSD_HEADS, 1), const2),
                  pl.BlockSpec((1, SSD_WIDTH), const2),
                  pl.BlockSpec((SMALL_W, SSD_WIDTH), const2),
                  pl.BlockSpec((SMALL_W, SSD_WIDTH), const2),
                  state_spec, state_spec],
        out_specs=[chunk_spec(cur_f, SSD_WIDTH), chunk_spec(cur_b, SSD_WIDTH), state_spec, state_spec],
        out_shape=[jax.ShapeDtypeStruct((bsz, length, SSD_WIDTH), F32),
                   jax.ShapeDtypeStruct((bsz, length, SSD_WIDTH), F32),
                   jax.ShapeDtypeStruct((bsz,) + state_shape, F32),
                   jax.ShapeDtypeStruct((bsz,) + state_shape, F32)],
        scratch_shapes=[pltpu.VMEM(state_shape, F32), pltpu.VMEM(state_shape, F32),
                        pltpu.VMEM((t + 2 * HALO, CONV_CH), F32),
                        pltpu.VMEM((t, CONV_CH), F32), pltpu.VMEM((t, CONV_CH), F32)],
        compiler_params=_params("arbitrary", "arbitrary"),
        name="ssd",
    )(xbc, xbc, xbc, small, xbc, xbc, xbc, small, conv_w, conv_b, bias_row, bias_col,
      alog_row, alog_col, dskip_row, ef, eb, sf0, sb0)


def _log_sigmoid(v):
    return jnp.minimum(v, 0.0) - jnp.log1p(jnp.exp(-jnp.abs(v)))


def _gla_kernel(fq_ref, fk_ref, fv_ref, fsm_ref, bq_ref, bk_ref, bv_ref, bsm_ref,
                wg_ref, gb_ref, sf0_ref, sb0_ref,
                oa_ref, ob_ref, sff_ref, sbf_ref,
                sf_s, sb_s):
    i = pl.program_id(1)
    n_steps = pl.num_programs(1)
    t = GLA_CHUNK
    rows = fq_ref.shape[0]
    n_sub = rows // t
    dk, dv = GLA_KEY_DIM, GLA_VAL_DIM
    qscale = dk ** -0.5
    inv_norm = 1.0 / GLA_GATE_NORMALIZER

    @pl.when(i == 0)
    def _():
        sf_s[...] = sf0_ref[...]
        sb_s[...] = sb0_ref[...]

    lower, upper = _tri(t)
    ltri = jnp.where(lower, 1.0, 0.0).astype(BF16)
    utri = jnp.where(upper, 1.0, 0.0).astype(BF16)
    wg = wg_ref[...]
    wg_hi, wg_lo = _split(wg)
    gb = gb_ref[...]

    def log_gates(sm, c0, c1):
        hi, lo = _split(sm)
        logits = (jnp.dot(hi, wg_hi[:, c0:c1], preferred_element_type=F32)
                  + jnp.dot(lo, wg_hi[:, c0:c1], preferred_element_type=F32)
                  + jnp.dot(hi, wg_lo[:, c0:c1], preferred_element_type=F32)) + gb[:, c0:c1]
        return _log_sigmoid(logits) * inv_norm

    for s in range(n_sub):
        rs = slice(s * t, (s + 1) * t)
        lg = log_gates(fsm_ref[rs, :], 0, 2 * GLA_K)
        cs = _dot_split_rhs(ltri, lg[:, 0:GLA_K])
        rc = _dot_split_rhs(utri, lg[:, GLA_K:2 * GLA_K])
        for h in range(GLA_HEADS):
            ks = slice(h * dk, (h + 1) * dk)
            vs = slice(h * dv, (h + 1) * dv)
            c = cs[:, ks]
            r = rc[:, ks]
            qh = fq_ref[rs, ks] * qscale
            kh = fk_ref[rs, ks]
            vh = fv_ref[rs, vs]
            qdf = qh * jnp.exp(c)
            att = (jnp.where(lower, _dot_nt(qdf, kh * jnp.exp(-c)), 0.0)
                   + jnp.where(upper, _dot_nt(qh * jnp.exp(r), kh * jnp.exp(-r)), 0.0))
            oa_ref[rs, vs] = _dot(att, vh) + _dot_nt(qdf, sf_s[h])
            kw = kh * jnp.exp(c[t - 1:t, :] - c)
            sf_s[h] = sf_s[h] * jnp.exp(c[t - 1:t, :]) + _dot_tn(vh, kw)

    for s in reversed(range(n_sub)):
        rs = slice(s * t, (s + 1) * t)
        lg = log_gates(bsm_ref[rs, :], GLA_K, 2 * GLA_K)
        rc = _dot_split_rhs(utri, lg)
        for h in range(GLA_HEADS):
            ks = slice(h * dk, (h + 1) * dk)
            vs = slice(h * dv, (h + 1) * dv)
            r = rc[:, ks]
            kh = bk_ref[rs, ks]
            vh = bv_ref[rs, vs]
            qdb = (bq_ref[rs, ks] * qscale) * jnp.exp(r)
            ob_ref[rs, vs] = _dot_nt(qdb, sb_s[h])
            kw = kh * jnp.exp(r[0:1, :] - r)
            sb_s[h] = sb_s[h] * jnp.exp(r[0:1, :]) + _dot_tn(vh, kw)

    @pl.when(i == n_steps - 1)
    def _():
        sff_ref[...] = sf_s[...]
        sbf_ref[...] = sb_s[...]


def _gla(q, k, v, small, wg, gb, sf0, sb0, rows):
    bsz, length, _ = q.shape
    rows = _row_tile(length, rows)
    n_steps = length // rows
    state_shape = (GLA_HEADS, GLA_VAL_DIM, GLA_KEY_DIM)
    fwd = lambda b, i: (b, i, 0)
    bwd = lambda b, i: (b, n_steps - 1 - i, 0)
    const2 = lambda b, i: (0, 0)
    state = lambda b, i: (b, 0, 0, 0)
    blk = lambda f, w: pl.BlockSpec((None, rows, w), f)
    state_spec = pl.BlockSpec((None,) + state_shape, state)
    return pl.pallas_call(
        _gla_kernel,
        grid=(bsz, n_steps),
        in_specs=[blk(fwd, GLA_K), blk(fwd, GLA_K), blk(fwd, GLA_V), blk(fwd, SMALL_W),
                  blk(bwd, GLA_K), blk(bwd, GLA_K), blk(bwd, GLA_V), blk(bwd, SMALL_W),
                  pl.BlockSpec((SMALL_W, 2 * GLA_K), const2),
                  pl.BlockSpec((1, 2 * GLA_K), const2),
                  state_spec, state_spec],
        out_specs=[blk(fwd, GLA_V), blk(bwd, GLA_V), state_spec, state_spec],
        out_shape=[jax.ShapeDtypeStruct((bsz, length, GLA_V), F32),
                   jax.ShapeDtypeStruct((bsz, length, GLA_V), F32),
                   jax.ShapeDtypeStruct((bsz,) + state_shape, F32),
                   jax.ShapeDtypeStruct((bsz,) + state_shape, F32)],
        scratch_shapes=[pltpu.VMEM(state_shape, F32), pltpu.VMEM(state_shape, F32)],
        compiler_params=_params("arbitrary", "arbitrary"),
        name="gla",
    )(q, k, v, small, q, k, v, small, wg, gb, sf0, sb0)


def _out_even_kernel(x_ref, gate_ref, ya_ref, yb_ref, z_ref, oa_ref, ob_ref, g_ref,
                     sn_ref, gn_ref, w_ref, o_ref):
    acc = None
    for g in range(SSD_GROUPS):
        cs = slice(g * SSD_GROUP_W, (g + 1) * SSD_GROUP_W)
        y = (ya_ref[:, cs] + yb_ref[:, cs]) * _silu(z_ref[:, cs])
        part = _dot(_head_norm(y, sn_ref[:, cs]), w_ref[cs, :])
        acc = part if acc is None else acc + part
    for h in range(GLA_HEADS):
        cs = slice(h * GLA_VAL_DIM, (h + 1) * GLA_VAL_DIM)
        o = _head_norm(oa_ref[:, cs] + ob_ref[:, cs], gn_ref[:, cs]) * _silu(g_ref[:, cs])
        acc = acc + _dot(o, w_ref[SSD_WIDTH + h * GLA_VAL_DIM:SSD_WIDTH + (h + 1) * GLA_VAL_DIM, :])
    o_ref[...] = x_ref[...] + gate_ref[...] * acc


def _out_even(x, gate, ya, yb, z, oa, ob, g, ssd_norm, gla_norm, w_bf16, tm):
    bsz, length, d = x.shape
    tm = _row_tile(length, tm)
    row = lambda b, i: (b, i, 0)
    const2 = lambda b, i: (0, 0)
    blk = lambda w: pl.BlockSpec((None, tm, w), row)
    return pl.pallas_call(
        _out_even_kernel,
        grid=(bsz, length // tm),
        in_specs=[blk(d), pl.BlockSpec((None, 1, d), lambda b, i: (b, 0, 0)),
                  blk(SSD_WIDTH), blk(SSD_WIDTH), blk(SSD_WIDTH), blk(GLA_V), blk(GLA_V), blk(GLA_V),
                  pl.BlockSpec((1, SSD_WIDTH), const2), pl.BlockSpec((1, GLA_V), const2),
                  pl.BlockSpec(w_bf16.shape, const2)],
        out_specs=blk(d),
        out_shape=jax.ShapeDtypeStruct((bsz, length, d), F32),
        compiler_params=_params("parallel", "parallel"),
        name="out_even",
    )(x, gate, ya, yb, z, oa, ob, g, ssd_norm.reshape(1, -1), gla_norm.reshape(1, -1), w_bf16)


def _attn_kernel(sink_ref, q_ref, k_ref, v_ref, kc_ref, vc_ref, g_ref, o_ref):
    j = pl.program_id(1)
    i = pl.program_id(2)
    length = k_ref.shape[0]
    blk = ATT_BLOCK
    band = 3 * blk
    dh = ATT_HEAD_DIM
    start = pl.multiple_of(jnp.clip((i - 1) * blk, 0, length - band), blk)
    kw = k_ref[pl.ds(start, band), :]
    vw = v_ref[pl.ds(start, band), :]
    q = jnp.concatenate([q_ref[:, g * dh:(g + 1) * dh] for g in range(ATT_GROUP)], axis=0)
    rows = ATT_GROUP * blk
    s_band = _dot_nt(q, kw)
    s_ctx = _dot_nt(q, kc_ref[...])
    rid = lax.broadcasted_iota(jnp.int32, (rows, band), 0)
    cid = lax.broadcasted_iota(jnp.int32, (rows, band), 1)
    qpos = i * blk + rid % blk
    kpos = start + cid
    s_band = jnp.where(jnp.abs(qpos - kpos) <= WINDOW, s_band, float("-inf"))
    rid1 = lax.broadcasted_iota(jnp.int32, (rows, 1), 0)
    sink = jnp.zeros((rows, 1), F32)
    for g in range(ATT_GROUP):
        sink = jnp.where(rid1 // blk == g, sink_ref[j * ATT_GROUP + g], sink)
    m = jnp.maximum(sink, jnp.maximum(jnp.max(s_band, axis=-1, keepdims=True),
                                      jnp.max(s_ctx, axis=-1, keepdims=True)))
    p_band = jnp.exp(s_band - m)
    p_ctx = jnp.exp(s_ctx - m)
    denom = (jnp.exp(sink - m) + jnp.sum(p_band, axis=-1, keepdims=True)
             + jnp.sum(p_ctx, axis=-1, keepdims=True))
    out = (_dot(p_band, vw) + _dot(p_ctx, vc_ref[...])) / denom
    for g in range(ATT_GROUP):
        cs = slice(g * dh, (g + 1) * dh)
        o_ref[:, cs] = (out[g * blk:(g + 1) * blk, :] * _silu(g_ref[:, cs])).astype(o_ref.dtype)


def _attention(sink, q, k, v, kc, vc, gate):
    bsz, length, _ = q.shape
    n_ctx = kc.shape[1]
    nb = length // ATT_BLOCK
    gw = ATT_GROUP * ATT_HEAD_DIM
    qmap = lambda b, j, i, s: (b, i, j)
    kvmap = lambda b, j, i, s: (b, 0, j)
    grid_spec = pltpu.PrefetchScalarGridSpec(
        num_scalar_prefetch=1,
        grid=(bsz, ATT_KV_HEADS, nb),
        in_specs=[pl.BlockSpec((None, ATT_BLOCK, gw), qmap),
                  pl.BlockSpec((None, length, ATT_HEAD_DIM), kvmap),
                  pl.BlockSpec((None, length, ATT_HEAD_DIM), kvmap),
                  pl.BlockSpec((None, n_ctx, ATT_HEAD_DIM), kvmap),
                  pl.BlockSpec((None, n_ctx, ATT_HEAD_DIM), kvmap),
                  pl.BlockSpec((None, ATT_BLOCK, gw), qmap)],
        out_specs=pl.BlockSpec((None, ATT_BLOCK, gw), qmap),
    )
    return pl.pallas_call(
        _attn_kernel,
        grid_spec=grid_spec,
        out_shape=jax.ShapeDtypeStruct((bsz, length, ATT_W), BF16),
        compiler_params=_params("parallel", "parallel", "arbitrary"),
        name="attention",
    )(sink, q, k, v, kc, vc, gate)


def _out_odd_kernel(x_ref, gate_ref, o_ref, w_ref, y_ref):
    y_ref[...] = x_ref[...] + gate_ref[...] * jnp.dot(o_ref[...], w_ref[...], preferred_element_type=F32)


def _out_odd(x, gate, o, w_bf16, tm):
    bsz, length, d = x.shape
    tm = _row_tile(length, tm)
    row = lambda b, i: (b, i, 0)
    return pl.pallas_call(
        _out_odd_kernel,
        grid=(bsz, length // tm),
        in_specs=[pl.BlockSpec((None, tm, d), row),
                  pl.BlockSpec((None, 1, d), lambda b, i: (b, 0, 0)),
                  pl.BlockSpec((None, tm, o.shape[-1]), row),
                  pl.BlockSpec(w_bf16.shape, lambda b, i: (0, 0))],
        out_specs=pl.BlockSpec((None, tm, d), row),
        out_shape=jax.ShapeDtypeStruct((bsz, length, d), F32),
        compiler_params=_params("parallel", "parallel"),
        name="out_odd",
    )(x, gate, o, w_bf16)


EVEN_WIDTHS = (SSD_WIDTH, CONV_CH, GLA_K, GLA_K, GLA_V, GLA_V, SMALL_W)


def _even_weight_layout(w_in):
    d = w_in.shape[0]
    sizes = (SSD_WIDTH, CONV_CH, 2 * SSD_HEADS, GLA_K, GLA_K, GLA_V, GLA_V, 2 * GLA_RANK)
    z, xbc, dt, q, k, v, g, lr = jnp.split(w_in, np.cumsum(sizes)[:-1].tolist(), axis=1)
    pad = jnp.zeros((d, SMALL_W - 2 * SSD_HEADS - 2 * GLA_RANK), w_in.dtype)
    return jnp.concatenate([z, xbc, q, k, v, g, dt, lr, pad], axis=1).astype(BF16)


def _expansion_matrices():
    rows = np.arange(SMALL_W)[:, None]
    heads = (np.arange(SSD_WIDTH) // SSD_HEAD_DIM)[None, :]
    ef = (rows == heads).astype(np.float32)
    eb = (rows - SSD_HEADS == heads).astype(np.float32)
    return jnp.asarray(ef, BF16), jnp.asarray(eb, BF16)


def _pad_lanes(v, width):
    return jnp.pad(v, ((0, 0), (0, width - v.shape[1])))


def _rope_tables(length):
    rows = length // GRID_W
    row = jnp.repeat(jnp.arange(rows, dtype=F32), GRID_W)
    col = jnp.tile(jnp.arange(GRID_W, dtype=F32), rows)
    inv = 1.0 / (ROPE_BASE ** (jnp.arange(ROPE_FREQS, dtype=F32) / ROPE_FREQS))
    ang_r = row[:, None] * inv
    ang_c = col[:, None] * inv
    cos = jnp.concatenate([jnp.cos(ang_r), jnp.cos(ang_r), jnp.cos(ang_c), jnp.cos(ang_c)], axis=1)
    sin = jnp.concatenate([-jnp.sin(ang_r), jnp.sin(ang_r), -jnp.sin(ang_c), jnp.sin(ang_c)], axis=1)
    return cos, sin


def _mod_rows(mod, rows, bsz, d):
    picked = jnp.broadcast_to(mod[rows], (bsz, 3 * d)) if isinstance(rows, int) else mod[rows]
    return [picked[:, None, j * d:(j + 1) * d] for j in range(3)]


PROJ_ROWS = 256
OUT_ROWS = 256
GLA_ROWS = 256
CVEC_ROWS = SUBLANES


def kernel(x, c, ctx, c_ctx, e_norm, e_mod_w, e_mod_b, e_w_in, e_conv_w, e_conv_b, e_dt_bias, e_a_log,
           e_d_skip, e_ssd_norm, e_gla_gate_w, e_gla_gate_b, e_gla_norm, e_w_out, o_norm, o_mod_w, o_mod_b,
           o_w_in, o_q_norm, o_k_norm, o_sink, o_w_out):
    bsz, length, d = x.shape
    n_ctx = ctx.shape[1]
    assert e_norm.shape[0] == 1 and o_norm.shape[0] == 1, "two-layer block only"
    assert length % SSD_CHUNK == 0 and n_ctx % SSD_CHUNK == 0 and length >= 3 * ATT_BLOCK
    assert bsz + 1 <= CVEC_ROWS

    cvecs = jnp.zeros((CVEC_ROWS, d), F32).at[:bsz].set(c).at[bsz].set(c_ctx)
    lat_rows = slice(0, bsz)

    mod = _adaln(cvecs, e_mod_w[0], e_mod_b[0])
    shift, scale, gate = _mod_rows(mod, lat_rows, bsz, d)
    c_shift, c_scale, c_gate = _mod_rows(mod, bsz, bsz, d)
    w_in = _even_weight_layout(e_w_in[0])
    w_out = e_w_out[0].astype(BF16)
    ef, eb = _expansion_matrices()
    nh2 = 2 * SSD_HEADS
    bias_flat = e_dt_bias[0].reshape(1, nh2)
    alog_flat = e_a_log[0].reshape(1, nh2)
    bias_row, alog_row = _pad_lanes(bias_flat, SMALL_W), _pad_lanes(alog_flat, SMALL_W)
    bias_col, alog_col = bias_flat.reshape(nh2, 1), alog_flat.reshape(nh2, 1)
    dskip_row = jnp.repeat(e_d_skip[0], SSD_HEAD_DIM).reshape(1, SSD_WIDTH)
    conv_b = e_conv_b[0].reshape(1, CONV_CH)
    wg = jnp.zeros((SMALL_W, 2 * GLA_K), F32)
    wg = wg.at[nh2:nh2 + GLA_RANK, 0:GLA_K].set(e_gla_gate_w[0, 0])
    wg = wg.at[nh2 + GLA_RANK:nh2 + 2 * GLA_RANK, GLA_K:2 * GLA_K].set(e_gla_gate_w[0, 1])
    gb = e_gla_gate_b[0].reshape(1, 2 * GLA_K)

    def mixers(stream, sc, sh, ssd_init, gla_init):
        z, xbc, q, k, v, g, small = _proj_even(stream, e_norm[0], sc, sh, w_in, EVEN_WIDTHS, PROJ_ROWS)
        ya, yb, ssd_f, ssd_b = _ssd(xbc, small, e_conv_w[0], conv_b, bias_row, bias_col, alog_row, alog_col,
                                    dskip_row, ef, eb, *ssd_init)
        oa, ob, gla_f, gla_b = _gla(q, k, v, small, wg, gb, *gla_init, GLA_ROWS)
        return (ya, yb, z, oa, ob, g), (ssd_f, ssd_b), (gla_f, gla_b)

    ssd0 = jnp.zeros((bsz, SSD_GROUPS, SSD_STATE, SSD_GROUP_W), F32)
    gla0 = jnp.zeros((bsz, GLA_HEADS, GLA_VAL_DIM, GLA_KEY_DIM), F32)
    ctx_mix, ssd_fin, gla_fin = mixers(ctx, c_scale, c_shift, (ssd0, ssd0), (gla0, gla0))
    lat_mix, _, _ = mixers(x, scale, shift, ssd_fin, gla_fin)
    x = _out_even(x, gate, *lat_mix, e_ssd_norm[0], e_gla_norm[0], w_out, OUT_ROWS)
    xc = _out_even(ctx, c_gate, *ctx_mix, e_ssd_norm[0], e_gla_norm[0], w_out, OUT_ROWS)

    mod = _adaln(cvecs, o_mod_w[0], o_mod_b[0])
    shift, scale, gate = _mod_rows(mod, lat_rows, bsz, d)
    c_shift, c_scale, _ = _mod_rows(mod, bsz, bsz, d)
    w_in = o_w_in[0].astype(BF16)
    cos, sin = _rope_tables(length)
    no_rot = (jnp.ones((n_ctx, ATT_HEAD_DIM), F32), jnp.zeros((n_ctx, ATT_HEAD_DIM), F32))
    kc, vc = _proj_odd(xc, o_norm[0], c_scale, c_shift, w_in[:, :2 * ATT_KV_W], *no_rot,
                       o_q_norm[0], o_k_norm[0], False, PROJ_ROWS)
    k, v, q, g = _proj_odd(x, o_norm[0], scale, shift, w_in, cos, sin, o_q_norm[0], o_k_norm[0], True, PROJ_ROWS)
    o = _attention(o_sink[0].astype(F32), q, k, v, kc, vc, g)
    return _out_odd(x, gate, o, o_w_out[0].astype(BF16), OUT_ROWS)
```

```python
import functools

import jax
import jax.numpy as jnp
import numpy as np
from jax import lax
from jax.experimental import pallas as pl
from jax.experimental.pallas import tpu as pltpu

F32 = jnp.float32
BF16 = jnp.bfloat16

GRID_W = 64
SSD_HEADS = 16
SSD_HEAD_DIM = 64
SSD_WIDTH = SSD_HEADS * SSD_HEAD_DIM
SSD_GROUPS = 2
SSD_STATE = 128
SSD_CHUNK = 128
CONV_K = 5
CONV_CH = SSD_WIDTH + 2 * SSD_GROUPS * SSD_STATE
GLA_HEADS = 4
GLA_KEY_DIM = 128
GLA_VAL_DIM = 256
GLA_K = GLA_HEADS * GLA_KEY_DIM
GLA_V = GLA_HEADS * GLA_VAL_DIM
GLA_RANK = 16
GLA_GATE_NORMALIZER = 16.0
GLA_CHUNK = 64
ATT_HEADS = 16
ATT_KV_HEADS = 4
ATT_GROUP = ATT_HEADS // ATT_KV_HEADS
ATT_HEAD_DIM = 128
ATT_W = ATT_HEADS * ATT_HEAD_DIM
ATT_KV_W = ATT_KV_HEADS * ATT_HEAD_DIM
WINDOW = 128
ATT_BLOCK = 128
ROPE_BASE = 10000.0
ROPE_FREQS = ATT_HEAD_DIM // 4
NORM_EPS = 1e-6
LOG2E = 1.4426950408889634

LANES = 128
SUBLANES = 8
VMEM_LIMIT_BYTES = 56 * 1024 * 1024

SSD_GROUP_W = SSD_WIDTH // SSD_GROUPS
SMALL_W = LANES
HALO = SUBLANES


def _dot(a, b):
    return jnp.dot(a.astype(BF16), b.astype(BF16), preferred_element_type=F32)


def _dot_nt(a, b):
    return lax.dot_general(a.astype(BF16), b.astype(BF16), (((1,), (1,)), ((), ())),
                           preferred_element_type=F32)


def _dot_tn(a, b):
    return lax.dot_general(a.astype(BF16), b.astype(BF16), (((0,), (0,)), ((), ())),
                           preferred_element_type=F32)


def _split(v):
    hi = v.astype(BF16)
    lo = (v - hi.astype(F32)).astype(BF16)
    return hi, lo


def _dot_split_lhs(v, m):
    hi, lo = _split(v)
    return (jnp.dot(hi, m, preferred_element_type=F32) + jnp.dot(lo, m, preferred_element_type=F32))


def _dot_split_rhs(m, v):
    hi, lo = _split(v)
    return (jnp.dot(m, hi, preferred_element_type=F32) + jnp.dot(m, lo, preferred_element_type=F32))


def _dot3(a, b):
    ah, al = _split(a)
    bh, bl = _split(b)
    return (jnp.dot(ah, bh, preferred_element_type=F32) + jnp.dot(al, bh, preferred_element_type=F32)
            + jnp.dot(ah, bl, preferred_element_type=F32))


def _silu(v):
    return v * jax.nn.sigmoid(v)


def _tri(n):
    row = lax.broadcasted_iota(jnp.int32, (n, n), 0)
    col = lax.broadcasted_iota(jnp.int32, (n, n), 1)
    return row >= col, col >= row


def _params(*sem):
    return pltpu.CompilerParams(dimension_semantics=sem, vmem_limit_bytes=VMEM_LIMIT_BYTES)


def _adaln_kernel(c_ref, w_ref, b_ref, o_ref):
    o_ref[...] = _dot3(_silu(c_ref[...]), w_ref[...]) + b_ref[...]


def _adaln(cvecs, w, b):
    rows, d = cvecs.shape
    n = w.shape[1]
    tn = 1024
    return pl.pallas_call(
        _adaln_kernel,
        grid=(n // tn,),
        in_specs=[pl.BlockSpec((rows, d), lambda j: (0, 0)),
                  pl.BlockSpec((d, tn), lambda j: (0, j)),
                  pl.BlockSpec((1, tn), lambda j: (0, j))],
        out_specs=pl.BlockSpec((rows, tn), lambda j: (0, j)),
        out_shape=jax.ShapeDtypeStruct((rows, n), F32),
        compiler_params=_params("parallel"),
        name="adaln",
    )(cvecs, w, b.reshape(1, n))


def _modulated_norm(x, g, sc, sh):
    r = lax.rsqrt(jnp.mean(x * x, axis=-1, keepdims=True) + NORM_EPS)
    return ((x * r) * g) * (1.0 + sc) + sh


def _store_cols(h, w_ref, off, ref, act=None):
    n = ref.shape[-1]
    for c0 in range(0, n, 512):
        c1 = min(n, c0 + 512)
        t = jnp.dot(h, w_ref[:, off + c0:off + c1], preferred_element_type=F32)
        ref[:, c0:c1] = (t if act is None else act(t)).astype(ref.dtype)
    return off + n


def _proj_even_kernel(x_ref, g_ref, sc_ref, sh_ref, w_ref, *out_refs):
    h = _modulated_norm(x_ref[...], g_ref[...], sc_ref[...], sh_ref[...]).astype(BF16)
    off = 0
    for ref in out_refs:
        off = _store_cols(h, w_ref, off, ref)


def _head_norm(t, gain):
    r = lax.rsqrt(jnp.mean(t * t, axis=-1, keepdims=True) + NORM_EPS)
    return (t * r) * gain


def _rope(t, cos, sin_signed, first_half):
    partner = jnp.where(first_half, pltpu.roll(t, LANES - ROPE_FREQS, 1), pltpu.roll(t, ROPE_FREQS, 1))
    return t * cos + partner * sin_signed


def _proj_odd_kernel(x_ref, g_ref, sc_ref, sh_ref, w_ref, cos_ref, sin_ref, qn_ref, kn_ref,
                     k_ref, v_ref, *qg_refs):
    h = _modulated_norm(x_ref[...], g_ref[...], sc_ref[...], sh_ref[...]).astype(BF16)
    cos = cos_ref[...]
    sin = sin_ref[...]
    lane = lax.broadcasted_iota(jnp.int32, cos.shape, 1)
    first_half = (lane % (2 * ROPE_FREQS)) < ROPE_FREQS
    scale = ATT_HEAD_DIM ** -0.5 * LOG2E

    def head(col, gain):
        t = jnp.dot(h, w_ref[:, col:col + ATT_HEAD_DIM], preferred_element_type=F32)
        return _rope(_head_norm(t, gain), cos, sin, first_half)

    for j in range(ATT_KV_HEADS):
        c0 = j * ATT_HEAD_DIM
        k_ref[:, c0:c0 + ATT_HEAD_DIM] = head(c0, kn_ref[...]).astype(k_ref.dtype)
    _store_cols(h, w_ref, ATT_KV_W, v_ref)
    if qg_refs:
        q_ref, gate_ref = qg_refs
        for j in range(ATT_HEADS):
            c0 = j * ATT_HEAD_DIM
            q_ref[:, c0:c0 + ATT_HEAD_DIM] = (head(2 * ATT_KV_W + c0, qn_ref[...]) * scale).astype(q_ref.dtype)
        _store_cols(h, w_ref, 2 * ATT_KV_W + ATT_W, gate_ref, _silu)


def _row_tile(length, want):
    return min(length, want)


def _proj_even(x, norm_g, scale, shift, w_bf16, widths, tm):
    bsz, length, d = x.shape
    tm = _row_tile(length, tm)
    n = w_bf16.shape[1]
    row = lambda b, i: (b, i, 0)
    mod = lambda b, i: (b, 0, 0)
    return pl.pallas_call(
        _proj_even_kernel,
        grid=(bsz, length // tm),
        in_specs=[pl.BlockSpec((None, tm, d), row),
                  pl.BlockSpec((1, d), lambda b, i: (0, 0)),
                  pl.BlockSpec((None, 1, d), mod),
                  pl.BlockSpec((None, 1, d), mod),
                  pl.BlockSpec((d, n), lambda b, i: (0, 0), pipeline_mode=pl.Buffered(1))],
        out_specs=[pl.BlockSpec((None, tm, wd), row) for wd in widths],
        out_shape=[jax.ShapeDtypeStruct((bsz, length, wd), F32) for wd in widths],
        compiler_params=_params("parallel", "parallel"),
        name="proj_even",
    )(x, norm_g.reshape(1, d), scale, shift, w_bf16)


def _proj_odd(x, norm_g, scale, shift, w_bf16, cos, sin, q_norm, k_norm, with_queries, tm):
    bsz, length, d = x.shape
    tm = _row_tile(length, tm)
    n = w_bf16.shape[1]
    row = lambda b, i: (b, i, 0)
    mod = lambda b, i: (b, 0, 0)
    const = lambda b, i: (0, 0)
    widths = [(ATT_KV_W, BF16), (ATT_KV_W, BF16)]
    if with_queries:
        widths += [(ATT_W, BF16), (ATT_W, F32)]
    return pl.pallas_call(
        _proj_odd_kernel,
        grid=(bsz, length // tm),
        in_specs=[pl.BlockSpec((None, tm, d), row),
                  pl.BlockSpec((1, d), const),
                  pl.BlockSpec((None, 1, d), mod),
                  pl.BlockSpec((None, 1, d), mod),
                  pl.BlockSpec((d, n), const, pipeline_mode=pl.Buffered(1)),
                  pl.BlockSpec((tm, ATT_HEAD_DIM), lambda b, i: (i, 0)),
                  pl.BlockSpec((tm, ATT_HEAD_DIM), lambda b, i: (i, 0)),
                  pl.BlockSpec((1, ATT_HEAD_DIM), const),
                  pl.BlockSpec((1, ATT_HEAD_DIM), const)],
        out_specs=[pl.BlockSpec((None, tm, wd), row) for wd, _ in widths],
        out_shape=[jax.ShapeDtypeStruct((bsz, length, wd), dt) for wd, dt in widths],
        compiler_params=_params("parallel", "parallel"),
        name="proj_odd_q" if with_queries else "proj_odd_kv",
    )(x, norm_g.reshape(1, d), scale, shift, w_bf16, cos, sin,
      q_norm.reshape(1, ATT_HEAD_DIM), k_norm.reshape(1, ATT_HEAD_DIM))


def _conv_silu(win_s, prev_ref, cur_ref, next_ref, has_prev, has_next, cw_ref, cb_ref, out_s):
    t = cur_ref.shape[0]
    half = (CONV_K - 1) // 2
    win_s[0:HALO, :] = jnp.where(has_prev, prev_ref[...], 0.0)
    win_s[HALO:HALO + t, :] = cur_ref[...]
    win_s[HALO + t:2 * HALO + t, :] = jnp.where(has_next, next_ref[...], 0.0)
    for c0 in range(0, CONV_CH, 2 * LANES):
        cs = slice(c0, c0 + 2 * LANES)
        acc = cb_ref[:, cs] + cw_ref[0:1, cs] * win_s[HALO - half:HALO - half + t, cs]
        for tap in range(1, CONV_K):
            r0 = HALO - half + tap
            acc = acc + cw_ref[tap:tap + 1, cs] * win_s[r0:r0 + t, cs]
        out_s[:, cs] = _silu(acc)


def _ssd_kernel(fp_ref, fc_ref, fn_ref, fsm_ref, bp_ref, bc_ref, bn_ref, bsm_ref,
                cw_ref, cb_ref, bias_row_ref, bias_col_ref, alog_row_ref, alog_col_ref, dskip_ref,
                ef_ref, eb_ref, sf0_ref, sb0_ref,
                ya_ref, yb_ref, sff_ref, sbf_ref,
                sf_s, sb_s, win_s, xf_s, xb_s):
    i = pl.program_id(1)
    nc = pl.num_programs(1)
    t = SSD_CHUNK
    nh = SSD_HEADS
    xoff_b = SSD_WIDTH
    xoff_c = SSD_WIDTH + SSD_GROUPS * SSD_STATE

    @pl.when(i == 0)
    def _():
        sf_s[...] = sf0_ref[...]
        sb_s[...] = sb0_ref[...]

    lower, upper = _tri(t)
    ltri = jnp.where(lower, 1.0, 0.0).astype(BF16)
    utri = jnp.where(upper, 1.0, 0.0).astype(BF16)
    a_row = -jnp.exp(alog_row_ref[...])
    a_col = -jnp.exp(alog_col_ref[...])
    ef = ef_ref[...]
    eb = eb_ref[...]
    neg_inf = float("-inf")

    _conv_silu(win_s, fp_ref, fc_ref, fn_ref, i > 0, i < nc - 1, cw_ref, cb_ref, xf_s)
    sm = fsm_ref[...]
    dt = jax.nn.softplus(sm + bias_row_ref[...])
    dta = dt * a_row
    cs = _dot_split_rhs(ltri, dta)
    rc = _dot_split_rhs(utri, dta)
    dt_t = jax.nn.softplus(sm.T[0:2 * nh, :] + bias_col_ref[...])
    dta_t = dt_t * a_col
    cs_t = _dot_split_lhs(dta_t, utri)
    rc_t = _dot_split_lhs(dta_t, ltri)
    dec_f = _dot_split_lhs(jnp.exp(cs), ef)
    wx_f = xf_s[:, 0:SSD_WIDTH] * _dot_split_lhs(dt * jnp.exp(cs[t - 1:t, :] - cs), ef)
    lane = lax.broadcasted_iota(jnp.int32, (t, LANES), 1)
    left = lane < SSD_HEAD_DIM

    def decay_matrix(h, cbg):
        a = cs[:, h:h + 1] - cs_t[h:h + 1, :]
        lf = jnp.exp(jnp.where(lower, a, neg_inf))
        b = rc[:, nh + h:nh + h + 1] - rc_t[nh + h:nh + h + 1, :]
        ub = jnp.exp(jnp.where(upper, b, neg_inf))
        return cbg * (lf * dt_t[h:h + 1, :] + ub * dt_t[nh + h:nh + h + 1, :])

    heads_per_group = nh // SSD_GROUPS
    for g in range(SSD_GROUPS):
        cg = xf_s[:, xoff_c + g * SSD_STATE:xoff_c + (g + 1) * SSD_STATE]
        bg = xf_s[:, xoff_b + g * SSD_STATE:xoff_b + (g + 1) * SSD_STATE]
        gs = slice(g * SSD_GROUP_W, (g + 1) * SSD_GROUP_W)
        cbg = _dot_nt(cg, bg)
        y_off = _dot(cg, sf_s[g]) * dec_f[:, gs]
        for pair in range(heads_per_group // 2):
            h0 = g * heads_per_group + 2 * pair
            p0 = (h0 // 2) * LANES
            xp = xf_s[:, p0:p0 + LANES]
            y = (_dot(decay_matrix(h0, cbg), jnp.where(left, xp, 0.0))
                 + _dot(decay_matrix(h0 + 1, cbg), jnp.where(left, 0.0, xp)))
            ya_ref[:, p0:p0 + LANES] = (y + y_off[:, pair * LANES:(pair + 1) * LANES]
                                        + dskip_ref[:, p0:p0 + LANES] * xp)
        sf_s[g] = dec_f[t - 1:t, gs] * sf_s[g] + _dot(bg.T, wx_f[:, gs])

    _conv_silu(win_s, bp_ref, bc_ref, bn_ref, i < nc - 1, i > 0, cw_ref, cb_ref, xb_s)
    dt2 = jax.nn.softplus(bsm_ref[...] + bias_row_ref[...])
    rc2 = _dot_split_rhs(utri, dt2 * a_row)
    dec_b = _dot_split_lhs(jnp.exp(rc2), eb)
    wx_b = xb_s[:, 0:SSD_WIDTH] * _dot_split_lhs(dt2 * jnp.exp(rc2[0:1, :] - rc2), eb)
    for g in range(SSD_GROUPS):
        cg = xb_s[:, xoff_c + g * SSD_STATE:xoff_c + (g + 1) * SSD_STATE]
        bg = xb_s[:, xoff_b + g * SSD_STATE:xoff_b + (g + 1) * SSD_STATE]
        gs = slice(g * SSD_GROUP_W, (g + 1) * SSD_GROUP_W)
        yb_ref[:, gs] = _dot(cg, sb_s[g]) * dec_b[:, gs]
        sb_s[g] = dec_b[0:1, gs] * sb_s[g] + _dot(bg.T, wx_b[:, gs])

    @pl.when(i == nc - 1)
    def _():
        sff_ref[...] = sf_s[...]
        sbf_ref[...] = sb_s[...]


def _ssd(xbc, small, conv_w, conv_b, bias_row, bias_col, alog_row, alog_col, dskip_row, ef, eb, sf0, sb0):
    bsz, length, _ = xbc.shape
    t = SSD_CHUNK
    nc = length // t
    per = t // HALO
    last_halo = length // HALO - 1
    state_shape = (SSD_GROUPS, SSD_STATE, SSD_GROUP_W)

    def cur_f(b, i): return (b, i, 0)
    def prev_f(b, i): return (b, jnp.maximum(i * per - 1, 0), 0)
    def next_f(b, i): return (b, jnp.minimum((i + 1) * per, last_halo), 0)
    def cur_b(b, i): return (b, nc - 1 - i, 0)
    def prev_b(b, i): return (b, jnp.maximum((nc - 1 - i) * per - 1, 0), 0)
    def next_b(b, i): return (b, jnp.minimum((nc - i) * per, last_halo), 0)
    const2 = lambda b, i: (0, 0)
    state = lambda b, i: (b, 0, 0, 0)

    halo_spec = lambda f: pl.BlockSpec((None, HALO, CONV_CH), f)
    chunk_spec = lambda f, w: pl.BlockSpec((None, t, w), f)
    state_spec = pl.BlockSpec((None,) + state_shape, state)
    return pl.pallas_call(
        _ssd_kernel,
        grid=(bsz, nc),
        in_specs=[halo_spec(prev_f), chunk_spec(cur_f, CONV_CH), halo_spec(next_f), chunk_spec(cur_f, SMALL_W),
                  halo_spec(prev_b), chunk_spec(cur_b, CONV_CH), halo_spec(next_b), chunk_spec(cur_b, SMALL_W),
                  pl.BlockSpec((CONV_K, CONV_CH), const2),
                  pl.BlockSpec((1, CONV_CH), const2),
                  pl.BlockSpec((1, SMALL_W), const2),
                  pl.BlockSpec((2 * SSD_HEADS, 1), const2),
                  pl.BlockSpec((1, SMALL_W), const2),
                  pl.BlockSpec((2 * SSD_HEADS, 1), const2),
                  pl.BlockSpec((1, SSD_WIDTH), const2),
                  pl.BlockSpec((SMALL_W, SSD_WIDTH), const2),
                  pl.BlockSpec((SMALL_W, SSD_WIDTH), const2),
                  state_spec, state_spec],
        out_specs=[chunk_spec(cur_f, SSD_WIDTH), chunk_spec(cur_b, SSD_WIDTH), state_spec, state_spec],
        out_shape=[jax.ShapeDtypeStruct((bsz, length, SSD_WIDTH), F32),
                   jax.ShapeDtypeStruct((bsz, length, SSD_WIDTH), F32),
                   jax.ShapeDtypeStruct((bsz,) + state_shape, F32),
                   jax.ShapeDtypeStruct((bsz,) + state_shape, F32)],
        scratch_shapes=[pltpu.VMEM(state_shape, F32), pltpu.VMEM(state_shape, F32),
                        pltpu.VMEM((t + 2 * HALO, CONV_CH), F32),
                        pltpu.VMEM((t, CONV_CH), F32), pltpu.VMEM((t, CONV_CH), F32)],
        compiler_params=_params("arbitrary", "arbitrary"),
        name="ssd",
    )(xbc, xbc, xbc, small, xbc, xbc, xbc, small, conv_w, conv_b, bias_row, bias_col,
      alog_row, alog_col, dskip_row, ef, eb, sf0, sb0)


def _log_sigmoid(v):
    return jnp.minimum(v, 0.0) - jnp.log1p(jnp.exp(-jnp.abs(v)))


def _gla_kernel(fq_ref, fk_ref, fv_ref, fsm_ref, bq_ref, bk_ref, bv_ref, bsm_ref,
                wg_ref, gb_ref, sf0_ref, sb0_ref,
                oa_ref, ob_ref, sff_ref, sbf_ref,
                sf_s, sb_s):
    i = pl.program_id(1)
    n_steps = pl.num_programs(1)
    t = GLA_CHUNK
    rows = fq_ref.shape[0]
    n_sub = rows // t
    dk, dv = GLA_KEY_DIM, GLA_VAL_DIM
    qscale = dk ** -0.5
    inv_norm = 1.0 / GLA_GATE_NORMALIZER

    @pl.when(i == 0)
    def _():
        sf_s[...] = sf0_ref[...]
        sb_s[...] = sb0_ref[...]

    lower, upper = _tri(t)
    ltri = jnp.where(lower, 1.0, 0.0).astype(BF16)
    utri = jnp.where(upper, 1.0, 0.0).astype(BF16)
    wg = wg_ref[...]
    wg_hi, wg_lo = _split(wg)
    gb = gb_ref[...]

    def log_gates(sm, c0, c1):
        hi, lo = _split(sm)
        logits = (jnp.dot(hi, wg_hi[:, c0:c1], preferred_element_type=F32)
                  + jnp.dot(lo, wg_hi[:, c0:c1], preferred_element_type=F32)
                  + jnp.dot(hi, wg_lo[:, c0:c1], preferred_element_type=F32)) + gb[:, c0:c1]
        return _log_sigmoid(logits) * inv_norm

    for s in range(n_sub):
        rs = slice(s * t, (s + 1) * t)
        lg = log_gates(fsm_ref[rs, :], 0, 2 * GLA_K)
        cs = _dot_split_rhs(ltri, lg[:, 0:GLA_K])
        rc = _dot_split_rhs(utri, lg[:, GLA_K:2 * GLA_K])
        for h in range(GLA_HEADS):
            ks = slice(h * dk, (h + 1) * dk)
            vs = slice(h * dv, (h + 1) * dv)
            c = cs[:, ks]
            r = rc[:, ks]
            qh = fq_ref[rs, ks] * qscale
            kh = fk_ref[rs, ks]
            vh = fv_ref[rs, vs]
            qdf = qh * jnp.exp(c)
            att = (jnp.where(lower, _dot_nt(qdf, kh * jnp.exp(-c)), 0.0)
                   + jnp.where(upper, _dot_nt(qh * jnp.exp(r), kh * jnp.exp(-r)), 0.0))
            oa_ref[rs, vs] = _dot(att, vh) + _dot_nt(qdf, sf_s[h])
            kw = kh * jnp.exp(c[t - 1:t, :] - c)
            sf_s[h] = sf_s[h] * jnp.exp(c[t - 1:t, :]) + _dot_tn(vh, kw)

    for s in reversed(range(n_sub)):
        rs = slice(s * t, (s + 1) * t)
        lg = log_gates(bsm_ref[rs, :], GLA_K, 2 * GLA_K)
        rc = _dot_split_rhs(utri, lg)
        for h in range(GLA_HEADS):
            ks = slice(h * dk, (h + 1) * dk)
            vs = slice(h * dv, (h + 1) * dv)
            r = rc[:, ks]
            kh = bk_ref[rs, ks]
            vh = bv_ref[rs, vs]
            qdb = (bq_ref[rs, ks] * qscale) * jnp.exp(r)
            ob_ref[rs, vs] = _dot_nt(qdb, sb_s[h])
            kw = kh * jnp.exp(r[0:1, :] - r)
            sb_s[h] = sb_s[h] * jnp.exp(r[0:1, :]) + _dot_tn(vh, kw)

    @pl.when(i == n_steps - 1)
    def _():
        sff_ref[...] = sf_s[...]
        sbf_ref[...] = sb_s[...]


def _gla(q, k, v, small, wg, gb, sf0, sb0, rows):
    bsz, length, _ = q.shape
    rows = _row_tile(length, rows)
    n_steps = length // rows
    state_shape = (GLA_HEADS, GLA_VAL_DIM, GLA_KEY_DIM)
    fwd = lambda b, i: (b, i, 0)
    bwd = lambda b, i: (b, n_steps - 1 - i, 0)
    const2 = lambda b, i: (0, 0)
    state = lambda b, i: (b, 0, 0, 0)
    blk = lambda f, w: pl.BlockSpec((None, rows, w), f)
    state_spec = pl.BlockSpec((None,) + state_shape, state)
    return pl.pallas_call(
        _gla_kernel,
        grid=(bsz, n_steps),
        in_specs=[blk(fwd, GLA_K), blk(fwd, GLA_K), blk(fwd, GLA_V), blk(fwd, SMALL_W),
                  blk(bwd, GLA_K), blk(bwd, GLA_K), blk(bwd, GLA_V), blk(bwd, SMALL_W),
                  pl.BlockSpec((SMALL_W, 2 * GLA_K), const2),
                  pl.BlockSpec((1, 2 * GLA_K), const2),
                  state_spec, state_spec],
        out_specs=[blk(fwd, GLA_V), blk(bwd, GLA_V), state_spec, state_spec],
        out_shape=[jax.ShapeDtypeStruct((bsz, length, GLA_V), F32),
                   jax.ShapeDtypeStruct((bsz, length, GLA_V), F32),
                   jax.ShapeDtypeStruct((bsz,) + state_shape, F32),
                   jax.ShapeDtypeStruct((bsz,) + state_shape, F32)],
        scratch_shapes=[pltpu.VMEM(state_shape, F32), pltpu.VMEM(state_shape, F32)],
        compiler_params=_params("arbitrary", "arbitrary"),
        name="gla",
    )(q, k, v, small, q, k, v, small, wg, gb, sf0, sb0)


def _out_even_kernel(x_ref, gate_ref, ya_ref, yb_ref, z_ref, oa_ref, ob_ref, g_ref,
                     sn_ref, gn_ref, w_ref, o_ref):
    acc = None
    for g in range(SSD_GROUPS):
        cs = slice(g * SSD_GROUP_W, (g + 1) * SSD_GROUP_W)
        y = (ya_ref[:, cs] + yb_ref[:, cs]) * _silu(z_ref[:, cs])
        part = _dot(_head_norm(y, sn_ref[:, cs]), w_ref[cs, :])
        acc = part if acc is None else acc + part
    for h in range(GLA_HEADS):
        cs = slice(h * GLA_VAL_DIM, (h + 1) * GLA_VAL_DIM)
        o = _head_norm(oa_ref[:, cs] + ob_ref[:, cs], gn_ref[:, cs]) * _silu(g_ref[:, cs])
        acc = acc + _dot(o, w_ref[SSD_WIDTH + h * GLA_VAL_DIM:SSD_WIDTH + (h + 1) * GLA_VAL_DIM, :])
    o_ref[...] = x_ref[...] + gate_ref[...] * acc


def _out_even(x, gate, ya, yb, z, oa, ob, g, ssd_norm, gla_norm, w_bf16, tm):
    bsz, length, d = x.shape
    tm = _row_tile(length, tm)
    row = lambda b, i: (b, i, 0)
    const2 = lambda b, i: (0, 0)
    blk = lambda w: pl.BlockSpec((None, tm, w), row)
    return pl.pallas_call(
        _out_even_kernel,
        grid=(bsz, length // tm),
        in_specs=[blk(d), pl.BlockSpec((None, 1, d), lambda b, i: (b, 0, 0)),
                  blk(SSD_WIDTH), blk(SSD_WIDTH), blk(SSD_WIDTH), blk(GLA_V), blk(GLA_V), blk(GLA_V),
                  pl.BlockSpec((1, SSD_WIDTH), const2), pl.BlockSpec((1, GLA_V), const2),
                  pl.BlockSpec(w_bf16.shape, const2)],
        out_specs=blk(d),
        out_shape=jax.ShapeDtypeStruct((bsz, length, d), F32),
        compiler_params=_params("parallel", "parallel"),
        name="out_even",
    )(x, gate, ya, yb, z, oa, ob, g, ssd_norm.reshape(1, -1), gla_norm.reshape(1, -1), w_bf16)


def _attn_kernel(sink_ref, q_ref, k_ref, v_ref, kc_ref, vc_ref, g_ref, o_ref):
    j = pl.program_id(1)
    i = pl.program_id(2)
    length = k_ref.shape[0]
    blk = ATT_BLOCK
    band = 3 * blk
    dh = ATT_HEAD_DIM
    n_q = q_ref.shape[0] // blk
    n_ctx = kc_ref.shape[0]
    row_minus_col = (lax.broadcasted_iota(jnp.int32, (blk, band), 0)
                     - lax.broadcasted_iota(jnp.int32, (blk, band), 1))
    sink = jnp.concatenate([jnp.full((blk, LANES), sink_ref[j * ATT_GROUP + g] * LOG2E, F32)
                            for g in range(ATT_GROUP)], axis=0)
    kc = kc_ref[...]
    vc_ext = jnp.concatenate([vc_ref[...], jnp.ones((n_ctx, dh), BF16)], axis=1)
    ones_band = jnp.ones((band, dh), BF16)
    for qb in range(n_q):
        blk_idx = i * n_q + qb
        start = pl.multiple_of(jnp.clip((blk_idx - 1) * blk, 0, length - band), blk)
        inside = jnp.abs(row_minus_col + (blk_idx * blk - start)) <= WINDOW
        kw = k_ref[pl.ds(start, band), :]
        v_all = jnp.concatenate([jnp.concatenate([v_ref[pl.ds(start, band), :], ones_band], axis=1),
                                 vc_ext], axis=0)
        rs = slice(qb * blk, (qb + 1) * blk)
        q = jnp.concatenate([q_ref[rs, g * dh:(g + 1) * dh] for g in range(ATT_GROUP)], axis=0)
        s_band = _dot_nt(q, kw)
        s_ctx = _dot_nt(q, kc)
        cols = []
        for c in range(band // LANES):
            ls = slice(c * LANES, (c + 1) * LANES)
            cols.append(jnp.concatenate(
                [jnp.where(inside[:, ls], s_band[g * blk:(g + 1) * blk, ls], float("-inf"))
                 for g in range(ATT_GROUP)], axis=0))
        for c in range(n_ctx // LANES):
            cols.append(s_ctx[:, c * LANES:(c + 1) * LANES])
        m = jnp.maximum(sink, jnp.max(functools.reduce(jnp.maximum, cols), axis=-1, keepdims=True))
        p = jnp.concatenate([jnp.exp2(col - m) for col in cols], axis=1).astype(BF16)
        acc = jnp.dot(p, v_all, preferred_element_type=F32)
        out = acc[:, 0:dh] / (acc[:, dh:2 * dh] + jnp.exp2(sink - m))
        for g in range(ATT_GROUP):
            cs = slice(g * dh, (g + 1) * dh)
            o_ref[rs, cs] = (out[g * blk:(g + 1) * blk, :] * g_ref[rs, cs]).astype(o_ref.dtype)


def _attention(sink, q, k, v, kc, vc, gate):
    bsz, length, _ = q.shape
    n_ctx = kc.shape[1]
    rows = ATT_Q_BLOCKS * ATT_BLOCK
    gw = ATT_GROUP * ATT_HEAD_DIM
    qmap = lambda b, j, i, s: (b, i, j)
    kvmap = lambda b, j, i, s: (b, 0, j)
    grid_spec = pltpu.PrefetchScalarGridSpec(
        num_scalar_prefetch=1,
        grid=(bsz, ATT_KV_HEADS, length // rows),
        in_specs=[pl.BlockSpec((None, rows, gw), qmap),
                  pl.BlockSpec((None, length, ATT_HEAD_DIM), kvmap),
                  pl.BlockSpec((None, length, ATT_HEAD_DIM), kvmap),
                  pl.BlockSpec((None, n_ctx, ATT_HEAD_DIM), kvmap),
                  pl.BlockSpec((None, n_ctx, ATT_HEAD_DIM), kvmap),
                  pl.BlockSpec((None, rows, gw), qmap)],
        out_specs=pl.BlockSpec((None, rows, gw), qmap),
    )
    return pl.pallas_call(
        _attn_kernel,
        grid_spec=grid_spec,
        out_shape=jax.ShapeDtypeStruct((bsz, length, ATT_W), BF16),
        compiler_params=_params("parallel", "parallel", "arbitrary"),
        name="attention",
    )(sink, q, k, v, kc, vc, gate)


def _out_odd_kernel(x_ref, gate_ref, o_ref, w_ref, y_ref):
    y_ref[...] = x_ref[...] + gate_ref[...] * jnp.dot(o_ref[...], w_ref[...], preferred_element_type=F32)


def _out_odd(x, gate, o, w_bf16, tm):
    bsz, length, d = x.shape
    tm = _row_tile(length, tm)
    row = lambda b, i: (b, i, 0)
    return pl.pallas_call(
        _out_odd_kernel,
        grid=(bsz, length // tm),
        in_specs=[pl.BlockSpec((None, tm, d), row),
                  pl.BlockSpec((None, 1, d), lambda b, i: (b, 0, 0)),
                  pl.BlockSpec((None, tm, o.shape[-1]), row),
                  pl.BlockSpec(w_bf16.shape, lambda b, i: (0, 0))],
        out_specs=pl.BlockSpec((None, tm, d), row),
        out_shape=jax.ShapeDtypeStruct((bsz, length, d), F32),
        compiler_params=_params("parallel", "parallel"),
        name="out_odd",
    )(x, gate, o, w_bf16)


EVEN_WIDTHS = (SSD_WIDTH, CONV_CH, GLA_K, GLA_K, GLA_V, GLA_V, SMALL_W)


def _even_weight_layout(w_in):
    d = w_in.shape[0]
    sizes = (SSD_WIDTH, CONV_CH, 2 * SSD_HEADS, GLA_K, GLA_K, GLA_V, GLA_V, 2 * GLA_RANK)
    z, xbc, dt, q, k, v, g, lr = jnp.split(w_in, np.cumsum(sizes)[:-1].tolist(), axis=1)
    pad = jnp.zeros((d, SMALL_W - 2 * SSD_HEADS - 2 * GLA_RANK), w_in.dtype)
    return jnp.concatenate([z, xbc, q, k, v, g, dt, lr, pad], axis=1).astype(BF16)


def _expansion_matrices():
    rows = np.arange(SMALL_W)[:, None]
    heads = (np.arange(SSD_WIDTH) // SSD_HEAD_DIM)[None, :]
    ef = (rows == heads).astype(np.float32)
    eb = (rows - SSD_HEADS == heads).astype(np.float32)
    return jnp.asarray(ef, BF16), jnp.asarray(eb, BF16)


def _pad_lanes(v, width):
    return jnp.pad(v, ((0, 0), (0, width - v.shape[1])))


def _rope_tables(length):
    rows = length // GRID_W
    row = jnp.repeat(jnp.arange(rows, dtype=F32), GRID_W)
    col = jnp.tile(jnp.arange(GRID_W, dtype=F32), rows)
    inv = 1.0 / (ROPE_BASE ** (jnp.arange(ROPE_FREQS, dtype=F32) / ROPE_FREQS))
    ang_r = row[:, None] * inv
    ang_c = col[:, None] * inv
    cos = jnp.concatenate([jnp.cos(ang_r), jnp.cos(ang_r), jnp.cos(ang_c), jnp.cos(ang_c)], axis=1)
    sin = jnp.concatenate([-jnp.sin(ang_r), jnp.sin(ang_r), -jnp.sin(ang_c), jnp.sin(ang_c)], axis=1)
    return cos, sin


def _mod_rows(mod, rows, bsz, d):
    picked = jnp.broadcast_to(mod[rows], (bsz, 3 * d)) if isinstance(rows, int) else mod[rows]
    return [picked[:, None, j * d:(j + 1) * d] for j in range(3)]


PROJ_ROWS = 512
OUT_ROWS = 512
ATT_Q_BLOCKS = 8
GLA_ROWS = 256
CVEC_ROWS = SUBLANES


def kernel(x, c, ctx, c_ctx, e_norm, e_mod_w, e_mod_b, e_w_in, e_conv_w, e_conv_b, e_dt_bias, e_a_log,
           e_d_skip, e_ssd_norm, e_gla_gate_w, e_gla_gate_b, e_gla_norm, e_w_out, o_norm, o_mod_w, o_mod_b,
           o_w_in, o_q_norm, o_k_norm, o_sink, o_w_out):
    bsz, length, d = x.shape
    n_ctx = ctx.shape[1]
    assert e_norm.shape[0] == 1 and o_norm.shape[0] == 1, "two-layer block only"
    assert length % SSD_CHUNK == 0 and n_ctx % SSD_CHUNK == 0 and length >= 3 * ATT_BLOCK
    assert bsz + 1 <= CVEC_ROWS

    cvecs = jnp.zeros((CVEC_ROWS, d), F32).at[:bsz].set(c).at[bsz].set(c_ctx)
    lat_rows = slice(0, bsz)

    mod = _adaln(cvecs, e_mod_w[0], e_mod_b[0])
    shift, scale, gate = _mod_rows(mod, lat_rows, bsz, d)
    c_shift, c_scale, c_gate = _mod_rows(mod, bsz, bsz, d)
    w_in = _even_weight_layout(e_w_in[0])
    w_out = e_w_out[0].astype(BF16)
    ef, eb = _expansion_matrices()
    nh2 = 2 * SSD_HEADS
    bias_flat = e_dt_bias[0].reshape(1, nh2)
    alog_flat = e_a_log[0].reshape(1, nh2)
    bias_row, alog_row = _pad_lanes(bias_flat, SMALL_W), _pad_lanes(alog_flat, SMALL_W)
    bias_col, alog_col = bias_flat.reshape(nh2, 1), alog_flat.reshape(nh2, 1)
    dskip_row = jnp.repeat(e_d_skip[0], SSD_HEAD_DIM).reshape(1, SSD_WIDTH)
    conv_b = e_conv_b[0].reshape(1, CONV_CH)
    wg = jnp.zeros((SMALL_W, 2 * GLA_K), F32)
    wg = wg.at[nh2:nh2 + GLA_RANK, 0:GLA_K].set(e_gla_gate_w[0, 0])
    wg = wg.at[nh2 + GLA_RANK:nh2 + 2 * GLA_RANK, GLA_K:2 * GLA_K].set(e_gla_gate_w[0, 1])
    gb = e_gla_gate_b[0].reshape(1, 2 * GLA_K)

    def mixers(stream, sc, sh, ssd_init, gla_init):
        z, xbc, q, k, v, g, small = _proj_even(stream, e_norm[0], sc, sh, w_in, EVEN_WIDTHS, PROJ_ROWS)
        ya, yb, ssd_f, ssd_b = _ssd(xbc, small, e_conv_w[0], conv_b, bias_row, bias_col, alog_row, alog_col,
                                    dskip_row, ef, eb, *ssd_init)
        oa, ob, gla_f, gla_b = _gla(q, k, v, small, wg, gb, *gla_init, GLA_ROWS)
        return (ya, yb, z, oa, ob, g), (ssd_f, ssd_b), (gla_f, gla_b)

    ssd0 = jnp.zeros((bsz, SSD_GROUPS, SSD_STATE, SSD_GROUP_W), F32)
    gla0 = jnp.zeros((bsz, GLA_HEADS, GLA_VAL_DIM, GLA_KEY_DIM), F32)
    ctx_mix, ssd_fin, gla_fin = mixers(ctx, c_scale, c_shift, (ssd0, ssd0), (gla0, gla0))
    lat_mix, _, _ = mixers(x, scale, shift, ssd_fin, gla_fin)
    x = _out_even(x, gate, *lat_mix, e_ssd_norm[0], e_gla_norm[0], w_out, OUT_ROWS)
    xc = _out_even(ctx, c_gate, *ctx_mix, e_ssd_norm[0], e_gla_norm[0], w_out, OUT_ROWS)

    mod = _adaln(cvecs, o_mod_w[0], o_mod_b[0])
    shift, scale, gate = _mod_rows(mod, lat_rows, bsz, d)
    c_shift, c_scale, _ = _mod_rows(mod, bsz, bsz, d)
    w_in = o_w_in[0].astype(BF16)
    cos, sin = _rope_tables(length)
    no_rot = (jnp.ones((n_ctx, ATT_HEAD_DIM), F32), jnp.zeros((n_ctx, ATT_HEAD_DIM), F32))
    kc, vc = _proj_odd(xc, o_norm[0], c_scale, c_shift, w_in[:, :2 * ATT_KV_W], *no_rot,
                       o_q_norm[0], o_k_norm[0], False, PROJ_ROWS)
    k, v, q, g = _proj_odd(x, o_norm[0], scale, shift, w_in, cos, sin, o_q_norm[0], o_k_norm[0], True, PROJ_ROWS)
    o = _attention(o_sink[0].astype(F32), q, k, v, kc, vc, g)
    return _out_odd(x, gate, o, o_w_out[0].astype(BF16), OUT_ROWS)
```

```python
import functools

import jax
import jax.numpy as jnp
import numpy as np
from jax import lax
from jax.experimental import pallas as pl
from jax.experimental.pallas import tpu as pltpu

F32 = jnp.float32
BF16 = jnp.bfloat16

GRID_W = 64
SSD_HEADS = 16
SSD_HEAD_DIM = 64
SSD_WIDTH = SSD_HEADS * SSD_HEAD_DIM
SSD_GROUPS = 2
SSD_STATE = 128
SSD_CHUNK = 128
CONV_K = 5
CONV_CH = SSD_WIDTH + 2 * SSD_GROUPS * SSD_STATE
GLA_HEADS = 4
GLA_KEY_DIM = 128
GLA_VAL_DIM = 256
GLA_K = GLA_HEADS * GLA_KEY_DIM
GLA_V = GLA_HEADS * GLA_VAL_DIM
GLA_RANK = 16
GLA_GATE_NORMALIZER = 16.0
GLA_CHUNK = 64
ATT_HEADS = 16
ATT_KV_HEADS = 4
ATT_GROUP = ATT_HEADS // ATT_KV_HEADS
ATT_HEAD_DIM = 128
ATT_W = ATT_HEADS * ATT_HEAD_DIM
ATT_KV_W = ATT_KV_HEADS * ATT_HEAD_DIM
WINDOW = 128
ATT_BLOCK = 128
ROPE_BASE = 10000.0
ROPE_FREQS = ATT_HEAD_DIM // 4
NORM_EPS = 1e-6
LOG2E = 1.4426950408889634

LANES = 128
SUBLANES = 8
VMEM_LIMIT_BYTES = 56 * 1024 * 1024

SSD_GROUP_W = SSD_WIDTH // SSD_GROUPS
SSD_BC_W = SSD_GROUPS * SSD_STATE
SMALL_W = LANES
HALO = SUBLANES


def _dot(a, b):
    return jnp.dot(a.astype(BF16), b.astype(BF16), preferred_element_type=F32)


def _dot_nt(a, b):
    return lax.dot_general(a.astype(BF16), b.astype(BF16), (((1,), (1,)), ((), ())),
                           preferred_element_type=F32)


def _dot_tn(a, b):
    return lax.dot_general(a.astype(BF16), b.astype(BF16), (((0,), (0,)), ((), ())),
                           preferred_element_type=F32)


def _split(v):
    hi = v.astype(BF16)
    lo = (v - hi.astype(F32)).astype(BF16)
    return hi, lo


def _dot_split_lhs(v, m):
    hi, lo = _split(v)
    return (jnp.dot(hi, m, preferred_element_type=F32) + jnp.dot(lo, m, preferred_element_type=F32))


def _dot_split_rhs(m, v):
    hi, lo = _split(v)
    return (jnp.dot(m, hi, preferred_element_type=F32) + jnp.dot(m, lo, preferred_element_type=F32))


def _dot3(a, b):
    ah, al = _split(a)
    bh, bl = _split(b)
    return (jnp.dot(ah, bh, preferred_element_type=F32) + jnp.dot(al, bh, preferred_element_type=F32)
            + jnp.dot(ah, bl, preferred_element_type=F32))


def _silu(v):
    return v * jax.nn.sigmoid(v)


def _softplus(v):
    return jnp.maximum(v, 0.0) + jnp.log(1.0 + jnp.exp(-jnp.abs(v)))


def _log_sigmoid(v):
    return jnp.minimum(v, 0.0) - jnp.log(1.0 + jnp.exp(-jnp.abs(v)))


def _tri(n):
    row = lax.broadcasted_iota(jnp.int32, (n, n), 0)
    col = lax.broadcasted_iota(jnp.int32, (n, n), 1)
    return row >= col, col >= row


def _params(*sem):
    return pltpu.CompilerParams(dimension_semantics=sem, vmem_limit_bytes=VMEM_LIMIT_BYTES)


def _adaln_kernel(c_ref, w_ref, b_ref, o_ref):
    o_ref[...] = _dot3(_silu(c_ref[...]), w_ref[...]) + b_ref[...]


def _adaln(cvecs, w, b):
    rows, d = cvecs.shape
    n = w.shape[1]
    tn = 1024
    return pl.pallas_call(
        _adaln_kernel,
        grid=(n // tn,),
        in_specs=[pl.BlockSpec((rows, d), lambda j: (0, 0)),
                  pl.BlockSpec((d, tn), lambda j: (0, j)),
                  pl.BlockSpec((1, tn), lambda j: (0, j))],
        out_specs=pl.BlockSpec((rows, tn), lambda j: (0, j)),
        out_shape=jax.ShapeDtypeStruct((rows, n), F32),
        compiler_params=_params("parallel"),
        name="adaln",
    )(cvecs, w, b.reshape(1, n))


def _modulated_norm(x, g, sc, sh):
    r = lax.rsqrt(jnp.mean(x * x, axis=-1, keepdims=True) + NORM_EPS)
    return ((x * r) * g) * (1.0 + sc) + sh


def _store_cols(h, w_ref, off, ref, act=None):
    n = ref.shape[-1]
    for c0 in range(0, n, 512):
        c1 = min(n, c0 + 512)
        t = jnp.dot(h, w_ref[:, off + c0:off + c1], preferred_element_type=F32)
        ref[:, c0:c1] = (t if act is None else act(t)).astype(ref.dtype)
    return off + n


def _proj_even_kernel(x_ref, g_ref, sc_ref, sh_ref, w_ref, *out_refs):
    h = _modulated_norm(x_ref[...], g_ref[...], sc_ref[...], sh_ref[...]).astype(BF16)
    off = 0
    for ref in out_refs:
        off = _store_cols(h, w_ref, off, ref)


def _head_norm(t, gain):
    r = lax.rsqrt(jnp.mean(t * t, axis=-1, keepdims=True) + NORM_EPS)
    return (t * r) * gain


def _rope(t, cos, sin_signed, first_half):
    partner = jnp.where(first_half, pltpu.roll(t, LANES - ROPE_FREQS, 1), pltpu.roll(t, ROPE_FREQS, 1))
    return t * cos + partner * sin_signed


def _proj_odd_kernel(x_ref, g_ref, sc_ref, sh_ref, w_ref, cos_ref, sin_ref, qn_ref, kn_ref,
                     k_ref, v_ref, *qg_refs):
    h = _modulated_norm(x_ref[...], g_ref[...], sc_ref[...], sh_ref[...]).astype(BF16)
    cos = cos_ref[...]
    sin = sin_ref[...]
    lane = lax.broadcasted_iota(jnp.int32, cos.shape, 1)
    first_half = (lane % (2 * ROPE_FREQS)) < ROPE_FREQS
    scale = ATT_HEAD_DIM ** -0.5 * LOG2E

    def head(col, gain):
        t = jnp.dot(h, w_ref[:, col:col + ATT_HEAD_DIM], preferred_element_type=F32)
        return _rope(_head_norm(t, gain), cos, sin, first_half)

    for j in range(ATT_KV_HEADS):
        c0 = j * ATT_HEAD_DIM
        k_ref[:, c0:c0 + ATT_HEAD_DIM] = head(c0, kn_ref[...]).astype(k_ref.dtype)
    _store_cols(h, w_ref, ATT_KV_W, v_ref)
    if qg_refs:
        q_ref, gate_ref = qg_refs
        for j in range(ATT_HEADS):
            c0 = j * ATT_HEAD_DIM
            q_ref[:, c0:c0 + ATT_HEAD_DIM] = (head(2 * ATT_KV_W + c0, qn_ref[...]) * scale).astype(q_ref.dtype)
        _store_cols(h, w_ref, 2 * ATT_KV_W + ATT_W, gate_ref, _silu)


def _row_tile(length, want):
    return min(length, want)


def _proj_even(x, norm_g, scale, shift, w_bf16, widths, tm):
    bsz, length, d = x.shape
    tm = _row_tile(length, tm)
    n = w_bf16.shape[1]
    row = lambda b, i: (b, i, 0)
    mod = lambda b, i: (b, 0, 0)
    return pl.pallas_call(
        _proj_even_kernel,
        grid=(bsz, length // tm),
        in_specs=[pl.BlockSpec((None, tm, d), row),
                  pl.BlockSpec((1, d), lambda b, i: (0, 0)),
                  pl.BlockSpec((None, 1, d), mod),
                  pl.BlockSpec((None, 1, d), mod),
                  pl.BlockSpec((d, n), lambda b, i: (0, 0), pipeline_mode=pl.Buffered(1))],
        out_specs=[pl.BlockSpec((None, tm, wd), row) for wd in widths],
        out_shape=[jax.ShapeDtypeStruct((bsz, length, wd), F32) for wd in widths],
        compiler_params=_params("parallel", "parallel"),
        name="proj_even",
    )(x, norm_g.reshape(1, d), scale, shift, w_bf16)


def _proj_odd(x, norm_g, scale, shift, w_bf16, cos, sin, q_norm, k_norm, with_queries, tm):
    bsz, length, d = x.shape
    tm = _row_tile(length, tm)
    n = w_bf16.shape[1]
    row = lambda b, i: (b, i, 0)
    mod = lambda b, i: (b, 0, 0)
    const = lambda b, i: (0, 0)
    widths = [(ATT_KV_W, BF16), (ATT_KV_W, BF16)]
    if with_queries:
        widths += [(ATT_W, BF16), (ATT_W, F32)]
    return pl.pallas_call(
        _proj_odd_kernel,
        grid=(bsz, length // tm),
        in_specs=[pl.BlockSpec((None, tm, d), row),
                  pl.BlockSpec((1, d), const),
                  pl.BlockSpec((None, 1, d), mod),
                  pl.BlockSpec((None, 1, d), mod),
                  pl.BlockSpec((d, n), const, pipeline_mode=pl.Buffered(1)),
                  pl.BlockSpec((tm, ATT_HEAD_DIM), lambda b, i: (i, 0)),
                  pl.BlockSpec((tm, ATT_HEAD_DIM), lambda b, i: (i, 0)),
                  pl.BlockSpec((1, ATT_HEAD_DIM), const),
                  pl.BlockSpec((1, ATT_HEAD_DIM), const)],
        out_specs=[pl.BlockSpec((None, tm, wd), row) for wd, _ in widths],
        out_shape=[jax.ShapeDtypeStruct((bsz, length, wd), dt) for wd, dt in widths],
        compiler_params=_params("parallel", "parallel"),
        name="proj_odd_q" if with_queries else "proj_odd_kv",
    )(x, norm_g.reshape(1, d), scale, shift, w_bf16, cos, sin,
      q_norm.reshape(1, ATT_HEAD_DIM), k_norm.reshape(1, ATT_HEAD_DIM))


def _conv_silu(win_s, prev_ref, cur_ref, next_ref, has_prev, has_next, cw_ref, cb_ref, out_s):
    t = cur_ref.shape[0]
    half = (CONV_K - 1) // 2
    win_s[0:HALO, :] = jnp.where(has_prev, prev_ref[...], 0.0)
    win_s[HALO:HALO + t, :] = cur_ref[...]
    win_s[HALO + t:2 * HALO + t, :] = jnp.where(has_next, next_ref[...], 0.0)
    for c0 in range(0, CONV_CH, 2 * LANES):
        cs = slice(c0, c0 + 2 * LANES)
        acc = cb_ref[:, cs] + cw_ref[0:1, cs] * win_s[HALO - half:HALO - half + t, cs]
        for tap in range(1, CONV_K):
            r0 = HALO - half + tap
            acc = acc + cw_ref[tap:tap + 1, cs] * win_s[r0:r0 + t, cs]
        out_s[:, cs] = _silu(acc)


def _ssd_fwd_kernel(p_ref, c_ref, n_ref, sm_ref, cw_ref, cb_ref, bias_row_ref, bias_col_ref,
                    alog_row_ref, alog_col_ref, dskip_ref, ef_ref, eb_ref, sf0_ref,
                    ya_ref, cm_ref, bt_ref, wxb_ref, p3_ref, sff_ref,
                    sf_s, win_s, x_s):
    i = pl.program_id(1)
    nc = pl.num_programs(1)
    t = SSD_CHUNK
    nh = SSD_HEADS
    xoff_b = SSD_WIDTH
    xoff_c = SSD_WIDTH + SSD_BC_W

    @pl.when(i == 0)
    def _():
        sf_s[...] = sf0_ref[...]

    lower, upper = _tri(t)
    ltri = jnp.where(lower, 1.0, 0.0).astype(BF16)
    utri = jnp.where(upper, 1.0, 0.0).astype(BF16)
    a_row = -jnp.exp(alog_row_ref[...])
    a_col = -jnp.exp(alog_col_ref[...])
    ef = ef_ref[...]
    eb = eb_ref[...]
    neg_inf = float("-inf")

    _conv_silu(win_s, p_ref, c_ref, n_ref, i > 0, i < nc - 1, cw_ref, cb_ref, x_s)
    xs = x_s[:, 0:SSD_WIDTH]
    sm = sm_ref[...]
    dt = _softplus(sm + bias_row_ref[...])
    dta = dt * a_row
    cs = _dot_split_rhs(ltri, dta)
    rc = _dot_split_rhs(utri, dta)
    dt_t = _softplus(sm.T[0:2 * nh, :] + bias_col_ref[...])
    dta_t = dt_t * a_col
    cs_t = _dot_split_lhs(dta_t, utri)
    rc_t = _dot_split_lhs(dta_t, ltri)
    dec_f = _dot_split_lhs(jnp.exp(cs), ef)
    wx_f = xs * _dot_split_lhs(dt * jnp.exp(cs[t - 1:t, :] - cs), ef)

    p3 = jnp.exp(rc)
    p3_ref[...] = p3
    wxb_ref[...] = (xs * _dot_split_lhs(dt * jnp.exp(rc[0:1, :] - rc), eb)).astype(BF16)
    cm_ref[...] = x_s[:, xoff_c:xoff_c + SSD_BC_W].astype(BF16)

    lane = lax.broadcasted_iota(jnp.int32, (t, LANES), 1)
    left = lane < SSD_HEAD_DIM

    def decay_matrix(h, cbg):
        a = cs[:, h:h + 1] - cs_t[h:h + 1, :]
        lf = jnp.exp(jnp.where(lower, a, neg_inf))
        b = rc[:, nh + h:nh + h + 1] - rc_t[nh + h:nh + h + 1, :]
        ub = jnp.exp(jnp.where(upper, b, neg_inf))
        return cbg * (lf * dt_t[h:h + 1, :] + ub * dt_t[nh + h:nh + h + 1, :])

    heads_per_group = nh // SSD_GROUPS
    for g in range(SSD_GROUPS):
        cg = x_s[:, xoff_c + g * SSD_STATE:xoff_c + (g + 1) * SSD_STATE]
        bg = x_s[:, xoff_b + g * SSD_STATE:xoff_b + (g + 1) * SSD_STATE]
        gs = slice(g * SSD_GROUP_W, (g + 1) * SSD_GROUP_W)
        cbg = _dot_nt(cg, bg)
        y_off = _dot(cg, sf_s[g]) * dec_f[:, gs]
        for pair in range(heads_per_group // 2):
            h0 = g * heads_per_group + 2 * pair
            p0 = (h0 // 2) * LANES
            xp = x_s[:, p0:p0 + LANES]
            y = (_dot(decay_matrix(h0, cbg), jnp.where(left, xp, 0.0))
                 + _dot(decay_matrix(h0 + 1, cbg), jnp.where(left, 0.0, xp)))
            ya_ref[:, p0:p0 + LANES] = (y + y_off[:, pair * LANES:(pair + 1) * LANES]
                                        + dskip_ref[:, p0:p0 + LANES] * xp)
        bt = bg.T.astype(BF16)
        bt_ref[g * SSD_STATE:(g + 1) * SSD_STATE, :] = bt
        sf_s[g] = dec_f[t - 1:t, gs] * sf_s[g] + jnp.dot(bt, wx_f[:, gs].astype(BF16),
                                                        preferred_element_type=F32)

    @pl.when(i == nc - 1)
    def _():
        sff_ref[...] = sf_s[...]


def _ssd_bwd_kernel(cm_ref, bt_ref, wxb_ref, p3_ref, ya_ref, z_ref, eb_ref, sn_ref, sb0_ref,
                    ys_ref, sbf_ref, sb_s):
    i = pl.program_id(1)
    nc = pl.num_programs(1)

    @pl.when(i == 0)
    def _():
        sb_s[...] = sb0_ref[...]

    dec_b = _dot_split_lhs(p3_ref[...], eb_ref[...])
    for g in range(SSD_GROUPS):
        gs = slice(g * SSD_GROUP_W, (g + 1) * SSD_GROUP_W)
        ss = slice(g * SSD_STATE, (g + 1) * SSD_STATE)
        y_off = jnp.dot(cm_ref[:, ss], sb_s[g].astype(BF16), preferred_element_type=F32) * dec_b[:, gs]
        sb_s[g] = dec_b[0:1, gs] * sb_s[g] + jnp.dot(bt_ref[ss, :], wxb_ref[:, gs],
                                                     preferred_element_type=F32)
        y = (ya_ref[:, gs] + y_off) * _silu(z_ref[:, gs])
        ys_ref[:, gs] = _head_norm(y, sn_ref[:, gs]).astype(ys_ref.dtype)

    @pl.when(i == nc - 1)
    def _():
        sbf_ref[...] = sb_s[...]


SSD_STATE_SHAPE = (SSD_GROUPS, SSD_STATE, SSD_GROUP_W)


def _ssd_fwd(xbc, small, conv_w, conv_b, bias_row, bias_col, alog_row, alog_col, dskip_row, ef, eb, sf0):
    bsz, length, _ = xbc.shape
    t = SSD_CHUNK
    nc = length // t
    per = t // HALO
    last_halo = length // HALO - 1

    def cur(b, i): return (b, i, 0)
    def prev(b, i): return (b, jnp.maximum(i * per - 1, 0), 0)
    def nxt(b, i): return (b, jnp.minimum((i + 1) * per, last_halo), 0)
    const2 = lambda b, i: (0, 0)
    state = lambda b, i: (b, 0, 0, 0)

    halo_spec = lambda f: pl.BlockSpec((None, HALO, CONV_CH), f)
    chunk_spec = lambda w: pl.BlockSpec((None, t, w), cur)
    state_spec = pl.BlockSpec((None,) + SSD_STATE_SHAPE, state)
    return pl.pallas_call(
        _ssd_fwd_kernel,
        grid=(bsz, nc),
        in_specs=[halo_spec(prev), chunk_spec(CONV_CH), halo_spec(nxt), chunk_spec(SMALL_W),
                  pl.BlockSpec((CONV_K, CONV_CH), const2),
                  pl.BlockSpec((1, CONV_CH), const2),
                  pl.BlockSpec((1, SMALL_W), const2),
                  pl.BlockSpec((2 * SSD_HEADS, 1), const2),
                  pl.BlockSpec((1, SMALL_W), const2),
                  pl.BlockSpec((2 * SSD_HEADS, 1), const2),
                  pl.BlockSpec((1, SSD_WIDTH), const2),
                  pl.BlockSpec((SMALL_W, SSD_WIDTH), const2),
                  pl.BlockSpec((SMALL_W, SSD_WIDTH), const2),
                  state_spec],
        out_specs=[chunk_spec(SSD_WIDTH), chunk_spec(SSD_BC_W),
                   pl.BlockSpec((None, None, SSD_BC_W, t), lambda b, i: (b, i, 0, 0)),
                   chunk_spec(SSD_WIDTH), chunk_spec(SMALL_W), state_spec],
        out_shape=[jax.ShapeDtypeStruct((bsz, length, SSD_WIDTH), F32),
                   jax.ShapeDtypeStruct((bsz, length, SSD_BC_W), BF16),
                   jax.ShapeDtypeStruct((bsz, nc, SSD_BC_W, t), BF16),
                   jax.ShapeDtypeStruct((bsz, length, SSD_WIDTH), BF16),
                   jax.ShapeDtypeStruct((bsz, length, SMALL_W), F32),
                   jax.ShapeDtypeStruct((bsz,) + SSD_STATE_SHAPE, F32)],
        scratch_shapes=[pltpu.VMEM(SSD_STATE_SHAPE, F32),
                        pltpu.VMEM((t + 2 * HALO, CONV_CH), F32),
                        pltpu.VMEM((t, CONV_CH), F32)],
        compiler_params=_params("arbitrary", "arbitrary"),
        name="ssd_fwd",
    )(xbc, xbc, xbc, small, conv_w, conv_b, bias_row, bias_col, alog_row, alog_col, dskip_row, ef, eb, sf0)


def _ssd_bwd(cm, bt, wxb, p3, ya, z, eb, ssd_norm, sb0):
    bsz, length, _ = ya.shape
    t = SSD_CHUNK
    nc = length // t
    rev = lambda b, i: (b, nc - 1 - i, 0)
    const2 = lambda b, i: (0, 0)
    state = lambda b, i: (b, 0, 0, 0)
    chunk_spec = lambda w: pl.BlockSpec((None, t, w), rev)
    state_spec = pl.BlockSpec((None,) + SSD_STATE_SHAPE, state)
    return pl.pallas_call(
        _ssd_bwd_kernel,
        grid=(bsz, nc),
        in_specs=[chunk_spec(SSD_BC_W),
                  pl.BlockSpec((None, None, SSD_BC_W, t), lambda b, i: (b, nc - 1 - i, 0, 0)),
                  chunk_spec(SSD_WIDTH), chunk_spec(SMALL_W), chunk_spec(SSD_WIDTH), chunk_spec(SSD_WIDTH),
                  pl.BlockSpec((SMALL_W, SSD_WIDTH), const2),
                  pl.BlockSpec((1, SSD_WIDTH), const2),
                  state_spec],
        out_specs=[chunk_spec(SSD_WIDTH), state_spec],
        out_shape=[jax.ShapeDtypeStruct((bsz, length, SSD_WIDTH), BF16),
                   jax.ShapeDtypeStruct((bsz,) + SSD_STATE_SHAPE, F32)],
        scratch_shapes=[pltpu.VMEM(SSD_STATE_SHAPE, F32)],
        compiler_params=_params("arbitrary", "arbitrary"),
        name="ssd_bwd",
    )(cm, bt, wxb, p3, ya, z, eb, ssd_norm.reshape(1, SSD_WIDTH), sb0)


def _gla_fwd_kernel(q_ref, k_ref, v_ref, sm_ref, wg_ref, gb_ref, sf0_ref,
                    oa_ref, qdb_ref, kwb_ref, decb_ref, sff_ref, sf_s):
    i = pl.program_id(1)
    n_steps = pl.num_programs(1)
    t = GLA_CHUNK
    rows = q_ref.shape[0]
    n_sub = rows // t
    dk, dv = GLA_KEY_DIM, GLA_VAL_DIM
    qscale = dk ** -0.5
    inv_norm = 1.0 / GLA_GATE_NORMALIZER

    @pl.when(i == 0)
    def _():
        sf_s[...] = sf0_ref[...]

    lower, upper = _tri(t)
    row = lax.broadcasted_iota(jnp.int32, (rows, rows), 0)
    col = lax.broadcasted_iota(jnp.int32, (rows, rows), 1)
    diff = row - col
    pos = row % t
    bd_lower = jnp.where(diff >= 0, jnp.where(diff <= pos, 1.0, 0.0), 0.0).astype(BF16)
    bd_upper = jnp.where(diff <= 0, jnp.where(-diff <= t - 1 - pos, 1.0, 0.0), 0.0).astype(BF16)

    hi, lo = _split(sm_ref[...])
    wg_hi, wg_lo = _split(wg_ref[...])
    logits = (jnp.dot(hi, wg_hi, preferred_element_type=F32) + jnp.dot(lo, wg_hi, preferred_element_type=F32)
              + jnp.dot(hi, wg_lo, preferred_element_type=F32)) + gb_ref[...]
    lg = _log_sigmoid(logits) * inv_norm
    cs = _dot_split_rhs(bd_lower, lg[:, 0:GLA_K])
    rc = _dot_split_rhs(bd_upper, lg[:, GLA_K:2 * GLA_K])

    def per_chunk_row(v, offset):
        return jnp.concatenate([jnp.broadcast_to(v[s * t + offset:s * t + offset + 1, :], (t, v.shape[1]))
                                for s in range(n_sub)], axis=0)

    for h in range(GLA_HEADS):
        ks = slice(h * dk, (h + 1) * dk)
        vs = slice(h * dv, (h + 1) * dv)
        c = cs[:, ks]
        r = rc[:, ks]
        qh = q_ref[:, ks] * qscale
        kh = k_ref[:, ks]
        qdf = (qh * jnp.exp(c)).astype(BF16)
        kif = (kh * jnp.exp(-c)).astype(BF16)
        qdb = (qh * jnp.exp(r)).astype(BF16)
        kib = (kh * jnp.exp(-r)).astype(BF16)
        kwf = (kh * jnp.exp(per_chunk_row(c, t - 1) - c)).astype(BF16)
        qdb_ref[:, ks] = qdb
        kwb_ref[:, ks] = (kh * jnp.exp(per_chunk_row(r, 0) - r)).astype(BF16)
        for s in range(n_sub):
            rs = slice(s * t, (s + 1) * t)
            decb_ref[s, :, ks] = jnp.exp(r[s * t:s * t + 1, :])
            vh = v_ref[rs, vs].astype(BF16)
            att = (jnp.where(lower, _dot_nt(qdf[rs], kif[rs]), 0.0)
                   + jnp.where(upper, _dot_nt(qdb[rs], kib[rs]), 0.0))
            oa_ref[rs, vs] = _dot(att, vh) + _dot_nt(qdf[rs], sf_s[h])
            sf_s[h] = sf_s[h] * jnp.exp(c[s * t + t - 1:s * t + t, :]) + _dot_tn(vh, kwf[rs])

    @pl.when(i == n_steps - 1)
    def _():
        sff_ref[...] = sf_s[...]


def _gla_bwd_kernel(qdb_ref, kwb_ref, decb_ref, v_ref, oa_ref, g_ref, gn_ref, sb0_ref,
                    os_ref, sbf_ref, sb_s):
    i = pl.program_id(1)
    n_steps = pl.num_programs(1)
    t = GLA_CHUNK
    n_sub = qdb_ref.shape[0] // t
    dk, dv = GLA_KEY_DIM, GLA_VAL_DIM

    @pl.when(i == 0)
    def _():
        sb_s[...] = sb0_ref[...]

    for s in reversed(range(n_sub)):
        rs = slice(s * t, (s + 1) * t)
        for h in range(GLA_HEADS):
            ks = slice(h * dk, (h + 1) * dk)
            vs = slice(h * dv, (h + 1) * dv)
            o = oa_ref[rs, vs] + _dot_nt(qdb_ref[rs, ks], sb_s[h])
            sb_s[h] = sb_s[h] * decb_ref[s, :, ks] + _dot_tn(v_ref[rs, vs], kwb_ref[rs, ks])
            os_ref[rs, vs] = (_head_norm(o, gn_ref[:, vs]) * _silu(g_ref[rs, vs])).astype(os_ref.dtype)

    @pl.when(i == n_steps - 1)
    def _():
        sbf_ref[...] = sb_s[...]


GLA_STATE_SHAPE = (GLA_HEADS, GLA_VAL_DIM, GLA_KEY_DIM)


def _gla_fwd(q, k, v, small, wg, gb, sf0, rows):
    bsz, length, _ = q.shape
    rows = _row_tile(length, rows)
    n_steps = length // rows
    n_sub = rows // GLA_CHUNK
    fwd = lambda b, i: (b, i, 0)
    const2 = lambda b, i: (0, 0)
    state = lambda b, i: (b, 0, 0, 0)
    blk = lambda w: pl.BlockSpec((None, rows, w), fwd)
    state_spec = pl.BlockSpec((None,) + GLA_STATE_SHAPE, state)
    return pl.pallas_call(
        _gla_fwd_kernel,
        grid=(bsz, n_steps),
        in_specs=[blk(GLA_K), blk(GLA_K), blk(GLA_V), blk(SMALL_W),
                  pl.BlockSpec((SMALL_W, 2 * GLA_K), const2),
                  pl.BlockSpec((1, 2 * GLA_K), const2),
                  state_spec],
        out_specs=[blk(GLA_V), blk(GLA_K), blk(GLA_K),
                   pl.BlockSpec((None, n_sub, 1, GLA_K), lambda b, i: (b, i, 0, 0)),
                   state_spec],
        out_shape=[jax.ShapeDtypeStruct((bsz, length, GLA_V), F32),
                   jax.ShapeDtypeStruct((bsz, length, GLA_K), BF16),
                   jax.ShapeDtypeStruct((bsz, length, GLA_K), BF16),
                   jax.ShapeDtypeStruct((bsz, length // GLA_CHUNK, 1, GLA_K), F32),
                   jax.ShapeDtypeStruct((bsz,) + GLA_STATE_SHAPE, F32)],
        scratch_shapes=[pltpu.VMEM(GLA_STATE_SHAPE, F32)],
        compiler_params=_params("arbitrary", "arbitrary"),
        name="gla_fwd",
    )(q, k, v, small, wg, gb, sf0)


def _gla_bwd(qdb, kwb, decb, v, oa, g, gla_norm, sb0, rows):
    bsz, length, _ = oa.shape
    rows = _row_tile(length, rows)
    n_steps = length // rows
    n_sub = rows // GLA_CHUNK
    rev = lambda b, i: (b, n_steps - 1 - i, 0)
    const2 = lambda b, i: (0, 0)
    state = lambda b, i: (b, 0, 0, 0)
    blk = lambda w: pl.BlockSpec((None, rows, w), rev)
    state_spec = pl.BlockSpec((None,) + GLA_STATE_SHAPE, state)
    return pl.pallas_call(
        _gla_bwd_kernel,
        grid=(bsz, n_steps),
        in_specs=[blk(GLA_K), blk(GLA_K),
                  pl.BlockSpec((None, n_sub, 1, GLA_K), lambda b, i: (b, n_steps - 1 - i, 0, 0)),
                  blk(GLA_V), blk(GLA_V), blk(GLA_V),
                  pl.BlockSpec((1, GLA_V), const2),
                  state_spec],
        out_specs=[blk(GLA_V), state_spec],
        out_shape=[jax.ShapeDtypeStruct((bsz, length, GLA_V), BF16),
                   jax.ShapeDtypeStruct((bsz,) + GLA_STATE_SHAPE, F32)],
        scratch_shapes=[pltpu.VMEM(GLA_STATE_SHAPE, F32)],
        compiler_params=_params("arbitrary", "arbitrary"),
        name="gla_bwd",
    )(qdb, kwb, decb, v, oa, g, gla_norm.reshape(1, GLA_V), sb0)


def _out_proj_kernel(x_ref, gate_ref, *refs):
    *in_refs, w_ref, y_ref = refs
    acc = None
    off = 0
    for ref in in_refs:
        n = ref.shape[-1]
        part = jnp.dot(ref[...], w_ref[off:off + n, :], preferred_element_type=F32)
        acc = part if acc is None else acc + part
        off += n
    y_ref[...] = x_ref[...] + gate_ref[...] * acc


def _out_proj(x, gate, mixed, w_bf16, tm, name):
    bsz, length, d = x.shape
    tm = _row_tile(length, tm)
    row = lambda b, i: (b, i, 0)
    return pl.pallas_call(
        _out_proj_kernel,
        grid=(bsz, length // tm),
        in_specs=([pl.BlockSpec((None, tm, d), row),
                   pl.BlockSpec((None, 1, d), lambda b, i: (b, 0, 0))]
                  + [pl.BlockSpec((None, tm, m.shape[-1]), row) for m in mixed]
                  + [pl.BlockSpec(w_bf16.shape, lambda b, i: (0, 0), pipeline_mode=pl.Buffered(1))]),
        out_specs=pl.BlockSpec((None, tm, d), row),
        out_shape=jax.ShapeDtypeStruct((bsz, length, d), F32),
        compiler_params=_params("parallel", "parallel"),
        name=name,
    )(x, gate, *mixed, w_bf16)


def _attn_kernel(sink_ref, q_ref, k_ref, v_ref, kc_ref, vc_ref, g_ref, o_ref):
    j = pl.program_id(1)
    i = pl.program_id(2)
    length = k_ref.shape[0]
    blk = ATT_BLOCK
    band = 3 * blk
    dh = ATT_HEAD_DIM
    n_q = q_ref.shape[0] // blk
    n_ctx = kc_ref.shape[0]
    row_minus_col = (lax.broadcasted_iota(jnp.int32, (blk, band), 0)
                     - lax.broadcasted_iota(jnp.int32, (blk, band), 1))
    sink = jnp.concatenate([jnp.full((blk, LANES), sink_ref[j * ATT_GROUP + g] * LOG2E, F32)
                            for g in range(ATT_GROUP)], axis=0)
    kc = kc_ref[...]
    vc_ext = jnp.concatenate([vc_ref[...], jnp.ones((n_ctx, dh), BF16)], axis=1)
    ones_band = jnp.ones((band, dh), BF16)
    for qb in range(n_q):
        blk_idx = i * n_q + qb
        start = pl.multiple_of(jnp.clip((blk_idx - 1) * blk, 0, length - band), blk)
        inside = jnp.abs(row_minus_col + (blk_idx * blk - start)) <= WINDOW
        kw = k_ref[pl.ds(start, band), :]
        v_all = jnp.concatenate([jnp.concatenate([v_ref[pl.ds(start, band), :], ones_band], axis=1),
                                 vc_ext], axis=0)
        rs = slice(qb * blk, (qb + 1) * blk)
        q = jnp.concatenate([q_ref[rs, g * dh:(g + 1) * dh] for g in range(ATT_GROUP)], axis=0)
        s_band = _dot_nt(q, kw)
        s_ctx = _dot_nt(q, kc)
        cols = []
        for c in range(band // LANES):
            ls = slice(c * LANES, (c + 1) * LANES)
            cols.append(jnp.concatenate(
                [jnp.where(inside[:, ls], s_band[g * blk:(g + 1) * blk, ls], float("-inf"))
                 for g in range(ATT_GROUP)], axis=0))
        for c in range(n_ctx // LANES):
            cols.append(s_ctx[:, c * LANES:(c + 1) * LANES])
        m = jnp.maximum(sink, jnp.max(functools.reduce(jnp.maximum, cols), axis=-1, keepdims=True))
        p = jnp.concatenate([jnp.exp2(col - m) for col in cols], axis=1).astype(BF16)
        acc = jnp.dot(p, v_all, preferred_element_type=F32)
        out = acc[:, 0:dh] / (acc[:, dh:2 * dh] + jnp.exp2(sink - m))
        for g in range(ATT_GROUP):
            cs = slice(g * dh, (g + 1) * dh)
            o_ref[rs, cs] = (out[g * blk:(g + 1) * blk, :] * g_ref[rs, cs]).astype(o_ref.dtype)


def _attention(sink, q, k, v, kc, vc, gate):
    bsz, length, _ = q.shape
    n_ctx = kc.shape[1]
    rows = _row_tile(length, ATT_Q_BLOCKS * ATT_BLOCK)
    gw = ATT_GROUP * ATT_HEAD_DIM
    qmap = lambda b, j, i, s: (b, i, j)
    kvmap = lambda b, j, i, s: (b, 0, j)
    grid_spec = pltpu.PrefetchScalarGridSpec(
        num_scalar_prefetch=1,
        grid=(bsz, ATT_KV_HEADS, length // rows),
        in_specs=[pl.BlockSpec((None, rows, gw), qmap),
                  pl.BlockSpec((None, length, ATT_HEAD_DIM), kvmap),
                  pl.BlockSpec((None, length, ATT_HEAD_DIM), kvmap),
                  pl.BlockSpec((None, n_ctx, ATT_HEAD_DIM), kvmap),
                  pl.BlockSpec((None, n_ctx, ATT_HEAD_DIM), kvmap),
                  pl.BlockSpec((None, rows, gw), qmap)],
        out_specs=pl.BlockSpec((None, rows, gw), qmap),
    )
    return pl.pallas_call(
        _attn_kernel,
        grid_spec=grid_spec,
        out_shape=jax.ShapeDtypeStruct((bsz, length, ATT_W), BF16),
        compiler_params=_params("parallel", "parallel", "arbitrary"),
        name="attention",
    )(sink, q, k, v, kc, vc, gate)


EVEN_WIDTHS = (SSD_WIDTH, CONV_CH, GLA_K, GLA_K, GLA_V, GLA_V, SMALL_W)


def _even_weight_layout(w_in):
    d = w_in.shape[0]
    sizes = (SSD_WIDTH, CONV_CH, 2 * SSD_HEADS, GLA_K, GLA_K, GLA_V, GLA_V, 2 * GLA_RANK)
    z, xbc, dt, q, k, v, g, lr = jnp.split(w_in, np.cumsum(sizes)[:-1].tolist(), axis=1)
    pad = jnp.zeros((d, SMALL_W - 2 * SSD_HEADS - 2 * GLA_RANK), w_in.dtype)
    return jnp.concatenate([z, xbc, q, k, v, g, dt, lr, pad], axis=1).astype(BF16)


def _expansion_matrices():
    rows = np.arange(SMALL_W)[:, None]
    heads = (np.arange(SSD_WIDTH) // SSD_HEAD_DIM)[None, :]
    ef = (rows == heads).astype(np.float32)
    eb = (rows - SSD_HEADS == heads).astype(np.float32)
    return jnp.asarray(ef, BF16), jnp.asarray(eb, BF16)


def _pad_lanes(v, width):
    return jnp.pad(v, ((0, 0), (0, width - v.shape[1])))


def _rope_tables(length):
    rows = length // GRID_W
    row = jnp.repeat(jnp.arange(rows, dtype=F32), GRID_W)
    col = jnp.tile(jnp.arange(GRID_W, dtype=F32), rows)
    inv = 1.0 / (ROPE_BASE ** (jnp.arange(ROPE_FREQS, dtype=F32) / ROPE_FREQS))
    ang_r = row[:, None] * inv
    ang_c = col[:, None] * inv
    cos = jnp.concatenate([jnp.cos(ang_r), jnp.cos(ang_r), jnp.cos(ang_c), jnp.cos(ang_c)], axis=1)
    sin = jnp.concatenate([-jnp.sin(ang_r), jnp.sin(ang_r), -jnp.sin(ang_c), jnp.sin(ang_c)], axis=1)
    return cos, sin


def _mod_rows(mod, rows, bsz, d):
    picked = jnp.broadcast_to(mod[rows], (bsz, 3 * d)) if isinstance(rows, int) else mod[rows]
    return [picked[:, None, j * d:(j + 1) * d] for j in range(3)]


PROJ_ROWS = 512
OUT_ROWS = 512
ATT_Q_BLOCKS = 8
GLA_ROWS = 256
CVEC_ROWS = SUBLANES


def kernel(x, c, ctx, c_ctx, e_norm, e_mod_w, e_mod_b, e_w_in, e_conv_w, e_conv_b, e_dt_bias, e_a_log,
           e_d_skip, e_ssd_norm, e_gla_gate_w, e_gla_gate_b, e_gla_norm, e_w_out, o_norm, o_mod_w, o_mod_b,
           o_w_in, o_q_norm, o_k_norm, o_sink, o_w_out):
    bsz, length, d = x.shape
    n_ctx = ctx.shape[1]
    assert e_norm.shape[0] == 1 and o_norm.shape[0] == 1, "two-layer block only"
    assert length % SSD_CHUNK == 0 and n_ctx % SSD_CHUNK == 0 and length >= 3 * ATT_BLOCK
    assert bsz + 1 <= CVEC_ROWS

    cvecs = jnp.zeros((CVEC_ROWS, d), F32).at[:bsz].set(c).at[bsz].set(c_ctx)
    lat_rows = slice(0, bsz)

    mod = _adaln(cvecs, e_mod_w[0], e_mod_b[0])
    shift, scale, gate = _mod_rows(mod, lat_rows, bsz, d)
    c_shift, c_scale, c_gate = _mod_rows(mod, bsz, bsz, d)
    w_in = _even_weight_layout(e_w_in[0])
    w_out = e_w_out[0].astype(BF16)
    ef, eb = _expansion_matrices()
    nh2 = 2 * SSD_HEADS
    bias_flat = e_dt_bias[0].reshape(1, nh2)
    alog_flat = e_a_log[0].reshape(1, nh2)
    bias_row, alog_row = _pad_lanes(bias_flat, SMALL_W), _pad_lanes(alog_flat, SMALL_W)
    bias_col, alog_col = bias_flat.reshape(nh2, 1), alog_flat.reshape(nh2, 1)
    dskip_row = jnp.repeat(e_d_skip[0], SSD_HEAD_DIM).reshape(1, SSD_WIDTH)
    conv_b = e_conv_b[0].reshape(1, CONV_CH)
    wg = jnp.zeros((SMALL_W, 2 * GLA_K), F32)
    wg = wg.at[nh2:nh2 + GLA_RANK, 0:GLA_K].set(e_gla_gate_w[0, 0])
    wg = wg.at[nh2 + GLA_RANK:nh2 + 2 * GLA_RANK, GLA_K:2 * GLA_K].set(e_gla_gate_w[0, 1])
    gb = e_gla_gate_b[0].reshape(1, 2 * GLA_K)

    def mixers(stream, sc, sh, ssd_init, gla_init):
        z, xbc, q, k, v, g, small = _proj_even(stream, e_norm[0], sc, sh, w_in, EVEN_WIDTHS, PROJ_ROWS)
        ya, cm, bt, wxb, p3, ssd_f = _ssd_fwd(xbc, small, e_conv_w[0], conv_b, bias_row, bias_col,
                                              alog_row, alog_col, dskip_row, ef, eb, ssd_init[0])
        ys, ssd_b = _ssd_bwd(cm, bt, wxb, p3, ya, z, eb, e_ssd_norm[0], ssd_init[1])
        oa, qdb, kwb, decb, gla_f = _gla_fwd(q, k, v, small, wg, gb, gla_init[0], GLA_ROWS)
        os_, gla_b = _gla_bwd(qdb, kwb, decb, v, oa, g, e_gla_norm[0], gla_init[1], GLA_ROWS)
        return (ys, os_), (ssd_f, ssd_b), (gla_f, gla_b)

    ssd0 = jnp.zeros((bsz,) + SSD_STATE_SHAPE, F32)
    gla0 = jnp.zeros((bsz,) + GLA_STATE_SHAPE, F32)
    ctx_mix, ssd_fin, gla_fin = mixers(ctx, c_scale, c_shift, (ssd0, ssd0), (gla0, gla0))
    lat_mix, _, _ = mixers(x, scale, shift, ssd_fin, gla_fin)
    x = _out_proj(x, gate, lat_mix, w_out, OUT_ROWS, "out_even")
    xc = _out_proj(ctx, c_gate, ctx_mix, w_out, OUT_ROWS, "out_even")

    mod = _adaln(cvecs, o_mod_w[0], o_mod_b[0])
    shift, scale, gate = _mod_rows(mod, lat_rows, bsz, d)
    c_shift, c_scale, _ = _mod_rows(mod, bsz, bsz, d)
    w_in = o_w_in[0].astype(BF16)
    cos, sin = _rope_tables(length)
    no_rot = (jnp.ones((n_ctx, ATT_HEAD_DIM), F32), jnp.zeros((n_ctx, ATT_HEAD_DIM), F32))
    kc, vc = _proj_odd(xc, o_norm[0], c_scale, c_shift, w_in[:, :2 * ATT_KV_W], *no_rot,
                       o_q_norm[0], o_k_norm[0], False, PROJ_ROWS)
    k, v, q, g = _proj_odd(x, o_norm[0], scale, shift, w_in, cos, sin, o_q_norm[0], o_k_norm[0], True, PROJ_ROWS)
    o = _attention(o_sink[0].astype(F32), q, k, v, kc, vc, g)
    return _out_proj(x, gate, (o,), o_w_out[0].astype(BF16), OUT_ROWS, "out_odd")
```

```python
import functools

import jax
import jax.numpy as jnp
import numpy as np
from jax import lax
from jax.experimental import pallas as pl
from jax.experimental.pallas import tpu as pltpu

F32 = jnp.float32
BF16 = jnp.bfloat16

GRID_W = 64
SSD_HEADS = 16
SSD_HEAD_DIM = 64
SSD_WIDTH = SSD_HEADS * SSD_HEAD_DIM
SSD_GROUPS = 2
SSD_STATE = 128
SSD_CHUNK = 128
CONV_K = 5
CONV_CH = SSD_WIDTH + 2 * SSD_GROUPS * SSD_STATE
GLA_HEADS = 4
GLA_KEY_DIM = 128
GLA_VAL_DIM = 256
GLA_K = GLA_HEADS * GLA_KEY_DIM
GLA_V = GLA_HEADS * GLA_VAL_DIM
GLA_RANK = 16
GLA_GATE_NORMALIZER = 16.0
GLA_CHUNK = 64
ATT_HEADS = 16
ATT_KV_HEADS = 4
ATT_GROUP = ATT_HEADS // ATT_KV_HEADS
ATT_HEAD_DIM = 128
ATT_W = ATT_HEADS * ATT_HEAD_DIM
ATT_KV_W = ATT_KV_HEADS * ATT_HEAD_DIM
WINDOW = 128
ATT_BLOCK = 128
ROPE_BASE = 10000.0
ROPE_FREQS = ATT_HEAD_DIM // 4
NORM_EPS = 1e-6
LOG2E = 1.4426950408889634

LANES = 128
SUBLANES = 8
VMEM_LIMIT_BYTES = 56 * 1024 * 1024

SSD_GROUP_W = SSD_WIDTH // SSD_GROUPS
SSD_BC_W = SSD_GROUPS * SSD_STATE
SMALL_W = LANES
HALO = SUBLANES
HEADS_PER_DOT = 4
CONV_ROW_STRIDE = 2 * SUBLANES + 1
CONV_OUT_ROWS = CONV_ROW_STRIDE * SUBLANES
CONV_WIN_ROWS = 152
assert CONV_OUT_ROWS >= SSD_CHUNK and CONV_WIN_ROWS >= HALO + CONV_OUT_ROWS + (CONV_K - 1) // 2


def _dot(a, b):
    return jnp.dot(a.astype(BF16), b.astype(BF16), preferred_element_type=F32)


def _dot_nt(a, b):
    return lax.dot_general(a.astype(BF16), b.astype(BF16), (((1,), (1,)), ((), ())),
                           preferred_element_type=F32)


def _dot_tn(a, b):
    return lax.dot_general(a.astype(BF16), b.astype(BF16), (((0,), (0,)), ((), ())),
                           preferred_element_type=F32)


def _split(v):
    hi = v.astype(BF16)
    lo = (v - hi.astype(F32)).astype(BF16)
    return hi, lo


def _dot_split_lhs(v, m):
    hi, lo = _split(v)
    return (jnp.dot(hi, m, preferred_element_type=F32) + jnp.dot(lo, m, preferred_element_type=F32))


def _dot_split_rhs(m, v):
    hi, lo = _split(v)
    return (jnp.dot(m, hi, preferred_element_type=F32) + jnp.dot(m, lo, preferred_element_type=F32))


def _dot3(a, b):
    ah, al = _split(a)
    bh, bl = _split(b)
    return (jnp.dot(ah, bh, preferred_element_type=F32) + jnp.dot(al, bh, preferred_element_type=F32)
            + jnp.dot(ah, bl, preferred_element_type=F32))


def _silu(v):
    return v * jax.nn.sigmoid(v)


def _softplus(v):
    return jnp.maximum(v, 0.0) + jnp.log(1.0 + jnp.exp(-jnp.abs(v)))


def _log_sigmoid(v):
    return jnp.minimum(v, 0.0) - jnp.log(1.0 + jnp.exp(-jnp.abs(v)))


def _tri(n):
    row = lax.broadcasted_iota(jnp.int32, (n, n), 0)
    col = lax.broadcasted_iota(jnp.int32, (n, n), 1)
    return row >= col, col >= row


def _params(*sem):
    return pltpu.CompilerParams(dimension_semantics=sem, vmem_limit_bytes=VMEM_LIMIT_BYTES)


def _adaln_kernel(c_ref, w_ref, b_ref, o_ref):
    o_ref[...] = _dot3(_silu(c_ref[...]), w_ref[...]) + b_ref[...]


def _adaln(cvecs, w, b):
    rows, d = cvecs.shape
    n = w.shape[1]
    tn = 1024
    return pl.pallas_call(
        _adaln_kernel,
        grid=(n // tn,),
        in_specs=[pl.BlockSpec((rows, d), lambda j: (0, 0)),
                  pl.BlockSpec((d, tn), lambda j: (0, j)),
                  pl.BlockSpec((1, tn), lambda j: (0, j))],
        out_specs=pl.BlockSpec((rows, tn), lambda j: (0, j)),
        out_shape=jax.ShapeDtypeStruct((rows, n), F32),
        compiler_params=_params("parallel"),
        name="adaln",
    )(cvecs, w, b.reshape(1, n))


def _modulated_norm(x, g, sc, sh):
    r = lax.rsqrt(jnp.mean(x * x, axis=-1, keepdims=True) + NORM_EPS)
    return ((x * r) * g) * (1.0 + sc) + sh


def _store_cols(h, w_ref, off, ref, act=None):
    n = ref.shape[-1]
    for c0 in range(0, n, 512):
        c1 = min(n, c0 + 512)
        t = jnp.dot(h, w_ref[:, off + c0:off + c1], preferred_element_type=F32)
        ref[:, c0:c1] = (t if act is None else act(t)).astype(ref.dtype)
    return off + n


def _proj_even_kernel(x_ref, g_ref, sc_ref, sh_ref, w_ref, *out_refs):
    h = _modulated_norm(x_ref[...], g_ref[...], sc_ref[...], sh_ref[...]).astype(BF16)
    off = 0
    for ref in out_refs:
        off = _store_cols(h, w_ref, off, ref)


def _head_norm(t, gain):
    r = lax.rsqrt(jnp.mean(t * t, axis=-1, keepdims=True) + NORM_EPS)
    return (t * r) * gain


def _rope(t, cos, sin_signed):
    return t * cos + pltpu.roll(t, ATT_HEAD_DIM // 2, 1) * sin_signed


def _proj_odd_kernel(x_ref, g_ref, sc_ref, sh_ref, w_ref, cos_ref, sin_ref, qn_ref, kn_ref,
                     k_ref, v_ref, *qg_refs):
    h = _modulated_norm(x_ref[...], g_ref[...], sc_ref[...], sh_ref[...]).astype(BF16)
    cos = cos_ref[...]
    sin = sin_ref[...]
    scale = ATT_HEAD_DIM ** -0.5 * LOG2E

    width = HEADS_PER_DOT * ATT_HEAD_DIM
    dh = ATT_HEAD_DIM
    pr = lax.broadcasted_iota(jnp.int32, (2 * dh, 2 * dh), 0) // dh
    pc = lax.broadcasted_iota(jnp.int32, (2 * dh, 2 * dh), 1) // dh
    head_ones = jnp.where(pr == pc, 1.0, 0.0).astype(BF16)

    def project(col):
        return jnp.dot(h, w_ref[:, col:col + width], preferred_element_type=F32)

    def finish_heads(t4, ref, j0, gain, out_scale):
        for j in range(0, HEADS_PER_DOT, 2):
            t2 = t4[:, j * dh:(j + 2) * dh]
            mean_sq = jnp.dot((t2 * t2).astype(BF16), head_ones, preferred_element_type=F32) * (1.0 / dh)
            t2 = t2 * lax.rsqrt(mean_sq + NORM_EPS)
            for jj in range(2):
                t = _rope(t2[:, jj * dh:(jj + 1) * dh] * gain, cos, sin)
                c0 = (j0 + j + jj) * dh
                ref[:, c0:c0 + dh] = (t if out_scale is None else t * out_scale).astype(ref.dtype)

    work = []
    if qg_refs:
        q_ref, gate_ref = qg_refs
        for j0 in range(0, ATT_HEADS, HEADS_PER_DOT):
            c0 = j0 * dh

            def finish_gate(t4, c0=c0):
                gate_ref[:, c0:c0 + width] = _silu(t4)

            work.append((2 * ATT_KV_W + ATT_W + c0, finish_gate))
            work.append((2 * ATT_KV_W + c0,
                         functools.partial(finish_heads, ref=q_ref, j0=j0, gain=qn_ref[...], out_scale=scale)))
    for j0 in range(0, ATT_KV_HEADS, HEADS_PER_DOT):
        work.append((j0 * dh, functools.partial(finish_heads, ref=k_ref, j0=j0, gain=kn_ref[...], out_scale=None)))
    for col, finish in work:
        finish(project(col))
    _store_cols(h, w_ref, ATT_KV_W, v_ref)


def _row_tile(length, want):
    return min(length, want)


def _proj_even(x, norm_g, scale, shift, w_bf16, widths, tm):
    bsz, length, d = x.shape
    tm = _row_tile(length, tm)
    n = w_bf16.shape[1]
    row = lambda b, i: (b, i, 0)
    mod = lambda b, i: (b, 0, 0)
    return pl.pallas_call(
        _proj_even_kernel,
        grid=(bsz, length // tm),
        in_specs=[pl.BlockSpec((None, tm, d), row),
                  pl.BlockSpec((1, d), lambda b, i: (0, 0)),
                  pl.BlockSpec((None, 1, d), mod),
                  pl.BlockSpec((None, 1, d), mod),
                  pl.BlockSpec((d, n), lambda b, i: (0, 0), pipeline_mode=pl.Buffered(1))],
        out_specs=[pl.BlockSpec((None, tm, wd), row) for wd in widths],
        out_shape=[jax.ShapeDtypeStruct((bsz, length, wd), F32) for wd in widths],
        compiler_params=_params("parallel", "parallel"),
        name="proj_even",
    )(x, norm_g.reshape(1, d), scale, shift, w_bf16)


def _proj_odd(x, norm_g, scale, shift, w_bf16, cos, sin, q_norm, k_norm, with_queries, tm):
    bsz, length, d = x.shape
    tm = _row_tile(length, tm)
    n = w_bf16.shape[1]
    row = lambda b, i: (b, i, 0)
    mod = lambda b, i: (b, 0, 0)
    const = lambda b, i: (0, 0)
    widths = [(ATT_KV_W, BF16), (ATT_KV_W, BF16)]
    if with_queries:
        widths += [(ATT_W, BF16), (ATT_W, F32)]
    return pl.pallas_call(
        _proj_odd_kernel,
        grid=(bsz, length // tm),
        in_specs=[pl.BlockSpec((None, tm, d), row),
                  pl.BlockSpec((1, d), const),
                  pl.BlockSpec((None, 1, d), mod),
                  pl.BlockSpec((None, 1, d), mod),
                  pl.BlockSpec((d, n), const, pipeline_mode=pl.Buffered(1)),
                  pl.BlockSpec((tm, ATT_HEAD_DIM), lambda b, i: (i, 0)),
                  pl.BlockSpec((tm, ATT_HEAD_DIM), lambda b, i: (i, 0)),
                  pl.BlockSpec((1, ATT_HEAD_DIM), const),
                  pl.BlockSpec((1, ATT_HEAD_DIM), const)],
        out_specs=[pl.BlockSpec((None, tm, wd), row) for wd, _ in widths],
        out_shape=[jax.ShapeDtypeStruct((bsz, length, wd), dt) for wd, dt in widths],
        compiler_params=_params("parallel", "parallel"),
        name="proj_odd_q" if with_queries else "proj_odd_kv",
    )(x, norm_g.reshape(1, d), scale, shift, w_bf16, cos, sin,
      q_norm.reshape(1, ATT_HEAD_DIM), k_norm.reshape(1, ATT_HEAD_DIM))


def _conv_silu(win_s, prev_ref, cur_ref, next_ref, has_prev, has_next, cw_ref, cb_ref, x_s):
    t = cur_ref.shape[0]
    half = (CONV_K - 1) // 2
    for j in range(CONV_CH // LANES):
        ls = slice(j * LANES, (j + 1) * LANES)
        win_s[j, 0:HALO, :] = jnp.where(has_prev, prev_ref[:, ls], 0.0)
        win_s[j, HALO:HALO + t, :] = cur_ref[:, ls]
        win_s[j, HALO + t:2 * HALO + t, :] = jnp.where(has_next, next_ref[:, ls], 0.0)
        win_s[j, 2 * HALO + t:CONV_WIN_ROWS, :] = jnp.zeros((CONV_WIN_ROWS - 2 * HALO - t, LANES), F32)
        taps = [cw_ref[k:k + 1, ls] for k in range(CONV_K)]
        bias = cb_ref[:, ls]
        win = win_s.at[j]
        out = x_s.at[j]
        for a in range(CONV_ROW_STRIDE):
            acc = bias + taps[0] * win[pl.ds(HALO - half + a, SUBLANES, stride=CONV_ROW_STRIDE), :]
            for k in range(1, CONV_K):
                acc = acc + taps[k] * win[pl.ds(HALO - half + k + a, SUBLANES, stride=CONV_ROW_STRIDE), :]
            hv = 0.5 * acc
            out[pl.ds(a, SUBLANES, stride=CONV_ROW_STRIDE), :] = hv + hv * jnp.tanh(hv)


def _ssd_fwd_kernel(p_ref, c_ref, n_ref, sm_ref, cw_ref, cb_ref, bias_row_ref, bias_col_ref,
                    alog_row_ref, alog_col_ref, dskip_ref, ef_ref, eb_ref, sf0_ref,
                    ya_ref, cm_ref, bt_ref, wxb_ref, p3_ref, sff_ref,
                    sf_s, win_s, x_s):
    i = pl.program_id(1)
    nc = pl.num_programs(1)
    t = SSD_CHUNK
    nh = SSD_HEADS
    slabs_x = SSD_WIDTH // LANES
    slab_b = slabs_x
    slab_c = slabs_x + SSD_GROUPS

    @pl.when(i == 0)
    def _():
        sf_s[...] = sf0_ref[...]

    lower, upper = _tri(t)
    ltri = jnp.where(lower, 1.0, 0.0).astype(BF16)
    utri = jnp.where(upper, 1.0, 0.0).astype(BF16)
    a_row = -jnp.exp(alog_row_ref[...])
    a_col = -jnp.exp(alog_col_ref[...])
    ef = ef_ref[...]
    eb = eb_ref[...]
    neg_inf = float("-inf")

    _conv_silu(win_s, p_ref, c_ref, n_ref, i > 0, i < nc - 1, cw_ref, cb_ref, x_s)
    sm = sm_ref[...]
    dt = _softplus(sm + bias_row_ref[...])
    dta = dt * a_row
    cs = _dot_split_rhs(ltri, dta)
    rc = _dot_split_rhs(utri, dta)
    dt_t = _softplus(sm.T[0:2 * nh, :] + bias_col_ref[...])
    dta_t = dt_t * a_col
    log2_dt_t = jnp.log2(dt_t)
    f_q = cs * LOG2E
    g_q = rc * LOG2E
    f_k = _dot_split_lhs(dta_t, utri) * LOG2E - log2_dt_t
    g_k = _dot_split_lhs(dta_t, ltri) * LOG2E - log2_dt_t
    dec_f = _dot_split_lhs(jnp.exp2(f_q), ef)
    wgt_f = _dot_split_lhs(dt * jnp.exp(cs[t - 1:t, :] - cs), ef)
    wgt_b = _dot_split_lhs(dt * jnp.exp(rc[0:1, :] - rc), eb)

    p3_ref[...] = jnp.exp2(g_q)
    for p in range(slabs_x):
        ls = slice(p * LANES, (p + 1) * LANES)
        wxb_ref[:, ls] = (x_s[p, 0:t, :] * wgt_b[:, ls]).astype(BF16)

    lane = lax.broadcasted_iota(jnp.int32, (t, LANES), 1)
    left = lane < SSD_HEAD_DIM

    def decay_matrix(h, cbg):
        lf = jnp.exp2(jnp.where(lower, f_q[:, h:h + 1] - f_k[h:h + 1, :], neg_inf))
        ub = jnp.exp2(jnp.where(upper, g_q[:, nh + h:nh + h + 1] - g_k[nh + h:nh + h + 1, :], neg_inf))
        return cbg * (lf + ub)

    heads_per_group = nh // SSD_GROUPS
    pairs_per_group = heads_per_group // 2
    for g in range(SSD_GROUPS):
        cg = x_s[slab_c + g, 0:t, :]
        bg = x_s[slab_b + g, 0:t, :]
        gs = slice(g * SSD_GROUP_W, (g + 1) * SSD_GROUP_W)
        cm_ref[:, g * SSD_STATE:(g + 1) * SSD_STATE] = cg.astype(BF16)
        cbg = _dot_nt(cg, bg)
        state = sf_s[g]
        y_off = _dot(cg, state) * dec_f[:, gs]
        wx = []
        for pair in range(pairs_per_group):
            h0 = g * heads_per_group + 2 * pair
            p = h0 // 2
            ls = slice(p * LANES, (p + 1) * LANES)
            xp = x_s[p, 0:t, :]
            y = (_dot(decay_matrix(h0, cbg), jnp.where(left, xp, 0.0))
                 + _dot(decay_matrix(h0 + 1, cbg), jnp.where(left, 0.0, xp)))
            ya_ref[:, ls] = y + y_off[:, pair * LANES:(pair + 1) * LANES] + dskip_ref[:, ls] * xp
            wx.append((xp * wgt_f[:, ls]).astype(BF16))
        bt = bg.T.astype(BF16)
        bt_ref[g * SSD_STATE:(g + 1) * SSD_STATE, :] = bt
        sf_s[g] = dec_f[t - 1:t, gs] * state + jnp.dot(bt, jnp.concatenate(wx, axis=1),
                                                       preferred_element_type=F32)

    @pl.when(i == nc - 1)
    def _():
        sff_ref[...] = sf_s[...]


def _ssd_bwd_kernel(cm_ref, bt_ref, wxb_ref, p3_ref, ya_ref, z_ref, eb_ref, sn_ref, sb0_ref,
                    ys_ref, sbf_ref, sb_s):
    i = pl.program_id(1)
    nc = pl.num_programs(1)

    @pl.when(i == 0)
    def _():
        sb_s[...] = sb0_ref[...]

    t = SSD_CHUNK
    n_sub = bt_ref.shape[0]
    dec_b = _dot_split_lhs(p3_ref[...], eb_ref[...])
    for g in range(SSD_GROUPS):
        gs = slice(g * SSD_GROUP_W, (g + 1) * SSD_GROUP_W)
        ss = slice(g * SSD_STATE, (g + 1) * SSD_STATE)
        chunks = [slice(s * t, (s + 1) * t) for s in range(n_sub)]
        incs = [jnp.dot(bt_ref[s, ss, :], wxb_ref[rs, gs], preferred_element_type=F32)
                for s, rs in enumerate(chunks)]
        states = [None] * n_sub
        state = sb_s[g]
        for s in reversed(range(n_sub)):
            states[s] = state
            state = dec_b[s * t:s * t + 1, gs] * state + incs[s]
        sb_s[g] = state
        for s, rs in enumerate(chunks):
            y_off = jnp.dot(cm_ref[rs, ss], states[s].astype(BF16), preferred_element_type=F32) * dec_b[rs, gs]
            y = (ya_ref[rs, gs] + y_off) * _silu(z_ref[rs, gs])
            ys_ref[rs, gs] = _head_norm(y, sn_ref[:, gs]).astype(ys_ref.dtype)

    @pl.when(i == nc - 1)
    def _():
        sbf_ref[...] = sb_s[...]


SSD_STATE_SHAPE = (SSD_GROUPS, SSD_STATE, SSD_GROUP_W)


def _ssd_fwd(xbc, small, conv_w, conv_b, bias_row, bias_col, alog_row, alog_col, dskip_row, ef, eb, sf0):
    bsz, length, _ = xbc.shape
    t = SSD_CHUNK
    nc = length // t
    per = t // HALO
    last_halo = length // HALO - 1

    def cur(b, i): return (b, i, 0)
    def prev(b, i): return (b, jnp.maximum(i * per - 1, 0), 0)
    def nxt(b, i): return (b, jnp.minimum((i + 1) * per, last_halo), 0)
    const2 = lambda b, i: (0, 0)
    state = lambda b, i: (b, 0, 0, 0)

    halo_spec = lambda f: pl.BlockSpec((None, HALO, CONV_CH), f)
    chunk_spec = lambda w: pl.BlockSpec((None, t, w), cur)
    state_spec = pl.BlockSpec((None,) + SSD_STATE_SHAPE, state)
    return pl.pallas_call(
        _ssd_fwd_kernel,
        grid=(bsz, nc),
        in_specs=[halo_spec(prev), chunk_spec(CONV_CH), halo_spec(nxt), chunk_spec(SMALL_W),
                  pl.BlockSpec((CONV_K, CONV_CH), const2),
                  pl.BlockSpec((1, CONV_CH), const2),
                  pl.BlockSpec((1, SMALL_W), const2),
                  pl.BlockSpec((2 * SSD_HEADS, 1), const2),
                  pl.BlockSpec((1, SMALL_W), const2),
                  pl.BlockSpec((2 * SSD_HEADS, 1), const2),
                  pl.BlockSpec((1, SSD_WIDTH), const2),
                  pl.BlockSpec((SMALL_W, SSD_WIDTH), const2),
                  pl.BlockSpec((SMALL_W, SSD_WIDTH), const2),
                  state_spec],
        out_specs=[chunk_spec(SSD_WIDTH), chunk_spec(SSD_BC_W),
                   pl.BlockSpec((None, None, SSD_BC_W, t), lambda b, i: (b, i, 0, 0)),
                   chunk_spec(SSD_WIDTH), chunk_spec(SMALL_W), state_spec],
        out_shape=[jax.ShapeDtypeStruct((bsz, length, SSD_WIDTH), F32),
                   jax.ShapeDtypeStruct((bsz, length, SSD_BC_W), BF16),
                   jax.ShapeDtypeStruct((bsz, nc, SSD_BC_W, t), BF16),
                   jax.ShapeDtypeStruct((bsz, length, SSD_WIDTH), BF16),
                   jax.ShapeDtypeStruct((bsz, length, SMALL_W), F32),
                   jax.ShapeDtypeStruct((bsz,) + SSD_STATE_SHAPE, F32)],
        scratch_shapes=[pltpu.VMEM(SSD_STATE_SHAPE, F32),
                        pltpu.VMEM((CONV_CH // LANES, CONV_WIN_ROWS, LANES), F32),
                        pltpu.VMEM((CONV_CH // LANES, CONV_OUT_ROWS, LANES), F32)],
        compiler_params=_params("arbitrary", "arbitrary"),
        name="ssd_fwd",
    )(xbc, xbc, xbc, small, conv_w, conv_b, bias_row, bias_col, alog_row, alog_col, dskip_row, ef, eb, sf0)


def _ssd_bwd(cm, bt, wxb, p3, ya, z, eb, ssd_norm, sb0):
    bsz, length, _ = ya.shape
    t = SSD_CHUNK
    rows = _row_tile(length, SSD_BWD_ROWS)
    n_steps = length // rows
    rev = lambda b, i: (b, n_steps - 1 - i, 0)
    const2 = lambda b, i: (0, 0)
    state = lambda b, i: (b, 0, 0, 0)
    chunk_spec = lambda w: pl.BlockSpec((None, rows, w), rev)
    state_spec = pl.BlockSpec((None,) + SSD_STATE_SHAPE, state)
    return pl.pallas_call(
        _ssd_bwd_kernel,
        grid=(bsz, n_steps),
        in_specs=[chunk_spec(SSD_BC_W),
                  pl.BlockSpec((None, rows // t, SSD_BC_W, t), lambda b, i: (b, n_steps - 1 - i, 0, 0)),
                  chunk_spec(SSD_WIDTH), chunk_spec(SMALL_W), chunk_spec(SSD_WIDTH), chunk_spec(SSD_WIDTH),
                  pl.BlockSpec((SMALL_W, SSD_WIDTH), const2),
                  pl.BlockSpec((1, SSD_WIDTH), const2),
                  state_spec],
        out_specs=[chunk_spec(SSD_WIDTH), state_spec],
        out_shape=[jax.ShapeDtypeStruct((bsz, length, SSD_WIDTH), BF16),
                   jax.ShapeDtypeStruct((bsz,) + SSD_STATE_SHAPE, F32)],
        scratch_shapes=[pltpu.VMEM(SSD_STATE_SHAPE, F32)],
        compiler_params=_params("arbitrary", "arbitrary"),
        name="ssd_bwd",
    )(cm, bt, wxb, p3, ya, z, eb, ssd_norm.reshape(1, SSD_WIDTH), sb0)


def _gla_fwd_kernel(q_ref, k_ref, v_ref, sm_ref, wg_ref, gb_ref, sf0_ref,
                    oa_ref, qdb_ref, kwb_ref, decb_ref, sff_ref, sf_s):
    i = pl.program_id(1)
    n_steps = pl.num_programs(1)
    t = GLA_CHUNK
    rows = q_ref.shape[0]
    n_sub = rows // t
    dk, dv = GLA_KEY_DIM, GLA_VAL_DIM
    qscale = dk ** -0.5
    inv_norm = 1.0 / GLA_GATE_NORMALIZER

    @pl.when(i == 0)
    def _():
        sf_s[...] = sf0_ref[...]

    lower, upper = _tri(t)
    row = lax.broadcasted_iota(jnp.int32, (rows, rows), 0)
    col = lax.broadcasted_iota(jnp.int32, (rows, rows), 1)
    diff = row - col
    pos = row % t
    bd_lower = jnp.where(diff >= 0, jnp.where(diff <= pos, 1.0, 0.0), 0.0).astype(BF16)
    bd_upper = jnp.where(diff <= 0, jnp.where(-diff <= t - 1 - pos, 1.0, 0.0), 0.0).astype(BF16)

    hi, lo = _split(sm_ref[...])
    wg_hi, wg_lo = _split(wg_ref[...])
    logits = (jnp.dot(hi, wg_hi, preferred_element_type=F32) + jnp.dot(lo, wg_hi, preferred_element_type=F32)
              + jnp.dot(hi, wg_lo, preferred_element_type=F32)) + gb_ref[...]
    lg = _log_sigmoid(logits) * inv_norm
    cs = _dot_split_rhs(bd_lower, lg[:, 0:GLA_K])
    rc = _dot_split_rhs(bd_upper, lg[:, GLA_K:2 * GLA_K])

    def per_chunk_row(v, offset):
        return jnp.concatenate([jnp.broadcast_to(v[s * t + offset:s * t + offset + 1, :], (t, v.shape[1]))
                                for s in range(n_sub)], axis=0)

    for h in range(GLA_HEADS):
        ks = slice(h * dk, (h + 1) * dk)
        vs = slice(h * dv, (h + 1) * dv)
        c = cs[:, ks]
        r = rc[:, ks]
        qh = q_ref[:, ks] * qscale
        kh = k_ref[:, ks]
        qdf = (qh * jnp.exp(c)).astype(BF16)
        kif = (kh * jnp.exp(-c)).astype(BF16)
        qdb = (qh * jnp.exp(r)).astype(BF16)
        kib = (kh * jnp.exp(-r)).astype(BF16)
        kwf = (kh * jnp.exp(per_chunk_row(c, t - 1) - c)).astype(BF16)
        qdb_ref[:, ks] = qdb
        kwb_ref[:, ks] = (kh * jnp.exp(per_chunk_row(r, 0) - r)).astype(BF16)
        chunks = [slice(s * t, (s + 1) * t) for s in range(n_sub)]
        vhs = [v_ref[rs, vs].astype(BF16) for rs in chunks]
        atts = [(jnp.where(lower, _dot_nt(qdf[rs], kif[rs]), 0.0)
                 + jnp.where(upper, _dot_nt(qdb[rs], kib[rs]), 0.0)).astype(BF16) for rs in chunks]
        incs = [_dot_tn(vh, kwf[rs]) for vh, rs in zip(vhs, chunks)]
        states = [sf_s[h]]
        for s in range(n_sub):
            decb_ref[s, :, ks] = jnp.exp(r[s * t:s * t + 1, :])
            states.append(states[s] * jnp.exp(c[s * t + t - 1:s * t + t, :]) + incs[s])
        sf_s[h] = states[n_sub]
        for s, rs in enumerate(chunks):
            oa_ref[rs, vs] = (jnp.dot(atts[s], vhs[s], preferred_element_type=F32)
                              + _dot_nt(qdf[rs], states[s]))

    @pl.when(i == n_steps - 1)
    def _():
        sff_ref[...] = sf_s[...]


def _gla_bwd_kernel(qdb_ref, kwb_ref, decb_ref, v_ref, oa_ref, g_ref, gn_ref, sb0_ref,
                    os_ref, sbf_ref, sb_s):
    i = pl.program_id(1)
    n_steps = pl.num_programs(1)
    t = GLA_CHUNK
    n_sub = qdb_ref.shape[0] // t
    dk, dv = GLA_KEY_DIM, GLA_VAL_DIM

    @pl.when(i == 0)
    def _():
        sb_s[...] = sb0_ref[...]

    for h in range(GLA_HEADS):
        ks = slice(h * dk, (h + 1) * dk)
        vs = slice(h * dv, (h + 1) * dv)
        state = sb_s[h]
        for s in reversed(range(n_sub)):
            rs = slice(s * t, (s + 1) * t)
            o = oa_ref[rs, vs] + _dot_nt(qdb_ref[rs, ks], state)
            state = state * decb_ref[s, :, ks] + _dot_tn(v_ref[rs, vs], kwb_ref[rs, ks])
            os_ref[rs, vs] = (_head_norm(o, gn_ref[:, vs]) * _silu(g_ref[rs, vs])).astype(os_ref.dtype)
        sb_s[h] = state

    @pl.when(i == n_steps - 1)
    def _():
        sbf_ref[...] = sb_s[...]


GLA_STATE_SHAPE = (GLA_HEADS, GLA_VAL_DIM, GLA_KEY_DIM)


def _gla_fwd(q, k, v, small, wg, gb, sf0, rows):
    bsz, length, _ = q.shape
    rows = _row_tile(length, rows)
    n_steps = length // rows
    n_sub = rows // GLA_CHUNK
    fwd = lambda b, i: (b, i, 0)
    const2 = lambda b, i: (0, 0)
    state = lambda b, i: (b, 0, 0, 0)
    blk = lambda w: pl.BlockSpec((None, rows, w), fwd)
    state_spec = pl.BlockSpec((None,) + GLA_STATE_SHAPE, state)
    return pl.pallas_call(
        _gla_fwd_kernel,
        grid=(bsz, n_steps),
        in_specs=[blk(GLA_K), blk(GLA_K), blk(GLA_V), blk(SMALL_W),
                  pl.BlockSpec((SMALL_W, 2 * GLA_K), const2),
                  pl.BlockSpec((1, 2 * GLA_K), const2),
                  state_spec],
        out_specs=[blk(GLA_V), blk(GLA_K), blk(GLA_K),
                   pl.BlockSpec((None, n_sub, 1, GLA_K), lambda b, i: (b, i, 0, 0)),
                   state_spec],
        out_shape=[jax.ShapeDtypeStruct((bsz, length, GLA_V), F32),
                   jax.ShapeDtypeStruct((bsz, length, GLA_K), BF16),
                   jax.ShapeDtypeStruct((bsz, length, GLA_K), BF16),
                   jax.ShapeDtypeStruct((bsz, length // GLA_CHUNK, 1, GLA_K), F32),
                   jax.ShapeDtypeStruct((bsz,) + GLA_STATE_SHAPE, F32)],
        scratch_shapes=[pltpu.VMEM(GLA_STATE_SHAPE, F32)],
        compiler_params=_params("arbitrary", "arbitrary"),
        name="gla_fwd",
    )(q, k, v, small, wg, gb, sf0)


def _gla_bwd(qdb, kwb, decb, v, oa, g, gla_norm, sb0, rows):
    bsz, length, _ = oa.shape
    rows = _row_tile(length, rows)
    n_steps = length // rows
    n_sub = rows // GLA_CHUNK
    rev = lambda b, i: (b, n_steps - 1 - i, 0)
    const2 = lambda b, i: (0, 0)
    state = lambda b, i: (b, 0, 0, 0)
    blk = lambda w: pl.BlockSpec((None, rows, w), rev)
    state_spec = pl.BlockSpec((None,) + GLA_STATE_SHAPE, state)
    return pl.pallas_call(
        _gla_bwd_kernel,
        grid=(bsz, n_steps),
        in_specs=[blk(GLA_K), blk(GLA_K),
                  pl.BlockSpec((None, n_sub, 1, GLA_K), lambda b, i: (b, n_steps - 1 - i, 0, 0)),
                  blk(GLA_V), blk(GLA_V), blk(GLA_V),
                  pl.BlockSpec((1, GLA_V), const2),
                  state_spec],
        out_specs=[blk(GLA_V), state_spec],
        out_shape=[jax.ShapeDtypeStruct((bsz, length, GLA_V), BF16),
                   jax.ShapeDtypeStruct((bsz,) + GLA_STATE_SHAPE, F32)],
        scratch_shapes=[pltpu.VMEM(GLA_STATE_SHAPE, F32)],
        compiler_params=_params("arbitrary", "arbitrary"),
        name="gla_bwd",
    )(qdb, kwb, decb, v, oa, g, gla_norm.reshape(1, GLA_V), sb0)


def _out_proj_kernel(x_ref, gate_ref, *refs):
    *in_refs, w_ref, y_ref = refs
    acc = None
    off = 0
    for ref in in_refs:
        n = ref.shape[-1]
        part = jnp.dot(ref[...], w_ref[off:off + n, :], preferred_element_type=F32)
        acc = part if acc is None else acc + part
        off += n
    y_ref[...] = x_ref[...] + gate_ref[...] * acc


def _out_proj(x, gate, mixed, w_bf16, tm, name):
    bsz, length, d = x.shape
    tm = _row_tile(length, tm)
    row = lambda b, i: (b, i, 0)
    return pl.pallas_call(
        _out_proj_kernel,
        grid=(bsz, length // tm),
        in_specs=([pl.BlockSpec((None, tm, d), row),
                   pl.BlockSpec((None, 1, d), lambda b, i: (b, 0, 0))]
                  + [pl.BlockSpec((None, tm, m.shape[-1]), row) for m in mixed]
                  + [pl.BlockSpec(w_bf16.shape, lambda b, i: (0, 0), pipeline_mode=pl.Buffered(1))]),
        out_specs=pl.BlockSpec((None, tm, d), row),
        out_shape=jax.ShapeDtypeStruct((bsz, length, d), F32),
        compiler_params=_params("parallel", "parallel"),
        name=name,
    )(x, gate, *mixed, w_bf16)


def _attn_kernel(sink_ref, q_ref, k_ref, v_ref, kc_ref, vc_ref, g_ref, o_ref):
    j = pl.program_id(1)
    i = pl.program_id(2)
    length = k_ref.shape[0]
    blk = ATT_BLOCK
    band = 3 * blk
    dh = ATT_HEAD_DIM
    n_q = q_ref.shape[0] // blk
    n_ctx = kc_ref.shape[0]
    row_minus_col = (lax.broadcasted_iota(jnp.int32, (blk, band), 0)
                     - lax.broadcasted_iota(jnp.int32, (blk, band), 1))
    sink = jnp.concatenate([jnp.full((blk, LANES), sink_ref[j * ATT_GROUP + g] * LOG2E, F32)
                            for g in range(ATT_GROUP)], axis=0)
    kc = kc_ref[...]
    vc_ext = jnp.concatenate([vc_ref[...], jnp.ones((n_ctx, dh), BF16)], axis=1)
    ones_band = jnp.ones((band, dh), BF16)

    def window_start(qb):
        blk_idx = i * n_q + qb
        return blk_idx, pl.multiple_of(jnp.clip((blk_idx - 1) * blk, 0, length - band), blk)

    def scores(qb):
        _, start = window_start(qb)
        rs = slice(qb * blk, (qb + 1) * blk)
        q = jnp.concatenate([q_ref[rs, g * dh:(g + 1) * dh] for g in range(ATT_GROUP)], axis=0)
        return _dot_nt(q, k_ref[pl.ds(start, band), :]), _dot_nt(q, kc)

    pending = scores(0)
    for qb in range(n_q):
        s_band, s_ctx = pending
        if qb + 1 < n_q:
            pending = scores(qb + 1)
        blk_idx, start = window_start(qb)
        inside = jnp.abs(row_minus_col + (blk_idx * blk - start)) <= WINDOW
        v_all = jnp.concatenate([jnp.concatenate([v_ref[pl.ds(start, band), :], ones_band], axis=1),
                                 vc_ext], axis=0)
        rs = slice(qb * blk, (qb + 1) * blk)
        cols = []
        for c in range(band // LANES):
            ls = slice(c * LANES, (c + 1) * LANES)
            cols.append(jnp.concatenate(
                [jnp.where(inside[:, ls], s_band[g * blk:(g + 1) * blk, ls], float("-inf"))
                 for g in range(ATT_GROUP)], axis=0))
        for c in range(n_ctx // LANES):
            cols.append(s_ctx[:, c * LANES:(c + 1) * LANES])
        m = jnp.maximum(sink, jnp.max(functools.reduce(jnp.maximum, cols), axis=-1, keepdims=True))
        p = jnp.concatenate([jnp.exp2(col - m) for col in cols], axis=1).astype(BF16)
        acc = jnp.dot(p, v_all, preferred_element_type=F32)
        out = acc[:, 0:dh] / (acc[:, dh:2 * dh] + jnp.exp2(sink - m))
        for g in range(ATT_GROUP):
            cs = slice(g * dh, (g + 1) * dh)
            o_ref[rs, cs] = (out[g * blk:(g + 1) * blk, :] * g_ref[rs, cs]).astype(o_ref.dtype)


def _attention(sink, q, k, v, kc, vc, gate):
    bsz, length, _ = q.shape
    n_ctx = kc.shape[1]
    rows = _row_tile(length, ATT_Q_BLOCKS * ATT_BLOCK)
    gw = ATT_GROUP * ATT_HEAD_DIM
    qmap = lambda b, j, i, s: (b, i, j)
    kvmap = lambda b, j, i, s: (b, 0, j)
    grid_spec = pltpu.PrefetchScalarGridSpec(
        num_scalar_prefetch=1,
        grid=(bsz, ATT_KV_HEADS, length // rows),
        in_specs=[pl.BlockSpec((None, rows, gw), qmap),
                  pl.BlockSpec((None, length, ATT_HEAD_DIM), kvmap),
                  pl.BlockSpec((None, length, ATT_HEAD_DIM), kvmap),
                  pl.BlockSpec((None, n_ctx, ATT_HEAD_DIM), kvmap),
                  pl.BlockSpec((None, n_ctx, ATT_HEAD_DIM), kvmap),
                  pl.BlockSpec((None, rows, gw), qmap)],
        out_specs=pl.BlockSpec((None, rows, gw), qmap),
    )
    return pl.pallas_call(
        _attn_kernel,
        grid_spec=grid_spec,
        out_shape=jax.ShapeDtypeStruct((bsz, length, ATT_W), BF16),
        compiler_params=_params("parallel", "parallel", "arbitrary"),
        name="attention",
    )(sink, q, k, v, kc, vc, gate)


EVEN_WIDTHS = (SSD_WIDTH, CONV_CH, GLA_K, GLA_K, GLA_V, GLA_V, SMALL_W)


def _even_weight_layout(w_in):
    d = w_in.shape[0]
    sizes = (SSD_WIDTH, CONV_CH, 2 * SSD_HEADS, GLA_K, GLA_K, GLA_V, GLA_V, 2 * GLA_RANK)
    z, xbc, dt, q, k, v, g, lr = jnp.split(w_in.astype(BF16), np.cumsum(sizes)[:-1].tolist(), axis=1)
    pad = jnp.zeros((d, SMALL_W - 2 * SSD_HEADS - 2 * GLA_RANK), BF16)
    return jnp.concatenate([z, xbc, q, k, v, g, dt, lr, pad], axis=1)


def _expansion_matrices():
    rows = np.arange(SMALL_W)[:, None]
    heads = (np.arange(SSD_WIDTH) // SSD_HEAD_DIM)[None, :]
    ef = (rows == heads).astype(np.float32)
    eb = (rows - SSD_HEADS == heads).astype(np.float32)
    return jnp.asarray(ef, BF16), jnp.asarray(eb, BF16)


def _pad_lanes(v, width):
    return jnp.pad(v, ((0, 0), (0, width - v.shape[1])))


def _rope_tables(length):
    rows = length // GRID_W
    row = np.repeat(np.arange(rows, dtype=np.float64), GRID_W)
    col = np.tile(np.arange(GRID_W, dtype=np.float64), rows)
    inv = 1.0 / (ROPE_BASE ** (np.arange(ROPE_FREQS, dtype=np.float64) / ROPE_FREQS))
    ang_r = row[:, None] * inv
    ang_c = col[:, None] * inv
    cos = np.concatenate([np.cos(ang_r), np.cos(ang_c), np.cos(ang_r), np.cos(ang_c)], axis=1)
    sin = np.concatenate([-np.sin(ang_r), -np.sin(ang_c), np.sin(ang_r), np.sin(ang_c)], axis=1)
    return jnp.asarray(cos, F32), jnp.asarray(sin, F32)


def _rope_head_layout(v, n_heads):
    lead = v.shape[:-1]
    v = v.reshape(lead + (n_heads, 2, 2, ROPE_FREQS))
    return jnp.swapaxes(v, -3, -2).reshape(lead + (n_heads * ATT_HEAD_DIM,))


def _mod_rows(mod, rows, bsz, d):
    picked = jnp.broadcast_to(mod[rows], (bsz, 3 * d)) if isinstance(rows, int) else mod[rows]
    return [picked[:, None, j * d:(j + 1) * d] for j in range(3)]


PROJ_ROWS = 512
OUT_ROWS = 512
ATT_Q_BLOCKS = 8
GLA_ROWS = 256
GLA_BWD_ROWS = 512
SSD_BWD_ROWS = 512
CVEC_ROWS = SUBLANES


def kernel(x, c, ctx, c_ctx, e_norm, e_mod_w, e_mod_b, e_w_in, e_conv_w, e_conv_b, e_dt_bias, e_a_log,
           e_d_skip, e_ssd_norm, e_gla_gate_w, e_gla_gate_b, e_gla_norm, e_w_out, o_norm, o_mod_w, o_mod_b,
           o_w_in, o_q_norm, o_k_norm, o_sink, o_w_out):
    bsz, length, d = x.shape
    n_ctx = ctx.shape[1]
    assert e_norm.shape[0] == 1 and o_norm.shape[0] == 1, "two-layer block only"
    assert length % SSD_CHUNK == 0 and n_ctx % SSD_CHUNK == 0 and length >= 3 * ATT_BLOCK
    assert bsz + 1 <= CVEC_ROWS

    cvecs = jnp.zeros((CVEC_ROWS, d), F32).at[:bsz].set(c).at[bsz].set(c_ctx)
    lat_rows = slice(0, bsz)

    mod = _adaln(cvecs, e_mod_w[0], e_mod_b[0])
    shift, scale, gate = _mod_rows(mod, lat_rows, bsz, d)
    c_shift, c_scale, c_gate = _mod_rows(mod, bsz, bsz, d)
    w_in = _even_weight_layout(e_w_in[0])
    w_out = e_w_out[0].astype(BF16)
    ef, eb = _expansion_matrices()
    nh2 = 2 * SSD_HEADS
    bias_flat = e_dt_bias[0].reshape(1, nh2)
    alog_flat = e_a_log[0].reshape(1, nh2)
    bias_row, alog_row = _pad_lanes(bias_flat, SMALL_W), _pad_lanes(alog_flat, SMALL_W)
    bias_col, alog_col = bias_flat.reshape(nh2, 1), alog_flat.reshape(nh2, 1)
    dskip_row = jnp.repeat(e_d_skip[0], SSD_HEAD_DIM).reshape(1, SSD_WIDTH)
    conv_b = e_conv_b[0].reshape(1, CONV_CH)
    wg = jnp.zeros((SMALL_W, 2 * GLA_K), F32)
    wg = wg.at[nh2:nh2 + GLA_RANK, 0:GLA_K].set(e_gla_gate_w[0, 0])
    wg = wg.at[nh2 + GLA_RANK:nh2 + 2 * GLA_RANK, GLA_K:2 * GLA_K].set(e_gla_gate_w[0, 1])
    gb = e_gla_gate_b[0].reshape(1, 2 * GLA_K)

    def mixers(stream, sc, sh, ssd_init, gla_init):
        z, xbc, q, k, v, g, small = _proj_even(stream, e_norm[0], sc, sh, w_in, EVEN_WIDTHS, PROJ_ROWS)
        ya, cm, bt, wxb, p3, ssd_f = _ssd_fwd(xbc, small, e_conv_w[0], conv_b, bias_row, bias_col,
                                              alog_row, alog_col, dskip_row, ef, eb, ssd_init[0])
        ys, ssd_b = _ssd_bwd(cm, bt, wxb, p3, ya, z, eb, e_ssd_norm[0], ssd_init[1])
        oa, qdb, kwb, decb, gla_f = _gla_fwd(q, k, v, small, wg, gb, gla_init[0], GLA_ROWS)
        os_, gla_b = _gla_bwd(qdb, kwb, decb, v, oa, g, e_gla_norm[0], gla_init[1], GLA_BWD_ROWS)
        return (ys, os_), (ssd_f, ssd_b), (gla_f, gla_b)

    ssd0 = jnp.zeros((bsz,) + SSD_STATE_SHAPE, F32)
    gla0 = jnp.zeros((bsz,) + GLA_STATE_SHAPE, F32)
    ctx_mix, ssd_fin, gla_fin = mixers(ctx, c_scale, c_shift, (ssd0, ssd0), (gla0, gla0))
    lat_mix, _, _ = mixers(x, scale, shift, ssd_fin, gla_fin)
    x = _out_proj(x, gate, lat_mix, w_out, OUT_ROWS, "out_even")
    xc = _out_proj(ctx, c_gate, ctx_mix, w_out, OUT_ROWS, "out_even")

    mod = _adaln(cvecs, o_mod_w[0], o_mod_b[0])
    shift, scale, gate = _mod_rows(mod, lat_rows, bsz, d)
    c_shift, c_scale, _ = _mod_rows(mod, bsz, bsz, d)
    w_k, w_v, w_q, w_g = jnp.split(o_w_in[0].astype(BF16), [ATT_KV_W, 2 * ATT_KV_W, 2 * ATT_KV_W + ATT_W], axis=1)
    w_in = jnp.concatenate([_rope_head_layout(w_k, ATT_KV_HEADS), w_v, _rope_head_layout(w_q, ATT_HEADS), w_g],
                           axis=1)
    q_norm = _rope_head_layout(o_q_norm[0], 1)
    k_norm = _rope_head_layout(o_k_norm[0], 1)
    cos, sin = _rope_tables(length)
    no_rot = (jnp.ones((n_ctx, ATT_HEAD_DIM), F32), jnp.zeros((n_ctx, ATT_HEAD_DIM), F32))
    kc, vc = _proj_odd(xc, o_norm[0], c_scale, c_shift, w_in[:, :2 * ATT_KV_W], *no_rot,
                       q_norm, k_norm, False, PROJ_ROWS)
    k, v, q, g = _proj_odd(x, o_norm[0], scale, shift, w_in, cos, sin, q_norm, k_norm, True, PROJ_ROWS)
    o = _attention(o_sink[0].astype(F32), q, k, v, kc, vc, g)
    return _out_proj(x, gate, (o,), o_w_out[0].astype(BF16), OUT_ROWS, "out_odd")
```

```python
import functools

import jax
import jax.numpy as jnp
import numpy as np
from jax import lax
from jax.experimental import pallas as pl
from jax.experimental.pallas import tpu as pltpu

F32 = jnp.float32
BF16 = jnp.bfloat16

GRID_W = 64
SSD_HEADS = 16
SSD_HEAD_DIM = 64
SSD_WIDTH = SSD_HEADS * SSD_HEAD_DIM
SSD_GROUPS = 2
SSD_STATE = 128
SSD_CHUNK = 128
CONV_K = 5
CONV_CH = SSD_WIDTH + 2 * SSD_GROUPS * SSD_STATE
GLA_HEADS = 4
GLA_KEY_DIM = 128
GLA_VAL_DIM = 256
GLA_K = GLA_HEADS * GLA_KEY_DIM
GLA_V = GLA_HEADS * GLA_VAL_DIM
GLA_RANK = 16
GLA_GATE_NORMALIZER = 16.0
GLA_CHUNK = 64
ATT_HEADS = 16
ATT_KV_HEADS = 4
ATT_GROUP = ATT_HEADS // ATT_KV_HEADS
ATT_HEAD_DIM = 128
ATT_W = ATT_HEADS * ATT_HEAD_DIM
ATT_KV_W = ATT_KV_HEADS * ATT_HEAD_DIM
WINDOW = 128
ATT_BLOCK = 128
ROPE_BASE = 10000.0
ROPE_FREQS = ATT_HEAD_DIM // 4
NORM_EPS = 1e-6
LOG2E = 1.4426950408889634

LANES = 128
SUBLANES = 8
VMEM_LIMIT_BYTES = 56 * 1024 * 1024

SSD_GROUP_W = SSD_WIDTH // SSD_GROUPS
SSD_BC_W = SSD_GROUPS * SSD_STATE
SMALL_W = LANES
HALO = 2 * SUBLANES
HEADS_PER_DOT = 4
CONV_ROW_STRIDE = 2 * SUBLANES + 1
CONV_OUT_ROWS = CONV_ROW_STRIDE * SUBLANES
assert SSD_CHUNK <= CONV_OUT_ROWS <= SSD_CHUNK + HALO - (CONV_K - 1) // 2


def _dot(a, b):
    return jnp.dot(a.astype(BF16), b.astype(BF16), preferred_element_type=F32)


def _dot_nt(a, b):
    return lax.dot_general(a.astype(BF16), b.astype(BF16), (((1,), (1,)), ((), ())),
                           preferred_element_type=F32)


def _dot_tn(a, b):
    return lax.dot_general(a.astype(BF16), b.astype(BF16), (((0,), (0,)), ((), ())),
                           preferred_element_type=F32)


def _split(v):
    hi = v.astype(BF16)
    lo = (v - hi.astype(F32)).astype(BF16)
    return hi, lo


def _dot_split_lhs(v, m):
    hi, lo = _split(v)
    return (jnp.dot(hi, m, preferred_element_type=F32) + jnp.dot(lo, m, preferred_element_type=F32))


def _dot_split_rhs(m, v):
    hi, lo = _split(v)
    return (jnp.dot(m, hi, preferred_element_type=F32) + jnp.dot(m, lo, preferred_element_type=F32))


def _dot3(a, b):
    ah, al = _split(a)
    bh, bl = _split(b)
    return (jnp.dot(ah, bh, preferred_element_type=F32) + jnp.dot(al, bh, preferred_element_type=F32)
            + jnp.dot(ah, bl, preferred_element_type=F32))


def _silu(v):
    return v * jax.nn.sigmoid(v)


def _softplus(v):
    return jnp.maximum(v, 0.0) + jnp.log(1.0 + jnp.exp(-jnp.abs(v)))


def _log_sigmoid(v):
    return jnp.minimum(v, 0.0) - jnp.log(1.0 + jnp.exp(-jnp.abs(v)))


def _tri(n):
    row = lax.broadcasted_iota(jnp.int32, (n, n), 0)
    col = lax.broadcasted_iota(jnp.int32, (n, n), 1)
    return row >= col, col >= row


def _params(*sem):
    return pltpu.CompilerParams(dimension_semantics=sem, vmem_limit_bytes=VMEM_LIMIT_BYTES)


def _adaln_kernel(c_ref, w_ref, b_ref, o_ref):
    o_ref[...] = _dot3(_silu(c_ref[...]), w_ref[...]) + b_ref[...]


def _adaln(cvecs, w, b):
    rows, d = cvecs.shape
    n = w.shape[1]
    tn = 1024
    return pl.pallas_call(
        _adaln_kernel,
        grid=(n // tn,),
        in_specs=[pl.BlockSpec((rows, d), lambda j: (0, 0)),
                  pl.BlockSpec((d, tn), lambda j: (0, j)),
                  pl.BlockSpec((1, tn), lambda j: (0, j))],
        out_specs=pl.BlockSpec((rows, tn), lambda j: (0, j)),
        out_shape=jax.ShapeDtypeStruct((rows, n), F32),
        compiler_params=_params("parallel"),
        name="adaln",
    )(cvecs, w, b.reshape(1, n))


def _modulated_norm(x, g, sc, sh):
    r = lax.rsqrt(jnp.mean(x * x, axis=-1, keepdims=True) + NORM_EPS)
    return ((x * r) * g) * (1.0 + sc) + sh


def _store_cols(h, w_ref, off, ref, act=None):
    n = ref.shape[-1]
    for c0 in range(0, n, 512):
        c1 = min(n, c0 + 512)
        t = jnp.dot(h, w_ref[:, off + c0:off + c1], preferred_element_type=F32)
        ref[:, c0:c1] = (t if act is None else act(t)).astype(ref.dtype)
    return off + n


def _proj_even_kernel(x_ref, g_ref, sc_ref, sh_ref, w_ref, *out_refs):
    h = _modulated_norm(x_ref[...], g_ref[...], sc_ref[...], sh_ref[...]).astype(BF16)
    off = 0
    for ref in out_refs:
        off = _store_cols(h, w_ref, off, ref)


def _head_norm(t, gain):
    r = lax.rsqrt(jnp.mean(t * t, axis=-1, keepdims=True) + NORM_EPS)
    return (t * r) * gain


def _rope(t, cos, sin_signed):
    return t * cos + pltpu.roll(t, ATT_HEAD_DIM // 2, 1) * sin_signed


def _proj_odd_kernel(x_ref, g_ref, sc_ref, sh_ref, w_ref, cos_ref, sin_ref, qn_ref, kn_ref,
                     k_ref, v_ref, *qg_refs):
    h = _modulated_norm(x_ref[...], g_ref[...], sc_ref[...], sh_ref[...]).astype(BF16)
    cos = cos_ref[...]
    sin = sin_ref[...]
    scale = ATT_HEAD_DIM ** -0.5 * LOG2E

    width = HEADS_PER_DOT * ATT_HEAD_DIM
    dh = ATT_HEAD_DIM
    pr = lax.broadcasted_iota(jnp.int32, (2 * dh, 2 * dh), 0) // dh
    pc = lax.broadcasted_iota(jnp.int32, (2 * dh, 2 * dh), 1) // dh
    head_ones = jnp.where(pr == pc, 1.0, 0.0).astype(BF16)

    def project(col):
        return jnp.dot(h, w_ref[:, col:col + width], preferred_element_type=F32)

    def finish_heads(t4, ref, j0, gain, out_scale):
        for j in range(0, HEADS_PER_DOT, 2):
            t2 = t4[:, j * dh:(j + 2) * dh]
            mean_sq = jnp.dot((t2 * t2).astype(BF16), head_ones, preferred_element_type=F32) * (1.0 / dh)
            t2 = t2 * lax.rsqrt(mean_sq + NORM_EPS)
            for jj in range(2):
                t = _rope(t2[:, jj * dh:(jj + 1) * dh] * gain, cos, sin)
                c0 = (j0 + j + jj) * dh
                ref[:, c0:c0 + dh] = (t if out_scale is None else t * out_scale).astype(ref.dtype)

    work = []
    if qg_refs:
        q_ref, gate_ref = qg_refs
        for j0 in range(0, ATT_HEADS, HEADS_PER_DOT):
            c0 = j0 * dh

            def finish_gate(t4, c0=c0):
                gate_ref[:, c0:c0 + width] = _silu(t4)

            work.append((2 * ATT_KV_W + ATT_W + c0, finish_gate))
            work.append((2 * ATT_KV_W + c0,
                         functools.partial(finish_heads, ref=q_ref, j0=j0, gain=qn_ref[...], out_scale=scale)))
    for j0 in range(0, ATT_KV_HEADS, HEADS_PER_DOT):
        work.append((j0 * dh, functools.partial(finish_heads, ref=k_ref, j0=j0, gain=kn_ref[...], out_scale=None)))
    for col, finish in work:
        finish(project(col))
    _store_cols(h, w_ref, ATT_KV_W, v_ref)


def _row_tile(length, want):
    return min(length, want)


def _proj_even(x, norm_g, scale, shift, w_bf16, widths, tm):
    bsz, length, d = x.shape
    tm = _row_tile(length, tm)
    n = w_bf16.shape[1]
    row = lambda b, i: (b, i, 0)
    mod = lambda b, i: (b, 0, 0)
    return pl.pallas_call(
        _proj_even_kernel,
        grid=(bsz, length // tm),
        in_specs=[pl.BlockSpec((None, tm, d), row),
                  pl.BlockSpec((1, d), lambda b, i: (0, 0)),
                  pl.BlockSpec((None, 1, d), mod),
                  pl.BlockSpec((None, 1, d), mod),
                  pl.BlockSpec((d, n), lambda b, i: (0, 0), pipeline_mode=pl.Buffered(1))],
        out_specs=[pl.BlockSpec((None, tm, wd), row) for wd, _ in widths],
        out_shape=[jax.ShapeDtypeStruct((bsz, length, wd), dt) for wd, dt in widths],
        compiler_params=_params("parallel", "parallel"),
        name="proj_even",
    )(x, norm_g.reshape(1, d), scale, shift, w_bf16)


def _proj_odd(x, norm_g, scale, shift, w_bf16, cos, sin, q_norm, k_norm, with_queries, tm):
    bsz, length, d = x.shape
    tm = _row_tile(length, tm)
    n = w_bf16.shape[1]
    row = lambda b, i: (b, i, 0)
    mod = lambda b, i: (b, 0, 0)
    const = lambda b, i: (0, 0)
    widths = [(ATT_KV_W, BF16), (ATT_KV_W, BF16)]
    if with_queries:
        widths += [(ATT_W, BF16), (ATT_W, F32)]
    return pl.pallas_call(
        _proj_odd_kernel,
        grid=(bsz, length // tm),
        in_specs=[pl.BlockSpec((None, tm, d), row),
                  pl.BlockSpec((1, d), const),
                  pl.BlockSpec((None, 1, d), mod),
                  pl.BlockSpec((None, 1, d), mod),
                  pl.BlockSpec((d, n), const, pipeline_mode=pl.Buffered(1)),
                  pl.BlockSpec((tm, ATT_HEAD_DIM), lambda b, i: (i, 0)),
                  pl.BlockSpec((tm, ATT_HEAD_DIM), lambda b, i: (i, 0)),
                  pl.BlockSpec((1, ATT_HEAD_DIM), const),
                  pl.BlockSpec((1, ATT_HEAD_DIM), const)],
        out_specs=[pl.BlockSpec((None, tm, wd), row) for wd, _ in widths],
        out_shape=[jax.ShapeDtypeStruct((bsz, length, wd), dt) for wd, dt in widths],
        compiler_params=_params("parallel", "parallel"),
        name="proj_odd_q" if with_queries else "proj_odd_kv",
    )(x, norm_g.reshape(1, d), scale, shift, w_bf16, cos, sin,
      q_norm.reshape(1, ATT_HEAD_DIM), k_norm.reshape(1, ATT_HEAD_DIM))


def _conv_silu(win_s, prev_ref, cur_ref, next_ref, has_prev, has_next, cw_ref, cb_ref, x_s):
    rows = cur_ref.shape[0]
    half = (CONV_K - 1) // 2
    for j in range(CONV_CH // LANES):
        ls = slice(j * LANES, (j + 1) * LANES)
        win_s[j, 0:HALO, :] = jnp.where(has_prev, prev_ref[:, ls].astype(F32), 0.0)
        win_s[j, HALO:HALO + rows, :] = cur_ref[:, ls].astype(F32)
        win_s[j, HALO + rows:2 * HALO + rows, :] = jnp.where(has_next, next_ref[:, ls].astype(F32), 0.0)
        taps = [cw_ref[k:k + 1, ls] for k in range(CONV_K)]
        bias = cb_ref[:, ls]
        win = win_s.at[j]
        out = x_s.at[j]
        for c in range(rows // SSD_CHUNK):
            for a in range(CONV_ROW_STRIDE):
                r0 = HALO + c * SSD_CHUNK - half + a
                acc = bias + taps[0] * win[pl.ds(r0, SUBLANES, stride=CONV_ROW_STRIDE), :]
                for k in range(1, CONV_K):
                    acc = acc + taps[k] * win[pl.ds(r0 + k, SUBLANES, stride=CONV_ROW_STRIDE), :]
                hv = 0.5 * acc
                out[pl.ds(c * CONV_OUT_ROWS + a, SUBLANES, stride=CONV_ROW_STRIDE), :] = hv + hv * jnp.tanh(hv)


def _ssd_fwd_kernel(p_ref, c_ref, n_ref, sm_ref, cw_ref, cb_ref, bias_row_ref, bias_col_ref,
                    alog_row_ref, alog_col_ref, dskip_ref, ef_ref, eb_ref, sf0_ref,
                    ya_ref, cm_ref, bt_ref, wxb_ref, p3_ref, sff_ref,
                    sf_s, win_s, x_s):
    i = pl.program_id(1)
    nc = pl.num_programs(1)
    t = SSD_CHUNK
    nh = SSD_HEADS
    slabs_x = SSD_WIDTH // LANES
    slab_b = slabs_x
    slab_c = slabs_x + SSD_GROUPS

    @pl.when(i == 0)
    def _():
        sf_s[...] = sf0_ref[...]

    lower, upper = _tri(t)
    ltri = jnp.where(lower, 1.0, 0.0).astype(BF16)
    utri = jnp.where(upper, 1.0, 0.0).astype(BF16)
    a_row = -jnp.exp(alog_row_ref[...])
    a_col = -jnp.exp(alog_col_ref[...])
    ef = ef_ref[...]
    eb = eb_ref[...]
    neg_inf = float("-inf")

    _conv_silu(win_s, p_ref, c_ref, n_ref, i > 0, i < nc - 1, cw_ref, cb_ref, x_s)
    n_chunks = sm_ref.shape[0] // t
    lane = lax.broadcasted_iota(jnp.int32, (t, LANES), 1)
    left = lane < SSD_HEAD_DIM
    heads_per_group = nh // SSD_GROUPS
    pairs_per_group = heads_per_group // 2

    def chunk_decays(c):
        rs = slice(c * t, (c + 1) * t)
        sm = sm_ref[rs, :]
        dt = _softplus(sm + bias_row_ref[...])
        dta = dt * a_row
        cs = _dot_split_rhs(ltri, dta)
        rc = _dot_split_rhs(utri, dta)
        dt_t = _softplus(sm.T[0:2 * nh, :] + bias_col_ref[...])
        dta_t = dt_t * a_col
        log2_dt_t = jnp.log2(dt_t)
        f_q = cs * LOG2E
        g_q = rc * LOG2E
        f_k = _dot_split_lhs(dta_t, utri) * LOG2E - log2_dt_t
        g_k = _dot_split_lhs(dta_t, ltri) * LOG2E - log2_dt_t
        dec_f = _dot_split_lhs(jnp.exp2(f_q), ef)
        wgt_f = _dot_split_lhs(dt * jnp.exp(cs[t - 1:t, :] - cs), ef)
        wgt_b = _dot_split_lhs(dt * jnp.exp(rc[0:1, :] - rc), eb)
        p3_ref[rs, :] = jnp.exp2(g_q)
        for p in range(slabs_x):
            ls = slice(p * LANES, (p + 1) * LANES)
            wxb_ref[rs, ls] = (x_s[p, c * CONV_OUT_ROWS:c * CONV_OUT_ROWS + t, :] * wgt_b[:, ls]).astype(BF16)
        return f_q, g_q, f_k, g_k, dec_f, wgt_f

    decays = [chunk_decays(c) for c in range(n_chunks)]

    for c in range(n_chunks):
        f_q, g_q, f_k, g_k, dec_f, wgt_f = decays[c]
        rs = slice(c * t, (c + 1) * t)
        xr = slice(c * CONV_OUT_ROWS, c * CONV_OUT_ROWS + t)

        def decay_matrix(h, cbg):
            lf = jnp.exp2(jnp.where(lower, f_q[:, h:h + 1] - f_k[h:h + 1, :], neg_inf))
            ub = jnp.exp2(jnp.where(upper, g_q[:, nh + h:nh + h + 1] - g_k[nh + h:nh + h + 1, :], neg_inf))
            return cbg * (lf + ub)

        for g in range(SSD_GROUPS):
            cg = x_s[slab_c + g, xr, :]
            bg = x_s[slab_b + g, xr, :]
            gs = slice(g * SSD_GROUP_W, (g + 1) * SSD_GROUP_W)
            cm_ref[rs, g * SSD_STATE:(g + 1) * SSD_STATE] = cg.astype(BF16)
            cbg = _dot_nt(cg, bg)
            state = sf_s[g]
            y_off = _dot(cg, state) * dec_f[:, gs]
            wx = []
            for pair in range(pairs_per_group):
                h0 = g * heads_per_group + 2 * pair
                p = h0 // 2
                ls = slice(p * LANES, (p + 1) * LANES)
                xp = x_s[p, xr, :]
                y = (_dot(decay_matrix(h0, cbg), jnp.where(left, xp, 0.0))
                     + _dot(decay_matrix(h0 + 1, cbg), jnp.where(left, 0.0, xp)))
                ya_ref[rs, ls] = y + y_off[:, pair * LANES:(pair + 1) * LANES] + dskip_ref[:, ls] * xp
                wx.append((xp * wgt_f[:, ls]).astype(BF16))
            bt = bg.T.astype(BF16)
            bt_ref[c, g * SSD_STATE:(g + 1) * SSD_STATE, :] = bt
            sf_s[g] = dec_f[t - 1:t, gs] * state + jnp.dot(bt, jnp.concatenate(wx, axis=1),
                                                           preferred_element_type=F32)

    @pl.when(i == nc - 1)
    def _():
        sff_ref[...] = sf_s[...]


def _ssd_bwd_kernel(cm_ref, bt_ref, wxb_ref, p3_ref, ya_ref, z_ref, eb_ref, sn_ref, sb0_ref,
                    ys_ref, sbf_ref, sb_s):
    i = pl.program_id(1)
    nc = pl.num_programs(1)

    @pl.when(i == 0)
    def _():
        sb_s[...] = sb0_ref[...]

    t = SSD_CHUNK
    n_sub = bt_ref.shape[0]
    dec_b = _dot_split_lhs(p3_ref[...], eb_ref[...])
    for g in range(SSD_GROUPS):
        gs = slice(g * SSD_GROUP_W, (g + 1) * SSD_GROUP_W)
        ss = slice(g * SSD_STATE, (g + 1) * SSD_STATE)
        chunks = [slice(s * t, (s + 1) * t) for s in range(n_sub)]
        incs = [jnp.dot(bt_ref[s, ss, :], wxb_ref[rs, gs], preferred_element_type=F32)
                for s, rs in enumerate(chunks)]
        states = [None] * n_sub
        state = sb_s[g]
        for s in reversed(range(n_sub)):
            states[s] = state
            state = dec_b[s * t:s * t + 1, gs] * state + incs[s]
        sb_s[g] = state
        for s, rs in enumerate(chunks):
            y_off = jnp.dot(cm_ref[rs, ss], states[s].astype(BF16), preferred_element_type=F32) * dec_b[rs, gs]
            y = (ya_ref[rs, gs] + y_off) * _silu(z_ref[rs, gs].astype(F32))
            ys_ref[rs, gs] = _head_norm(y, sn_ref[:, gs]).astype(ys_ref.dtype)

    @pl.when(i == nc - 1)
    def _():
        sbf_ref[...] = sb_s[...]


SSD_STATE_SHAPE = (SSD_GROUPS, SSD_STATE, SSD_GROUP_W)


def _ssd_fwd(xbc, small, conv_w, conv_b, bias_row, bias_col, alog_row, alog_col, dskip_row, ef, eb, sf0):
    bsz, length, _ = xbc.shape
    t = SSD_CHUNK
    rows = _row_tile(length, SSD_FWD_ROWS)
    nc = length // t
    per = rows // HALO
    last_halo = length // HALO - 1

    def cur(b, i): return (b, i, 0)
    def prev(b, i): return (b, jnp.maximum(i * per - 1, 0), 0)
    def nxt(b, i): return (b, jnp.minimum((i + 1) * per, last_halo), 0)
    const2 = lambda b, i: (0, 0)
    state = lambda b, i: (b, 0, 0, 0)

    halo_spec = lambda f: pl.BlockSpec((None, HALO, CONV_CH), f)
    chunk_spec = lambda w: pl.BlockSpec((None, rows, w), cur)
    state_spec = pl.BlockSpec((None,) + SSD_STATE_SHAPE, state)
    return pl.pallas_call(
        _ssd_fwd_kernel,
        grid=(bsz, length // rows),
        in_specs=[halo_spec(prev), chunk_spec(CONV_CH), halo_spec(nxt), chunk_spec(SMALL_W),
                  pl.BlockSpec((CONV_K, CONV_CH), const2),
                  pl.BlockSpec((1, CONV_CH), const2),
                  pl.BlockSpec((1, SMALL_W), const2),
                  pl.BlockSpec((2 * SSD_HEADS, 1), const2),
                  pl.BlockSpec((1, SMALL_W), const2),
                  pl.BlockSpec((2 * SSD_HEADS, 1), const2),
                  pl.BlockSpec((1, SSD_WIDTH), const2),
                  pl.BlockSpec((SMALL_W, SSD_WIDTH), const2),
                  pl.BlockSpec((SMALL_W, SSD_WIDTH), const2),
                  state_spec],
        out_specs=[chunk_spec(SSD_WIDTH), chunk_spec(SSD_BC_W),
                   pl.BlockSpec((None, rows // t, SSD_BC_W, t), lambda b, i: (b, i, 0, 0)),
                   chunk_spec(SSD_WIDTH), chunk_spec(SMALL_W), state_spec],
        out_shape=[jax.ShapeDtypeStruct((bsz, length, SSD_WIDTH), F32),
                   jax.ShapeDtypeStruct((bsz, length, SSD_BC_W), BF16),
                   jax.ShapeDtypeStruct((bsz, nc, SSD_BC_W, t), BF16),
                   jax.ShapeDtypeStruct((bsz, length, SSD_WIDTH), BF16),
                   jax.ShapeDtypeStruct((bsz, length, SMALL_W), F32),
                   jax.ShapeDtypeStruct((bsz,) + SSD_STATE_SHAPE, F32)],
        scratch_shapes=[pltpu.VMEM(SSD_STATE_SHAPE, F32),
                        pltpu.VMEM((CONV_CH // LANES, rows + 2 * HALO, LANES), F32),
                        pltpu.VMEM((CONV_CH // LANES, (rows // t) * CONV_OUT_ROWS, LANES), F32)],
        compiler_params=_params("arbitrary", "arbitrary"),
        name="ssd_fwd",
    )(xbc, xbc, xbc, small, conv_w, conv_b, bias_row, bias_col, alog_row, alog_col, dskip_row, ef, eb, sf0)


def _ssd_bwd(cm, bt, wxb, p3, ya, z, eb, ssd_norm, sb0):
    bsz, length, _ = ya.shape
    t = SSD_CHUNK
    rows = _row_tile(length, SSD_BWD_ROWS)
    n_steps = length // rows
    rev = lambda b, i: (b, n_steps - 1 - i, 0)
    const2 = lambda b, i: (0, 0)
    state = lambda b, i: (b, 0, 0, 0)
    chunk_spec = lambda w: pl.BlockSpec((None, rows, w), rev)
    state_spec = pl.BlockSpec((None,) + SSD_STATE_SHAPE, state)
    return pl.pallas_call(
        _ssd_bwd_kernel,
        grid=(bsz, n_steps),
        in_specs=[chunk_spec(SSD_BC_W),
                  pl.BlockSpec((None, rows // t, SSD_BC_W, t), lambda b, i: (b, n_steps - 1 - i, 0, 0)),
                  chunk_spec(SSD_WIDTH), chunk_spec(SMALL_W), chunk_spec(SSD_WIDTH), chunk_spec(SSD_WIDTH),
                  pl.BlockSpec((SMALL_W, SSD_WIDTH), const2),
                  pl.BlockSpec((1, SSD_WIDTH), const2),
                  state_spec],
        out_specs=[chunk_spec(SSD_WIDTH), state_spec],
        out_shape=[jax.ShapeDtypeStruct((bsz, length, SSD_WIDTH), BF16),
                   jax.ShapeDtypeStruct((bsz,) + SSD_STATE_SHAPE, F32)],
        scratch_shapes=[pltpu.VMEM(SSD_STATE_SHAPE, F32)],
        compiler_params=_params("arbitrary", "arbitrary"),
        name="ssd_bwd",
    )(cm, bt, wxb, p3, ya, z, eb, ssd_norm.reshape(1, SSD_WIDTH), sb0)


def _gla_fwd_kernel(q_ref, k_ref, v_ref, sm_ref, wg_ref, gb_ref, sf0_ref,
                    oa_ref, qdb_ref, kwb_ref, decb_ref, sff_ref, sf_s):
    i = pl.program_id(1)
    n_steps = pl.num_programs(1)
    t = GLA_CHUNK
    rows = q_ref.shape[0]
    n_sub = rows // t
    dk, dv = GLA_KEY_DIM, GLA_VAL_DIM
    qscale = dk ** -0.5
    inv_norm = 1.0 / GLA_GATE_NORMALIZER

    @pl.when(i == 0)
    def _():
        sf_s[...] = sf0_ref[...]

    lower, upper = _tri(t)
    row = lax.broadcasted_iota(jnp.int32, (rows, rows), 0)
    col = lax.broadcasted_iota(jnp.int32, (rows, rows), 1)
    diff = row - col
    pos = row % t
    bd_lower = jnp.where(diff >= 0, jnp.where(diff <= pos, 1.0, 0.0), 0.0).astype(BF16)
    bd_upper = jnp.where(diff <= 0, jnp.where(-diff <= t - 1 - pos, 1.0, 0.0), 0.0).astype(BF16)

    hi, lo = _split(sm_ref[...])
    wg_hi, wg_lo = _split(wg_ref[...])
    logits = (jnp.dot(hi, wg_hi, preferred_element_type=F32) + jnp.dot(lo, wg_hi, preferred_element_type=F32)
              + jnp.dot(hi, wg_lo, preferred_element_type=F32)) + gb_ref[...]
    lg = _log_sigmoid(logits) * inv_norm
    cs = _dot_split_rhs(bd_lower, lg[:, 0:GLA_K])
    rc = _dot_split_rhs(bd_upper, lg[:, GLA_K:2 * GLA_K])

    def per_chunk_row(v, offset):
        return jnp.concatenate([jnp.broadcast_to(v[s * t + offset:s * t + offset + 1, :], (t, v.shape[1]))
                                for s in range(n_sub)], axis=0)

    for h in range(GLA_HEADS):
        ks = slice(h * dk, (h + 1) * dk)
        vs = slice(h * dv, (h + 1) * dv)
        c = cs[:, ks]
        r = rc[:, ks]
        qh = q_ref[:, ks].astype(F32) * qscale
        kh = k_ref[:, ks].astype(F32)
        qdf = (qh * jnp.exp(c)).astype(BF16)
        kif = (kh * jnp.exp(-c)).astype(BF16)
        qdb = (qh * jnp.exp(r)).astype(BF16)
        kib = (kh * jnp.exp(-r)).astype(BF16)
        kwf = (kh * jnp.exp(per_chunk_row(c, t - 1) - c)).astype(BF16)
        qdb_ref[:, ks] = qdb
        kwb_ref[:, ks] = (kh * jnp.exp(per_chunk_row(r, 0) - r)).astype(BF16)
        chunks = [slice(s * t, (s + 1) * t) for s in range(n_sub)]
        vhs = [v_ref[rs, vs].astype(BF16) for rs in chunks]
        atts = [(jnp.where(lower, _dot_nt(qdf[rs], kif[rs]), 0.0)
                 + jnp.where(upper, _dot_nt(qdb[rs], kib[rs]), 0.0)).astype(BF16) for rs in chunks]
        incs = [_dot_tn(vh, kwf[rs]) for vh, rs in zip(vhs, chunks)]
        states = [sf_s[h]]
        for s in range(n_sub):
            decb_ref[s, :, ks] = jnp.exp(r[s * t:s * t + 1, :])
            states.append(states[s] * jnp.exp(c[s * t + t - 1:s * t + t, :]) + incs[s])
        sf_s[h] = states[n_sub]
        for s, rs in enumerate(chunks):
            oa_ref[rs, vs] = (jnp.dot(atts[s], vhs[s], preferred_element_type=F32)
                              + _dot_nt(qdf[rs], states[s]))

    @pl.when(i == n_steps - 1)
    def _():
        sff_ref[...] = sf_s[...]


def _gla_bwd_kernel(qdb_ref, kwb_ref, decb_ref, v_ref, oa_ref, g_ref, gn_ref, sb0_ref,
                    os_ref, sbf_ref, sb_s):
    i = pl.program_id(1)
    n_steps = pl.num_programs(1)
    t = GLA_CHUNK
    n_sub = qdb_ref.shape[0] // t
    dk, dv = GLA_KEY_DIM, GLA_VAL_DIM

    @pl.when(i == 0)
    def _():
        sb_s[...] = sb0_ref[...]

    for h in range(GLA_HEADS):
        ks = slice(h * dk, (h + 1) * dk)
        vs = slice(h * dv, (h + 1) * dv)
        state = sb_s[h]
        for s in reversed(range(n_sub)):
            rs = slice(s * t, (s + 1) * t)
            o = oa_ref[rs, vs] + _dot_nt(qdb_ref[rs, ks], state)
            state = state * decb_ref[s, :, ks] + _dot_tn(v_ref[rs, vs], kwb_ref[rs, ks])
            os_ref[rs, vs] = (_head_norm(o, gn_ref[:, vs]) * _silu(g_ref[rs, vs].astype(F32))).astype(os_ref.dtype)
        sb_s[h] = state

    @pl.when(i == n_steps - 1)
    def _():
        sbf_ref[...] = sb_s[...]


GLA_STATE_SHAPE = (GLA_HEADS, GLA_VAL_DIM, GLA_KEY_DIM)


def _gla_fwd(q, k, v, small, wg, gb, sf0, rows):
    bsz, length, _ = q.shape
    rows = _row_tile(length, rows)
    n_steps = length // rows
    n_sub = rows // GLA_CHUNK
    fwd = lambda b, i: (b, i, 0)
    const2 = lambda b, i: (0, 0)
    state = lambda b, i: (b, 0, 0, 0)
    blk = lambda w: pl.BlockSpec((None, rows, w), fwd)
    state_spec = pl.BlockSpec((None,) + GLA_STATE_SHAPE, state)
    return pl.pallas_call(
        _gla_fwd_kernel,
        grid=(bsz, n_steps),
        in_specs=[blk(GLA_K), blk(GLA_K), blk(GLA_V), blk(SMALL_W),
                  pl.BlockSpec((SMALL_W, 2 * GLA_K), const2),
                  pl.BlockSpec((1, 2 * GLA_K), const2),
                  state_spec],
        out_specs=[blk(GLA_V), blk(GLA_K), blk(GLA_K),
                   pl.BlockSpec((None, n_sub, 1, GLA_K), lambda b, i: (b, i, 0, 0)),
                   state_spec],
        out_shape=[jax.ShapeDtypeStruct((bsz, length, GLA_V), F32),
                   jax.ShapeDtypeStruct((bsz, length, GLA_K), BF16),
                   jax.ShapeDtypeStruct((bsz, length, GLA_K), BF16),
                   jax.ShapeDtypeStruct((bsz, length // GLA_CHUNK, 1, GLA_K), F32),
                   jax.ShapeDtypeStruct((bsz,) + GLA_STATE_SHAPE, F32)],
        scratch_shapes=[pltpu.VMEM(GLA_STATE_SHAPE, F32)],
        compiler_params=_params("arbitrary", "arbitrary"),
        name="gla_fwd",
    )(q, k, v, small, wg, gb, sf0)


def _gla_bwd(qdb, kwb, decb, v, oa, g, gla_norm, sb0, rows):
    bsz, length, _ = oa.shape
    rows = _row_tile(length, rows)
    n_steps = length // rows
    n_sub = rows // GLA_CHUNK
    rev = lambda b, i: (b, n_steps - 1 - i, 0)
    const2 = lambda b, i: (0, 0)
    state = lambda b, i: (b, 0, 0, 0)
    blk = lambda w: pl.BlockSpec((None, rows, w), rev)
    state_spec = pl.BlockSpec((None,) + GLA_STATE_SHAPE, state)
    return pl.pallas_call(
        _gla_bwd_kernel,
        grid=(bsz, n_steps),
        in_specs=[blk(GLA_K), blk(GLA_K),
                  pl.BlockSpec((None, n_sub, 1, GLA_K), lambda b, i: (b, n_steps - 1 - i, 0, 0)),
                  blk(GLA_V), blk(GLA_V), blk(GLA_V),
                  pl.BlockSpec((1, GLA_V), const2),
                  state_spec],
        out_specs=[blk(GLA_V), state_spec],
        out_shape=[jax.ShapeDtypeStruct((bsz, length, GLA_V), BF16),
                   jax.ShapeDtypeStruct((bsz,) + GLA_STATE_SHAPE, F32)],
        scratch_shapes=[pltpu.VMEM(GLA_STATE_SHAPE, F32)],
        compiler_params=_params("arbitrary", "arbitrary"),
        name="gla_bwd",
    )(qdb, kwb, decb, v, oa, g, gla_norm.reshape(1, GLA_V), sb0)


def _out_proj_kernel(x_ref, gate_ref, *refs):
    *in_refs, w_ref, y_ref = refs
    acc = None
    off = 0
    for ref in in_refs:
        n = ref.shape[-1]
        part = jnp.dot(ref[...], w_ref[off:off + n, :], preferred_element_type=F32)
        acc = part if acc is None else acc + part
        off += n
    y_ref[...] = x_ref[...] + gate_ref[...] * acc


def _out_proj(x, gate, mixed, w_bf16, tm, name):
    bsz, length, d = x.shape
    tm = _row_tile(length, tm)
    row = lambda b, i: (b, i, 0)
    return pl.pallas_call(
        _out_proj_kernel,
        grid=(bsz, length // tm),
        in_specs=([pl.BlockSpec((None, tm, d), row),
                   pl.BlockSpec((None, 1, d), lambda b, i: (b, 0, 0))]
                  + [pl.BlockSpec((None, tm, m.shape[-1]), row) for m in mixed]
                  + [pl.BlockSpec(w_bf16.shape, lambda b, i: (0, 0), pipeline_mode=pl.Buffered(1))]),
        out_specs=pl.BlockSpec((None, tm, d), row),
        out_shape=jax.ShapeDtypeStruct((bsz, length, d), F32),
        compiler_params=_params("parallel", "parallel"),
        name=name,
    )(x, gate, *mixed, w_bf16)


def _attn_kernel(sink_ref, q_ref, k_ref, v_ref, kc_ref, vc_ref, g_ref, o_ref):
    j = pl.program_id(1)
    i = pl.program_id(2)
    length = k_ref.shape[0]
    blk = ATT_BLOCK
    band = 3 * blk
    dh = ATT_HEAD_DIM
    n_q = q_ref.shape[0] // blk
    n_ctx = kc_ref.shape[0]
    row_minus_col = (lax.broadcasted_iota(jnp.int32, (blk, band), 0)
                     - lax.broadcasted_iota(jnp.int32, (blk, band), 1))
    sink = jnp.concatenate([jnp.full((blk, LANES), sink_ref[j * ATT_GROUP + g] * LOG2E, F32)
                            for g in range(ATT_GROUP)], axis=0)
    kc = kc_ref[...]
    vc_ext = jnp.concatenate([vc_ref[...], jnp.ones((n_ctx, dh), BF16)], axis=1)
    ones_band = jnp.ones((band, dh), BF16)

    def window_start(qb):
        blk_idx = i * n_q + qb
        return blk_idx, pl.multiple_of(jnp.clip((blk_idx - 1) * blk, 0, length - band), blk)

    def scores(qb):
        _, start = window_start(qb)
        rs = slice(qb * blk, (qb + 1) * blk)
        q = jnp.concatenate([q_ref[rs, g * dh:(g + 1) * dh] for g in range(ATT_GROUP)], axis=0)
        return _dot_nt(q, k_ref[pl.ds(start, band), :]), _dot_nt(q, kc)

    pending = scores(0)
    for qb in range(n_q):
        s_band, s_ctx = pending
        if qb + 1 < n_q:
            pending = scores(qb + 1)
        blk_idx, start = window_start(qb)
        inside = jnp.abs(row_minus_col + (blk_idx * blk - start)) <= WINDOW
        v_all = jnp.concatenate([jnp.concatenate([v_ref[pl.ds(start, band), :], ones_band], axis=1),
                                 vc_ext], axis=0)
        rs = slice(qb * blk, (qb + 1) * blk)
        cols = []
        for c in range(band // LANES):
            ls = slice(c * LANES, (c + 1) * LANES)
            cols.append(jnp.concatenate(
                [jnp.where(inside[:, ls], s_band[g * blk:(g + 1) * blk, ls], float("-inf"))
                 for g in range(ATT_GROUP)], axis=0))
        for c in range(n_ctx // LANES):
            cols.append(s_ctx[:, c * LANES:(c + 1) * LANES])
        m = jnp.maximum(sink, jnp.max(functools.reduce(jnp.maximum, cols), axis=-1, keepdims=True))
        p = jnp.concatenate([jnp.exp2(col - m) for col in cols], axis=1).astype(BF16)
        acc = jnp.dot(p, v_all, preferred_element_type=F32)
        out = acc[:, 0:dh] / (acc[:, dh:2 * dh] + jnp.exp2(sink - m))
        for g in range(ATT_GROUP):
            cs = slice(g * dh, (g + 1) * dh)
            o_ref[rs, cs] = (out[g * blk:(g + 1) * blk, :] * g_ref[rs, cs]).astype(o_ref.dtype)


def _attention(sink, q, k, v, kc, vc, gate):
    bsz, length, _ = q.shape
    n_ctx = kc.shape[1]
    rows = _row_tile(length, ATT_Q_BLOCKS * ATT_BLOCK)
    gw = ATT_GROUP * ATT_HEAD_DIM
    qmap = lambda b, j, i, s: (b, i, j)
    kvmap = lambda b, j, i, s: (b, 0, j)
    grid_spec = pltpu.PrefetchScalarGridSpec(
        num_scalar_prefetch=1,
        grid=(bsz, ATT_KV_HEADS, length // rows),
        in_specs=[pl.BlockSpec((None, rows, gw), qmap),
                  pl.BlockSpec((None, length, ATT_HEAD_DIM), kvmap),
                  pl.BlockSpec((None, length, ATT_HEAD_DIM), kvmap),
                  pl.BlockSpec((None, n_ctx, ATT_HEAD_DIM), kvmap),
                  pl.BlockSpec((None, n_ctx, ATT_HEAD_DIM), kvmap),
                  pl.BlockSpec((None, rows, gw), qmap)],
        out_specs=pl.BlockSpec((None, rows, gw), qmap),
    )
    return pl.pallas_call(
        _attn_kernel,
        grid_spec=grid_spec,
        out_shape=jax.ShapeDtypeStruct((bsz, length, ATT_W), BF16),
        compiler_params=_params("parallel", "parallel", "arbitrary"),
        name="attention",
    )(sink, q, k, v, kc, vc, gate)


EVEN_WIDTHS = ((SSD_WIDTH, BF16), (CONV_CH, BF16), (GLA_K, BF16), (GLA_K, BF16), (GLA_V, BF16), (GLA_V, BF16),
               (SMALL_W, F32))


def _even_weight_layout(w_in):
    d = w_in.shape[0]
    sizes = (SSD_WIDTH, CONV_CH, 2 * SSD_HEADS, GLA_K, GLA_K, GLA_V, GLA_V, 2 * GLA_RANK)
    z, xbc, dt, q, k, v, g, lr = jnp.split(w_in.astype(BF16), np.cumsum(sizes)[:-1].tolist(), axis=1)
    pad = jnp.zeros((d, SMALL_W - 2 * SSD_HEADS - 2 * GLA_RANK), BF16)
    return jnp.concatenate([z, xbc, q, k, v, g, dt, lr, pad], axis=1)


def _expansion_matrices():
    rows = np.arange(SMALL_W)[:, None]
    heads = (np.arange(SSD_WIDTH) // SSD_HEAD_DIM)[None, :]
    ef = (rows == heads).astype(np.float32)
    eb = (rows - SSD_HEADS == heads).astype(np.float32)
    return jnp.asarray(ef, BF16), jnp.asarray(eb, BF16)


def _pad_lanes(v, width):
    return jnp.pad(v, ((0, 0), (0, width - v.shape[1])))


def _rope_tables(length):
    rows = length // GRID_W
    row = np.repeat(np.arange(rows, dtype=np.float64), GRID_W)
    col = np.tile(np.arange(GRID_W, dtype=np.float64), rows)
    inv = 1.0 / (ROPE_BASE ** (np.arange(ROPE_FREQS, dtype=np.float64) / ROPE_FREQS))
    ang_r = row[:, None] * inv
    ang_c = col[:, None] * inv
    cos = np.concatenate([np.cos(ang_r), np.cos(ang_c), np.cos(ang_r), np.cos(ang_c)], axis=1)
    sin = np.concatenate([-np.sin(ang_r), -np.sin(ang_c), np.sin(ang_r), np.sin(ang_c)], axis=1)
    return jnp.asarray(cos, F32), jnp.asarray(sin, F32)


def _rope_head_layout(v, n_heads):
    lead = v.shape[:-1]
    v = v.reshape(lead + (n_heads, 2, 2, ROPE_FREQS))
    return jnp.swapaxes(v, -3, -2).reshape(lead + (n_heads * ATT_HEAD_DIM,))


def _mod_rows(mod, rows, bsz, d):
    picked = jnp.broadcast_to(mod[rows], (bsz, 3 * d)) if isinstance(rows, int) else mod[rows]
    return [picked[:, None, j * d:(j + 1) * d] for j in range(3)]


PROJ_ROWS = 512
OUT_ROWS = 512
ATT_Q_BLOCKS = 8
GLA_ROWS = 512
GLA_BWD_ROWS = 512
SSD_FWD_ROWS = 4 * SSD_CHUNK
SSD_BWD_ROWS = 512
CVEC_ROWS = SUBLANES


def kernel(x, c, ctx, c_ctx, e_norm, e_mod_w, e_mod_b, e_w_in, e_conv_w, e_conv_b, e_dt_bias, e_a_log,
           e_d_skip, e_ssd_norm, e_gla_gate_w, e_gla_gate_b, e_gla_norm, e_w_out, o_norm, o_mod_w, o_mod_b,
           o_w_in, o_q_norm, o_k_norm, o_sink, o_w_out):
    bsz, length, d = x.shape
    n_ctx = ctx.shape[1]
    assert e_norm.shape[0] == 1 and o_norm.shape[0] == 1, "two-layer block only"
    assert length % SSD_CHUNK == 0 and n_ctx % SSD_CHUNK == 0 and length >= 3 * ATT_BLOCK
    assert bsz + 1 <= CVEC_ROWS

    cvecs = jnp.zeros((CVEC_ROWS, d), F32).at[:bsz].set(c).at[bsz].set(c_ctx)
    lat_rows = slice(0, bsz)

    mod = _adaln(cvecs, e_mod_w[0], e_mod_b[0])
    shift, scale, gate = _mod_rows(mod, lat_rows, bsz, d)
    c_shift, c_scale, c_gate = _mod_rows(mod, bsz, bsz, d)
    w_in = _even_weight_layout(e_w_in[0])
    w_out = e_w_out[0].astype(BF16)
    ef, eb = _expansion_matrices()
    nh2 = 2 * SSD_HEADS
    bias_flat = e_dt_bias[0].reshape(1, nh2)
    alog_flat = e_a_log[0].reshape(1, nh2)
    bias_row, alog_row = _pad_lanes(bias_flat, SMALL_W), _pad_lanes(alog_flat, SMALL_W)
    bias_col, alog_col = bias_flat.reshape(nh2, 1), alog_flat.reshape(nh2, 1)
    dskip_row = jnp.repeat(e_d_skip[0], SSD_HEAD_DIM).reshape(1, SSD_WIDTH)
    conv_b = e_conv_b[0].reshape(1, CONV_CH)
    wg = jnp.zeros((SMALL_W, 2 * GLA_K), F32)
    wg = wg.at[nh2:nh2 + GLA_RANK, 0:GLA_K].set(e_gla_gate_w[0, 0])
    wg = wg.at[nh2 + GLA_RANK:nh2 + 2 * GLA_RANK, GLA_K:2 * GLA_K].set(e_gla_gate_w[0, 1])
    gb = e_gla_gate_b[0].reshape(1, 2 * GLA_K)

    def mixers(stream, sc, sh, ssd_init, gla_init):
        z, xbc, q, k, v, g, small = _proj_even(stream, e_norm[0], sc, sh, w_in, EVEN_WIDTHS, PROJ_ROWS)
        ya, cm, bt, wxb, p3, ssd_f = _ssd_fwd(xbc, small, e_conv_w[0], conv_b, bias_row, bias_col,
                                              alog_row, alog_col, dskip_row, ef, eb, ssd_init[0])
        ys, ssd_b = _ssd_bwd(cm, bt, wxb, p3, ya, z, eb, e_ssd_norm[0], ssd_init[1])
        oa, qdb, kwb, decb, gla_f = _gla_fwd(q, k, v, small, wg, gb, gla_init[0], GLA_ROWS)
        os_, gla_b = _gla_bwd(qdb, kwb, decb, v, oa, g, e_gla_norm[0], gla_init[1], GLA_BWD_ROWS)
        return (ys, os_), (ssd_f, ssd_b), (gla_f, gla_b)

    ssd0 = jnp.zeros((bsz,) + SSD_STATE_SHAPE, F32)
    gla0 = jnp.zeros((bsz,) + GLA_STATE_SHAPE, F32)
    ctx_mix, ssd_fin, gla_fin = mixers(ctx, c_scale, c_shift, (ssd0, ssd0), (gla0, gla0))
    lat_mix, _, _ = mixers(x, scale, shift, ssd_fin, gla_fin)
    x = _out_proj(x, gate, lat_mix, w_out, OUT_ROWS, "out_even")
    xc = _out_proj(ctx, c_gate, ctx_mix, w_out, OUT_ROWS, "out_even")

    mod = _adaln(cvecs, o_mod_w[0], o_mod_b[0])
    shift, scale, gate = _mod_rows(mod, lat_rows, bsz, d)
    c_shift, c_scale, _ = _mod_rows(mod, bsz, bsz, d)
    w_k, w_v, w_q, w_g = jnp.split(o_w_in[0].astype(BF16), [ATT_KV_W, 2 * ATT_KV_W, 2 * ATT_KV_W + ATT_W], axis=1)
    w_in = jnp.concatenate([_rope_head_layout(w_k, ATT_KV_HEADS), w_v, _rope_head_layout(w_q, ATT_HEADS), w_g],
                           axis=1)
    q_norm = _rope_head_layout(o_q_norm[0], 1)
    k_norm = _rope_head_layout(o_k_norm[0], 1)
    cos, sin = _rope_tables(length)
    no_rot = (jnp.ones((n_ctx, ATT_HEAD_DIM), F32), jnp.zeros((n_ctx, ATT_HEAD_DIM), F32))
    kc, vc = _proj_odd(xc, o_norm[0], c_scale, c_shift, w_in[:, :2 * ATT_KV_W], *no_rot,
                       q_norm, k_norm, False, PROJ_ROWS)
    k, v, q, g = _proj_odd(x, o_norm[0], scale, shift, w_in, cos, sin, q_norm, k_norm, True, PROJ_ROWS)
    o = _attention(o_sink[0].astype(F32), q, k, v, kc, vc, g)
    return _out_proj(x, gate, (o,), o_w_out[0].astype(BF16), OUT_ROWS, "out_odd")
```

```python
import functools

import jax
import jax.numpy as jnp
import numpy as np
from jax import lax
from jax.experimental import pallas as pl
from jax.experimental.pallas import tpu as pltpu

F32 = jnp.float32
BF16 = jnp.bfloat16

GRID_W = 64
SSD_HEADS = 16
SSD_HEAD_DIM = 64
SSD_WIDTH = SSD_HEADS * SSD_HEAD_DIM
SSD_GROUPS = 2
SSD_STATE = 128
SSD_CHUNK = 128
CONV_K = 5
CONV_CH = SSD_WIDTH + 2 * SSD_GROUPS * SSD_STATE
GLA_HEADS = 4
GLA_KEY_DIM = 128
GLA_VAL_DIM = 256
GLA_K = GLA_HEADS * GLA_KEY_DIM
GLA_V = GLA_HEADS * GLA_VAL_DIM
GLA_RANK = 16
GLA_GATE_NORMALIZER = 16.0
GLA_CHUNK = 64
ATT_HEADS = 16
ATT_KV_HEADS = 4
ATT_GROUP = ATT_HEADS // ATT_KV_HEADS
ATT_HEAD_DIM = 128
ATT_W = ATT_HEADS * ATT_HEAD_DIM
ATT_KV_W = ATT_KV_HEADS * ATT_HEAD_DIM
WINDOW = 128
ATT_BLOCK = 128
ROPE_BASE = 10000.0
ROPE_FREQS = ATT_HEAD_DIM // 4
NORM_EPS = 1e-6
LOG2E = 1.4426950408889634

LANES = 128
SUBLANES = 8
VMEM_LIMIT_BYTES = 56 * 1024 * 1024

SSD_GROUP_W = SSD_WIDTH // SSD_GROUPS
SSD_BC_W = SSD_GROUPS * SSD_STATE
SMALL_W = LANES
HALO = 2 * SUBLANES
HEADS_PER_DOT = 4
CONV_ROW_STRIDE = 2 * SUBLANES + 1
CONV_OUT_ROWS = CONV_ROW_STRIDE * SUBLANES
assert SSD_CHUNK <= CONV_OUT_ROWS <= SSD_CHUNK + HALO - (CONV_K - 1) // 2


def _dot(a, b):
    return jnp.dot(a.astype(BF16), b.astype(BF16), preferred_element_type=F32)


def _dot_nt(a, b):
    return lax.dot_general(a.astype(BF16), b.astype(BF16), (((1,), (1,)), ((), ())),
                           preferred_element_type=F32)


def _dot_tn(a, b):
    return lax.dot_general(a.astype(BF16), b.astype(BF16), (((0,), (0,)), ((), ())),
                           preferred_element_type=F32)


def _split(v):
    hi = v.astype(BF16)
    lo = (v - hi.astype(F32)).astype(BF16)
    return hi, lo


def _dot_split_lhs(v, m):
    hi, lo = _split(v)
    return (jnp.dot(hi, m, preferred_element_type=F32) + jnp.dot(lo, m, preferred_element_type=F32))


def _dot_split_rhs(m, v):
    hi, lo = _split(v)
    return (jnp.dot(m, hi, preferred_element_type=F32) + jnp.dot(m, lo, preferred_element_type=F32))


def _dot3(a, b):
    ah, al = _split(a)
    bh, bl = _split(b)
    return (jnp.dot(ah, bh, preferred_element_type=F32) + jnp.dot(al, bh, preferred_element_type=F32)
            + jnp.dot(ah, bl, preferred_element_type=F32))


def _silu(v):
    h = 0.5 * v
    return h + h * jnp.tanh(h)


def _softplus(v):
    return jnp.maximum(v, 0.0) + jnp.log(1.0 + jnp.exp(-jnp.abs(v)))


def _log_sigmoid(v):
    return jnp.minimum(v, 0.0) - jnp.log(1.0 + jnp.exp(-jnp.abs(v)))


def _tri(n):
    row = lax.broadcasted_iota(jnp.int32, (n, n), 0)
    col = lax.broadcasted_iota(jnp.int32, (n, n), 1)
    return row >= col, col >= row


def _params(*sem):
    return pltpu.CompilerParams(dimension_semantics=sem, vmem_limit_bytes=VMEM_LIMIT_BYTES)


def _adaln_kernel(c_ref, w_ref, b_ref, o_ref):
    o_ref[...] = _dot3(_silu(c_ref[...]), w_ref[...]) + b_ref[...]


def _adaln(cvecs, w, b):
    rows, d = cvecs.shape
    n = w.shape[1]
    tn = 1024
    return pl.pallas_call(
        _adaln_kernel,
        grid=(n // tn,),
        in_specs=[pl.BlockSpec((rows, d), lambda j: (0, 0)),
                  pl.BlockSpec((d, tn), lambda j: (0, j)),
                  pl.BlockSpec((1, tn), lambda j: (0, j))],
        out_specs=pl.BlockSpec((rows, tn), lambda j: (0, j)),
        out_shape=jax.ShapeDtypeStruct((rows, n), F32),
        compiler_params=_params("parallel"),
        name="adaln",
    )(cvecs, w, b.reshape(1, n))


def _modulated_norm(x, g, sc, sh):
    r = lax.rsqrt(jnp.mean(x * x, axis=-1, keepdims=True) + NORM_EPS)
    return ((x * r) * g) * (1.0 + sc) + sh


def _store_cols(h, w_ref, off, ref, act=None):
    n = ref.shape[-1]
    for c0 in range(0, n, 512):
        c1 = min(n, c0 + 512)
        t = jnp.dot(h, w_ref[:, off + c0:off + c1], preferred_element_type=F32)
        ref[:, c0:c1] = (t if act is None else act(t)).astype(ref.dtype)
    return off + n


def _proj_even_kernel(x_ref, g_ref, sc_ref, sh_ref, w_ref, *out_refs):
    h = _modulated_norm(x_ref[...], g_ref[...], sc_ref[...], sh_ref[...]).astype(BF16)
    off = 0
    for ref in out_refs:
        off = _store_cols(h, w_ref, off, ref)


def _head_norm(t, gain):
    r = lax.rsqrt(jnp.mean(t * t, axis=-1, keepdims=True) + NORM_EPS)
    return (t * r) * gain


def _rope(t, cos, sin_signed):
    return t * cos + pltpu.roll(t, ATT_HEAD_DIM // 2, 1) * sin_signed


def _proj_odd_kernel(x_ref, g_ref, sc_ref, sh_ref, w_ref, cos_ref, sin_ref, qn_ref, kn_ref,
                     k_ref, v_ref, *qg_refs):
    h = _modulated_norm(x_ref[...], g_ref[...], sc_ref[...], sh_ref[...]).astype(BF16)
    cos = cos_ref[...]
    sin = sin_ref[...]
    scale = ATT_HEAD_DIM ** -0.5 * LOG2E

    width = HEADS_PER_DOT * ATT_HEAD_DIM
    dh = ATT_HEAD_DIM
    pr = lax.broadcasted_iota(jnp.int32, (2 * dh, 2 * dh), 0) // dh
    pc = lax.broadcasted_iota(jnp.int32, (2 * dh, 2 * dh), 1) // dh
    head_ones = jnp.where(pr == pc, 1.0, 0.0).astype(BF16)

    def project(col):
        return jnp.dot(h, w_ref[:, col:col + width], preferred_element_type=F32)

    def finish_heads(t4, ref, j0, gain, out_scale):
        for j in range(0, HEADS_PER_DOT, 2):
            t2 = t4[:, j * dh:(j + 2) * dh]
            mean_sq = jnp.dot((t2 * t2).astype(BF16), head_ones, preferred_element_type=F32) * (1.0 / dh)
            t2 = t2 * lax.rsqrt(mean_sq + NORM_EPS)
            for jj in range(2):
                t = _rope(t2[:, jj * dh:(jj + 1) * dh] * gain, cos, sin)
                c0 = (j0 + j + jj) * dh
                ref[:, c0:c0 + dh] = (t if out_scale is None else t * out_scale).astype(ref.dtype)

    work = []
    if qg_refs:
        q_ref, gate_ref = qg_refs
        for j0 in range(0, ATT_HEADS, HEADS_PER_DOT):
            c0 = j0 * dh

            def finish_gate(t4, c0=c0):
                gate_ref[:, c0:c0 + width] = _silu(t4)

            work.append((2 * ATT_KV_W + ATT_W + c0, finish_gate))
            work.append((2 * ATT_KV_W + c0,
                         functools.partial(finish_heads, ref=q_ref, j0=j0, gain=qn_ref[...], out_scale=scale)))
    for j0 in range(0, ATT_KV_HEADS, HEADS_PER_DOT):
        work.append((j0 * dh, functools.partial(finish_heads, ref=k_ref, j0=j0, gain=kn_ref[...], out_scale=None)))
    for col, finish in work:
        finish(project(col))
    _store_cols(h, w_ref, ATT_KV_W, v_ref)


def _row_tile(length, want):
    return min(length, want)


def _proj_even(x, norm_g, scale, shift, w_bf16, widths, tm):
    bsz, length, d = x.shape
    tm = _row_tile(length, tm)
    n = w_bf16.shape[1]
    row = lambda b, i: (b, i, 0)
    mod = lambda b, i: (b, 0, 0)
    return pl.pallas_call(
        _proj_even_kernel,
        grid=(bsz, length // tm),
        in_specs=[pl.BlockSpec((None, tm, d), row),
                  pl.BlockSpec((1, d), lambda b, i: (0, 0)),
                  pl.BlockSpec((None, 1, d), mod),
                  pl.BlockSpec((None, 1, d), mod),
                  pl.BlockSpec((d, n), lambda b, i: (0, 0), pipeline_mode=pl.Buffered(1))],
        out_specs=[pl.BlockSpec((None, tm, wd), row) for wd, _ in widths],
        out_shape=[jax.ShapeDtypeStruct((bsz, length, wd), dt) for wd, dt in widths],
        compiler_params=_params("parallel", "parallel"),
        name="proj_even",
    )(x, norm_g.reshape(1, d), scale, shift, w_bf16)


def _proj_odd(x, norm_g, scale, shift, w_bf16, cos, sin, q_norm, k_norm, with_queries, tm):
    bsz, length, d = x.shape
    tm = _row_tile(length, tm)
    n = w_bf16.shape[1]
    row = lambda b, i: (b, i, 0)
    mod = lambda b, i: (b, 0, 0)
    const = lambda b, i: (0, 0)
    widths = [(ATT_KV_W, BF16), (ATT_KV_W, BF16)]
    if with_queries:
        widths += [(ATT_W, BF16), (ATT_W, F32)]
    return pl.pallas_call(
        _proj_odd_kernel,
        grid=(bsz, length // tm),
        in_specs=[pl.BlockSpec((None, tm, d), row),
                  pl.BlockSpec((1, d), const),
                  pl.BlockSpec((None, 1, d), mod),
                  pl.BlockSpec((None, 1, d), mod),
                  pl.BlockSpec((d, n), const, pipeline_mode=pl.Buffered(1)),
                  pl.BlockSpec((tm, ATT_HEAD_DIM), lambda b, i: (i, 0)),
                  pl.BlockSpec((tm, ATT_HEAD_DIM), lambda b, i: (i, 0)),
                  pl.BlockSpec((1, ATT_HEAD_DIM), const),
                  pl.BlockSpec((1, ATT_HEAD_DIM), const)],
        out_specs=[pl.BlockSpec((None, tm, wd), row) for wd, _ in widths],
        out_shape=[jax.ShapeDtypeStruct((bsz, length, wd), dt) for wd, dt in widths],
        compiler_params=_params("parallel", "parallel"),
        name="proj_odd_q" if with_queries else "proj_odd_kv",
    )(x, norm_g.reshape(1, d), scale, shift, w_bf16, cos, sin,
      q_norm.reshape(1, ATT_HEAD_DIM), k_norm.reshape(1, ATT_HEAD_DIM))


def _conv_silu(win_s, prev_ref, cur_ref, next_ref, has_prev, has_next, cw_ref, cb_ref, x_s):
    rows = cur_ref.shape[0]
    half = (CONV_K - 1) // 2
    for j in range(CONV_CH // LANES):
        ls = slice(j * LANES, (j + 1) * LANES)
        win_s[j, 0:HALO, :] = jnp.where(has_prev, prev_ref[:, ls].astype(F32), 0.0)
        win_s[j, HALO:HALO + rows, :] = cur_ref[:, ls].astype(F32)
        win_s[j, HALO + rows:2 * HALO + rows, :] = jnp.where(has_next, next_ref[:, ls].astype(F32), 0.0)
        taps = [cw_ref[k:k + 1, ls] for k in range(CONV_K)]
        bias = cb_ref[:, ls]
        win = win_s.at[j]
        out = x_s.at[j]
        for c in range(rows // SSD_CHUNK):
            for a in range(CONV_ROW_STRIDE):
                r0 = HALO + c * SSD_CHUNK - half + a
                acc = bias + taps[0] * win[pl.ds(r0, SUBLANES, stride=CONV_ROW_STRIDE), :]
                for k in range(1, CONV_K):
                    acc = acc + taps[k] * win[pl.ds(r0 + k, SUBLANES, stride=CONV_ROW_STRIDE), :]
                out[pl.ds(c * CONV_OUT_ROWS + a, SUBLANES, stride=CONV_ROW_STRIDE), :] = _silu(acc)


def _ssd_fwd_kernel(p_ref, c_ref, n_ref, sm_ref, cw_ref, cb_ref, bias_row_ref, bias_col_ref,
                    alog_row_ref, alog_col_ref, dskip_ref, ef_ref, eb_ref, sf0_ref,
                    ya_ref, cm_ref, bt_ref, wxb_ref, p3_ref, sff_ref,
                    sf_s, win_s, x_s):
    i = pl.program_id(1)
    nc = pl.num_programs(1)
    t = SSD_CHUNK
    nh = SSD_HEADS
    slabs_x = SSD_WIDTH // LANES
    slab_b = slabs_x
    slab_c = slabs_x + SSD_GROUPS

    @pl.when(i == 0)
    def _():
        sf_s[...] = sf0_ref[...]

    lower, upper = _tri(t)
    ltri = jnp.where(lower, 1.0, 0.0).astype(BF16)
    utri = jnp.where(upper, 1.0, 0.0).astype(BF16)
    a_row = -jnp.exp(alog_row_ref[...])
    a_col = -jnp.exp(alog_col_ref[...])
    ef = ef_ref[...]
    eb = eb_ref[...]
    neg_inf = float("-inf")

    _conv_silu(win_s, p_ref, c_ref, n_ref, i > 0, i < nc - 1, cw_ref, cb_ref, x_s)
    n_chunks = sm_ref.shape[0] // t
    lane = lax.broadcasted_iota(jnp.int32, (t, LANES), 1)
    left = lane < SSD_HEAD_DIM
    heads_per_group = nh // SSD_GROUPS
    pairs_per_group = heads_per_group // 2

    def chunk_decays(c):
        rs = slice(c * t, (c + 1) * t)
        sm = sm_ref[rs, :]
        dt = _softplus(sm + bias_row_ref[...])
        dta = dt * a_row
        cs = _dot_split_rhs(ltri, dta)
        rc = _dot_split_rhs(utri, dta)
        dt_t = _softplus(sm.T[0:2 * nh, :] + bias_col_ref[...])
        dta_t = dt_t * a_col
        log2_dt_t = jnp.log2(dt_t)
        f_q = cs * LOG2E
        g_q = rc * LOG2E
        f_k = _dot_split_lhs(dta_t, utri) * LOG2E - log2_dt_t
        g_k = _dot_split_lhs(dta_t, ltri) * LOG2E - log2_dt_t
        dec_f = _dot_split_lhs(jnp.exp2(f_q), ef)
        wgt_f = _dot_split_lhs(dt * jnp.exp(cs[t - 1:t, :] - cs), ef)
        wgt_b = _dot_split_lhs(dt * jnp.exp(rc[0:1, :] - rc), eb)
        p3_ref[rs, :] = jnp.exp2(g_q)
        for p in range(slabs_x):
            ls = slice(p * LANES, (p + 1) * LANES)
            wxb_ref[rs, ls] = (x_s[p, c * CONV_OUT_ROWS:c * CONV_OUT_ROWS + t, :] * wgt_b[:, ls]).astype(BF16)
        return f_q, g_q, f_k, g_k, dec_f, wgt_f

    decays = [chunk_decays(c) for c in range(n_chunks)]

    for c in range(n_chunks):
        f_q, g_q, f_k, g_k, dec_f, wgt_f = decays[c]
        rs = slice(c * t, (c + 1) * t)
        xr = slice(c * CONV_OUT_ROWS, c * CONV_OUT_ROWS + t)

        def decay_matrix(h, cbg):
            lf = jnp.exp2(jnp.where(lower, f_q[:, h:h + 1] - f_k[h:h + 1, :], neg_inf))
            ub = jnp.exp2(jnp.where(upper, g_q[:, nh + h:nh + h + 1] - g_k[nh + h:nh + h + 1, :], neg_inf))
            return cbg * (lf + ub)

        for g in range(SSD_GROUPS):
            cg = x_s[slab_c + g, xr, :]
            bg = x_s[slab_b + g, xr, :]
            gs = slice(g * SSD_GROUP_W, (g + 1) * SSD_GROUP_W)
            cm_ref[rs, g * SSD_STATE:(g + 1) * SSD_STATE] = cg.astype(BF16)
            cbg = _dot_nt(cg, bg)
            state = sf_s[g]
            y_off = _dot(cg, state) * dec_f[:, gs]
            wx = []
            for pair in range(pairs_per_group):
                h0 = g * heads_per_group + 2 * pair
                p = h0 // 2
                ls = slice(p * LANES, (p + 1) * LANES)
                xp = x_s[p, xr, :]
                y = (_dot(decay_matrix(h0, cbg), jnp.where(left, xp, 0.0))
                     + _dot(decay_matrix(h0 + 1, cbg), jnp.where(left, 0.0, xp)))
                ya_ref[rs, ls] = y + y_off[:, pair * LANES:(pair + 1) * LANES] + dskip_ref[:, ls] * xp
                wx.append((xp * wgt_f[:, ls]).astype(BF16))
            bt = bg.T.astype(BF16)
            bt_ref[c, g * SSD_STATE:(g + 1) * SSD_STATE, :] = bt
            sf_s[g] = dec_f[t - 1:t, gs] * state + jnp.dot(bt, jnp.concatenate(wx, axis=1),
                                                           preferred_element_type=F32)

    @pl.when(i == nc - 1)
    def _():
        sff_ref[...] = sf_s[...]


def _ssd_bwd_kernel(cm_ref, bt_ref, wxb_ref, p3_ref, ya_ref, z_ref, eb_ref, sn_ref, sb0_ref,
                    ys_ref, sbf_ref, sb_s):
    i = pl.program_id(1)
    nc = pl.num_programs(1)

    @pl.when(i == 0)
    def _():
        sb_s[...] = sb0_ref[...]

    t = SSD_CHUNK
    n_sub = bt_ref.shape[0]
    dec_b = _dot_split_lhs(p3_ref[...], eb_ref[...])
    for g in range(SSD_GROUPS):
        gs = slice(g * SSD_GROUP_W, (g + 1) * SSD_GROUP_W)
        ss = slice(g * SSD_STATE, (g + 1) * SSD_STATE)
        chunks = [slice(s * t, (s + 1) * t) for s in range(n_sub)]
        incs = [jnp.dot(bt_ref[s, ss, :], wxb_ref[rs, gs], preferred_element_type=F32)
                for s, rs in enumerate(chunks)]
        states = [None] * n_sub
        state = sb_s[g]
        for s in reversed(range(n_sub)):
            states[s] = state
            state = dec_b[s * t:s * t + 1, gs] * state + incs[s]
        sb_s[g] = state
        for s, rs in enumerate(chunks):
            y_off = jnp.dot(cm_ref[rs, ss], states[s].astype(BF16), preferred_element_type=F32) * dec_b[rs, gs]
            y = (ya_ref[rs, gs] + y_off) * _silu(z_ref[rs, gs].astype(F32))
            ys_ref[rs, gs] = _head_norm(y, sn_ref[:, gs]).astype(ys_ref.dtype)

    @pl.when(i == nc - 1)
    def _():
        sbf_ref[...] = sb_s[...]


SSD_STATE_SHAPE = (SSD_GROUPS, SSD_STATE, SSD_GROUP_W)


def _ssd_fwd(xbc, small, conv_w, conv_b, bias_row, bias_col, alog_row, alog_col, dskip_row, ef, eb, sf0):
    bsz, length, _ = xbc.shape
    t = SSD_CHUNK
    rows = _row_tile(length, SSD_FWD_ROWS)
    nc = length // t
    per = rows // HALO
    last_halo = length // HALO - 1

    def cur(b, i): return (b, i, 0)
    def prev(b, i): return (b, jnp.maximum(i * per - 1, 0), 0)
    def nxt(b, i): return (b, jnp.minimum((i + 1) * per, last_halo), 0)
    const2 = lambda b, i: (0, 0)
    state = lambda b, i: (b, 0, 0, 0)

    halo_spec = lambda f: pl.BlockSpec((None, HALO, CONV_CH), f)
    chunk_spec = lambda w: pl.BlockSpec((None, rows, w), cur)
    state_spec = pl.BlockSpec((None,) + SSD_STATE_SHAPE, state)
    return pl.pallas_call(
        _ssd_fwd_kernel,
        grid=(bsz, length // rows),
        in_specs=[halo_spec(prev), chunk_spec(CONV_CH), halo_spec(nxt), chunk_spec(SMALL_W),
                  pl.BlockSpec((CONV_K, CONV_CH), const2),
                  pl.BlockSpec((1, CONV_CH), const2),
                  pl.BlockSpec((1, SMALL_W), const2),
                  pl.BlockSpec((2 * SSD_HEADS, 1), const2),
                  pl.BlockSpec((1, SMALL_W), const2),
                  pl.BlockSpec((2 * SSD_HEADS, 1), const2),
                  pl.BlockSpec((1, SSD_WIDTH), const2),
                  pl.BlockSpec((SMALL_W, SSD_WIDTH), const2),
                  pl.BlockSpec((SMALL_W, SSD_WIDTH), const2),
                  state_spec],
        out_specs=[chunk_spec(SSD_WIDTH), chunk_spec(SSD_BC_W),
                   pl.BlockSpec((None, rows // t, SSD_BC_W, t), lambda b, i: (b, i, 0, 0)),
                   chunk_spec(SSD_WIDTH), chunk_spec(SMALL_W), state_spec],
        out_shape=[jax.ShapeDtypeStruct((bsz, length, SSD_WIDTH), F32),
                   jax.ShapeDtypeStruct((bsz, length, SSD_BC_W), BF16),
                   jax.ShapeDtypeStruct((bsz, nc, SSD_BC_W, t), BF16),
                   jax.ShapeDtypeStruct((bsz, length, SSD_WIDTH), BF16),
                   jax.ShapeDtypeStruct((bsz, length, SMALL_W), F32),
                   jax.ShapeDtypeStruct((bsz,) + SSD_STATE_SHAPE, F32)],
        scratch_shapes=[pltpu.VMEM(SSD_STATE_SHAPE, F32),
                        pltpu.VMEM((CONV_CH // LANES, rows + 2 * HALO, LANES), F32),
                        pltpu.VMEM((CONV_CH // LANES, (rows // t) * CONV_OUT_ROWS, LANES), F32)],
        compiler_params=_params("arbitrary", "arbitrary"),
        name="ssd_fwd",
    )(xbc, xbc, xbc, small, conv_w, conv_b, bias_row, bias_col, alog_row, alog_col, dskip_row, ef, eb, sf0)


def _ssd_bwd(cm, bt, wxb, p3, ya, z, eb, ssd_norm, sb0):
    bsz, length, _ = ya.shape
    t = SSD_CHUNK
    rows = _row_tile(length, SSD_BWD_ROWS)
    n_steps = length // rows
    rev = lambda b, i: (b, n_steps - 1 - i, 0)
    const2 = lambda b, i: (0, 0)
    state = lambda b, i: (b, 0, 0, 0)
    chunk_spec = lambda w: pl.BlockSpec((None, rows, w), rev)
    state_spec = pl.BlockSpec((None,) + SSD_STATE_SHAPE, state)
    return pl.pallas_call(
        _ssd_bwd_kernel,
        grid=(bsz, n_steps),
        in_specs=[chunk_spec(SSD_BC_W),
                  pl.BlockSpec((None, rows // t, SSD_BC_W, t), lambda b, i: (b, n_steps - 1 - i, 0, 0)),
                  chunk_spec(SSD_WIDTH), chunk_spec(SMALL_W), chunk_spec(SSD_WIDTH), chunk_spec(SSD_WIDTH),
                  pl.BlockSpec((SMALL_W, SSD_WIDTH), const2),
                  pl.BlockSpec((1, SSD_WIDTH), const2),
                  state_spec],
        out_specs=[chunk_spec(SSD_WIDTH), state_spec],
        out_shape=[jax.ShapeDtypeStruct((bsz, length, SSD_WIDTH), BF16),
                   jax.ShapeDtypeStruct((bsz,) + SSD_STATE_SHAPE, F32)],
        scratch_shapes=[pltpu.VMEM(SSD_STATE_SHAPE, F32)],
        compiler_params=_params("arbitrary", "arbitrary"),
        name="ssd_bwd",
    )(cm, bt, wxb, p3, ya, z, eb, ssd_norm.reshape(1, SSD_WIDTH), sb0)


def _gla_fwd_kernel(q_ref, k_ref, v_ref, sm_ref, wg_ref, gb_ref, sf0_ref,
                    oa_ref, qdb_ref, kwb_ref, decb_ref, sff_ref, sf_s):
    i = pl.program_id(1)
    n_steps = pl.num_programs(1)
    t = GLA_CHUNK
    rows = q_ref.shape[0]
    n_sub = rows // t
    dk, dv = GLA_KEY_DIM, GLA_VAL_DIM
    qscale = dk ** -0.5
    inv_norm = 1.0 / GLA_GATE_NORMALIZER

    @pl.when(i == 0)
    def _():
        sf_s[...] = sf0_ref[...]

    lower, upper = _tri(t)
    grp = min(rows, GLA_CUMSUM_ROWS)
    row = lax.broadcasted_iota(jnp.int32, (grp, grp), 0)
    col = lax.broadcasted_iota(jnp.int32, (grp, grp), 1)
    diff = row - col
    pos = row % t
    bd_lower = jnp.where(diff >= 0, jnp.where(diff <= pos, 1.0, 0.0), 0.0).astype(BF16)
    bd_upper = jnp.where(diff <= 0, jnp.where(-diff <= t - 1 - pos, 1.0, 0.0), 0.0).astype(BF16)

    hi, lo = _split(sm_ref[...])
    wg_hi, wg_lo = _split(wg_ref[...])
    logits = (jnp.dot(hi, wg_hi, preferred_element_type=F32) + jnp.dot(lo, wg_hi, preferred_element_type=F32)
              + jnp.dot(hi, wg_lo, preferred_element_type=F32)) + gb_ref[...]
    lg = _log_sigmoid(logits) * inv_norm
    groups = [slice(r0, r0 + grp) for r0 in range(0, rows, grp)]
    cs = jnp.concatenate([_dot_split_rhs(bd_lower, lg[gr, 0:GLA_K]) for gr in groups], axis=0)
    rc = jnp.concatenate([_dot_split_rhs(bd_upper, lg[gr, GLA_K:2 * GLA_K]) for gr in groups], axis=0)

    def per_chunk_row(v, offset):
        return jnp.concatenate([jnp.broadcast_to(v[s * t + offset:s * t + offset + 1, :], (t, v.shape[1]))
                                for s in range(n_sub)], axis=0)

    for h in range(GLA_HEADS):
        ks = slice(h * dk, (h + 1) * dk)
        vs = slice(h * dv, (h + 1) * dv)
        c = cs[:, ks]
        r = rc[:, ks]
        qh = q_ref[:, ks].astype(F32) * qscale
        kh = k_ref[:, ks].astype(F32)
        qdf = (qh * jnp.exp(c)).astype(BF16)
        kif = (kh * jnp.exp(-c)).astype(BF16)
        qdb = (qh * jnp.exp(r)).astype(BF16)
        kib = (kh * jnp.exp(-r)).astype(BF16)
        kwf = (kh * jnp.exp(per_chunk_row(c, t - 1) - c)).astype(BF16)
        qdb_ref[:, ks] = qdb
        kwb_ref[:, ks] = (kh * jnp.exp(per_chunk_row(r, 0) - r)).astype(BF16)
        chunks = [slice(s * t, (s + 1) * t) for s in range(n_sub)]
        vhs = [v_ref[rs, vs].astype(BF16) for rs in chunks]
        atts = [(jnp.where(lower, _dot_nt(qdf[rs], kif[rs]), 0.0)
                 + jnp.where(upper, _dot_nt(qdb[rs], kib[rs]), 0.0)).astype(BF16) for rs in chunks]
        incs = [_dot_tn(vh, kwf[rs]) for vh, rs in zip(vhs, chunks)]
        states = [sf_s[h]]
        for s in range(n_sub):
            decb_ref[s, :, ks] = jnp.exp(r[s * t:s * t + 1, :])
            states.append(states[s] * jnp.exp(c[s * t + t - 1:s * t + t, :]) + incs[s])
        sf_s[h] = states[n_sub]
        for s, rs in enumerate(chunks):
            oa_ref[rs, vs] = (jnp.dot(atts[s], vhs[s], preferred_element_type=F32)
                              + _dot_nt(qdf[rs], states[s]))

    @pl.when(i == n_steps - 1)
    def _():
        sff_ref[...] = sf_s[...]


def _gla_bwd_kernel(qdb_ref, kwb_ref, decb_ref, v_ref, oa_ref, g_ref, gn_ref, sb0_ref,
                    os_ref, sbf_ref, sb_s):
    i = pl.program_id(1)
    n_steps = pl.num_programs(1)
    t = GLA_CHUNK
    n_sub = qdb_ref.shape[0] // t
    dk, dv = GLA_KEY_DIM, GLA_VAL_DIM

    @pl.when(i == 0)
    def _():
        sb_s[...] = sb0_ref[...]

    for h in range(GLA_HEADS):
        ks = slice(h * dk, (h + 1) * dk)
        vs = slice(h * dv, (h + 1) * dv)
        state = sb_s[h]
        for s in reversed(range(n_sub)):
            rs = slice(s * t, (s + 1) * t)
            o = oa_ref[rs, vs] + _dot_nt(qdb_ref[rs, ks], state)
            state = state * decb_ref[s, :, ks] + _dot_tn(v_ref[rs, vs], kwb_ref[rs, ks])
            os_ref[rs, vs] = (_head_norm(o, gn_ref[:, vs]) * _silu(g_ref[rs, vs].astype(F32))).astype(os_ref.dtype)
        sb_s[h] = state

    @pl.when(i == n_steps - 1)
    def _():
        sbf_ref[...] = sb_s[...]


GLA_STATE_SHAPE = (GLA_HEADS, GLA_VAL_DIM, GLA_KEY_DIM)


def _gla_fwd(q, k, v, small, wg, gb, sf0, rows):
    bsz, length, _ = q.shape
    rows = _row_tile(length, rows)
    n_steps = length // rows
    n_sub = rows // GLA_CHUNK
    fwd = lambda b, i: (b, i, 0)
    const2 = lambda b, i: (0, 0)
    state = lambda b, i: (b, 0, 0, 0)
    blk = lambda w: pl.BlockSpec((None, rows, w), fwd)
    state_spec = pl.BlockSpec((None,) + GLA_STATE_SHAPE, state)
    return pl.pallas_call(
        _gla_fwd_kernel,
        grid=(bsz, n_steps),
        in_specs=[blk(GLA_K), blk(GLA_K), blk(GLA_V), blk(SMALL_W),
                  pl.BlockSpec((SMALL_W, 2 * GLA_K), const2),
                  pl.BlockSpec((1, 2 * GLA_K), const2),
                  state_spec],
        out_specs=[blk(GLA_V), blk(GLA_K), blk(GLA_K),
                   pl.BlockSpec((None, n_sub, 1, GLA_K), lambda b, i: (b, i, 0, 0)),
                   state_spec],
        out_shape=[jax.ShapeDtypeStruct((bsz, length, GLA_V), F32),
                   jax.ShapeDtypeStruct((bsz, length, GLA_K), BF16),
                   jax.ShapeDtypeStruct((bsz, length, GLA_K), BF16),
                   jax.ShapeDtypeStruct((bsz, length // GLA_CHUNK, 1, GLA_K), F32),
                   jax.ShapeDtypeStruct((bsz,) + GLA_STATE_SHAPE, F32)],
        scratch_shapes=[pltpu.VMEM(GLA_STATE_SHAPE, F32)],
        compiler_params=_params("arbitrary", "arbitrary"),
        name="gla_fwd",
    )(q, k, v, small, wg, gb, sf0)


def _gla_bwd(qdb, kwb, decb, v, oa, g, gla_norm, sb0, rows):
    bsz, length, _ = oa.shape
    rows = _row_tile(length, rows)
    n_steps = length // rows
    n_sub = rows // GLA_CHUNK
    rev = lambda b, i: (b, n_steps - 1 - i, 0)
    const2 = lambda b, i: (0, 0)
    state = lambda b, i: (b, 0, 0, 0)
    blk = lambda w: pl.BlockSpec((None, rows, w), rev)
    state_spec = pl.BlockSpec((None,) + GLA_STATE_SHAPE, state)
    return pl.pallas_call(
        _gla_bwd_kernel,
        grid=(bsz, n_steps),
        in_specs=[blk(GLA_K), blk(GLA_K),
                  pl.BlockSpec((None, n_sub, 1, GLA_K), lambda b, i: (b, n_steps - 1 - i, 0, 0)),
                  blk(GLA_V), blk(GLA_V), blk(GLA_V),
                  pl.BlockSpec((1, GLA_V), const2),
                  state_spec],
        out_specs=[blk(GLA_V), state_spec],
        out_shape=[jax.ShapeDtypeStruct((bsz, length, GLA_V), BF16),
                   jax.ShapeDtypeStruct((bsz,) + GLA_STATE_SHAPE, F32)],
        scratch_shapes=[pltpu.VMEM(GLA_STATE_SHAPE, F32)],
        compiler_params=_params("arbitrary", "arbitrary"),
        name="gla_bwd",
    )(qdb, kwb, decb, v, oa, g, gla_norm.reshape(1, GLA_V), sb0)


def _out_proj_kernel(x_ref, gate_ref, *refs):
    *in_refs, w_ref, y_ref = refs
    acc = None
    off = 0
    for ref in in_refs:
        n = ref.shape[-1]
        part = jnp.dot(ref[...], w_ref[off:off + n, :], preferred_element_type=F32)
        acc = part if acc is None else acc + part
        off += n
    y_ref[...] = x_ref[...] + gate_ref[...] * acc


def _out_proj(x, gate, mixed, w_bf16, tm, name):
    bsz, length, d = x.shape
    tm = _row_tile(length, tm)
    row = lambda b, i: (b, i, 0)
    return pl.pallas_call(
        _out_proj_kernel,
        grid=(bsz, length // tm),
        in_specs=([pl.BlockSpec((None, tm, d), row),
                   pl.BlockSpec((None, 1, d), lambda b, i: (b, 0, 0))]
                  + [pl.BlockSpec((None, tm, m.shape[-1]), row) for m in mixed]
                  + [pl.BlockSpec(w_bf16.shape, lambda b, i: (0, 0), pipeline_mode=pl.Buffered(1))]),
        out_specs=pl.BlockSpec((None, tm, d), row),
        out_shape=jax.ShapeDtypeStruct((bsz, length, d), F32),
        compiler_params=_params("parallel", "parallel"),
        name=name,
    )(x, gate, *mixed, w_bf16)


def _attn_kernel(sink_ref, q_ref, k_ref, v_ref, kc_ref, vc_ref, g_ref, o_ref):
    j = pl.program_id(1)
    i = pl.program_id(2)
    length = k_ref.shape[0]
    blk = ATT_BLOCK
    band = 3 * blk
    dh = ATT_HEAD_DIM
    n_q = q_ref.shape[0] // blk
    n_ctx = kc_ref.shape[0]
    row_minus_col = (lax.broadcasted_iota(jnp.int32, (blk, band), 0)
                     - lax.broadcasted_iota(jnp.int32, (blk, band), 1))
    sink = jnp.concatenate([jnp.full((blk, LANES), sink_ref[j * ATT_GROUP + g] * LOG2E, F32)
                            for g in range(ATT_GROUP)], axis=0)
    kc = kc_ref[...]
    vc_ext = jnp.concatenate([vc_ref[...], jnp.ones((n_ctx, dh), BF16)], axis=1)
    ones_band = jnp.ones((band, dh), BF16)

    def window_start(qb):
        blk_idx = i * n_q + qb
        return blk_idx, pl.multiple_of(jnp.clip((blk_idx - 1) * blk, 0, length - band), blk)

    def scores(qb):
        _, start = window_start(qb)
        rs = slice(qb * blk, (qb + 1) * blk)
        q = jnp.concatenate([q_ref[rs, g * dh:(g + 1) * dh] for g in range(ATT_GROUP)], axis=0)
        return _dot_nt(q, k_ref[pl.ds(start, band), :]), _dot_nt(q, kc)

    pending = scores(0)
    for qb in range(n_q):
        s_band, s_ctx = pending
        if qb + 1 < n_q:
            pending = scores(qb + 1)
        blk_idx, start = window_start(qb)
        inside = jnp.abs(row_minus_col + (blk_idx * blk - start)) <= WINDOW
        v_all = jnp.concatenate([jnp.concatenate([v_ref[pl.ds(start, band), :], ones_band], axis=1),
                                 vc_ext], axis=0)
        rs = slice(qb * blk, (qb + 1) * blk)
        cols = []
        for c in range(band // LANES):
            ls = slice(c * LANES, (c + 1) * LANES)
            cols.append(jnp.concatenate(
                [jnp.where(inside[:, ls], s_band[g * blk:(g + 1) * blk, ls], float("-inf"))
                 for g in range(ATT_GROUP)], axis=0))
        for c in range(n_ctx // LANES):
            cols.append(s_ctx[:, c * LANES:(c + 1) * LANES])
        m = jnp.maximum(sink, jnp.max(functools.reduce(jnp.maximum, cols), axis=-1, keepdims=True))
        p = jnp.concatenate([jnp.exp2(col - m) for col in cols], axis=1).astype(BF16)
        acc = jnp.dot(p, v_all, preferred_element_type=F32)
        out = acc[:, 0:dh] / (acc[:, dh:2 * dh] + jnp.exp2(sink - m))
        for g in range(ATT_GROUP):
            cs = slice(g * dh, (g + 1) * dh)
            o_ref[rs, cs] = (out[g * blk:(g + 1) * blk, :] * g_ref[rs, cs]).astype(o_ref.dtype)


def _attention(sink, q, k, v, kc, vc, gate):
    bsz, length, _ = q.shape
    n_ctx = kc.shape[1]
    rows = _row_tile(length, ATT_Q_BLOCKS * ATT_BLOCK)
    gw = ATT_GROUP * ATT_HEAD_DIM
    qmap = lambda b, j, i, s: (b, i, j)
    kvmap = lambda b, j, i, s: (b, 0, j)
    grid_spec = pltpu.PrefetchScalarGridSpec(
        num_scalar_prefetch=1,
        grid=(bsz, ATT_KV_HEADS, length // rows),
        in_specs=[pl.BlockSpec((None, rows, gw), qmap),
                  pl.BlockSpec((None, length, ATT_HEAD_DIM), kvmap),
                  pl.BlockSpec((None, length, ATT_HEAD_DIM), kvmap),
                  pl.BlockSpec((None, n_ctx, ATT_HEAD_DIM), kvmap),
                  pl.BlockSpec((None, n_ctx, ATT_HEAD_DIM), kvmap),
                  pl.BlockSpec((None, rows, gw), qmap)],
        out_specs=pl.BlockSpec((None, rows, gw), qmap),
    )
    return pl.pallas_call(
        _attn_kernel,
        grid_spec=grid_spec,
        out_shape=jax.ShapeDtypeStruct((bsz, length, ATT_W), BF16),
        compiler_params=_params("parallel", "parallel", "arbitrary"),
        name="attention",
    )(sink, q, k, v, kc, vc, gate)


E_IN_SIZES = (SSD_WIDTH, CONV_CH, 2 * SSD_HEADS, GLA_K, GLA_K, GLA_V, GLA_V, 2 * GLA_RANK)
E_IN = sum(E_IN_SIZES)
E_ALIGNED = SSD_WIDTH + CONV_CH
E_SHIFT = 2 * SSD_HEADS
E_MAIN = 2 * GLA_K + 2 * GLA_V
E_OUT = E_ALIGNED + E_MAIN + SMALL_W
PREP_ROWS = 256
O_IN = 2 * ATT_KV_W + 2 * ATT_W


def _prep_even_kernel(w_ref, shift_ref, small_ref, o_ref):
    rows = w_ref.shape[0]
    o_ref[:, 0:E_ALIGNED] = w_ref[:, 0:E_ALIGNED].astype(BF16)

    def tile(c0):
        if c0 + LANES <= E_IN:
            return w_ref[:, c0:c0 + LANES].astype(BF16)
        return jnp.concatenate([w_ref[:, c0:E_IN].astype(BF16), jnp.zeros((rows, c0 + LANES - E_IN), BF16)],
                               axis=1)

    for j in range(E_MAIN // LANES):
        c0 = E_ALIGNED + j * LANES
        pair = jnp.concatenate([tile(c0), tile(c0 + LANES)], axis=1)
        o_ref[:, c0:c0 + LANES] = jnp.dot(pair, shift_ref[...], preferred_element_type=F32).astype(BF16)
    ends = jnp.concatenate([tile(E_ALIGNED), tile(E_ALIGNED + E_MAIN)], axis=1)
    o_ref[:, E_ALIGNED + E_MAIN:E_OUT] = jnp.dot(ends, small_ref[...], preferred_element_type=F32).astype(BF16)


def _prep_even(w_in):
    d = w_in.shape[0]
    r = np.arange(2 * LANES)[:, None]
    c = np.arange(LANES)[None, :]
    shift = jnp.asarray(r == c + E_SHIFT, BF16)
    lr0 = E_SHIFT
    small = jnp.asarray(((c < E_SHIFT) & (r == c))
                        | ((c >= E_SHIFT) & (c < E_SHIFT + 2 * GLA_RANK) & (r == LANES + lr0 + c - E_SHIFT)), BF16)
    const = lambda i: (0, 0)
    return pl.pallas_call(
        _prep_even_kernel,
        grid=(d // PREP_ROWS,),
        in_specs=[pl.BlockSpec((PREP_ROWS, E_IN), lambda i: (i, 0)),
                  pl.BlockSpec((2 * LANES, LANES), const), pl.BlockSpec((2 * LANES, LANES), const)],
        out_specs=pl.BlockSpec((PREP_ROWS, E_OUT), lambda i: (i, 0)),
        out_shape=jax.ShapeDtypeStruct((d, E_OUT), BF16),
        compiler_params=_params("parallel"),
        name="prep_even",
    )(w_in, shift, small)


def _prep_odd_kernel(w_ref, perm_ref, o_ref):
    def copy(c0, width):
        o_ref[:, c0:c0 + width] = w_ref[:, c0:c0 + width].astype(BF16)

    def permute(c0, width):
        for c in range(c0, c0 + width, 2 * LANES):
            o_ref[:, c:c + 2 * LANES] = jnp.dot(w_ref[:, c:c + 2 * LANES].astype(BF16), perm_ref[...],
                                                preferred_element_type=F32).astype(BF16)

    permute(0, ATT_KV_W)
    copy(ATT_KV_W, ATT_KV_W)
    permute(2 * ATT_KV_W, ATT_W)
    copy(2 * ATT_KV_W + ATT_W, ATT_W)


def _rope_perm(n):
    out = np.arange(n)
    head, rem = out // ATT_HEAD_DIM, out % ATT_HEAD_DIM
    half, axis, f = rem // (2 * ROPE_FREQS), (rem // ROPE_FREQS) % 2, rem % ROPE_FREQS
    return head * ATT_HEAD_DIM + axis * 2 * ROPE_FREQS + half * ROPE_FREQS + f


def _prep_odd(w_in):
    d = w_in.shape[0]
    src = _rope_perm(2 * LANES)
    perm = jnp.asarray(np.arange(2 * LANES)[:, None] == src[None, :], BF16)
    return pl.pallas_call(
        _prep_odd_kernel,
        grid=(d // PREP_ROWS,),
        in_specs=[pl.BlockSpec((PREP_ROWS, O_IN), lambda i: (i, 0)),
                  pl.BlockSpec((2 * LANES, 2 * LANES), lambda i: (0, 0))],
        out_specs=pl.BlockSpec((PREP_ROWS, O_IN), lambda i: (i, 0)),
        out_shape=jax.ShapeDtypeStruct((d, O_IN), BF16),
        compiler_params=_params("parallel"),
        name="prep_odd",
    )(w_in, perm)


EVEN_WIDTHS = ((SSD_WIDTH, BF16), (CONV_CH, BF16), (GLA_K, BF16), (GLA_K, BF16), (GLA_V, BF16), (GLA_V, BF16),
               (SMALL_W, F32))


def _expansion_matrices():
    rows = np.arange(SMALL_W)[:, None]
    heads = (np.arange(SSD_WIDTH) // SSD_HEAD_DIM)[None, :]
    ef = (rows == heads).astype(np.float32)
    eb = (rows - SSD_HEADS == heads).astype(np.float32)
    return jnp.asarray(ef, BF16), jnp.asarray(eb, BF16)


def _pad_lanes(v, width):
    return jnp.pad(v, ((0, 0), (0, width - v.shape[1])))


def _rope_tables(length):
    rows = length // GRID_W
    row = np.repeat(np.arange(rows, dtype=np.float64), GRID_W)
    col = np.tile(np.arange(GRID_W, dtype=np.float64), rows)
    inv = 1.0 / (ROPE_BASE ** (np.arange(ROPE_FREQS, dtype=np.float64) / ROPE_FREQS))
    ang_r = row[:, None] * inv
    ang_c = col[:, None] * inv
    cos = np.concatenate([np.cos(ang_r), np.cos(ang_c), np.cos(ang_r), np.cos(ang_c)], axis=1)
    sin = np.concatenate([-np.sin(ang_r), -np.sin(ang_c), np.sin(ang_r), np.sin(ang_c)], axis=1)
    return jnp.asarray(cos, F32), jnp.asarray(sin, F32)


def _rope_head_layout(v):
    return v[_rope_perm(ATT_HEAD_DIM)]


def _mod_rows(mod, rows, bsz, d):
    picked = jnp.broadcast_to(mod[rows], (bsz, 3 * d)) if isinstance(rows, int) else mod[rows]
    return [picked[:, None, j * d:(j + 1) * d] for j in range(3)]


PROJ_ROWS = 512
OUT_ROWS = 512
ATT_Q_BLOCKS = 16
GLA_ROWS = 512
GLA_BWD_ROWS = 512
GLA_CUMSUM_ROWS = 256
SSD_FWD_ROWS = 4 * SSD_CHUNK
SSD_BWD_ROWS = 512
CVEC_ROWS = SUBLANES


def kernel(x, c, ctx, c_ctx, e_norm, e_mod_w, e_mod_b, e_w_in, e_conv_w, e_conv_b, e_dt_bias, e_a_log,
           e_d_skip, e_ssd_norm, e_gla_gate_w, e_gla_gate_b, e_gla_norm, e_w_out, o_norm, o_mod_w, o_mod_b,
           o_w_in, o_q_norm, o_k_norm, o_sink, o_w_out):
    bsz, length, d = x.shape
    n_ctx = ctx.shape[1]
    assert e_norm.shape[0] == 1 and o_norm.shape[0] == 1, "two-layer block only"
    assert length % SSD_CHUNK == 0 and n_ctx % SSD_CHUNK == 0 and length >= 3 * ATT_BLOCK
    assert bsz + 1 <= CVEC_ROWS

    cvecs = jnp.zeros((CVEC_ROWS, d), F32).at[:bsz].set(c).at[bsz].set(c_ctx)
    lat_rows = slice(0, bsz)

    mod = _adaln(cvecs, e_mod_w[0], e_mod_b[0])
    shift, scale, gate = _mod_rows(mod, lat_rows, bsz, d)
    c_shift, c_scale, c_gate = _mod_rows(mod, bsz, bsz, d)
    w_in = _prep_even(e_w_in[0])
    w_out = e_w_out[0].astype(BF16)
    ef, eb = _expansion_matrices()
    nh2 = 2 * SSD_HEADS
    bias_flat = e_dt_bias[0].reshape(1, nh2)
    alog_flat = e_a_log[0].reshape(1, nh2)
    bias_row, alog_row = _pad_lanes(bias_flat, SMALL_W), _pad_lanes(alog_flat, SMALL_W)
    bias_col, alog_col = bias_flat.reshape(nh2, 1), alog_flat.reshape(nh2, 1)
    dskip_row = jnp.repeat(e_d_skip[0], SSD_HEAD_DIM).reshape(1, SSD_WIDTH)
    conv_b = e_conv_b[0].reshape(1, CONV_CH)
    wg = jnp.zeros((SMALL_W, 2 * GLA_K), F32)
    wg = wg.at[nh2:nh2 + GLA_RANK, 0:GLA_K].set(e_gla_gate_w[0, 0])
    wg = wg.at[nh2 + GLA_RANK:nh2 + 2 * GLA_RANK, GLA_K:2 * GLA_K].set(e_gla_gate_w[0, 1])
    gb = e_gla_gate_b[0].reshape(1, 2 * GLA_K)

    def mixers(stream, sc, sh, ssd_init, gla_init):
        z, xbc, q, k, v, g, small = _proj_even(stream, e_norm[0], sc, sh, w_in, EVEN_WIDTHS, PROJ_ROWS)
        ya, cm, bt, wxb, p3, ssd_f = _ssd_fwd(xbc, small, e_conv_w[0], conv_b, bias_row, bias_col,
                                              alog_row, alog_col, dskip_row, ef, eb, ssd_init[0])
        ys, ssd_b = _ssd_bwd(cm, bt, wxb, p3, ya, z, eb, e_ssd_norm[0], ssd_init[1])
        oa, qdb, kwb, decb, gla_f = _gla_fwd(q, k, v, small, wg, gb, gla_init[0], GLA_ROWS)
        os_, gla_b = _gla_bwd(qdb, kwb, decb, v, oa, g, e_gla_norm[0], gla_init[1], GLA_BWD_ROWS)
        return (ys, os_), (ssd_f, ssd_b), (gla_f, gla_b)

    ssd0 = jnp.zeros((bsz,) + SSD_STATE_SHAPE, F32)
    gla0 = jnp.zeros((bsz,) + GLA_STATE_SHAPE, F32)
    ctx_mix, ssd_fin, gla_fin = mixers(ctx, c_scale, c_shift, (ssd0, ssd0), (gla0, gla0))
    lat_mix, _, _ = mixers(x, scale, shift, ssd_fin, gla_fin)
    x = _out_proj(x, gate, lat_mix, w_out, OUT_ROWS, "out_even")
    xc = _out_proj(ctx, c_gate, ctx_mix, w_out, OUT_ROWS, "out_even")

    mod = _adaln(cvecs, o_mod_w[0], o_mod_b[0])
    shift, scale, gate = _mod_rows(mod, lat_rows, bsz, d)
    c_shift, c_scale, _ = _mod_rows(mod, bsz, bsz, d)
    w_in = _prep_odd(o_w_in[0])
    q_norm = _rope_head_layout(o_q_norm[0])
    k_norm = _rope_head_layout(o_k_norm[0])
    cos, sin = _rope_tables(length)
    no_rot = (jnp.ones((n_ctx, ATT_HEAD_DIM), F32), jnp.zeros((n_ctx, ATT_HEAD_DIM), F32))
    kc, vc = _proj_odd(xc, o_norm[0], c_scale, c_shift, w_in[:, :2 * ATT_KV_W], *no_rot,
                       q_norm, k_norm, False, PROJ_ROWS)
    k, v, q, g = _proj_odd(x, o_norm[0], scale, shift, w_in, cos, sin, q_norm, k_norm, True, PROJ_ROWS)
    o = _attention(o_sink[0].astype(F32), q, k, v, kc, vc, g)
    return _out_proj(x, gate, (o,), o_w_out[0].astype(BF16), OUT_ROWS, "out_odd")
```

```python
import functools

import jax
import jax.numpy as jnp
import numpy as np
from jax import lax
from jax.experimental import pallas as pl
from jax.experimental.pallas import tpu as pltpu

F32 = jnp.float32
BF16 = jnp.bfloat16

GRID_W = 64
SSD_HEADS = 16
SSD_HEAD_DIM = 64
SSD_WIDTH = SSD_HEADS * SSD_HEAD_DIM
SSD_GROUPS = 2
SSD_STATE = 128
SSD_CHUNK = 128
CONV_K = 5
CONV_CH = SSD_WIDTH + 2 * SSD_GROUPS * SSD_STATE
GLA_HEADS = 4
GLA_KEY_DIM = 128
GLA_VAL_DIM = 256
GLA_K = GLA_HEADS * GLA_KEY_DIM
GLA_V = GLA_HEADS * GLA_VAL_DIM
GLA_RANK = 16
GLA_GATE_NORMALIZER = 16.0
GLA_CHUNK = 64
ATT_HEADS = 16
ATT_KV_HEADS = 4
ATT_GROUP = ATT_HEADS // ATT_KV_HEADS
ATT_HEAD_DIM = 128
ATT_W = ATT_HEADS * ATT_HEAD_DIM
ATT_KV_W = ATT_KV_HEADS * ATT_HEAD_DIM
WINDOW = 128
ATT_BLOCK = 128
ROPE_BASE = 10000.0
ROPE_FREQS = ATT_HEAD_DIM // 4
NORM_EPS = 1e-6
LOG2E = 1.4426950408889634

LANES = 128
SUBLANES = 8
VMEM_LIMIT_BYTES = 56 * 1024 * 1024

SSD_GROUP_W = SSD_WIDTH // SSD_GROUPS
SSD_BC_W = SSD_GROUPS * SSD_STATE
SMALL_W = LANES
HALO = 2 * SUBLANES
HEADS_PER_DOT = 4
CONV_ROW_STRIDE = 2 * SUBLANES + 1
CONV_OUT_ROWS = CONV_ROW_STRIDE * SUBLANES
assert SSD_CHUNK <= CONV_OUT_ROWS <= SSD_CHUNK + HALO - (CONV_K - 1) // 2


def _dot(a, b):
    return jnp.dot(a.astype(BF16), b.astype(BF16), preferred_element_type=F32)


def _dot_nt(a, b):
    return lax.dot_general(a.astype(BF16), b.astype(BF16), (((1,), (1,)), ((), ())),
                           preferred_element_type=F32)


def _dot_tn(a, b):
    return lax.dot_general(a.astype(BF16), b.astype(BF16), (((0,), (0,)), ((), ())),
                           preferred_element_type=F32)


def _split(v):
    hi = v.astype(BF16)
    lo = (v - hi.astype(F32)).astype(BF16)
    return hi, lo


def _dot_split_lhs(v, m):
    hi, lo = _split(v)
    return (jnp.dot(hi, m, preferred_element_type=F32) + jnp.dot(lo, m, preferred_element_type=F32))


def _dot_split_rhs(m, v):
    hi, lo = _split(v)
    return (jnp.dot(m, hi, preferred_element_type=F32) + jnp.dot(m, lo, preferred_element_type=F32))


def _dot3(a, b):
    ah, al = _split(a)
    bh, bl = _split(b)
    return (jnp.dot(ah, bh, preferred_element_type=F32) + jnp.dot(al, bh, preferred_element_type=F32)
            + jnp.dot(ah, bl, preferred_element_type=F32))


def _silu(v):
    h = 0.5 * v
    return h + h * jnp.tanh(h)


def _softplus(v):
    return jnp.maximum(v, 0.0) + jnp.log(1.0 + jnp.exp(-jnp.abs(v)))


def _log_sigmoid(v):
    return jnp.minimum(v, 0.0) - jnp.log(1.0 + jnp.exp(-jnp.abs(v)))


def _tri(n):
    row = lax.broadcasted_iota(jnp.int32, (n, n), 0)
    col = lax.broadcasted_iota(jnp.int32, (n, n), 1)
    return row >= col, col >= row


def _params(*sem):
    return pltpu.CompilerParams(dimension_semantics=sem, vmem_limit_bytes=VMEM_LIMIT_BYTES)


def _adaln_kernel(c_ref, w_ref, b_ref, o_ref):
    o_ref[...] = _dot3(_silu(c_ref[...]), w_ref[...]) + b_ref[...]


def _adaln(cvecs, w, b):
    rows, d = cvecs.shape
    n = w.shape[1]
    tn = 1024
    return pl.pallas_call(
        _adaln_kernel,
        grid=(n // tn,),
        in_specs=[pl.BlockSpec((rows, d), lambda j: (0, 0)),
                  pl.BlockSpec((d, tn), lambda j: (0, j)),
                  pl.BlockSpec((1, tn), lambda j: (0, j))],
        out_specs=pl.BlockSpec((rows, tn), lambda j: (0, j)),
        out_shape=jax.ShapeDtypeStruct((rows, n), F32),
        compiler_params=_params("parallel"),
        name="adaln",
    )(cvecs, w, b.reshape(1, n))


def _modulated_norm(x, g, sc, sh):
    r = lax.rsqrt(jnp.mean(x * x, axis=-1, keepdims=True) + NORM_EPS)
    return ((x * r) * g) * (1.0 + sc) + sh


def _store_cols(h, w_ref, off, ref, act=None):
    n = ref.shape[-1]
    for c0 in range(0, n, 512):
        c1 = min(n, c0 + 512)
        t = jnp.dot(h, w_ref[:, off + c0:off + c1], preferred_element_type=F32)
        ref[:, c0:c1] = (t if act is None else act(t)).astype(ref.dtype)
    return off + n


def _proj_even_kernel(x_ref, g_ref, sc_ref, sh_ref, w_ref, *out_refs):
    h = _modulated_norm(x_ref[...], g_ref[...], sc_ref[...], sh_ref[...]).astype(BF16)
    off = 0
    for ref in out_refs:
        off = _store_cols(h, w_ref, off, ref)


def _head_norm(t, gain):
    r = lax.rsqrt(jnp.mean(t * t, axis=-1, keepdims=True) + NORM_EPS)
    return (t * r) * gain


def _rope(t, cos, sin_signed):
    return t * cos + pltpu.roll(t, ATT_HEAD_DIM // 2, 1) * sin_signed


def _proj_odd_kernel(x_ref, g_ref, sc_ref, sh_ref, w_ref, cos_ref, sin_ref, qn_ref, kn_ref,
                     k_ref, v_ref, *qg_refs):
    h = _modulated_norm(x_ref[...], g_ref[...], sc_ref[...], sh_ref[...]).astype(BF16)
    cos = cos_ref[...]
    sin = sin_ref[...]
    scale = ATT_HEAD_DIM ** -0.5 * LOG2E

    width = HEADS_PER_DOT * ATT_HEAD_DIM
    dh = ATT_HEAD_DIM
    pr = lax.broadcasted_iota(jnp.int32, (2 * dh, 2 * dh), 0) // dh
    pc = lax.broadcasted_iota(jnp.int32, (2 * dh, 2 * dh), 1) // dh
    head_ones = jnp.where(pr == pc, 1.0, 0.0).astype(BF16)

    def project(col):
        return jnp.dot(h, w_ref[:, col:col + width], preferred_element_type=F32)

    def finish_heads(t4, ref, j0, gain, out_scale):
        for j in range(0, HEADS_PER_DOT, 2):
            t2 = t4[:, j * dh:(j + 2) * dh]
            mean_sq = jnp.dot((t2 * t2).astype(BF16), head_ones, preferred_element_type=F32) * (1.0 / dh)
            t2 = t2 * lax.rsqrt(mean_sq + NORM_EPS)
            for jj in range(2):
                t = _rope(t2[:, jj * dh:(jj + 1) * dh] * gain, cos, sin)
                c0 = (j0 + j + jj) * dh
                ref[:, c0:c0 + dh] = (t if out_scale is None else t * out_scale).astype(ref.dtype)

    work = []
    if qg_refs:
        q_ref, gate_ref = qg_refs
        for j0 in range(0, ATT_HEADS, HEADS_PER_DOT):
            c0 = j0 * dh

            def finish_gate(t4, c0=c0):
                gate_ref[:, c0:c0 + width] = _silu(t4)

            work.append((2 * ATT_KV_W + ATT_W + c0, finish_gate))
            work.append((2 * ATT_KV_W + c0,
                         functools.partial(finish_heads, ref=q_ref, j0=j0, gain=qn_ref[...], out_scale=scale)))
    for j0 in range(0, ATT_KV_HEADS, HEADS_PER_DOT):
        work.append((j0 * dh, functools.partial(finish_heads, ref=k_ref, j0=j0, gain=kn_ref[...], out_scale=None)))
    for col, finish in work:
        finish(project(col))
    _store_cols(h, w_ref, ATT_KV_W, v_ref)


def _row_tile(length, want):
    return min(length, want)


def _proj_even(x, norm_g, scale, shift, w_bf16, widths, tm):
    bsz, length, d = x.shape
    tm = _row_tile(length, tm)
    n = w_bf16.shape[1]
    row = lambda b, i: (b, i, 0)
    mod = lambda b, i: (b, 0, 0)
    return pl.pallas_call(
        _proj_even_kernel,
        grid=(bsz, length // tm),
        in_specs=[pl.BlockSpec((None, tm, d), row),
                  pl.BlockSpec((1, d), lambda b, i: (0, 0)),
                  pl.BlockSpec((None, 1, d), mod),
                  pl.BlockSpec((None, 1, d), mod),
                  pl.BlockSpec((d, n), lambda b, i: (0, 0), pipeline_mode=pl.Buffered(1))],
        out_specs=[pl.BlockSpec((None, tm, wd), row) for wd, _ in widths],
        out_shape=[jax.ShapeDtypeStruct((bsz, length, wd), dt) for wd, dt in widths],
        compiler_params=_params("parallel", "parallel"),
        name="proj_even",
    )(x, norm_g.reshape(1, d), scale, shift, w_bf16)


def _proj_odd(x, norm_g, scale, shift, w_bf16, cos, sin, q_norm, k_norm, with_queries, tm):
    bsz, length, d = x.shape
    tm = _row_tile(length, tm)
    n = w_bf16.shape[1]
    row = lambda b, i: (b, i, 0)
    mod = lambda b, i: (b, 0, 0)
    const = lambda b, i: (0, 0)
    widths = [(ATT_KV_W, BF16), (ATT_KV_W, BF16)]
    if with_queries:
        widths += [(ATT_W, BF16), (ATT_W, F32)]
    return pl.pallas_call(
        _proj_odd_kernel,
        grid=(bsz, length // tm),
        in_specs=[pl.BlockSpec((None, tm, d), row),
                  pl.BlockSpec((1, d), const),
                  pl.BlockSpec((None, 1, d), mod),
                  pl.BlockSpec((None, 1, d), mod),
                  pl.BlockSpec((d, n), const, pipeline_mode=pl.Buffered(1)),
                  pl.BlockSpec((tm, ATT_HEAD_DIM), lambda b, i: (i, 0)),
                  pl.BlockSpec((tm, ATT_HEAD_DIM), lambda b, i: (i, 0)),
                  pl.BlockSpec((1, ATT_HEAD_DIM), const),
                  pl.BlockSpec((1, ATT_HEAD_DIM), const)],
        out_specs=[pl.BlockSpec((None, tm, wd), row) for wd, _ in widths],
        out_shape=[jax.ShapeDtypeStruct((bsz, length, wd), dt) for wd, dt in widths],
        compiler_params=_params("parallel", "parallel"),
        name="proj_odd_q" if with_queries else "proj_odd_kv",
    )(x, norm_g.reshape(1, d), scale, shift, w_bf16, cos, sin,
      q_norm.reshape(1, ATT_HEAD_DIM), k_norm.reshape(1, ATT_HEAD_DIM))


def _conv_silu(win_s, prev_ref, cur_ref, next_ref, has_prev, has_next, cw_ref, cb_ref, x_s):
    rows = cur_ref.shape[0]
    half = (CONV_K - 1) // 2
    for j in range(CONV_CH // LANES):
        ls = slice(j * LANES, (j + 1) * LANES)
        win_s[j, 0:HALO, :] = jnp.where(has_prev, prev_ref[:, ls].astype(F32), 0.0)
        win_s[j, HALO:HALO + rows, :] = cur_ref[:, ls].astype(F32)
        win_s[j, HALO + rows:2 * HALO + rows, :] = jnp.where(has_next, next_ref[:, ls].astype(F32), 0.0)
        taps = [cw_ref[k:k + 1, ls] for k in range(CONV_K)]
        bias = cb_ref[:, ls]
        win = win_s.at[j]
        out = x_s.at[j]
        for c in range(rows // SSD_CHUNK):
            for a in range(CONV_ROW_STRIDE):
                r0 = HALO + c * SSD_CHUNK - half + a
                acc = bias + taps[0] * win[pl.ds(r0, SUBLANES, stride=CONV_ROW_STRIDE), :]
                for k in range(1, CONV_K):
                    acc = acc + taps[k] * win[pl.ds(r0 + k, SUBLANES, stride=CONV_ROW_STRIDE), :]
                out[pl.ds(c * CONV_OUT_ROWS + a, SUBLANES, stride=CONV_ROW_STRIDE), :] = _silu(acc)


def _ssd_fwd_kernel(p_ref, c_ref, n_ref, sm_ref, cw_ref, cb_ref, bias_row_ref, bias_col_ref,
                    alog_row_ref, alog_col_ref, dskip_ref, ef_ref, eb_ref, sf0_ref,
                    ya_ref, cm_ref, bt_ref, wxb_ref, p3_ref, sff_ref,
                    sf_s, win_s, x_s):
    i = pl.program_id(1)
    nc = pl.num_programs(1)
    t = SSD_CHUNK
    nh = SSD_HEADS
    slabs_x = SSD_WIDTH // LANES
    slab_b = slabs_x
    slab_c = slabs_x + SSD_GROUPS

    @pl.when(i == 0)
    def _():
        sf_s[...] = sf0_ref[...]

    lower, upper = _tri(t)
    ltri = jnp.where(lower, 1.0, 0.0).astype(BF16)
    utri = jnp.where(upper, 1.0, 0.0).astype(BF16)
    a_row = -jnp.exp(alog_row_ref[...])
    a_col = -jnp.exp(alog_col_ref[...])
    ef = ef_ref[...]
    eb = eb_ref[...]
    neg_inf = float("-inf")

    _conv_silu(win_s, p_ref, c_ref, n_ref, i > 0, i < nc - 1, cw_ref, cb_ref, x_s)
    n_chunks = sm_ref.shape[0] // t
    lane = lax.broadcasted_iota(jnp.int32, (t, LANES), 1)
    left = lane < SSD_HEAD_DIM
    heads_per_group = nh // SSD_GROUPS
    pairs_per_group = heads_per_group // 2

    def chunk_decays(c):
        rs = slice(c * t, (c + 1) * t)
        sm = sm_ref[rs, :]
        dt = _softplus(sm + bias_row_ref[...])
        dta = dt * a_row
        cs = _dot_split_rhs(ltri, dta)
        rc = _dot_split_rhs(utri, dta)
        dt_t = _softplus(sm.T[0:2 * nh, :] + bias_col_ref[...])
        dta_t = dt_t * a_col
        log2_dt_t = jnp.log2(dt_t)
        f_q = cs * LOG2E
        g_q = rc * LOG2E
        f_k = _dot_split_lhs(dta_t, utri) * LOG2E - log2_dt_t
        g_k = _dot_split_lhs(dta_t, ltri) * LOG2E - log2_dt_t
        dec_f = _dot_split_lhs(jnp.exp2(f_q), ef)
        wgt_f = _dot_split_lhs(dt * jnp.exp(cs[t - 1:t, :] - cs), ef)
        wgt_b = _dot_split_lhs(dt * jnp.exp(rc[0:1, :] - rc), eb)
        p3_ref[rs, :] = jnp.exp2(g_q)
        for p in range(slabs_x):
            ls = slice(p * LANES, (p + 1) * LANES)
            wxb_ref[rs, ls] = (x_s[p, c * CONV_OUT_ROWS:c * CONV_OUT_ROWS + t, :] * wgt_b[:, ls]).astype(BF16)
        return f_q, g_q, f_k, g_k, dec_f, wgt_f

    decays = [chunk_decays(c) for c in range(n_chunks)]

    for c in range(n_chunks):
        f_q, g_q, f_k, g_k, dec_f, wgt_f = decays[c]
        rs = slice(c * t, (c + 1) * t)
        xr = slice(c * CONV_OUT_ROWS, c * CONV_OUT_ROWS + t)

        def decay_matrix(h, cbg):
            lf = jnp.exp2(jnp.where(lower, f_q[:, h:h + 1] - f_k[h:h + 1, :], neg_inf))
            ub = jnp.exp2(jnp.where(upper, g_q[:, nh + h:nh + h + 1] - g_k[nh + h:nh + h + 1, :], neg_inf))
            return cbg * (lf + ub)

        for g in range(SSD_GROUPS):
            cg = x_s[slab_c + g, xr, :]
            bg = x_s[slab_b + g, xr, :]
            gs = slice(g * SSD_GROUP_W, (g + 1) * SSD_GROUP_W)
            cm_ref[rs, g * SSD_STATE:(g + 1) * SSD_STATE] = cg.astype(BF16)
            cbg = _dot_nt(cg, bg)
            state = sf_s[g]
            y_off = _dot(cg, state) * dec_f[:, gs]
            wx = []
            for pair in range(pairs_per_group):
                h0 = g * heads_per_group + 2 * pair
                p = h0 // 2
                ls = slice(p * LANES, (p + 1) * LANES)
                xp = x_s[p, xr, :]
                y = (_dot(decay_matrix(h0, cbg), jnp.where(left, xp, 0.0))
                     + _dot(decay_matrix(h0 + 1, cbg), jnp.where(left, 0.0, xp)))
                ya_ref[rs, ls] = y + y_off[:, pair * LANES:(pair + 1) * LANES] + dskip_ref[:, ls] * xp
                wx.append((xp * wgt_f[:, ls]).astype(BF16))
            bt = bg.T.astype(BF16)
            bt_ref[c, g * SSD_STATE:(g + 1) * SSD_STATE, :] = bt
            sf_s[g] = dec_f[t - 1:t, gs] * state + jnp.dot(bt, jnp.concatenate(wx, axis=1),
                                                           preferred_element_type=F32)

    @pl.when(i == nc - 1)
    def _():
        sff_ref[...] = sf_s[...]


def _ssd_bwd_kernel(cm_ref, bt_ref, wxb_ref, p3_ref, ya_ref, z_ref, eb_ref, sn_ref, sb0_ref,
                    ys_ref, sbf_ref, sb_s):
    i = pl.program_id(1)
    nc = pl.num_programs(1)

    @pl.when(i == 0)
    def _():
        sb_s[...] = sb0_ref[...]

    t = SSD_CHUNK
    n_sub = bt_ref.shape[0]
    dec_b = _dot_split_lhs(p3_ref[...], eb_ref[...])
    for g in range(SSD_GROUPS):
        gs = slice(g * SSD_GROUP_W, (g + 1) * SSD_GROUP_W)
        ss = slice(g * SSD_STATE, (g + 1) * SSD_STATE)
        chunks = [slice(s * t, (s + 1) * t) for s in range(n_sub)]
        incs = [jnp.dot(bt_ref[s, ss, :], wxb_ref[rs, gs], preferred_element_type=F32)
                for s, rs in enumerate(chunks)]
        states = [None] * n_sub
        state = sb_s[g]
        for s in reversed(range(n_sub)):
            states[s] = state
            state = dec_b[s * t:s * t + 1, gs] * state + incs[s]
        sb_s[g] = state
        for s, rs in enumerate(chunks):
            y_off = jnp.dot(cm_ref[rs, ss], states[s].astype(BF16), preferred_element_type=F32) * dec_b[rs, gs]
            y = (ya_ref[rs, gs] + y_off) * _silu(z_ref[rs, gs].astype(F32))
            ys_ref[rs, gs] = _head_norm(y, sn_ref[:, gs]).astype(ys_ref.dtype)

    @pl.when(i == nc - 1)
    def _():
        sbf_ref[...] = sb_s[...]


SSD_STATE_SHAPE = (SSD_GROUPS, SSD_STATE, SSD_GROUP_W)


def _ssd_fwd(xbc, small, conv_w, conv_b, bias_row, bias_col, alog_row, alog_col, dskip_row, ef, eb, sf0):
    bsz, length, _ = xbc.shape
    t = SSD_CHUNK
    rows = _row_tile(length, SSD_FWD_ROWS)
    nc = length // t
    per = rows // HALO
    last_halo = length // HALO - 1

    def cur(b, i): return (b, i, 0)
    def prev(b, i): return (b, jnp.maximum(i * per - 1, 0), 0)
    def nxt(b, i): return (b, jnp.minimum((i + 1) * per, last_halo), 0)
    const2 = lambda b, i: (0, 0)
    state = lambda b, i: (b, 0, 0, 0)

    halo_spec = lambda f: pl.BlockSpec((None, HALO, CONV_CH), f)
    chunk_spec = lambda w: pl.BlockSpec((None, rows, w), cur)
    state_spec = pl.BlockSpec((None,) + SSD_STATE_SHAPE, state)
    return dict(
        grid=(bsz, length // rows),
        in_specs=[halo_spec(prev), chunk_spec(CONV_CH), halo_spec(nxt), chunk_spec(SMALL_W),
                  pl.BlockSpec((CONV_K, CONV_CH), const2),
                  pl.BlockSpec((1, CONV_CH), const2),
                  pl.BlockSpec((1, SMALL_W), const2),
                  pl.BlockSpec((2 * SSD_HEADS, 1), const2),
                  pl.BlockSpec((1, SMALL_W), const2),
                  pl.BlockSpec((2 * SSD_HEADS, 1), const2),
                  pl.BlockSpec((1, SSD_WIDTH), const2),
                  pl.BlockSpec((SMALL_W, SSD_WIDTH), const2),
                  pl.BlockSpec((SMALL_W, SSD_WIDTH), const2),
                  state_spec],
        out_specs=[chunk_spec(SSD_WIDTH), chunk_spec(SSD_BC_W),
                   pl.BlockSpec((None, rows // t, SSD_BC_W, t), lambda b, i: (b, i, 0, 0)),
                   chunk_spec(SSD_WIDTH), chunk_spec(SMALL_W), state_spec],
        out_shape=[jax.ShapeDtypeStruct((bsz, length, SSD_WIDTH), F32),
                   jax.ShapeDtypeStruct((bsz, length, SSD_BC_W), BF16),
                   jax.ShapeDtypeStruct((bsz, nc, SSD_BC_W, t), BF16),
                   jax.ShapeDtypeStruct((bsz, length, SSD_WIDTH), BF16),
                   jax.ShapeDtypeStruct((bsz, length, SMALL_W), F32),
                   jax.ShapeDtypeStruct((bsz,) + SSD_STATE_SHAPE, F32)],
        scratch_shapes=[pltpu.VMEM(SSD_STATE_SHAPE, F32),
                        pltpu.VMEM((CONV_CH // LANES, rows + 2 * HALO, LANES), F32),
                        pltpu.VMEM((CONV_CH // LANES, (rows // t) * CONV_OUT_ROWS, LANES), F32)],
        args=(xbc, xbc, xbc, small, conv_w, conv_b, bias_row, bias_col, alog_row, alog_col, dskip_row, ef, eb, sf0))


def _ssd_bwd(cm, bt, wxb, p3, ya, z, eb, ssd_norm, sb0):
    bsz, length, _ = ya.shape
    t = SSD_CHUNK
    rows = _row_tile(length, SSD_BWD_ROWS)
    n_steps = length // rows
    rev = lambda b, i: (b, n_steps - 1 - i, 0)
    const2 = lambda b, i: (0, 0)
    state = lambda b, i: (b, 0, 0, 0)
    chunk_spec = lambda w: pl.BlockSpec((None, rows, w), rev)
    state_spec = pl.BlockSpec((None,) + SSD_STATE_SHAPE, state)
    return pl.pallas_call(
        _ssd_bwd_kernel,
        grid=(bsz, n_steps),
        in_specs=[chunk_spec(SSD_BC_W),
                  pl.BlockSpec((None, rows // t, SSD_BC_W, t), lambda b, i: (b, n_steps - 1 - i, 0, 0)),
                  chunk_spec(SSD_WIDTH), chunk_spec(SMALL_W), chunk_spec(SSD_WIDTH), chunk_spec(SSD_WIDTH),
                  pl.BlockSpec((SMALL_W, SSD_WIDTH), const2),
                  pl.BlockSpec((1, SSD_WIDTH), const2),
                  state_spec],
        out_specs=[chunk_spec(SSD_WIDTH), state_spec],
        out_shape=[jax.ShapeDtypeStruct((bsz, length, SSD_WIDTH), BF16),
                   jax.ShapeDtypeStruct((bsz,) + SSD_STATE_SHAPE, F32)],
        scratch_shapes=[pltpu.VMEM(SSD_STATE_SHAPE, F32)],
        compiler_params=_params("arbitrary", "arbitrary"),
        name="ssd_bwd",
    )(cm, bt, wxb, p3, ya, z, eb, ssd_norm.reshape(1, SSD_WIDTH), sb0)


def _gla_fwd_kernel(q_ref, k_ref, v_ref, sm_ref, wg_ref, gb_ref, sf0_ref,
                    oa_ref, qdb_ref, kwb_ref, decb_ref, sff_ref, sf_s):
    i = pl.program_id(1)
    n_steps = pl.num_programs(1)
    t = GLA_CHUNK
    rows = q_ref.shape[0]
    n_sub = rows // t
    dk, dv = GLA_KEY_DIM, GLA_VAL_DIM
    qscale = dk ** -0.5
    inv_norm = 1.0 / GLA_GATE_NORMALIZER

    @pl.when(i == 0)
    def _():
        sf_s[...] = sf0_ref[...]

    lower, upper = _tri(t)
    grp = min(rows, GLA_CUMSUM_ROWS)
    row = lax.broadcasted_iota(jnp.int32, (grp, grp), 0)
    col = lax.broadcasted_iota(jnp.int32, (grp, grp), 1)
    diff = row - col
    pos = row % t
    bd_lower = jnp.where(diff >= 0, jnp.where(diff <= pos, 1.0, 0.0), 0.0).astype(BF16)
    bd_upper = jnp.where(diff <= 0, jnp.where(-diff <= t - 1 - pos, 1.0, 0.0), 0.0).astype(BF16)

    hi, lo = _split(sm_ref[...])
    wg_hi, wg_lo = _split(wg_ref[...])
    logits = (jnp.dot(hi, wg_hi, preferred_element_type=F32) + jnp.dot(lo, wg_hi, preferred_element_type=F32)
              + jnp.dot(hi, wg_lo, preferred_element_type=F32)) + gb_ref[...]
    lg = _log_sigmoid(logits) * inv_norm
    groups = [slice(r0, r0 + grp) for r0 in range(0, rows, grp)]
    cs = jnp.concatenate([_dot_split_rhs(bd_lower, lg[gr, 0:GLA_K]) for gr in groups], axis=0)
    rc = jnp.concatenate([_dot_split_rhs(bd_upper, lg[gr, GLA_K:2 * GLA_K]) for gr in groups], axis=0)

    def per_chunk_row(v, offset):
        return jnp.concatenate([jnp.broadcast_to(v[s * t + offset:s * t + offset + 1, :], (t, v.shape[1]))
                                for s in range(n_sub)], axis=0)

    for h in range(GLA_HEADS):
        ks = slice(h * dk, (h + 1) * dk)
        vs = slice(h * dv, (h + 1) * dv)
        c = cs[:, ks]
        r = rc[:, ks]
        qh = q_ref[:, ks].astype(F32) * qscale
        kh = k_ref[:, ks].astype(F32)
        qdf = (qh * jnp.exp(c)).astype(BF16)
        kif = (kh * jnp.exp(-c)).astype(BF16)
        qdb = (qh * jnp.exp(r)).astype(BF16)
        kib = (kh * jnp.exp(-r)).astype(BF16)
        kwf = (kh * jnp.exp(per_chunk_row(c, t - 1) - c)).astype(BF16)
        qdb_ref[:, ks] = qdb
        kwb_ref[:, ks] = (kh * jnp.exp(per_chunk_row(r, 0) - r)).astype(BF16)
        chunks = [slice(s * t, (s + 1) * t) for s in range(n_sub)]
        vhs = [v_ref[rs, vs].astype(BF16) for rs in chunks]
        atts = [(jnp.where(lower, _dot_nt(qdf[rs], kif[rs]), 0.0)
                 + jnp.where(upper, _dot_nt(qdb[rs], kib[rs]), 0.0)).astype(BF16) for rs in chunks]
        incs = [_dot_tn(vh, kwf[rs]) for vh, rs in zip(vhs, chunks)]
        states = [sf_s[h]]
        for s in range(n_sub):
            decb_ref[s, :, ks] = jnp.exp(r[s * t:s * t + 1, :])
            states.append(states[s] * jnp.exp(c[s * t + t - 1:s * t + t, :]) + incs[s])
        sf_s[h] = states[n_sub]
        for s, rs in enumerate(chunks):
            oa_ref[rs, vs] = (jnp.dot(atts[s], vhs[s], preferred_element_type=F32)
                              + _dot_nt(qdf[rs], states[s]))

    @pl.when(i == n_steps - 1)
    def _():
        sff_ref[...] = sf_s[...]


def _gla_bwd_kernel(qdb_ref, kwb_ref, decb_ref, v_ref, oa_ref, g_ref, gn_ref, sb0_ref,
                    os_ref, sbf_ref, sb_s):
    i = pl.program_id(1)
    n_steps = pl.num_programs(1)
    t = GLA_CHUNK
    n_sub = qdb_ref.shape[0] // t
    dk, dv = GLA_KEY_DIM, GLA_VAL_DIM

    @pl.when(i == 0)
    def _():
        sb_s[...] = sb0_ref[...]

    for h in range(GLA_HEADS):
        ks = slice(h * dk, (h + 1) * dk)
        vs = slice(h * dv, (h + 1) * dv)
        state = sb_s[h]
        for s in reversed(range(n_sub)):
            rs = slice(s * t, (s + 1) * t)
            o = oa_ref[rs, vs] + _dot_nt(qdb_ref[rs, ks], state)
            state = state * decb_ref[s, :, ks] + _dot_tn(v_ref[rs, vs], kwb_ref[rs, ks])
            os_ref[rs, vs] = (_head_norm(o, gn_ref[:, vs]) * _silu(g_ref[rs, vs].astype(F32))).astype(os_ref.dtype)
        sb_s[h] = state

    @pl.when(i == n_steps - 1)
    def _():
        sbf_ref[...] = sb_s[...]


GLA_STATE_SHAPE = (GLA_HEADS, GLA_VAL_DIM, GLA_KEY_DIM)


def _gla_fwd(q, k, v, small, wg, gb, sf0, rows):
    bsz, length, _ = q.shape
    rows = _row_tile(length, rows)
    n_steps = length // rows
    n_sub = rows // GLA_CHUNK
    fwd = lambda b, i: (b, i, 0)
    const2 = lambda b, i: (0, 0)
    state = lambda b, i: (b, 0, 0, 0)
    blk = lambda w: pl.BlockSpec((None, rows, w), fwd)
    state_spec = pl.BlockSpec((None,) + GLA_STATE_SHAPE, state)
    return dict(
        grid=(bsz, n_steps),
        in_specs=[blk(GLA_K), blk(GLA_K), blk(GLA_V), blk(SMALL_W),
                  pl.BlockSpec((SMALL_W, 2 * GLA_K), const2),
                  pl.BlockSpec((1, 2 * GLA_K), const2),
                  state_spec],
        out_specs=[blk(GLA_V), blk(GLA_K), blk(GLA_K),
                   pl.BlockSpec((None, n_sub, 1, GLA_K), lambda b, i: (b, i, 0, 0)),
                   state_spec],
        out_shape=[jax.ShapeDtypeStruct((bsz, length, GLA_V), F32),
                   jax.ShapeDtypeStruct((bsz, length, GLA_K), BF16),
                   jax.ShapeDtypeStruct((bsz, length, GLA_K), BF16),
                   jax.ShapeDtypeStruct((bsz, length // GLA_CHUNK, 1, GLA_K), F32),
                   jax.ShapeDtypeStruct((bsz,) + GLA_STATE_SHAPE, F32)],
        scratch_shapes=[pltpu.VMEM(GLA_STATE_SHAPE, F32)],
        args=(q, k, v, small, wg, gb, sf0))


def _sweep_call(body, name, call):
    return pl.pallas_call(
        body,
        grid=call["grid"],
        in_specs=call["in_specs"],
        out_specs=call["out_specs"],
        out_shape=call["out_shape"],
        scratch_shapes=call["scratch_shapes"],
        compiler_params=_params("arbitrary", "arbitrary"),
        name=name,
    )(*call["args"])


def _gla_bwd(qdb, kwb, decb, v, oa, g, gla_norm, sb0, rows):
    bsz, length, _ = oa.shape
    rows = _row_tile(length, rows)
    n_steps = length // rows
    n_sub = rows // GLA_CHUNK
    rev = lambda b, i: (b, n_steps - 1 - i, 0)
    const2 = lambda b, i: (0, 0)
    state = lambda b, i: (b, 0, 0, 0)
    blk = lambda w: pl.BlockSpec((None, rows, w), rev)
    state_spec = pl.BlockSpec((None,) + GLA_STATE_SHAPE, state)
    return pl.pallas_call(
        _gla_bwd_kernel,
        grid=(bsz, n_steps),
        in_specs=[blk(GLA_K), blk(GLA_K),
                  pl.BlockSpec((None, n_sub, 1, GLA_K), lambda b, i: (b, n_steps - 1 - i, 0, 0)),
                  blk(GLA_V), blk(GLA_V), blk(GLA_V),
                  pl.BlockSpec((1, GLA_V), const2),
                  state_spec],
        out_specs=[blk(GLA_V), state_spec],
        out_shape=[jax.ShapeDtypeStruct((bsz, length, GLA_V), BF16),
                   jax.ShapeDtypeStruct((bsz,) + GLA_STATE_SHAPE, F32)],
        scratch_shapes=[pltpu.VMEM(GLA_STATE_SHAPE, F32)],
        compiler_params=_params("arbitrary", "arbitrary"),
        name="gla_bwd",
    )(qdb, kwb, decb, v, oa, g, gla_norm.reshape(1, GLA_V), sb0)


def _out_proj_kernel(x_ref, gate_ref, *refs):
    *in_refs, w_ref, y_ref = refs
    acc = None
    off = 0
    for ref in in_refs:
        n = ref.shape[-1]
        part = jnp.dot(ref[...], w_ref[off:off + n, :], preferred_element_type=F32)
        acc = part if acc is None else acc + part
        off += n
    y_ref[...] = x_ref[...] + gate_ref[...] * acc


def _out_proj(x, gate, mixed, w_bf16, tm, name):
    bsz, length, d = x.shape
    tm = _row_tile(length, tm)
    row = lambda b, i: (b, i, 0)
    return pl.pallas_call(
        _out_proj_kernel,
        grid=(bsz, length // tm),
        in_specs=([pl.BlockSpec((None, tm, d), row),
                   pl.BlockSpec((None, 1, d), lambda b, i: (b, 0, 0))]
                  + [pl.BlockSpec((None, tm, m.shape[-1]), row) for m in mixed]
                  + [pl.BlockSpec(w_bf16.shape, lambda b, i: (0, 0), pipeline_mode=pl.Buffered(1))]),
        out_specs=pl.BlockSpec((None, tm, d), row),
        out_shape=jax.ShapeDtypeStruct((bsz, length, d), F32),
        compiler_params=_params("parallel", "parallel"),
        name=name,
    )(x, gate, *mixed, w_bf16)


def _attn_kernel(sink_ref, q_ref, k_ref, v_ref, kc_ref, vc_ref, g_ref, o_ref):
    j = pl.program_id(1)
    i = pl.program_id(2)
    length = k_ref.shape[0]
    blk = ATT_BLOCK
    band = 3 * blk
    dh = ATT_HEAD_DIM
    n_q = q_ref.shape[0] // blk
    n_ctx = kc_ref.shape[0]
    row_minus_col = (lax.broadcasted_iota(jnp.int32, (blk, band), 0)
                     - lax.broadcasted_iota(jnp.int32, (blk, band), 1))
    sink = jnp.concatenate([jnp.full((blk, LANES), sink_ref[j * ATT_GROUP + g] * LOG2E, F32)
                            for g in range(ATT_GROUP)], axis=0)
    kc = kc_ref[...]
    vc_ext = jnp.concatenate([vc_ref[...], jnp.ones((n_ctx, dh), BF16)], axis=1)
    ones_band = jnp.ones((band, dh), BF16)

    def window_start(qb):
        blk_idx = i * n_q + qb
        return blk_idx, pl.multiple_of(jnp.clip((blk_idx - 1) * blk, 0, length - band), blk)

    def scores(qb):
        _, start = window_start(qb)
        rs = slice(qb * blk, (qb + 1) * blk)
        q = jnp.concatenate([q_ref[rs, g * dh:(g + 1) * dh] for g in range(ATT_GROUP)], axis=0)
        return _dot_nt(q, k_ref[pl.ds(start, band), :]), _dot_nt(q, kc)

    pending = scores(0)
    for qb in range(n_q):
        s_band, s_ctx = pending
        if qb + 1 < n_q:
            pending = scores(qb + 1)
        blk_idx, start = window_start(qb)
        inside = jnp.abs(row_minus_col + (blk_idx * blk - start)) <= WINDOW
        v_all = jnp.concatenate([jnp.concatenate([v_ref[pl.ds(start, band), :], ones_band], axis=1),
                                 vc_ext], axis=0)
        rs = slice(qb * blk, (qb + 1) * blk)
        cols = []
        for c in range(band // LANES):
            ls = slice(c * LANES, (c + 1) * LANES)
            cols.append(jnp.concatenate(
                [jnp.where(inside[:, ls], s_band[g * blk:(g + 1) * blk, ls], float("-inf"))
                 for g in range(ATT_GROUP)], axis=0))
        for c in range(n_ctx // LANES):
            cols.append(s_ctx[:, c * LANES:(c + 1) * LANES])
        m = jnp.maximum(sink, jnp.max(functools.reduce(jnp.maximum, cols), axis=-1, keepdims=True))
        p = jnp.concatenate([jnp.exp2(col - m) for col in cols], axis=1).astype(BF16)
        acc = jnp.dot(p, v_all, preferred_element_type=F32)
        out = acc[:, 0:dh] / (acc[:, dh:2 * dh] + jnp.exp2(sink - m))
        for g in range(ATT_GROUP):
            cs = slice(g * dh, (g + 1) * dh)
            o_ref[rs, cs] = (out[g * blk:(g + 1) * blk, :] * g_ref[rs, cs]).astype(o_ref.dtype)


def _attention(sink, q, k, v, kc, vc, gate):
    bsz, length, _ = q.shape
    n_ctx = kc.shape[1]
    rows = _row_tile(length, ATT_Q_BLOCKS * ATT_BLOCK)
    gw = ATT_GROUP * ATT_HEAD_DIM
    qmap = lambda b, j, i, s: (b, i, j)
    kvmap = lambda b, j, i, s: (b, 0, j)
    grid_spec = pltpu.PrefetchScalarGridSpec(
        num_scalar_prefetch=1,
        grid=(bsz, ATT_KV_HEADS, length // rows),
        in_specs=[pl.BlockSpec((None, rows, gw), qmap),
                  pl.BlockSpec((None, length, ATT_HEAD_DIM), kvmap),
                  pl.BlockSpec((None, length, ATT_HEAD_DIM), kvmap),
                  pl.BlockSpec((None, n_ctx, ATT_HEAD_DIM), kvmap),
                  pl.BlockSpec((None, n_ctx, ATT_HEAD_DIM), kvmap),
                  pl.BlockSpec((None, rows, gw), qmap)],
        out_specs=pl.BlockSpec((None, rows, gw), qmap),
    )
    return pl.pallas_call(
        _attn_kernel,
        grid_spec=grid_spec,
        out_shape=jax.ShapeDtypeStruct((bsz, length, ATT_W), BF16),
        compiler_params=_params("parallel", "parallel", "arbitrary"),
        name="attention",
    )(sink, q, k, v, kc, vc, gate)


E_IN_SIZES = (SSD_WIDTH, CONV_CH, 2 * SSD_HEADS, GLA_K, GLA_K, GLA_V, GLA_V, 2 * GLA_RANK)
E_IN = sum(E_IN_SIZES)
E_ALIGNED = SSD_WIDTH + CONV_CH
E_SHIFT = 2 * SSD_HEADS
E_MAIN = 2 * GLA_K + 2 * GLA_V
E_OUT = E_ALIGNED + E_MAIN + SMALL_W
PREP_ROWS = 256
O_IN = 2 * ATT_KV_W + 2 * ATT_W


def _prep_even_kernel(wt_ref, o_ref):
    cols = wt_ref.shape[1]
    for dst in range(0, E_ALIGNED + E_MAIN, LANES):
        src = dst if dst < E_ALIGNED else dst + E_SHIFT
        o_ref[:, dst:dst + LANES] = wt_ref[src:src + LANES, :].T.astype(BF16)
    small = jnp.concatenate([wt_ref[E_ALIGNED:E_ALIGNED + E_SHIFT, :],
                             wt_ref[E_IN - 2 * GLA_RANK:E_IN, :],
                             jnp.zeros((SMALL_W - E_SHIFT - 2 * GLA_RANK, cols), F32)], axis=0)
    o_ref[:, E_ALIGNED + E_MAIN:E_OUT] = small.T.astype(BF16)


def _prep_even(w_in):
    d = w_in.shape[0]
    return pl.pallas_call(
        _prep_even_kernel,
        grid=(d // PREP_ROWS,),
        in_specs=[pl.BlockSpec((E_IN, PREP_ROWS), lambda i: (0, i))],
        out_specs=pl.BlockSpec((PREP_ROWS, E_OUT), lambda i: (i, 0)),
        out_shape=jax.ShapeDtypeStruct((d, E_OUT), BF16),
        compiler_params=_params("parallel"),
        name="prep_even",
    )(jnp.swapaxes(w_in, 0, 1))


def _prep_odd_kernel(w_ref, perm_ref, o_ref):
    def copy(c0, width):
        o_ref[:, c0:c0 + width] = w_ref[:, c0:c0 + width].astype(BF16)

    def permute(c0, width):
        for c in range(c0, c0 + width, 2 * LANES):
            o_ref[:, c:c + 2 * LANES] = jnp.dot(w_ref[:, c:c + 2 * LANES].astype(BF16), perm_ref[...],
                                                preferred_element_type=F32).astype(BF16)

    permute(0, ATT_KV_W)
    copy(ATT_KV_W, ATT_KV_W)
    permute(2 * ATT_KV_W, ATT_W)
    copy(2 * ATT_KV_W + ATT_W, ATT_W)


def _rope_perm(n):
    out = np.arange(n)
    head, rem = out // ATT_HEAD_DIM, out % ATT_HEAD_DIM
    half, axis, f = rem // (2 * ROPE_FREQS), (rem // ROPE_FREQS) % 2, rem % ROPE_FREQS
    return head * ATT_HEAD_DIM + axis * 2 * ROPE_FREQS + half * ROPE_FREQS + f


def _prep_odd(w_in):
    d = w_in.shape[0]
    src = _rope_perm(2 * LANES)
    perm = jnp.asarray(np.arange(2 * LANES)[:, None] == src[None, :], BF16)
    return pl.pallas_call(
        _prep_odd_kernel,
        grid=(d // PREP_ROWS,),
        in_specs=[pl.BlockSpec((PREP_ROWS, O_IN), lambda i: (i, 0)),
                  pl.BlockSpec((2 * LANES, 2 * LANES), lambda i: (0, 0))],
        out_specs=pl.BlockSpec((PREP_ROWS, O_IN), lambda i: (i, 0)),
        out_shape=jax.ShapeDtypeStruct((d, O_IN), BF16),
        compiler_params=_params("parallel"),
        name="prep_odd",
    )(w_in, perm)


EVEN_WIDTHS = ((SSD_WIDTH, BF16), (CONV_CH, BF16), (GLA_K, BF16), (GLA_K, BF16), (GLA_V, BF16), (GLA_V, BF16),
               (SMALL_W, F32))


def _expansion_matrices():
    rows = np.arange(SMALL_W)[:, None]
    heads = (np.arange(SSD_WIDTH) // SSD_HEAD_DIM)[None, :]
    ef = (rows == heads).astype(np.float32)
    eb = (rows - SSD_HEADS == heads).astype(np.float32)
    return jnp.asarray(ef, BF16), jnp.asarray(eb, BF16)


def _pad_lanes(v, width):
    return jnp.pad(v, ((0, 0), (0, width - v.shape[1])))


def _rope_tables(length):
    rows = length // GRID_W
    row = np.repeat(np.arange(rows, dtype=np.float64), GRID_W)
    col = np.tile(np.arange(GRID_W, dtype=np.float64), rows)
    inv = 1.0 / (ROPE_BASE ** (np.arange(ROPE_FREQS, dtype=np.float64) / ROPE_FREQS))
    ang_r = row[:, None] * inv
    ang_c = col[:, None] * inv
    cos = np.concatenate([np.cos(ang_r), np.cos(ang_c), np.cos(ang_r), np.cos(ang_c)], axis=1)
    sin = np.concatenate([-np.sin(ang_r), -np.sin(ang_c), np.sin(ang_r), np.sin(ang_c)], axis=1)
    return jnp.asarray(cos, F32), jnp.asarray(sin, F32)


def _rope_head_layout(v):
    return v[_rope_perm(ATT_HEAD_DIM)]


def _mod_rows(mod, rows, bsz, d):
    picked = jnp.broadcast_to(mod[rows], (bsz, 3 * d)) if isinstance(rows, int) else mod[rows]
    return [picked[:, None, j * d:(j + 1) * d] for j in range(3)]


PROJ_ROWS = 512
OUT_ROWS = 512
ATT_Q_BLOCKS = 16
GLA_ROWS = 512
GLA_BWD_ROWS = 1024
GLA_CUMSUM_ROWS = 256
SSD_FWD_ROWS = 4 * SSD_CHUNK
SSD_BWD_ROWS = 1024
CVEC_ROWS = SUBLANES


def kernel(x, c, ctx, c_ctx, e_norm, e_mod_w, e_mod_b, e_w_in, e_conv_w, e_conv_b, e_dt_bias, e_a_log,
           e_d_skip, e_ssd_norm, e_gla_gate_w, e_gla_gate_b, e_gla_norm, e_w_out, o_norm, o_mod_w, o_mod_b,
           o_w_in, o_q_norm, o_k_norm, o_sink, o_w_out):
    bsz, length, d = x.shape
    n_ctx = ctx.shape[1]
    assert e_norm.shape[0] == 1 and o_norm.shape[0] == 1, "two-layer block only"
    assert length % SSD_CHUNK == 0 and n_ctx % SSD_CHUNK == 0 and length >= 3 * ATT_BLOCK
    assert bsz + 1 <= CVEC_ROWS

    cvecs = jnp.zeros((CVEC_ROWS, d), F32).at[:bsz].set(c).at[bsz].set(c_ctx)
    lat_rows = slice(0, bsz)

    mod = _adaln(cvecs, e_mod_w[0], e_mod_b[0])
    shift, scale, gate = _mod_rows(mod, lat_rows, bsz, d)
    c_shift, c_scale, c_gate = _mod_rows(mod, bsz, bsz, d)
    w_in = _prep_even(e_w_in[0])
    w_out = e_w_out[0].astype(BF16)
    ef, eb = _expansion_matrices()
    nh2 = 2 * SSD_HEADS
    bias_flat = e_dt_bias[0].reshape(1, nh2)
    alog_flat = e_a_log[0].reshape(1, nh2)
    bias_row, alog_row = _pad_lanes(bias_flat, SMALL_W), _pad_lanes(alog_flat, SMALL_W)
    bias_col, alog_col = bias_flat.reshape(nh2, 1), alog_flat.reshape(nh2, 1)
    dskip_row = jnp.repeat(e_d_skip[0], SSD_HEAD_DIM).reshape(1, SSD_WIDTH)
    conv_b = e_conv_b[0].reshape(1, CONV_CH)
    wg = jnp.zeros((SMALL_W, 2 * GLA_K), F32)
    wg = wg.at[nh2:nh2 + GLA_RANK, 0:GLA_K].set(e_gla_gate_w[0, 0])
    wg = wg.at[nh2 + GLA_RANK:nh2 + 2 * GLA_RANK, GLA_K:2 * GLA_K].set(e_gla_gate_w[0, 1])
    gb = e_gla_gate_b[0].reshape(1, 2 * GLA_K)

    def mixers(stream, sc, sh, ssd_init, gla_init):
        z, xbc, q, k, v, g, small = _proj_even(stream, e_norm[0], sc, sh, w_in, EVEN_WIDTHS, PROJ_ROWS)
        ya, cm, bt, wxb, p3, ssd_f = _sweep_call(
            _ssd_fwd_kernel, "ssd_fwd",
            _ssd_fwd(xbc, small, e_conv_w[0], conv_b, bias_row, bias_col, alog_row, alog_col, dskip_row,
                     ef, eb, ssd_init[0]))
        ys, ssd_b = _ssd_bwd(cm, bt, wxb, p3, ya, z, eb, e_ssd_norm[0], ssd_init[1])
        oa, qdb, kwb, decb, gla_f = _sweep_call(
            _gla_fwd_kernel, "gla_fwd", _gla_fwd(q, k, v, small, wg, gb, gla_init[0], GLA_ROWS))
        os_, gla_b = _gla_bwd(qdb, kwb, decb, v, oa, g, e_gla_norm[0], gla_init[1], GLA_BWD_ROWS)
        return (ys, os_), (ssd_f, ssd_b), (gla_f, gla_b)

    ssd0 = jnp.zeros((bsz,) + SSD_STATE_SHAPE, F32)
    gla0 = jnp.zeros((bsz,) + GLA_STATE_SHAPE, F32)
    ctx_mix, ssd_fin, gla_fin = mixers(ctx, c_scale, c_shift, (ssd0, ssd0), (gla0, gla0))
    lat_mix, _, _ = mixers(x, scale, shift, ssd_fin, gla_fin)
    x = _out_proj(x, gate, lat_mix, w_out, OUT_ROWS, "out_even")
    xc = _out_proj(ctx, c_gate, ctx_mix, w_out, OUT_ROWS, "out_even")

    mod = _adaln(cvecs, o_mod_w[0], o_mod_b[0])
    shift, scale, gate = _mod_rows(mod, lat_rows, bsz, d)
    c_shift, c_scale, _ = _mod_rows(mod, bsz, bsz, d)
    w_in = _prep_odd(o_w_in[0])
    q_norm = _rope_head_layout(o_q_norm[0])
    k_norm = _rope_head_layout(o_k_norm[0])
    cos, sin = _rope_tables(length)
    no_rot = (jnp.ones((n_ctx, ATT_HEAD_DIM), F32), jnp.zeros((n_ctx, ATT_HEAD_DIM), F32))
    kc, vc = _proj_odd(xc, o_norm[0], c_scale, c_shift, w_in[:, :2 * ATT_KV_W], *no_rot,
                       q_norm, k_norm, False, PROJ_ROWS)
    k, v, q, g = _proj_odd(x, o_norm[0], scale, shift, w_in, cos, sin, q_norm, k_norm, True, PROJ_ROWS)
    o = _attention(o_sink[0].astype(F32), q, k, v, kc, vc, g)
    return _out_proj(x, gate, (o,), o_w_out[0].astype(BF16), OUT_ROWS, "out_odd")
```

```python
import functools

import jax
import jax.numpy as jnp
import numpy as np
from jax import lax
from jax.experimental import pallas as pl
from jax.experimental.pallas import tpu as pltpu

F32 = jnp.float32
BF16 = jnp.bfloat16

GRID_W = 64
SSD_HEADS = 16
SSD_HEAD_DIM = 64
SSD_WIDTH = SSD_HEADS * SSD_HEAD_DIM
SSD_GROUPS = 2
SSD_STATE = 128
SSD_CHUNK = 128
CONV_K = 5
CONV_CH = SSD_WIDTH + 2 * SSD_GROUPS * SSD_STATE
GLA_HEADS = 4
GLA_KEY_DIM = 128
GLA_VAL_DIM = 256
GLA_K = GLA_HEADS * GLA_KEY_DIM
GLA_V = GLA_HEADS * GLA_VAL_DIM
GLA_RANK = 16
GLA_GATE_NORMALIZER = 16.0
GLA_CHUNK = 64
ATT_HEADS = 16
ATT_KV_HEADS = 4
ATT_GROUP = ATT_HEADS // ATT_KV_HEADS
ATT_HEAD_DIM = 128
ATT_W = ATT_HEADS * ATT_HEAD_DIM
ATT_KV_W = ATT_KV_HEADS * ATT_HEAD_DIM
WINDOW = 128
ATT_BLOCK = 128
ROPE_BASE = 10000.0
ROPE_FREQS = ATT_HEAD_DIM // 4
NORM_EPS = 1e-6
LOG2E = 1.4426950408889634

LANES = 128
SUBLANES = 8
VMEM_LIMIT_BYTES = 56 * 1024 * 1024

SSD_GROUP_W = SSD_WIDTH // SSD_GROUPS
SSD_BC_W = SSD_GROUPS * SSD_STATE
SMALL_W = LANES
HALO = 2 * SUBLANES
HEADS_PER_DOT = 4
PROJ_ROW_PARTS = 4
CONV_ROW_STRIDE = 2 * SUBLANES + 1
CONV_OUT_ROWS = CONV_ROW_STRIDE * SUBLANES
assert SSD_CHUNK <= CONV_OUT_ROWS <= SSD_CHUNK + HALO - (CONV_K - 1) // 2


def _dot(a, b):
    return jnp.dot(a.astype(BF16), b.astype(BF16), preferred_element_type=F32)


def _dot_nt(a, b):
    return lax.dot_general(a.astype(BF16), b.astype(BF16), (((1,), (1,)), ((), ())),
                           preferred_element_type=F32)


def _dot_tn(a, b):
    return lax.dot_general(a.astype(BF16), b.astype(BF16), (((0,), (0,)), ((), ())),
                           preferred_element_type=F32)


def _split(v):
    hi = v.astype(BF16)
    lo = (v - hi.astype(F32)).astype(BF16)
    return hi, lo


def _dot_split_lhs(v, m):
    hi, lo = _split(v)
    return (jnp.dot(hi, m, preferred_element_type=F32) + jnp.dot(lo, m, preferred_element_type=F32))


def _dot_split_rhs(m, v):
    hi, lo = _split(v)
    return (jnp.dot(m, hi, preferred_element_type=F32) + jnp.dot(m, lo, preferred_element_type=F32))


def _dot3(a, b):
    ah, al = _split(a)
    bh, bl = _split(b)
    return (jnp.dot(ah, bh, preferred_element_type=F32) + jnp.dot(al, bh, preferred_element_type=F32)
            + jnp.dot(ah, bl, preferred_element_type=F32))


def _silu(v):
    h = 0.5 * v
    return h + h * jnp.tanh(h)


def _softplus(v):
    return jnp.maximum(v, 0.0) + jnp.log(1.0 + jnp.exp(-jnp.abs(v)))


def _log_sigmoid(v):
    return jnp.minimum(v, 0.0) - jnp.log(1.0 + jnp.exp(-jnp.abs(v)))


def _tri(n):
    row = lax.broadcasted_iota(jnp.int32, (n, n), 0)
    col = lax.broadcasted_iota(jnp.int32, (n, n), 1)
    return row >= col, col >= row


def _params(*sem):
    return pltpu.CompilerParams(dimension_semantics=sem, vmem_limit_bytes=VMEM_LIMIT_BYTES)


def _adaln_kernel(c_ref, w_ref, b_ref, o_ref):
    o_ref[...] = _dot3(_silu(c_ref[...]), w_ref[...]) + b_ref[...]


def _adaln(cvecs, w, b):
    rows, d = cvecs.shape
    n = w.shape[1]
    tn = 1024
    return pl.pallas_call(
        _adaln_kernel,
        grid=(n // tn,),
        in_specs=[pl.BlockSpec((rows, d), lambda j: (0, 0)),
                  pl.BlockSpec((d, tn), lambda j: (0, j)),
                  pl.BlockSpec((1, tn), lambda j: (0, j))],
        out_specs=pl.BlockSpec((rows, tn), lambda j: (0, j)),
        out_shape=jax.ShapeDtypeStruct((rows, n), F32),
        compiler_params=_params("parallel"),
        name="adaln",
    )(cvecs, w, b.reshape(1, n))


def _modulated_norm(x, g, sc, sh):
    r = lax.rsqrt(jnp.mean(x * x, axis=-1, keepdims=True) + NORM_EPS)
    return ((x * r) * g) * (1.0 + sc) + sh


def _store_cols(h, w_ref, off, ref, rs=slice(None), act=None):
    n = ref.shape[-1]
    for c0 in range(0, n, 512):
        c1 = min(n, c0 + 512)
        t = jnp.dot(h, w_ref[:, off + c0:off + c1], preferred_element_type=F32)
        ref[rs, c0:c1] = (t if act is None else act(t)).astype(ref.dtype)
    return off + n


def _row_parts(rows):
    step = max(rows // PROJ_ROW_PARTS, SUBLANES)
    return [slice(r0, r0 + step) for r0 in range(0, rows, step)]


def _proj_even_kernel(x_ref, g_ref, sc_ref, sh_ref, w_ref, *out_refs):
    for rs in _row_parts(x_ref.shape[0]):
        h = _modulated_norm(x_ref[rs, :], g_ref[...], sc_ref[...], sh_ref[...]).astype(BF16)
        off = 0
        for ref in out_refs:
            off = _store_cols(h, w_ref, off, ref, rs)


def _head_norm(t, gain):
    r = lax.rsqrt(jnp.mean(t * t, axis=-1, keepdims=True) + NORM_EPS)
    return (t * r) * gain


def _rope(t, cos, sin_signed):
    return t * cos + pltpu.roll(t, ATT_HEAD_DIM // 2, 1) * sin_signed


def _proj_odd_kernel(x_ref, g_ref, sc_ref, sh_ref, w_ref, cos_ref, sin_ref, qn_ref, kn_ref,
                     k_ref, v_ref, *qg_refs):
    h = _modulated_norm(x_ref[...], g_ref[...], sc_ref[...], sh_ref[...]).astype(BF16)
    cos = cos_ref[...]
    sin = sin_ref[...]
    scale = ATT_HEAD_DIM ** -0.5 * LOG2E

    width = HEADS_PER_DOT * ATT_HEAD_DIM
    dh = ATT_HEAD_DIM
    pr = lax.broadcasted_iota(jnp.int32, (2 * dh, 2 * dh), 0) // dh
    pc = lax.broadcasted_iota(jnp.int32, (2 * dh, 2 * dh), 1) // dh
    head_ones = jnp.where(pr == pc, 1.0, 0.0).astype(BF16)

    def project(col):
        return jnp.dot(h, w_ref[:, col:col + width], preferred_element_type=F32)

    def finish_heads(t4, ref, j0, gain, out_scale):
        for j in range(0, HEADS_PER_DOT, 2):
            t2 = t4[:, j * dh:(j + 2) * dh]
            mean_sq = jnp.dot((t2 * t2).astype(BF16), head_ones, preferred_element_type=F32) * (1.0 / dh)
            t2 = t2 * lax.rsqrt(mean_sq + NORM_EPS)
            for jj in range(2):
                t = _rope(t2[:, jj * dh:(jj + 1) * dh] * gain, cos, sin)
                c0 = (j0 + j + jj) * dh
                ref[:, c0:c0 + dh] = (t if out_scale is None else t * out_scale).astype(ref.dtype)

    work = []
    if qg_refs:
        q_ref, gate_ref = qg_refs
        for j0 in range(0, ATT_HEADS, HEADS_PER_DOT):
            c0 = j0 * dh

            def finish_gate(t4, c0=c0):
                gate_ref[:, c0:c0 + width] = _silu(t4)

            work.append((2 * ATT_KV_W + ATT_W + c0, finish_gate))
            work.append((2 * ATT_KV_W + c0,
                         functools.partial(finish_heads, ref=q_ref, j0=j0, gain=qn_ref[...], out_scale=scale)))
    for j0 in range(0, ATT_KV_HEADS, HEADS_PER_DOT):
        work.append((j0 * dh, functools.partial(finish_heads, ref=k_ref, j0=j0, gain=kn_ref[...], out_scale=None)))
    for col, finish in work:
        finish(project(col))
    _store_cols(h, w_ref, ATT_KV_W, v_ref)


def _row_tile(length, want):
    return min(length, want)


def _proj_even(x, norm_g, scale, shift, w_bf16, widths, tm):
    bsz, length, d = x.shape
    tm = _row_tile(length, tm)
    n = w_bf16.shape[1]
    row = lambda b, i: (b, i, 0)
    mod = lambda b, i: (b, 0, 0)
    return pl.pallas_call(
        _proj_even_kernel,
        grid=(bsz, length // tm),
        in_specs=[pl.BlockSpec((None, tm, d), row),
                  pl.BlockSpec((1, d), lambda b, i: (0, 0)),
                  pl.BlockSpec((None, 1, d), mod),
                  pl.BlockSpec((None, 1, d), mod),
                  pl.BlockSpec((d, n), lambda b, i: (0, 0), pipeline_mode=pl.Buffered(1))],
        out_specs=[pl.BlockSpec((None, tm, wd), row) for wd, _ in widths],
        out_shape=[jax.ShapeDtypeStruct((bsz, length, wd), dt) for wd, dt in widths],
        compiler_params=_params("parallel", "parallel"),
        name="proj_even",
    )(x, norm_g.reshape(1, d), scale, shift, w_bf16)


def _proj_odd(x, norm_g, scale, shift, w_bf16, cos, sin, q_norm, k_norm, with_queries, tm):
    bsz, length, d = x.shape
    tm = _row_tile(length, tm)
    n = w_bf16.shape[1]
    row = lambda b, i: (b, i, 0)
    mod = lambda b, i: (b, 0, 0)
    const = lambda b, i: (0, 0)
    widths = [(ATT_KV_W, BF16), (ATT_KV_W, BF16)]
    if with_queries:
        widths += [(ATT_W, BF16), (ATT_W, F32)]
    return pl.pallas_call(
        _proj_odd_kernel,
        grid=(bsz, length // tm),
        in_specs=[pl.BlockSpec((None, tm, d), row),
                  pl.BlockSpec((1, d), const),
                  pl.BlockSpec((None, 1, d), mod),
                  pl.BlockSpec((None, 1, d), mod),
                  pl.BlockSpec((d, n), const, pipeline_mode=pl.Buffered(1)),
                  pl.BlockSpec((tm, ATT_HEAD_DIM), lambda b, i: (i, 0)),
                  pl.BlockSpec((tm, ATT_HEAD_DIM), lambda b, i: (i, 0)),
                  pl.BlockSpec((1, ATT_HEAD_DIM), const),
                  pl.BlockSpec((1, ATT_HEAD_DIM), const)],
        out_specs=[pl.BlockSpec((None, tm, wd), row) for wd, _ in widths],
        out_shape=[jax.ShapeDtypeStruct((bsz, length, wd), dt) for wd, dt in widths],
        compiler_params=_params("parallel", "parallel"),
        name="proj_odd_q" if with_queries else "proj_odd_kv",
    )(x, norm_g.reshape(1, d), scale, shift, w_bf16, cos, sin,
      q_norm.reshape(1, ATT_HEAD_DIM), k_norm.reshape(1, ATT_HEAD_DIM))


def _conv_silu(win_s, prev_ref, cur_ref, next_ref, has_prev, has_next, cw_ref, cb_ref, x_s):
    rows = cur_ref.shape[0]
    half = (CONV_K - 1) // 2
    for j in range(CONV_CH // LANES):
        ls = slice(j * LANES, (j + 1) * LANES)
        win_s[j, 0:HALO, :] = jnp.where(has_prev, prev_ref[:, ls].astype(F32), 0.0)
        win_s[j, HALO:HALO + rows, :] = cur_ref[:, ls].astype(F32)
        win_s[j, HALO + rows:2 * HALO + rows, :] = jnp.where(has_next, next_ref[:, ls].astype(F32), 0.0)
        taps = [cw_ref[k:k + 1, ls] for k in range(CONV_K)]
        bias = cb_ref[:, ls]
        win = win_s.at[j]
        out = x_s.at[j]
        for c in range(rows // SSD_CHUNK):
            for a in range(CONV_ROW_STRIDE):
                r0 = HALO + c * SSD_CHUNK - half + a
                acc = bias + taps[0] * win[pl.ds(r0, SUBLANES, stride=CONV_ROW_STRIDE), :]
                for k in range(1, CONV_K):
                    acc = acc + taps[k] * win[pl.ds(r0 + k, SUBLANES, stride=CONV_ROW_STRIDE), :]
                out[pl.ds(c * CONV_OUT_ROWS + a, SUBLANES, stride=CONV_ROW_STRIDE), :] = _silu(acc)


def _ssd_fwd_kernel(p_ref, c_ref, n_ref, sm_ref, cw_ref, cb_ref, bias_row_ref, bias_col_ref,
                    alog_row_ref, alog_col_ref, dskip_ref, ef_ref, eb_ref, sf0_ref,
                    ya_ref, cm_ref, bt_ref, wxb_ref, p3_ref, sff_ref,
                    sf_s, win_s, x_s):
    i = pl.program_id(1)
    nc = pl.num_programs(1)
    t = SSD_CHUNK
    nh = SSD_HEADS
    slabs_x = SSD_WIDTH // LANES
    slab_b = slabs_x
    slab_c = slabs_x + SSD_GROUPS

    @pl.when(i == 0)
    def _():
        sf_s[...] = sf0_ref[...]

    lower, upper = _tri(t)
    ltri = jnp.where(lower, 1.0, 0.0).astype(BF16)
    utri = jnp.where(upper, 1.0, 0.0).astype(BF16)
    a_row = -jnp.exp(alog_row_ref[...])
    a_col = -jnp.exp(alog_col_ref[...])
    ef = ef_ref[...]
    eb = eb_ref[...]
    neg_inf = float("-inf")

    _conv_silu(win_s, p_ref, c_ref, n_ref, i > 0, i < nc - 1, cw_ref, cb_ref, x_s)
    n_chunks = sm_ref.shape[0] // t
    lane = lax.broadcasted_iota(jnp.int32, (t, LANES), 1)
    left = lane < SSD_HEAD_DIM
    heads_per_group = nh // SSD_GROUPS
    pairs_per_group = heads_per_group // 2

    def chunk_decays(c):
        rs = slice(c * t, (c + 1) * t)
        sm = sm_ref[rs, :]
        dt = _softplus(sm + bias_row_ref[...])
        dta = dt * a_row
        cs = _dot_split_rhs(ltri, dta)
        rc = _dot_split_rhs(utri, dta)
        dt_t = _softplus(sm.T[0:2 * nh, :] + bias_col_ref[...])
        dta_t = dt_t * a_col
        log2_dt_t = jnp.log2(dt_t)
        f_q = cs * LOG2E
        g_q = rc * LOG2E
        f_k = _dot_split_lhs(dta_t, utri) * LOG2E - log2_dt_t
        g_k = _dot_split_lhs(dta_t, ltri) * LOG2E - log2_dt_t
        dec_f = _dot_split_lhs(jnp.exp2(f_q), ef)
        wgt_f = _dot_split_lhs(dt * jnp.exp(cs[t - 1:t, :] - cs), ef)
        wgt_b = _dot_split_lhs(dt * jnp.exp(rc[0:1, :] - rc), eb)
        p3_ref[rs, :] = jnp.exp2(g_q)
        for p in range(slabs_x):
            ls = slice(p * LANES, (p + 1) * LANES)
            wxb_ref[rs, ls] = (x_s[p, c * CONV_OUT_ROWS:c * CONV_OUT_ROWS + t, :] * wgt_b[:, ls]).astype(BF16)
        return f_q, g_q, f_k, g_k, dec_f, wgt_f

    decays = [chunk_decays(c) for c in range(n_chunks)]

    for c in range(n_chunks):
        f_q, g_q, f_k, g_k, dec_f, wgt_f = decays[c]
        rs = slice(c * t, (c + 1) * t)
        xr = slice(c * CONV_OUT_ROWS, c * CONV_OUT_ROWS + t)

        def decay_matrix(h, cbg):
            lf = jnp.exp2(jnp.where(lower, f_q[:, h:h + 1] - f_k[h:h + 1, :], neg_inf))
            ub = jnp.exp2(jnp.where(upper, g_q[:, nh + h:nh + h + 1] - g_k[nh + h:nh + h + 1, :], neg_inf))
            return cbg * (lf + ub)

        for g in range(SSD_GROUPS):
            cg = x_s[slab_c + g, xr, :]
            bg = x_s[slab_b + g, xr, :]
            gs = slice(g * SSD_GROUP_W, (g + 1) * SSD_GROUP_W)
            cm_ref[rs, g * SSD_STATE:(g + 1) * SSD_STATE] = cg.astype(BF16)
            cbg = _dot_nt(cg, bg)
            state = sf_s[g]
            y_off = _dot(cg, state) * dec_f[:, gs]
            wx = []
            for pair in range(pairs_per_group):
                h0 = g * heads_per_group + 2 * pair
                p = h0 // 2
                ls = slice(p * LANES, (p + 1) * LANES)
                xp = x_s[p, xr, :]
                y = (_dot(decay_matrix(h0, cbg), jnp.where(left, xp, 0.0))
                     + _dot(decay_matrix(h0 + 1, cbg), jnp.where(left, 0.0, xp)))
                ya_ref[rs, ls] = y + y_off[:, pair * LANES:(pair + 1) * LANES] + dskip_ref[:, ls] * xp
                wx.append((xp * wgt_f[:, ls]).astype(BF16))
            bt = bg.T.astype(BF16)
            bt_ref[c, g * SSD_STATE:(g + 1) * SSD_STATE, :] = bt
            sf_s[g] = dec_f[t - 1:t, gs] * state + jnp.dot(bt, jnp.concatenate(wx, axis=1),
                                                           preferred_element_type=F32)

    @pl.when(i == nc - 1)
    def _():
        sff_ref[...] = sf_s[...]


def _ssd_bwd_kernel(cm_ref, bt_ref, wxb_ref, p3_ref, ya_ref, z_ref, eb_ref, sn_ref, sb0_ref,
                    ys_ref, sbf_ref, sb_s):
    i = pl.program_id(1)
    nc = pl.num_programs(1)

    @pl.when(i == 0)
    def _():
        sb_s[...] = sb0_ref[...]

    t = SSD_CHUNK
    n_sub = bt_ref.shape[0]
    dec_b = _dot_split_lhs(p3_ref[...], eb_ref[...])
    for g in range(SSD_GROUPS):
        gs = slice(g * SSD_GROUP_W, (g + 1) * SSD_GROUP_W)
        ss = slice(g * SSD_STATE, (g + 1) * SSD_STATE)
        chunks = [slice(s * t, (s + 1) * t) for s in range(n_sub)]
        incs = [jnp.dot(bt_ref[s, ss, :], wxb_ref[rs, gs], preferred_element_type=F32)
                for s, rs in enumerate(chunks)]
        states = [None] * n_sub
        state = sb_s[g]
        for s in reversed(range(n_sub)):
            states[s] = state
            state = dec_b[s * t:s * t + 1, gs] * state + incs[s]
        sb_s[g] = state
        for s, rs in enumerate(chunks):
            y_off = jnp.dot(cm_ref[rs, ss], states[s].astype(BF16), preferred_element_type=F32) * dec_b[rs, gs]
            y = (ya_ref[rs, gs] + y_off) * _silu(z_ref[rs, gs].astype(F32))
            ys_ref[rs, gs] = _head_norm(y, sn_ref[:, gs]).astype(ys_ref.dtype)

    @pl.when(i == nc - 1)
    def _():
        sbf_ref[...] = sb_s[...]


SSD_STATE_SHAPE = (SSD_GROUPS, SSD_STATE, SSD_GROUP_W)


def _ssd_fwd(xbc, small, conv_w, conv_b, bias_row, bias_col, alog_row, alog_col, dskip_row, ef, eb, sf0):
    bsz, length, _ = xbc.shape
    t = SSD_CHUNK
    rows = _row_tile(length, SSD_FWD_ROWS)
    nc = length // t
    per = rows // HALO
    last_halo = length // HALO - 1

    def cur(b, i): return (b, i, 0)
    def prev(b, i): return (b, jnp.maximum(i * per - 1, 0), 0)
    def nxt(b, i): return (b, jnp.minimum((i + 1) * per, last_halo), 0)
    const2 = lambda b, i: (0, 0)
    state = lambda b, i: (b, 0, 0, 0)

    halo_spec = lambda f: pl.BlockSpec((None, HALO, CONV_CH), f)
    chunk_spec = lambda w: pl.BlockSpec((None, rows, w), cur)
    state_spec = pl.BlockSpec((None,) + SSD_STATE_SHAPE, state)
    return dict(
        grid=(bsz, length // rows),
        in_specs=[halo_spec(prev), chunk_spec(CONV_CH), halo_spec(nxt), chunk_spec(SMALL_W),
                  pl.BlockSpec((CONV_K, CONV_CH), const2),
                  pl.BlockSpec((1, CONV_CH), const2),
                  pl.BlockSpec((1, SMALL_W), const2),
                  pl.BlockSpec((2 * SSD_HEADS, 1), const2),
                  pl.BlockSpec((1, SMALL_W), const2),
                  pl.BlockSpec((2 * SSD_HEADS, 1), const2),
                  pl.BlockSpec((1, SSD_WIDTH), const2),
                  pl.BlockSpec((SMALL_W, SSD_WIDTH), const2),
                  pl.BlockSpec((SMALL_W, SSD_WIDTH), const2),
                  state_spec],
        out_specs=[chunk_spec(SSD_WIDTH), chunk_spec(SSD_BC_W),
                   pl.BlockSpec((None, rows // t, SSD_BC_W, t), lambda b, i: (b, i, 0, 0)),
                   chunk_spec(SSD_WIDTH), chunk_spec(SMALL_W), state_spec],
        out_shape=[jax.ShapeDtypeStruct((bsz, length, SSD_WIDTH), F32),
                   jax.ShapeDtypeStruct((bsz, length, SSD_BC_W), BF16),
                   jax.ShapeDtypeStruct((bsz, nc, SSD_BC_W, t), BF16),
                   jax.ShapeDtypeStruct((bsz, length, SSD_WIDTH), BF16),
                   jax.ShapeDtypeStruct((bsz, length, SMALL_W), F32),
                   jax.ShapeDtypeStruct((bsz,) + SSD_STATE_SHAPE, F32)],
        scratch_shapes=[pltpu.VMEM(SSD_STATE_SHAPE, F32),
                        pltpu.VMEM((CONV_CH // LANES, rows + 2 * HALO, LANES), F32),
                        pltpu.VMEM((CONV_CH // LANES, (rows // t) * CONV_OUT_ROWS, LANES), F32)],
        args=(xbc, xbc, xbc, small, conv_w, conv_b, bias_row, bias_col, alog_row, alog_col, dskip_row, ef, eb, sf0))


def _ssd_bwd(cm, bt, wxb, p3, ya, z, eb, ssd_norm, sb0, rows):
    bsz, length, _ = ya.shape
    t = SSD_CHUNK
    n_steps = length // rows
    rev = lambda b, i: (b, n_steps - 1 - i, 0)
    const2 = lambda b, i: (0, 0)
    state = lambda b, i: (b, 0, 0, 0)
    chunk_spec = lambda w: pl.BlockSpec((None, rows, w), rev)
    state_spec = pl.BlockSpec((None,) + SSD_STATE_SHAPE, state)
    return dict(
        in_specs=[chunk_spec(SSD_BC_W),
                  pl.BlockSpec((None, rows // t, SSD_BC_W, t), lambda b, i: (b, n_steps - 1 - i, 0, 0)),
                  chunk_spec(SSD_WIDTH), chunk_spec(SMALL_W), chunk_spec(SSD_WIDTH), chunk_spec(SSD_WIDTH),
                  pl.BlockSpec((SMALL_W, SSD_WIDTH), const2),
                  pl.BlockSpec((1, SSD_WIDTH), const2),
                  state_spec],
        state_spec=state_spec,
        state_shape=jax.ShapeDtypeStruct((bsz,) + SSD_STATE_SHAPE, F32),
        state_scratch=pltpu.VMEM(SSD_STATE_SHAPE, F32),
        args=(cm, bt, wxb, p3, ya, z, eb, ssd_norm.reshape(1, SSD_WIDTH), sb0))


def _gla_fwd_kernel(q_ref, k_ref, v_ref, sm_ref, wg_ref, gb_ref, sf0_ref,
                    oa_ref, qdb_ref, kwb_ref, decb_ref, sff_ref, sf_s):
    i = pl.program_id(1)
    n_steps = pl.num_programs(1)
    t = GLA_CHUNK
    rows = q_ref.shape[0]
    n_sub = rows // t
    dk, dv = GLA_KEY_DIM, GLA_VAL_DIM
    qscale = dk ** -0.5
    inv_norm = 1.0 / GLA_GATE_NORMALIZER

    @pl.when(i == 0)
    def _():
        sf_s[...] = sf0_ref[...]

    lower, upper = _tri(t)
    grp = min(rows, GLA_CUMSUM_ROWS)
    row = lax.broadcasted_iota(jnp.int32, (grp, grp), 0)
    col = lax.broadcasted_iota(jnp.int32, (grp, grp), 1)
    diff = row - col
    pos = row % t
    bd_lower = jnp.where(diff >= 0, jnp.where(diff <= pos, 1.0, 0.0), 0.0).astype(BF16)
    bd_upper = jnp.where(diff <= 0, jnp.where(-diff <= t - 1 - pos, 1.0, 0.0), 0.0).astype(BF16)

    hi, lo = _split(sm_ref[...])
    wg_hi, wg_lo = _split(wg_ref[...])
    logits = (jnp.dot(hi, wg_hi, preferred_element_type=F32) + jnp.dot(lo, wg_hi, preferred_element_type=F32)
              + jnp.dot(hi, wg_lo, preferred_element_type=F32)) + gb_ref[...]
    lg = _log_sigmoid(logits) * inv_norm
    groups = [slice(r0, r0 + grp) for r0 in range(0, rows, grp)]
    cs = jnp.concatenate([_dot_split_rhs(bd_lower, lg[gr, 0:GLA_K]) for gr in groups], axis=0)
    rc = jnp.concatenate([_dot_split_rhs(bd_upper, lg[gr, GLA_K:2 * GLA_K]) for gr in groups], axis=0)

    def per_chunk_row(v, offset):
        return jnp.concatenate([jnp.broadcast_to(v[s * t + offset:s * t + offset + 1, :], (t, v.shape[1]))
                                for s in range(n_sub)], axis=0)

    for h in range(GLA_HEADS):
        ks = slice(h * dk, (h + 1) * dk)
        vs = slice(h * dv, (h + 1) * dv)
        c = cs[:, ks]
        r = rc[:, ks]
        qh = q_ref[:, ks].astype(F32) * qscale
        kh = k_ref[:, ks].astype(F32)
        qdf = (qh * jnp.exp(c)).astype(BF16)
        kif = (kh * jnp.exp(-c)).astype(BF16)
        qdb = (qh * jnp.exp(r)).astype(BF16)
        kib = (kh * jnp.exp(-r)).astype(BF16)
        kwf = (kh * jnp.exp(per_chunk_row(c, t - 1) - c)).astype(BF16)
        qdb_ref[:, ks] = qdb
        kwb_ref[:, ks] = (kh * jnp.exp(per_chunk_row(r, 0) - r)).astype(BF16)
        chunks = [slice(s * t, (s + 1) * t) for s in range(n_sub)]
        vhs = [v_ref[rs, vs].astype(BF16) for rs in chunks]
        atts = [(jnp.where(lower, _dot_nt(qdf[rs], kif[rs]), 0.0)
                 + jnp.where(upper, _dot_nt(qdb[rs], kib[rs]), 0.0)).astype(BF16) for rs in chunks]
        incs = [_dot_tn(vh, kwf[rs]) for vh, rs in zip(vhs, chunks)]
        states = [sf_s[h]]
        for s in range(n_sub):
            decb_ref[s, :, ks] = jnp.exp(r[s * t:s * t + 1, :])
            states.append(states[s] * jnp.exp(c[s * t + t - 1:s * t + t, :]) + incs[s])
        sf_s[h] = states[n_sub]
        for s, rs in enumerate(chunks):
            oa_ref[rs, vs] = (jnp.dot(atts[s], vhs[s], preferred_element_type=F32)
                              + _dot_nt(qdf[rs], states[s]))

    @pl.when(i == n_steps - 1)
    def _():
        sff_ref[...] = sf_s[...]


def _gla_bwd_kernel(qdb_ref, kwb_ref, decb_ref, v_ref, oa_ref, g_ref, gn_ref, sb0_ref,
                    os_ref, sbf_ref, sb_s):
    i = pl.program_id(1)
    n_steps = pl.num_programs(1)
    t = GLA_CHUNK
    n_sub = qdb_ref.shape[0] // t
    dk, dv = GLA_KEY_DIM, GLA_VAL_DIM

    @pl.when(i == 0)
    def _():
        sb_s[...] = sb0_ref[...]

    for h in range(GLA_HEADS):
        ks = slice(h * dk, (h + 1) * dk)
        vs = slice(h * dv, (h + 1) * dv)
        state = sb_s[h]
        for s in reversed(range(n_sub)):
            rs = slice(s * t, (s + 1) * t)
            o = oa_ref[rs, vs] + _dot_nt(qdb_ref[rs, ks], state)
            state = state * decb_ref[s, :, ks] + _dot_tn(v_ref[rs, vs], kwb_ref[rs, ks])
            os_ref[rs, vs] = (_head_norm(o, gn_ref[:, vs]) * _silu(g_ref[rs, vs].astype(F32))).astype(os_ref.dtype)
        sb_s[h] = state

    @pl.when(i == n_steps - 1)
    def _():
        sbf_ref[...] = sb_s[...]


GLA_STATE_SHAPE = (GLA_HEADS, GLA_VAL_DIM, GLA_KEY_DIM)


def _gla_fwd(q, k, v, small, wg, gb, sf0, rows):
    bsz, length, _ = q.shape
    rows = _row_tile(length, rows)
    n_steps = length // rows
    n_sub = rows // GLA_CHUNK
    fwd = lambda b, i: (b, i, 0)
    const2 = lambda b, i: (0, 0)
    state = lambda b, i: (b, 0, 0, 0)
    blk = lambda w: pl.BlockSpec((None, rows, w), fwd)
    state_spec = pl.BlockSpec((None,) + GLA_STATE_SHAPE, state)
    return dict(
        grid=(bsz, n_steps),
        in_specs=[blk(GLA_K), blk(GLA_K), blk(GLA_V), blk(SMALL_W),
                  pl.BlockSpec((SMALL_W, 2 * GLA_K), const2),
                  pl.BlockSpec((1, 2 * GLA_K), const2),
                  state_spec],
        out_specs=[blk(GLA_V), blk(GLA_K), blk(GLA_K),
                   pl.BlockSpec((None, n_sub, 1, GLA_K), lambda b, i: (b, i, 0, 0)),
                   state_spec],
        out_shape=[jax.ShapeDtypeStruct((bsz, length, GLA_V), F32),
                   jax.ShapeDtypeStruct((bsz, length, GLA_K), BF16),
                   jax.ShapeDtypeStruct((bsz, length, GLA_K), BF16),
                   jax.ShapeDtypeStruct((bsz, length // GLA_CHUNK, 1, GLA_K), F32),
                   jax.ShapeDtypeStruct((bsz,) + GLA_STATE_SHAPE, F32)],
        scratch_shapes=[pltpu.VMEM(GLA_STATE_SHAPE, F32)],
        args=(q, k, v, small, wg, gb, sf0))


def _sweep_call(body, name, call):
    return pl.pallas_call(
        body,
        grid=call["grid"],
        in_specs=call["in_specs"],
        out_specs=call["out_specs"],
        out_shape=call["out_shape"],
        scratch_shapes=call["scratch_shapes"],
        compiler_params=_params("arbitrary", "arbitrary"),
        name=name,
    )(*call["args"])


def _gla_bwd(qdb, kwb, decb, v, oa, g, gla_norm, sb0, rows):
    bsz, length, _ = oa.shape
    n_steps = length // rows
    n_sub = rows // GLA_CHUNK
    rev = lambda b, i: (b, n_steps - 1 - i, 0)
    const2 = lambda b, i: (0, 0)
    state = lambda b, i: (b, 0, 0, 0)
    blk = lambda w: pl.BlockSpec((None, rows, w), rev)
    state_spec = pl.BlockSpec((None,) + GLA_STATE_SHAPE, state)
    return dict(
        in_specs=[blk(GLA_K), blk(GLA_K),
                  pl.BlockSpec((None, n_sub, 1, GLA_K), lambda b, i: (b, n_steps - 1 - i, 0, 0)),
                  blk(GLA_V), blk(GLA_V), blk(GLA_V),
                  pl.BlockSpec((1, GLA_V), const2),
                  state_spec],
        state_spec=state_spec,
        state_shape=jax.ShapeDtypeStruct((bsz,) + GLA_STATE_SHAPE, F32),
        state_scratch=pltpu.VMEM(GLA_STATE_SHAPE, F32),
        args=(qdb, kwb, decb, v, oa, g, gla_norm.reshape(1, GLA_V), sb0))


SSD_BWD_INPUTS = 9
GLA_BWD_INPUTS = 8


def _even_tail_kernel(*refs):
    ssd_in = refs[:SSD_BWD_INPUTS]
    gla_in = refs[SSD_BWD_INPUTS:SSD_BWD_INPUTS + GLA_BWD_INPUTS]
    x_ref, gate_ref, w_ref, y_ref, ssd_fin_ref, gla_fin_ref, ssd_s, gla_s, mix_s = \
        refs[SSD_BWD_INPUTS + GLA_BWD_INPUTS:]
    _ssd_bwd_kernel(*ssd_in, mix_s.at[:, 0:SSD_WIDTH], ssd_fin_ref, ssd_s)
    _gla_bwd_kernel(*gla_in, mix_s.at[:, SSD_WIDTH:SSD_WIDTH + GLA_V], gla_fin_ref, gla_s)
    y_ref[...] = x_ref[...] + gate_ref[...] * jnp.dot(mix_s[...], w_ref[...], preferred_element_type=F32)


def _even_tail(x, gate, w_bf16, ssd_args, gla_args):
    bsz, length, d = x.shape
    rows = _row_tile(length, EVEN_TAIL_ROWS)
    n_steps = length // rows
    ssd = _ssd_bwd(*ssd_args, rows)
    gla = _gla_bwd(*gla_args, rows)
    assert len(ssd["in_specs"]) == SSD_BWD_INPUTS and len(gla["in_specs"]) == GLA_BWD_INPUTS
    rev = lambda b, i: (b, n_steps - 1 - i, 0)
    return pl.pallas_call(
        _even_tail_kernel,
        grid=(bsz, n_steps),
        in_specs=(ssd["in_specs"] + gla["in_specs"]
                  + [pl.BlockSpec((None, rows, d), rev),
                     pl.BlockSpec((None, 1, d), lambda b, i: (b, 0, 0)),
                     pl.BlockSpec(w_bf16.shape, lambda b, i: (0, 0), pipeline_mode=pl.Buffered(1))]),
        out_specs=[pl.BlockSpec((None, rows, d), rev), ssd["state_spec"], gla["state_spec"]],
        out_shape=[jax.ShapeDtypeStruct((bsz, length, d), F32), ssd["state_shape"], gla["state_shape"]],
        scratch_shapes=[ssd["state_scratch"], gla["state_scratch"],
                        pltpu.VMEM((rows, SSD_WIDTH + GLA_V), BF16)],
        compiler_params=_params("arbitrary", "arbitrary"),
        name="even_tail",
    )(*ssd["args"], *gla["args"], x, gate, w_bf16)


def _out_proj_kernel(x_ref, gate_ref, *refs):
    *in_refs, w_ref, y_ref = refs
    acc = None
    off = 0
    for ref in in_refs:
        n = ref.shape[-1]
        part = jnp.dot(ref[...], w_ref[off:off + n, :], preferred_element_type=F32)
        acc = part if acc is None else acc + part
        off += n
    y_ref[...] = x_ref[...] + gate_ref[...] * acc


def _out_proj(x, gate, mixed, w_bf16, tm, name):
    bsz, length, d = x.shape
    tm = _row_tile(length, tm)
    row = lambda b, i: (b, i, 0)
    return pl.pallas_call(
        _out_proj_kernel,
        grid=(bsz, length // tm),
        in_specs=([pl.BlockSpec((None, tm, d), row),
                   pl.BlockSpec((None, 1, d), lambda b, i: (b, 0, 0))]
                  + [pl.BlockSpec((None, tm, m.shape[-1]), row) for m in mixed]
                  + [pl.BlockSpec(w_bf16.shape, lambda b, i: (0, 0), pipeline_mode=pl.Buffered(1))]),
        out_specs=pl.BlockSpec((None, tm, d), row),
        out_shape=jax.ShapeDtypeStruct((bsz, length, d), F32),
        compiler_params=_params("parallel", "parallel"),
        name=name,
    )(x, gate, *mixed, w_bf16)


def _attn_kernel(sink_ref, q_ref, k_ref, v_ref, kc_ref, vc_ref, g_ref, o_ref):
    j = pl.program_id(1)
    i = pl.program_id(2)
    length = k_ref.shape[0]
    blk = ATT_BLOCK
    band = 3 * blk
    dh = ATT_HEAD_DIM
    n_q = q_ref.shape[0] // blk
    n_ctx = kc_ref.shape[0]
    row_minus_col = (lax.broadcasted_iota(jnp.int32, (blk, band), 0)
                     - lax.broadcasted_iota(jnp.int32, (blk, band), 1))
    sink = jnp.concatenate([jnp.full((blk, LANES), sink_ref[j * ATT_GROUP + g] * LOG2E, F32)
                            for g in range(ATT_GROUP)], axis=0)
    kc = kc_ref[...]
    vc_ext = jnp.concatenate([vc_ref[...], jnp.ones((n_ctx, dh), BF16)], axis=1)
    ones_band = jnp.ones((band, dh), BF16)

    def window_start(qb):
        blk_idx = i * n_q + qb
        return blk_idx, pl.multiple_of(jnp.clip((blk_idx - 1) * blk, 0, length - band), blk)

    def scores(qb):
        _, start = window_start(qb)
        rs = slice(qb * blk, (qb + 1) * blk)
        q = jnp.concatenate([q_ref[rs, g * dh:(g + 1) * dh] for g in range(ATT_GROUP)], axis=0)
        return _dot_nt(q, k_ref[pl.ds(start, band), :]), _dot_nt(q, kc)

    pending = scores(0)
    for qb in range(n_q):
        s_band, s_ctx = pending
        if qb + 1 < n_q:
            pending = scores(qb + 1)
        blk_idx, start = window_start(qb)
        inside = jnp.abs(row_minus_col + (blk_idx * blk - start)) <= WINDOW
        v_all = jnp.concatenate([jnp.concatenate([v_ref[pl.ds(start, band), :], ones_band], axis=1),
                                 vc_ext], axis=0)
        rs = slice(qb * blk, (qb + 1) * blk)
        cols = []
        for c in range(band // LANES):
            ls = slice(c * LANES, (c + 1) * LANES)
            cols.append(jnp.concatenate(
                [jnp.where(inside[:, ls], s_band[g * blk:(g + 1) * blk, ls], float("-inf"))
                 for g in range(ATT_GROUP)], axis=0))
        for c in range(n_ctx // LANES):
            cols.append(s_ctx[:, c * LANES:(c + 1) * LANES])
        m = jnp.maximum(sink, jnp.max(functools.reduce(jnp.maximum, cols), axis=-1, keepdims=True))
        p = jnp.concatenate([jnp.exp2(col - m) for col in cols], axis=1).astype(BF16)
        acc = jnp.dot(p, v_all, preferred_element_type=F32)
        out = acc[:, 0:dh] / (acc[:, dh:2 * dh] + jnp.exp2(sink - m))
        for g in range(ATT_GROUP):
            cs = slice(g * dh, (g + 1) * dh)
            o_ref[rs, cs] = (out[g * blk:(g + 1) * blk, :] * g_ref[rs, cs]).astype(o_ref.dtype)


def _attention(sink, q, k, v, kc, vc, gate):
    bsz, length, _ = q.shape
    n_ctx = kc.shape[1]
    rows = _row_tile(length, ATT_Q_BLOCKS * ATT_BLOCK)
    gw = ATT_GROUP * ATT_HEAD_DIM
    qmap = lambda b, j, i, s: (b, i, j)
    kvmap = lambda b, j, i, s: (b, 0, j)
    grid_spec = pltpu.PrefetchScalarGridSpec(
        num_scalar_prefetch=1,
        grid=(bsz, ATT_KV_HEADS, length // rows),
        in_specs=[pl.BlockSpec((None, rows, gw), qmap),
                  pl.BlockSpec((None, length, ATT_HEAD_DIM), kvmap),
                  pl.BlockSpec((None, length, ATT_HEAD_DIM), kvmap),
                  pl.BlockSpec((None, n_ctx, ATT_HEAD_DIM), kvmap),
                  pl.BlockSpec((None, n_ctx, ATT_HEAD_DIM), kvmap),
                  pl.BlockSpec((None, rows, gw), qmap)],
        out_specs=pl.BlockSpec((None, rows, gw), qmap),
    )
    return pl.pallas_call(
        _attn_kernel,
        grid_spec=grid_spec,
        out_shape=jax.ShapeDtypeStruct((bsz, length, ATT_W), BF16),
        compiler_params=_params("parallel", "parallel", "arbitrary"),
        name="attention",
    )(sink, q, k, v, kc, vc, gate)


E_IN_SIZES = (SSD_WIDTH, CONV_CH, 2 * SSD_HEADS, GLA_K, GLA_K, GLA_V, GLA_V, 2 * GLA_RANK)
E_IN = sum(E_IN_SIZES)
E_ALIGNED = SSD_WIDTH + CONV_CH
E_SHIFT = 2 * SSD_HEADS
E_MAIN = 2 * GLA_K + 2 * GLA_V
E_OUT = E_ALIGNED + E_MAIN + SMALL_W
PREP_ROWS = 256
O_IN = 2 * ATT_KV_W + 2 * ATT_W


def _prep_even_kernel(wt_ref, o_ref):
    cols = wt_ref.shape[1]
    for dst in range(0, E_ALIGNED + E_MAIN, LANES):
        src = dst if dst < E_ALIGNED else dst + E_SHIFT
        o_ref[:, dst:dst + LANES] = wt_ref[src:src + LANES, :].T.astype(BF16)
    small = jnp.concatenate([wt_ref[E_ALIGNED:E_ALIGNED + E_SHIFT, :],
                             wt_ref[E_IN - 2 * GLA_RANK:E_IN, :],
                             jnp.zeros((SMALL_W - E_SHIFT - 2 * GLA_RANK, cols), F32)], axis=0)
    o_ref[:, E_ALIGNED + E_MAIN:E_OUT] = small.T.astype(BF16)


def _prep_even(w_in):
    d = w_in.shape[0]
    return pl.pallas_call(
        _prep_even_kernel,
        grid=(d // PREP_ROWS,),
        in_specs=[pl.BlockSpec((E_IN, PREP_ROWS), lambda i: (0, i))],
        out_specs=pl.BlockSpec((PREP_ROWS, E_OUT), lambda i: (i, 0)),
        out_shape=jax.ShapeDtypeStruct((d, E_OUT), BF16),
        compiler_params=_params("parallel"),
        name="prep_even",
    )(jnp.swapaxes(w_in, 0, 1))


def _prep_odd_kernel(w_ref, perm_ref, o_ref):
    def copy(c0, width):
        o_ref[:, c0:c0 + width] = w_ref[:, c0:c0 + width].astype(BF16)

    def permute(c0, width):
        for c in range(c0, c0 + width, 2 * LANES):
            o_ref[:, c:c + 2 * LANES] = jnp.dot(w_ref[:, c:c + 2 * LANES].astype(BF16), perm_ref[...],
                                                preferred_element_type=F32).astype(BF16)

    permute(0, ATT_KV_W)
    copy(ATT_KV_W, ATT_KV_W)
    permute(2 * ATT_KV_W, ATT_W)
    copy(2 * ATT_KV_W + ATT_W, ATT_W)


def _rope_perm(n):
    out = np.arange(n)
    head, rem = out // ATT_HEAD_DIM, out % ATT_HEAD_DIM
    half, axis, f = rem // (2 * ROPE_FREQS), (rem // ROPE_FREQS) % 2, rem % ROPE_FREQS
    return head * ATT_HEAD_DIM + axis * 2 * ROPE_FREQS + half * ROPE_FREQS + f


def _prep_odd(w_in):
    d = w_in.shape[0]
    src = _rope_perm(2 * LANES)
    perm = jnp.asarray(np.arange(2 * LANES)[:, None] == src[None, :], BF16)
    return pl.pallas_call(
        _prep_odd_kernel,
        grid=(d // PREP_ROWS,),
        in_specs=[pl.BlockSpec((PREP_ROWS, O_IN), lambda i: (i, 0)),
                  pl.BlockSpec((2 * LANES, 2 * LANES), lambda i: (0, 0))],
        out_specs=pl.BlockSpec((PREP_ROWS, O_IN), lambda i: (i, 0)),
        out_shape=jax.ShapeDtypeStruct((d, O_IN), BF16),
        compiler_params=_params("parallel"),
        name="prep_odd",
    )(w_in, perm)


EVEN_WIDTHS = ((SSD_WIDTH, BF16), (CONV_CH, BF16), (GLA_K, BF16), (GLA_K, BF16), (GLA_V, BF16), (GLA_V, BF16),
               (SMALL_W, F32))


def _expansion_matrices():
    rows = np.arange(SMALL_W)[:, None]
    heads = (np.arange(SSD_WIDTH) // SSD_HEAD_DIM)[None, :]
    ef = (rows == heads).astype(np.float32)
    eb = (rows - SSD_HEADS == heads).astype(np.float32)
    return jnp.asarray(ef, BF16), jnp.asarray(eb, BF16)


def _pad_lanes(v, width):
    return jnp.pad(v, ((0, 0), (0, width - v.shape[1])))


def _rope_tables(length):
    rows = length // GRID_W
    row = np.repeat(np.arange(rows, dtype=np.float64), GRID_W)
    col = np.tile(np.arange(GRID_W, dtype=np.float64), rows)
    inv = 1.0 / (ROPE_BASE ** (np.arange(ROPE_FREQS, dtype=np.float64) / ROPE_FREQS))
    ang_r = row[:, None] * inv
    ang_c = col[:, None] * inv
    cos = np.concatenate([np.cos(ang_r), np.cos(ang_c), np.cos(ang_r), np.cos(ang_c)], axis=1)
    sin = np.concatenate([-np.sin(ang_r), -np.sin(ang_c), np.sin(ang_r), np.sin(ang_c)], axis=1)
    return jnp.asarray(cos, F32), jnp.asarray(sin, F32)


def _rope_head_layout(v):
    return v[_rope_perm(ATT_HEAD_DIM)]


def _mod_rows(mod, rows, bsz, d):
    picked = jnp.broadcast_to(mod[rows], (bsz, 3 * d)) if isinstance(rows, int) else mod[rows]
    return [picked[:, None, j * d:(j + 1) * d] for j in range(3)]


PROJ_ROWS = 512
OUT_ROWS = 512
ATT_Q_BLOCKS = 16
GLA_ROWS = 512
EVEN_TAIL_ROWS = 512
GLA_CUMSUM_ROWS = 256
SSD_FWD_ROWS = 4 * SSD_CHUNK
CVEC_ROWS = SUBLANES


def kernel(x, c, ctx, c_ctx, e_norm, e_mod_w, e_mod_b, e_w_in, e_conv_w, e_conv_b, e_dt_bias, e_a_log,
           e_d_skip, e_ssd_norm, e_gla_gate_w, e_gla_gate_b, e_gla_norm, e_w_out, o_norm, o_mod_w, o_mod_b,
           o_w_in, o_q_norm, o_k_norm, o_sink, o_w_out):
    bsz, length, d = x.shape
    n_ctx = ctx.shape[1]
    assert e_norm.shape[0] == 1 and o_norm.shape[0] == 1, "two-layer block only"
    assert length % SSD_CHUNK == 0 and n_ctx % SSD_CHUNK == 0 and length >= 3 * ATT_BLOCK
    assert bsz + 1 <= CVEC_ROWS

    cvecs = jnp.zeros((CVEC_ROWS, d), F32).at[:bsz].set(c).at[bsz].set(c_ctx)
    lat_rows = slice(0, bsz)

    mod = _adaln(cvecs, e_mod_w[0], e_mod_b[0])
    shift, scale, gate = _mod_rows(mod, lat_rows, bsz, d)
    c_shift, c_scale, c_gate = _mod_rows(mod, bsz, bsz, d)
    w_in = _prep_even(e_w_in[0])
    w_out = e_w_out[0].astype(BF16)
    ef, eb = _expansion_matrices()
    nh2 = 2 * SSD_HEADS
    bias_flat = e_dt_bias[0].reshape(1, nh2)
    alog_flat = e_a_log[0].reshape(1, nh2)
    bias_row, alog_row = _pad_lanes(bias_flat, SMALL_W), _pad_lanes(alog_flat, SMALL_W)
    bias_col, alog_col = bias_flat.reshape(nh2, 1), alog_flat.reshape(nh2, 1)
    dskip_row = jnp.repeat(e_d_skip[0], SSD_HEAD_DIM).reshape(1, SSD_WIDTH)
    conv_b = e_conv_b[0].reshape(1, CONV_CH)
    wg = jnp.zeros((SMALL_W, 2 * GLA_K), F32)
    wg = wg.at[nh2:nh2 + GLA_RANK, 0:GLA_K].set(e_gla_gate_w[0, 0])
    wg = wg.at[nh2 + GLA_RANK:nh2 + 2 * GLA_RANK, GLA_K:2 * GLA_K].set(e_gla_gate_w[0, 1])
    gb = e_gla_gate_b[0].reshape(1, 2 * GLA_K)

    def even_layer(stream, sc, sh, gt, ssd_init, gla_init):
        z, xbc, q, k, v, g, small = _proj_even(stream, e_norm[0], sc, sh, w_in, EVEN_WIDTHS, PROJ_ROWS)
        ya, cm, bt, wxb, p3, ssd_f = _sweep_call(
            _ssd_fwd_kernel, "ssd_fwd",
            _ssd_fwd(xbc, small, e_conv_w[0], conv_b, bias_row, bias_col, alog_row, alog_col, dskip_row,
                     ef, eb, ssd_init[0]))
        oa, qdb, kwb, decb, gla_f = _sweep_call(
            _gla_fwd_kernel, "gla_fwd", _gla_fwd(q, k, v, small, wg, gb, gla_init[0], GLA_ROWS))
        out, ssd_b, gla_b = _even_tail(stream, gt, w_out,
                                       (cm, bt, wxb, p3, ya, z, eb, e_ssd_norm[0], ssd_init[1]),
                                       (qdb, kwb, decb, v, oa, g, e_gla_norm[0], gla_init[1]))
        return out, (ssd_f, ssd_b), (gla_f, gla_b)

    ssd0 = jnp.zeros((bsz,) + SSD_STATE_SHAPE, F32)
    gla0 = jnp.zeros((bsz,) + GLA_STATE_SHAPE, F32)
    xc, ssd_fin, gla_fin = even_layer(ctx, c_scale, c_shift, c_gate, (ssd0, ssd0), (gla0, gla0))
    x, _, _ = even_layer(x, scale, shift, gate, ssd_fin, gla_fin)

    mod = _adaln(cvecs, o_mod_w[0], o_mod_b[0])
    shift, scale, gate = _mod_rows(mod, lat_rows, bsz, d)
    c_shift, c_scale, _ = _mod_rows(mod, bsz, bsz, d)
    w_in = _prep_odd(o_w_in[0])
    q_norm = _rope_head_layout(o_q_norm[0])
    k_norm = _rope_head_layout(o_k_norm[0])
    cos, sin = _rope_tables(length)
    no_rot = (jnp.ones((n_ctx, ATT_HEAD_DIM), F32), jnp.zeros((n_ctx, ATT_HEAD_DIM), F32))
    kc, vc = _proj_odd(xc, o_norm[0], c_scale, c_shift, w_in[:, :2 * ATT_KV_W], *no_rot,
                       q_norm, k_norm, False, PROJ_ROWS)
    k, v, q, g = _proj_odd(x, o_norm[0], scale, shift, w_in, cos, sin, q_norm, k_norm, True, PROJ_ROWS)
    o = _attention(o_sink[0].astype(F32), q, k, v, kc, vc, g)
    return _out_proj(x, gate, (o,), o_w_out[0].astype(BF16), OUT_ROWS, "out_odd")
```

```python
import functools

import jax
import jax.numpy as jnp
import numpy as np
from jax import lax
from jax.experimental import pallas as pl
from jax.experimental.pallas import tpu as pltpu

F32 = jnp.float32
BF16 = jnp.bfloat16

GRID_W = 64
SSD_HEADS = 16
SSD_HEAD_DIM = 64
SSD_WIDTH = SSD_HEADS * SSD_HEAD_DIM
SSD_GROUPS = 2
SSD_STATE = 128
SSD_CHUNK = 128
CONV_K = 5
CONV_CH = SSD_WIDTH + 2 * SSD_GROUPS * SSD_STATE
GLA_HEADS = 4
GLA_KEY_DIM = 128
GLA_VAL_DIM = 256
GLA_K = GLA_HEADS * GLA_KEY_DIM
GLA_V = GLA_HEADS * GLA_VAL_DIM
GLA_RANK = 16
GLA_GATE_NORMALIZER = 16.0
GLA_CHUNK = 64
ATT_HEADS = 16
ATT_KV_HEADS = 4
ATT_GROUP = ATT_HEADS // ATT_KV_HEADS
ATT_HEAD_DIM = 128
ATT_W = ATT_HEADS * ATT_HEAD_DIM
ATT_KV_W = ATT_KV_HEADS * ATT_HEAD_DIM
WINDOW = 128
ATT_BLOCK = 128
ROPE_BASE = 10000.0
ROPE_FREQS = ATT_HEAD_DIM // 4
NORM_EPS = 1e-6
LOG2E = 1.4426950408889634

LANES = 128
SUBLANES = 8
VMEM_LIMIT_BYTES = 56 * 1024 * 1024

SSD_GROUP_W = SSD_WIDTH // SSD_GROUPS
SSD_BC_W = SSD_GROUPS * SSD_STATE
SMALL_W = LANES
HALO = 2 * SUBLANES
HEADS_PER_DOT = 4
PROJ_ROW_PARTS = 4
PROJ_MIN_PART_ROWS = 128
CONV_ROW_STRIDE = 2 * SUBLANES + 1
CONV_OUT_ROWS = CONV_ROW_STRIDE * SUBLANES
assert SSD_CHUNK <= CONV_OUT_ROWS <= SSD_CHUNK + HALO - (CONV_K - 1) // 2


def _dot(a, b):
    return jnp.dot(a.astype(BF16), b.astype(BF16), preferred_element_type=F32)


def _dot_nt(a, b):
    return lax.dot_general(a.astype(BF16), b.astype(BF16), (((1,), (1,)), ((), ())),
                           preferred_element_type=F32)


def _dot_tn(a, b):
    return lax.dot_general(a.astype(BF16), b.astype(BF16), (((0,), (0,)), ((), ())),
                           preferred_element_type=F32)


def _split(v):
    hi = v.astype(BF16)
    lo = (v - hi.astype(F32)).astype(BF16)
    return hi, lo


def _dot_split_lhs(v, m):
    hi, lo = _split(v)
    return (jnp.dot(hi, m, preferred_element_type=F32) + jnp.dot(lo, m, preferred_element_type=F32))


def _dot_split_rhs(m, v):
    hi, lo = _split(v)
    return (jnp.dot(m, hi, preferred_element_type=F32) + jnp.dot(m, lo, preferred_element_type=F32))


def _dot3(a, b):
    ah, al = _split(a)
    bh, bl = _split(b)
    return (jnp.dot(ah, bh, preferred_element_type=F32) + jnp.dot(al, bh, preferred_element_type=F32)
            + jnp.dot(ah, bl, preferred_element_type=F32))


def _silu(v):
    h = 0.5 * v
    return h + h * jnp.tanh(h)


def _softplus(v):
    return jnp.maximum(v, 0.0) + jnp.log(1.0 + jnp.exp(-jnp.abs(v)))


def _log_sigmoid(v):
    return jnp.minimum(v, 0.0) - jnp.log(1.0 + jnp.exp(-jnp.abs(v)))


def _tri(n):
    row = lax.broadcasted_iota(jnp.int32, (n, n), 0)
    col = lax.broadcasted_iota(jnp.int32, (n, n), 1)
    return row >= col, col >= row


def _params(*sem):
    return pltpu.CompilerParams(dimension_semantics=sem, vmem_limit_bytes=VMEM_LIMIT_BYTES)


def _adaln_kernel(c_ref, w_ref, b_ref, o_ref):
    o_ref[...] = _dot3(_silu(c_ref[...]), w_ref[...]) + b_ref[...]


def _adaln(cvecs, w, b):
    rows, d = cvecs.shape
    n = w.shape[1]
    tn = 1024
    return pl.pallas_call(
        _adaln_kernel,
        grid=(n // tn,),
        in_specs=[pl.BlockSpec((rows, d), lambda j: (0, 0)),
                  pl.BlockSpec((d, tn), lambda j: (0, j)),
                  pl.BlockSpec((1, tn), lambda j: (0, j))],
        out_specs=pl.BlockSpec((rows, tn), lambda j: (0, j)),
        out_shape=jax.ShapeDtypeStruct((rows, n), F32),
        compiler_params=_params("parallel"),
        name="adaln",
    )(cvecs, w, b.reshape(1, n))


def _modulated_norm(x, g, sc, sh):
    r = lax.rsqrt(jnp.mean(x * x, axis=-1, keepdims=True) + NORM_EPS)
    return ((x * r) * g) * (1.0 + sc) + sh


def _store_cols(h, w_ref, off, ref, rs=slice(None), act=None):
    n = ref.shape[-1]
    for c0 in range(0, n, 512):
        c1 = min(n, c0 + 512)
        t = jnp.dot(h, w_ref[:, off + c0:off + c1], preferred_element_type=F32)
        ref[rs, c0:c1] = (t if act is None else act(t)).astype(ref.dtype)
    return off + n


def _row_parts(rows):
    step = min(rows, max(rows // PROJ_ROW_PARTS, PROJ_MIN_PART_ROWS))
    return [slice(r0, r0 + step) for r0 in range(0, rows, step)]


def _proj_even_kernel(x_ref, g_ref, sc_ref, sh_ref, w_ref, *out_refs):
    for rs in _row_parts(x_ref.shape[0]):
        h = _modulated_norm(x_ref[rs, :], g_ref[...], sc_ref[...], sh_ref[...]).astype(BF16)
        off = 0
        for ref in out_refs:
            off = _store_cols(h, w_ref, off, ref, rs)


def _head_norm(t, gain):
    r = lax.rsqrt(jnp.mean(t * t, axis=-1, keepdims=True) + NORM_EPS)
    return (t * r) * gain


def _rope(t, cos, sin_signed):
    return t * cos + pltpu.roll(t, ATT_HEAD_DIM // 2, 1) * sin_signed


def _proj_odd_kernel(x_ref, g_ref, sc_ref, sh_ref, w_ref, cos_ref, sin_ref, qn_ref, kn_ref,
                     k_ref, v_ref, *qg_refs):
    h = _modulated_norm(x_ref[...], g_ref[...], sc_ref[...], sh_ref[...]).astype(BF16)
    cos = cos_ref[...]
    sin = sin_ref[...]
    scale = ATT_HEAD_DIM ** -0.5 * LOG2E

    width = HEADS_PER_DOT * ATT_HEAD_DIM
    dh = ATT_HEAD_DIM
    pr = lax.broadcasted_iota(jnp.int32, (2 * dh, 2 * dh), 0) // dh
    pc = lax.broadcasted_iota(jnp.int32, (2 * dh, 2 * dh), 1) // dh
    head_ones = jnp.where(pr == pc, 1.0, 0.0).astype(BF16)

    def project(col):
        return jnp.dot(h, w_ref[:, col:col + width], preferred_element_type=F32)

    def finish_heads(t4, ref, j0, gain, out_scale):
        for j in range(0, HEADS_PER_DOT, 2):
            t2 = t4[:, j * dh:(j + 2) * dh]
            mean_sq = jnp.dot((t2 * t2).astype(BF16), head_ones, preferred_element_type=F32) * (1.0 / dh)
            t2 = t2 * lax.rsqrt(mean_sq + NORM_EPS)
            for jj in range(2):
                t = _rope(t2[:, jj * dh:(jj + 1) * dh] * gain, cos, sin)
                c0 = (j0 + j + jj) * dh
                ref[:, c0:c0 + dh] = (t if out_scale is None else t * out_scale).astype(ref.dtype)

    work = []
    if qg_refs:
        q_ref, gate_ref = qg_refs
        for j0 in range(0, ATT_HEADS, HEADS_PER_DOT):
            c0 = j0 * dh

            def finish_gate(t4, c0=c0):
                gate_ref[:, c0:c0 + width] = _silu(t4)

            work.append((2 * ATT_KV_W + ATT_W + c0, finish_gate))
            work.append((2 * ATT_KV_W + c0,
                         functools.partial(finish_heads, ref=q_ref, j0=j0, gain=qn_ref[...], out_scale=scale)))
    for j0 in range(0, ATT_KV_HEADS, HEADS_PER_DOT):
        work.append((j0 * dh, functools.partial(finish_heads, ref=k_ref, j0=j0, gain=kn_ref[...], out_scale=None)))
    for col, finish in work:
        finish(project(col))
    _store_cols(h, w_ref, ATT_KV_W, v_ref)


def _row_tile(length, want):
    return min(length, want)


def _proj_even(x, norm_g, scale, shift, w_bf16, widths, tm):
    bsz, length, d = x.shape
    tm = _row_tile(length, tm)
    n = w_bf16.shape[1]
    row = lambda b, i: (b, i, 0)
    mod = lambda b, i: (b, 0, 0)
    return pl.pallas_call(
        _proj_even_kernel,
        grid=(bsz, length // tm),
        in_specs=[pl.BlockSpec((None, tm, d), row),
                  pl.BlockSpec((1, d), lambda b, i: (0, 0)),
                  pl.BlockSpec((None, 1, d), mod),
                  pl.BlockSpec((None, 1, d), mod),
                  pl.BlockSpec((d, n), lambda b, i: (0, 0), pipeline_mode=pl.Buffered(1))],
        out_specs=[pl.BlockSpec((None, tm, wd), row) for wd, _ in widths],
        out_shape=[jax.ShapeDtypeStruct((bsz, length, wd), dt) for wd, dt in widths],
        compiler_params=_params("parallel", "parallel"),
        name="proj_even",
    )(x, norm_g.reshape(1, d), scale, shift, w_bf16)


def _proj_odd(x, norm_g, scale, shift, w_bf16, cos, sin, q_norm, k_norm, with_queries, tm):
    bsz, length, d = x.shape
    tm = _row_tile(length, tm)
    n = w_bf16.shape[1]
    row = lambda b, i: (b, i, 0)
    mod = lambda b, i: (b, 0, 0)
    const = lambda b, i: (0, 0)
    widths = [(ATT_KV_W, BF16), (ATT_KV_W, BF16)]
    if with_queries:
        widths += [(ATT_W, BF16), (ATT_W, F32)]
    return pl.pallas_call(
        _proj_odd_kernel,
        grid=(bsz, length // tm),
        in_specs=[pl.BlockSpec((None, tm, d), row),
                  pl.BlockSpec((1, d), const),
                  pl.BlockSpec((None, 1, d), mod),
                  pl.BlockSpec((None, 1, d), mod),
                  pl.BlockSpec((d, n), const, pipeline_mode=pl.Buffered(1)),
                  pl.BlockSpec((tm, ATT_HEAD_DIM), lambda b, i: (i, 0)),
                  pl.BlockSpec((tm, ATT_HEAD_DIM), lambda b, i: (i, 0)),
                  pl.BlockSpec((1, ATT_HEAD_DIM), const),
                  pl.BlockSpec((1, ATT_HEAD_DIM), const)],
        out_specs=[pl.BlockSpec((None, tm, wd), row) for wd, _ in widths],
        out_shape=[jax.ShapeDtypeStruct((bsz, length, wd), dt) for wd, dt in widths],
        compiler_params=_params("parallel", "parallel"),
        name="proj_odd_q" if with_queries else "proj_odd_kv",
    )(x, norm_g.reshape(1, d), scale, shift, w_bf16, cos, sin,
      q_norm.reshape(1, ATT_HEAD_DIM), k_norm.reshape(1, ATT_HEAD_DIM))


def _conv_silu(win_s, prev_ref, cur_ref, next_ref, has_prev, has_next, cw_ref, cb_ref, x_s):
    rows = cur_ref.shape[0]
    half = (CONV_K - 1) // 2
    for j in range(CONV_CH // LANES):
        ls = slice(j * LANES, (j + 1) * LANES)
        win_s[j, 0:HALO, :] = jnp.where(has_prev, prev_ref[:, ls].astype(F32), 0.0)
        win_s[j, HALO:HALO + rows, :] = cur_ref[:, ls].astype(F32)
        win_s[j, HALO + rows:2 * HALO + rows, :] = jnp.where(has_next, next_ref[:, ls].astype(F32), 0.0)
        taps = [cw_ref[k:k + 1, ls] for k in range(CONV_K)]
        bias = cb_ref[:, ls]
        win = win_s.at[j]
        out = x_s.at[j]
        for c in range(rows // SSD_CHUNK):
            for a in range(CONV_ROW_STRIDE):
                r0 = HALO + c * SSD_CHUNK - half + a
                acc = bias + taps[0] * win[pl.ds(r0, SUBLANES, stride=CONV_ROW_STRIDE), :]
                for k in range(1, CONV_K):
                    acc = acc + taps[k] * win[pl.ds(r0 + k, SUBLANES, stride=CONV_ROW_STRIDE), :]
                out[pl.ds(c * CONV_OUT_ROWS + a, SUBLANES, stride=CONV_ROW_STRIDE), :] = _silu(acc)


def _ssd_fwd_kernel(p_ref, c_ref, n_ref, sm_ref, cw_ref, cb_ref, bias_row_ref, bias_col_ref,
                    alog_row_ref, alog_col_ref, dskip_ref, ef_ref, eb_ref, sf0_ref,
                    ya_ref, cm_ref, bt_ref, wxb_ref, p3_ref, sff_ref,
                    sf_s, win_s, x_s):
    i = pl.program_id(1)
    nc = pl.num_programs(1)
    t = SSD_CHUNK
    nh = SSD_HEADS
    slabs_x = SSD_WIDTH // LANES
    slab_b = slabs_x
    slab_c = slabs_x + SSD_GROUPS

    @pl.when(i == 0)
    def _():
        sf_s[...] = sf0_ref[...]

    lower, upper = _tri(t)
    ltri = jnp.where(lower, 1.0, 0.0).astype(BF16)
    utri = jnp.where(upper, 1.0, 0.0).astype(BF16)
    a_row = -jnp.exp(alog_row_ref[...])
    a_col = -jnp.exp(alog_col_ref[...])
    ef = ef_ref[...]
    eb = eb_ref[...]
    neg_inf = float("-inf")

    _conv_silu(win_s, p_ref, c_ref, n_ref, i > 0, i < nc - 1, cw_ref, cb_ref, x_s)
    n_chunks = sm_ref.shape[0] // t
    lane = lax.broadcasted_iota(jnp.int32, (t, LANES), 1)
    left = lane < SSD_HEAD_DIM
    heads_per_group = nh // SSD_GROUPS
    pairs_per_group = heads_per_group // 2

    def chunk_decays(c):
        rs = slice(c * t, (c + 1) * t)
        sm = sm_ref[rs, :]
        dt = _softplus(sm + bias_row_ref[...])
        dta = dt * a_row
        cs = _dot_split_rhs(ltri, dta)
        rc = _dot_split_rhs(utri, dta)
        dt_t = _softplus(sm.T[0:2 * nh, :] + bias_col_ref[...])
        dta_t = dt_t * a_col
        log2_dt_t = jnp.log2(dt_t)
        f_q = cs * LOG2E
        g_q = rc * LOG2E
        f_k = _dot_split_lhs(dta_t, utri) * LOG2E - log2_dt_t
        g_k = _dot_split_lhs(dta_t, ltri) * LOG2E - log2_dt_t
        dec_f = _dot_split_lhs(jnp.exp2(f_q), ef)
        wgt_f = _dot_split_lhs(dt * jnp.exp(cs[t - 1:t, :] - cs), ef)
        wgt_b = _dot_split_lhs(dt * jnp.exp(rc[0:1, :] - rc), eb)
        p3_ref[rs, :] = jnp.exp2(g_q)
        for p in range(slabs_x):
            ls = slice(p * LANES, (p + 1) * LANES)
            wxb_ref[rs, ls] = (x_s[p, c * CONV_OUT_ROWS:c * CONV_OUT_ROWS + t, :] * wgt_b[:, ls]).astype(BF16)
        return f_q, g_q, f_k, g_k, dec_f, wgt_f

    decays = [chunk_decays(c) for c in range(n_chunks)]

    for c in range(n_chunks):
        f_q, g_q, f_k, g_k, dec_f, wgt_f = decays[c]
        rs = slice(c * t, (c + 1) * t)
        xr = slice(c * CONV_OUT_ROWS, c * CONV_OUT_ROWS + t)

        def decay_matrix(h, cbg):
            lf = jnp.exp2(jnp.where(lower, f_q[:, h:h + 1] - f_k[h:h + 1, :], neg_inf))
            ub = jnp.exp2(jnp.where(upper, g_q[:, nh + h:nh + h + 1] - g_k[nh + h:nh + h + 1, :], neg_inf))
            return cbg * (lf + ub)

        for g in range(SSD_GROUPS):
            cg = x_s[slab_c + g, xr, :]
            bg = x_s[slab_b + g, xr, :]
            gs = slice(g * SSD_GROUP_W, (g + 1) * SSD_GROUP_W)
            cm_ref[rs, g * SSD_STATE:(g + 1) * SSD_STATE] = cg.astype(BF16)
            cbg = _dot_nt(cg, bg)
            state = sf_s[g]
            y_off = _dot(cg, state) * dec_f[:, gs]
            wx = []
            for pair in range(pairs_per_group):
                h0 = g * heads_per_group + 2 * pair
                p = h0 // 2
                ls = slice(p * LANES, (p + 1) * LANES)
                xp = x_s[p, xr, :]
                xb = xp.astype(BF16)
                y = jnp.where(left, _dot(decay_matrix(h0, cbg), xb), _dot(decay_matrix(h0 + 1, cbg), xb))
                ya_ref[rs, ls] = y + y_off[:, pair * LANES:(pair + 1) * LANES] + dskip_ref[:, ls] * xp
                wx.append((xp * wgt_f[:, ls]).astype(BF16))
            bt = bg.T.astype(BF16)
            bt_ref[c, g * SSD_STATE:(g + 1) * SSD_STATE, :] = bt
            sf_s[g] = dec_f[t - 1:t, gs] * state + jnp.dot(bt, jnp.concatenate(wx, axis=1),
                                                           preferred_element_type=F32)

    @pl.when(i == nc - 1)
    def _():
        sff_ref[...] = sf_s[...]


def _ssd_bwd_kernel(cm_ref, bt_ref, wxb_ref, p3_ref, ya_ref, z_ref, eb_ref, sn_ref, sb0_ref,
                    ys_ref, sbf_ref, sb_s):
    i = pl.program_id(1)
    nc = pl.num_programs(1)

    @pl.when(i == 0)
    def _():
        sb_s[...] = sb0_ref[...]

    t = SSD_CHUNK
    n_sub = bt_ref.shape[0]
    dec_b = _dot_split_lhs(p3_ref[...], eb_ref[...])
    for g in range(SSD_GROUPS):
        gs = slice(g * SSD_GROUP_W, (g + 1) * SSD_GROUP_W)
        ss = slice(g * SSD_STATE, (g + 1) * SSD_STATE)
        chunks = [slice(s * t, (s + 1) * t) for s in range(n_sub)]
        incs = [jnp.dot(bt_ref[s, ss, :], wxb_ref[rs, gs], preferred_element_type=F32)
                for s, rs in enumerate(chunks)]
        states = [None] * n_sub
        state = sb_s[g]
        for s in reversed(range(n_sub)):
            states[s] = state
            state = dec_b[s * t:s * t + 1, gs] * state + incs[s]
        sb_s[g] = state
        for s, rs in enumerate(chunks):
            y_off = jnp.dot(cm_ref[rs, ss], states[s].astype(BF16), preferred_element_type=F32) * dec_b[rs, gs]
            y = (ya_ref[rs, gs] + y_off) * _silu(z_ref[rs, gs].astype(F32))
            ys_ref[rs, gs] = _head_norm(y, sn_ref[:, gs]).astype(ys_ref.dtype)

    @pl.when(i == nc - 1)
    def _():
        sbf_ref[...] = sb_s[...]


SSD_STATE_SHAPE = (SSD_GROUPS, SSD_STATE, SSD_GROUP_W)


def _ssd_fwd(xbc, small, conv_w, conv_b, bias_row, bias_col, alog_row, alog_col, dskip_row, ef, eb, sf0):
    bsz, length, _ = xbc.shape
    t = SSD_CHUNK
    rows = _row_tile(length, SSD_FWD_ROWS)
    nc = length // t
    per = rows // HALO
    last_halo = length // HALO - 1

    def cur(b, i): return (b, i, 0)
    def prev(b, i): return (b, jnp.maximum(i * per - 1, 0), 0)
    def nxt(b, i): return (b, jnp.minimum((i + 1) * per, last_halo), 0)
    const2 = lambda b, i: (0, 0)
    state = lambda b, i: (b, 0, 0, 0)

    halo_spec = lambda f: pl.BlockSpec((None, HALO, CONV_CH), f)
    chunk_spec = lambda w: pl.BlockSpec((None, rows, w), cur)
    state_spec = pl.BlockSpec((None,) + SSD_STATE_SHAPE, state)
    return dict(
        grid=(bsz, length // rows),
        in_specs=[halo_spec(prev), chunk_spec(CONV_CH), halo_spec(nxt), chunk_spec(SMALL_W),
                  pl.BlockSpec((CONV_K, CONV_CH), const2),
                  pl.BlockSpec((1, CONV_CH), const2),
                  pl.BlockSpec((1, SMALL_W), const2),
                  pl.BlockSpec((2 * SSD_HEADS, 1), const2),
                  pl.BlockSpec((1, SMALL_W), const2),
                  pl.BlockSpec((2 * SSD_HEADS, 1), const2),
                  pl.BlockSpec((1, SSD_WIDTH), const2),
                  pl.BlockSpec((SMALL_W, SSD_WIDTH), const2),
                  pl.BlockSpec((SMALL_W, SSD_WIDTH), const2),
                  state_spec],
        out_specs=[chunk_spec(SSD_WIDTH), chunk_spec(SSD_BC_W),
                   pl.BlockSpec((None, rows // t, SSD_BC_W, t), lambda b, i: (b, i, 0, 0)),
                   chunk_spec(SSD_WIDTH), chunk_spec(SMALL_W), state_spec],
        out_shape=[jax.ShapeDtypeStruct((bsz, length, SSD_WIDTH), F32),
                   jax.ShapeDtypeStruct((bsz, length, SSD_BC_W), BF16),
                   jax.ShapeDtypeStruct((bsz, nc, SSD_BC_W, t), BF16),
                   jax.ShapeDtypeStruct((bsz, length, SSD_WIDTH), BF16),
                   jax.ShapeDtypeStruct((bsz, length, SMALL_W), F32),
                   jax.ShapeDtypeStruct((bsz,) + SSD_STATE_SHAPE, F32)],
        scratch_shapes=[pltpu.VMEM(SSD_STATE_SHAPE, F32),
                        pltpu.VMEM((CONV_CH // LANES, rows + 2 * HALO, LANES), F32),
                        pltpu.VMEM((CONV_CH // LANES, (rows // t) * CONV_OUT_ROWS, LANES), F32)],
        args=(xbc, xbc, xbc, small, conv_w, conv_b, bias_row, bias_col, alog_row, alog_col, dskip_row, ef, eb, sf0))


def _ssd_bwd(cm, bt, wxb, p3, ya, z, eb, ssd_norm, sb0, rows):
    bsz, length, _ = ya.shape
    t = SSD_CHUNK
    n_steps = length // rows
    rev = lambda b, i: (b, n_steps - 1 - i, 0)
    const2 = lambda b, i: (0, 0)
    state = lambda b, i: (b, 0, 0, 0)
    chunk_spec = lambda w: pl.BlockSpec((None, rows, w), rev)
    state_spec = pl.BlockSpec((None,) + SSD_STATE_SHAPE, state)
    return dict(
        in_specs=[chunk_spec(SSD_BC_W),
                  pl.BlockSpec((None, rows // t, SSD_BC_W, t), lambda b, i: (b, n_steps - 1 - i, 0, 0)),
                  chunk_spec(SSD_WIDTH), chunk_spec(SMALL_W), chunk_spec(SSD_WIDTH), chunk_spec(SSD_WIDTH),
                  pl.BlockSpec((SMALL_W, SSD_WIDTH), const2),
                  pl.BlockSpec((1, SSD_WIDTH), const2),
                  state_spec],
        state_spec=state_spec,
        state_shape=jax.ShapeDtypeStruct((bsz,) + SSD_STATE_SHAPE, F32),
        state_scratch=pltpu.VMEM(SSD_STATE_SHAPE, F32),
        args=(cm, bt, wxb, p3, ya, z, eb, ssd_norm.reshape(1, SSD_WIDTH), sb0))


def _gla_fwd_kernel(q_ref, k_ref, v_ref, sm_ref, wg_ref, gb_ref, sf0_ref,
                    oa_ref, qdb_ref, kwb_ref, decb_ref, sff_ref, sf_s):
    i = pl.program_id(1)
    n_steps = pl.num_programs(1)
    t = GLA_CHUNK
    rows = q_ref.shape[0]
    n_sub = rows // t
    dk, dv = GLA_KEY_DIM, GLA_VAL_DIM
    qscale = dk ** -0.5
    inv_norm = 1.0 / GLA_GATE_NORMALIZER

    @pl.when(i == 0)
    def _():
        sf_s[...] = sf0_ref[...]

    lower, upper = _tri(t)
    grp = min(rows, GLA_CUMSUM_ROWS)
    row = lax.broadcasted_iota(jnp.int32, (grp, grp), 0)
    col = lax.broadcasted_iota(jnp.int32, (grp, grp), 1)
    diff = row - col
    pos = row % t
    bd_lower = jnp.where(diff >= 0, jnp.where(diff <= pos, 1.0, 0.0), 0.0).astype(BF16)
    bd_upper = jnp.where(diff <= 0, jnp.where(-diff <= t - 1 - pos, 1.0, 0.0), 0.0).astype(BF16)

    hi, lo = _split(sm_ref[...])
    wg_hi, wg_lo = _split(wg_ref[...])
    logits = (jnp.dot(hi, wg_hi, preferred_element_type=F32) + jnp.dot(lo, wg_hi, preferred_element_type=F32)
              + jnp.dot(hi, wg_lo, preferred_element_type=F32)) + gb_ref[...]
    lg = _log_sigmoid(logits) * inv_norm
    groups = [slice(r0, r0 + grp) for r0 in range(0, rows, grp)]
    cs = jnp.concatenate([_dot_split_rhs(bd_lower, lg[gr, 0:GLA_K]) for gr in groups], axis=0)
    rc = jnp.concatenate([_dot_split_rhs(bd_upper, lg[gr, GLA_K:2 * GLA_K]) for gr in groups], axis=0)

    def per_chunk_row(v, offset):
        return jnp.concatenate([jnp.broadcast_to(v[s * t + offset:s * t + offset + 1, :], (t, v.shape[1]))
                                for s in range(n_sub)], axis=0)

    for h in range(GLA_HEADS):
        ks = slice(h * dk, (h + 1) * dk)
        vs = slice(h * dv, (h + 1) * dv)
        c = cs[:, ks]
        r = rc[:, ks]
        qh = q_ref[:, ks].astype(F32) * qscale
        kh = k_ref[:, ks].astype(F32)
        qdf = (qh * jnp.exp(c)).astype(BF16)
        kif = (kh * jnp.exp(-c)).astype(BF16)
        qdb = (qh * jnp.exp(r)).astype(BF16)
        kib = (kh * jnp.exp(-r)).astype(BF16)
        kwf = (kh * jnp.exp(per_chunk_row(c, t - 1) - c)).astype(BF16)
        qdb_ref[:, ks] = qdb
        kwb_ref[:, ks] = (kh * jnp.exp(per_chunk_row(r, 0) - r)).astype(BF16)
        chunks = [slice(s * t, (s + 1) * t) for s in range(n_sub)]
        vhs = [v_ref[rs, vs].astype(BF16) for rs in chunks]
        atts = [(jnp.where(lower, _dot_nt(qdf[rs], kif[rs]), 0.0)
                 + jnp.where(upper, _dot_nt(qdb[rs], kib[rs]), 0.0)).astype(BF16) for rs in chunks]
        incs = [_dot_tn(vh, kwf[rs]) for vh, rs in zip(vhs, chunks)]
        states = [sf_s[h]]
        for s in range(n_sub):
            decb_ref[s, :, ks] = jnp.exp(r[s * t:s * t + 1, :])
            states.append(states[s] * jnp.exp(c[s * t + t - 1:s * t + t, :]) + incs[s])
        sf_s[h] = states[n_sub]
        for s, rs in enumerate(chunks):
            oa_ref[rs, vs] = (jnp.dot(atts[s], vhs[s], preferred_element_type=F32)
                              + _dot_nt(qdf[rs], states[s]))

    @pl.when(i == n_steps - 1)
    def _():
        sff_ref[...] = sf_s[...]


def _gla_bwd_kernel(qdb_ref, kwb_ref, decb_ref, v_ref, oa_ref, g_ref, gn_ref, sb0_ref,
                    os_ref, sbf_ref, sb_s):
    i = pl.program_id(1)
    n_steps = pl.num_programs(1)
    t = GLA_CHUNK
    n_sub = qdb_ref.shape[0] // t
    dk, dv = GLA_KEY_DIM, GLA_VAL_DIM

    @pl.when(i == 0)
    def _():
        sb_s[...] = sb0_ref[...]

    for h in range(GLA_HEADS):
        ks = slice(h * dk, (h + 1) * dk)
        vs = slice(h * dv, (h + 1) * dv)
        state = sb_s[h]
        for s in reversed(range(n_sub)):
            rs = slice(s * t, (s + 1) * t)
            o = oa_ref[rs, vs] + _dot_nt(qdb_ref[rs, ks], state)
            state = state * decb_ref[s, :, ks] + _dot_tn(v_ref[rs, vs], kwb_ref[rs, ks])
            os_ref[rs, vs] = (_head_norm(o, gn_ref[:, vs]) * _silu(g_ref[rs, vs].astype(F32))).astype(os_ref.dtype)
        sb_s[h] = state

    @pl.when(i == n_steps - 1)
    def _():
        sbf_ref[...] = sb_s[...]


GLA_STATE_SHAPE = (GLA_HEADS, GLA_VAL_DIM, GLA_KEY_DIM)


def _gla_fwd(q, k, v, small, wg, gb, sf0, rows):
    bsz, length, _ = q.shape
    rows = _row_tile(length, rows)
    n_steps = length // rows
    n_sub = rows // GLA_CHUNK
    fwd = lambda b, i: (b, i, 0)
    const2 = lambda b, i: (0, 0)
    state = lambda b, i: (b, 0, 0, 0)
    blk = lambda w: pl.BlockSpec((None, rows, w), fwd)
    state_spec = pl.BlockSpec((None,) + GLA_STATE_SHAPE, state)
    return dict(
        grid=(bsz, n_steps),
        in_specs=[blk(GLA_K), blk(GLA_K), blk(GLA_V), blk(SMALL_W),
                  pl.BlockSpec((SMALL_W, 2 * GLA_K), const2),
                  pl.BlockSpec((1, 2 * GLA_K), const2),
                  state_spec],
        out_specs=[blk(GLA_V), blk(GLA_K), blk(GLA_K),
                   pl.BlockSpec((None, n_sub, 1, GLA_K), lambda b, i: (b, i, 0, 0)),
                   state_spec],
        out_shape=[jax.ShapeDtypeStruct((bsz, length, GLA_V), F32),
                   jax.ShapeDtypeStruct((bsz, length, GLA_K), BF16),
                   jax.ShapeDtypeStruct((bsz, length, GLA_K), BF16),
                   jax.ShapeDtypeStruct((bsz, length // GLA_CHUNK, 1, GLA_K), F32),
                   jax.ShapeDtypeStruct((bsz,) + GLA_STATE_SHAPE, F32)],
        scratch_shapes=[pltpu.VMEM(GLA_STATE_SHAPE, F32)],
        args=(q, k, v, small, wg, gb, sf0))


def _sweep_call(body, name, call):
    return pl.pallas_call(
        body,
        grid=call["grid"],
        in_specs=call["in_specs"],
        out_specs=call["out_specs"],
        out_shape=call["out_shape"],
        scratch_shapes=call["scratch_shapes"],
        compiler_params=_params("arbitrary", "arbitrary"),
        name=name,
    )(*call["args"])


def _gla_bwd(qdb, kwb, decb, v, oa, g, gla_norm, sb0, rows):
    bsz, length, _ = oa.shape
    n_steps = length // rows
    n_sub = rows // GLA_CHUNK
    rev = lambda b, i: (b, n_steps - 1 - i, 0)
    const2 = lambda b, i: (0, 0)
    state = lambda b, i: (b, 0, 0, 0)
    blk = lambda w: pl.BlockSpec((None, rows, w), rev)
    state_spec = pl.BlockSpec((None,) + GLA_STATE_SHAPE, state)
    return dict(
        in_specs=[blk(GLA_K), blk(GLA_K),
                  pl.BlockSpec((None, n_sub, 1, GLA_K), lambda b, i: (b, n_steps - 1 - i, 0, 0)),
                  blk(GLA_V), blk(GLA_V), blk(GLA_V),
                  pl.BlockSpec((1, GLA_V), const2),
                  state_spec],
        state_spec=state_spec,
        state_shape=jax.ShapeDtypeStruct((bsz,) + GLA_STATE_SHAPE, F32),
        state_scratch=pltpu.VMEM(GLA_STATE_SHAPE, F32),
        args=(qdb, kwb, decb, v, oa, g, gla_norm.reshape(1, GLA_V), sb0))


SSD_BWD_INPUTS = 9
GLA_BWD_INPUTS = 8


def _even_tail_kernel(*refs):
    ssd_in = refs[:SSD_BWD_INPUTS]
    gla_in = refs[SSD_BWD_INPUTS:SSD_BWD_INPUTS + GLA_BWD_INPUTS]
    x_ref, gate_ref, w_ref, y_ref, ssd_fin_ref, gla_fin_ref, ssd_s, gla_s, mix_s = \
        refs[SSD_BWD_INPUTS + GLA_BWD_INPUTS:]
    _ssd_bwd_kernel(*ssd_in, mix_s.at[:, 0:SSD_WIDTH], ssd_fin_ref, ssd_s)
    _gla_bwd_kernel(*gla_in, mix_s.at[:, SSD_WIDTH:SSD_WIDTH + GLA_V], gla_fin_ref, gla_s)
    y_ref[...] = x_ref[...] + gate_ref[...] * jnp.dot(mix_s[...], w_ref[...], preferred_element_type=F32)


def _even_tail(x, gate, w_bf16, ssd_args, gla_args):
    bsz, length, d = x.shape
    rows = _row_tile(length, EVEN_TAIL_ROWS)
    n_steps = length // rows
    ssd = _ssd_bwd(*ssd_args, rows)
    gla = _gla_bwd(*gla_args, rows)
    assert len(ssd["in_specs"]) == SSD_BWD_INPUTS and len(gla["in_specs"]) == GLA_BWD_INPUTS
    rev = lambda b, i: (b, n_steps - 1 - i, 0)
    return pl.pallas_call(
        _even_tail_kernel,
        grid=(bsz, n_steps),
        in_specs=(ssd["in_specs"] + gla["in_specs"]
                  + [pl.BlockSpec((None, rows, d), rev),
                     pl.BlockSpec((None, 1, d), lambda b, i: (b, 0, 0)),
                     pl.BlockSpec(w_bf16.shape, lambda b, i: (0, 0), pipeline_mode=pl.Buffered(1))]),
        out_specs=[pl.BlockSpec((None, rows, d), rev), ssd["state_spec"], gla["state_spec"]],
        out_shape=[jax.ShapeDtypeStruct((bsz, length, d), F32), ssd["state_shape"], gla["state_shape"]],
        scratch_shapes=[ssd["state_scratch"], gla["state_scratch"],
                        pltpu.VMEM((rows, SSD_WIDTH + GLA_V), BF16)],
        compiler_params=_params("arbitrary", "arbitrary"),
        name="even_tail",
    )(*ssd["args"], *gla["args"], x, gate, w_bf16)


def _out_proj_kernel(x_ref, gate_ref, *refs):
    *in_refs, w_ref, y_ref = refs
    acc = None
    off = 0
    for ref in in_refs:
        n = ref.shape[-1]
        part = jnp.dot(ref[...], w_ref[off:off + n, :], preferred_element_type=F32)
        acc = part if acc is None else acc + part
        off += n
    y_ref[...] = x_ref[...] + gate_ref[...] * acc


def _out_proj(x, gate, mixed, w_bf16, tm, name):
    bsz, length, d = x.shape
    tm = _row_tile(length, tm)
    row = lambda b, i: (b, i, 0)
    return pl.pallas_call(
        _out_proj_kernel,
        grid=(bsz, length // tm),
        in_specs=([pl.BlockSpec((None, tm, d), row),
                   pl.BlockSpec((None, 1, d), lambda b, i: (b, 0, 0))]
                  + [pl.BlockSpec((None, tm, m.shape[-1]), row) for m in mixed]
                  + [pl.BlockSpec(w_bf16.shape, lambda b, i: (0, 0), pipeline_mode=pl.Buffered(1))]),
        out_specs=pl.BlockSpec((None, tm, d), row),
        out_shape=jax.ShapeDtypeStruct((bsz, length, d), F32),
        compiler_params=_params("parallel", "parallel"),
        name=name,
    )(x, gate, *mixed, w_bf16)


def _attn_kernel(sink_ref, q_ref, k_ref, v_ref, kc_ref, vc_ref, g_ref, o_ref):
    j = pl.program_id(1)
    i = pl.program_id(2)
    length = k_ref.shape[0]
    blk = ATT_BLOCK
    band = 3 * blk
    dh = ATT_HEAD_DIM
    n_q = q_ref.shape[0] // blk
    n_ctx = kc_ref.shape[0]
    row_minus_col = (lax.broadcasted_iota(jnp.int32, (blk, band), 0)
                     - lax.broadcasted_iota(jnp.int32, (blk, band), 1))
    sink = jnp.concatenate([jnp.full((blk, LANES), sink_ref[j * ATT_GROUP + g] * LOG2E, F32)
                            for g in range(ATT_GROUP)], axis=0)
    kc = kc_ref[...]
    vc_ext = jnp.concatenate([vc_ref[...], jnp.ones((n_ctx, dh), BF16)], axis=1)
    ones_band = jnp.ones((band, dh), BF16)

    def window_start(qb):
        blk_idx = i * n_q + qb
        return blk_idx, pl.multiple_of(jnp.clip((blk_idx - 1) * blk, 0, length - band), blk)

    def scores(qb):
        _, start = window_start(qb)
        rs = slice(qb * blk, (qb + 1) * blk)
        q = jnp.concatenate([q_ref[rs, g * dh:(g + 1) * dh] for g in range(ATT_GROUP)], axis=0)
        return _dot_nt(q, k_ref[pl.ds(start, band), :]), _dot_nt(q, kc)

    pending = scores(0)
    for qb in range(n_q):
        s_band, s_ctx = pending
        if qb + 1 < n_q:
            pending = scores(qb + 1)
        blk_idx, start = window_start(qb)
        inside = jnp.abs(row_minus_col + (blk_idx * blk - start)) <= WINDOW
        v_all = jnp.concatenate([jnp.concatenate([v_ref[pl.ds(start, band), :], ones_band], axis=1),
                                 vc_ext], axis=0)
        rs = slice(qb * blk, (qb + 1) * blk)
        cols = []
        for c in range(band // LANES):
            ls = slice(c * LANES, (c + 1) * LANES)
            cols.append(jnp.concatenate(
                [jnp.where(inside[:, ls], s_band[g * blk:(g + 1) * blk, ls], float("-inf"))
                 for g in range(ATT_GROUP)], axis=0))
        for c in range(n_ctx // LANES):
            cols.append(s_ctx[:, c * LANES:(c + 1) * LANES])
        m = jnp.maximum(sink, jnp.max(functools.reduce(jnp.maximum, cols), axis=-1, keepdims=True))
        p = jnp.concatenate([jnp.exp2(col - m) for col in cols], axis=1).astype(BF16)
        acc = jnp.dot(p, v_all, preferred_element_type=F32)
        out = acc[:, 0:dh] / (acc[:, dh:2 * dh] + jnp.exp2(sink - m))
        for g in range(ATT_GROUP):
            cs = slice(g * dh, (g + 1) * dh)
            o_ref[rs, cs] = (out[g * blk:(g + 1) * blk, :] * g_ref[rs, cs]).astype(o_ref.dtype)


def _attention(sink, q, k, v, kc, vc, gate):
    bsz, length, _ = q.shape
    n_ctx = kc.shape[1]
    rows = _row_tile(length, ATT_Q_BLOCKS * ATT_BLOCK)
    gw = ATT_GROUP * ATT_HEAD_DIM
    qmap = lambda b, j, i, s: (b, i, j)
    kvmap = lambda b, j, i, s: (b, 0, j)
    grid_spec = pltpu.PrefetchScalarGridSpec(
        num_scalar_prefetch=1,
        grid=(bsz, ATT_KV_HEADS, length // rows),
        in_specs=[pl.BlockSpec((None, rows, gw), qmap),
                  pl.BlockSpec((None, length, ATT_HEAD_DIM), kvmap),
                  pl.BlockSpec((None, length, ATT_HEAD_DIM), kvmap),
                  pl.BlockSpec((None, n_ctx, ATT_HEAD_DIM), kvmap),
                  pl.BlockSpec((None, n_ctx, ATT_HEAD_DIM), kvmap),
                  pl.BlockSpec((None, rows, gw), qmap)],
        out_specs=pl.BlockSpec((None, rows, gw), qmap),
    )
    return pl.pallas_call(
        _attn_kernel,
        grid_spec=grid_spec,
        out_shape=jax.ShapeDtypeStruct((bsz, length, ATT_W), BF16),
        compiler_params=_params("parallel", "parallel", "arbitrary"),
        name="attention",
    )(sink, q, k, v, kc, vc, gate)


E_IN_SIZES = (SSD_WIDTH, CONV_CH, 2 * SSD_HEADS, GLA_K, GLA_K, GLA_V, GLA_V, 2 * GLA_RANK)
E_IN = sum(E_IN_SIZES)
E_ALIGNED = SSD_WIDTH + CONV_CH
E_SHIFT = 2 * SSD_HEADS
E_MAIN = 2 * GLA_K + 2 * GLA_V
E_OUT = E_ALIGNED + E_MAIN + SMALL_W
PREP_ROWS = 256
O_IN = 2 * ATT_KV_W + 2 * ATT_W


def _prep_even_kernel(wt_ref, o_ref):
    cols = wt_ref.shape[1]
    for dst in range(0, E_ALIGNED + E_MAIN, LANES):
        src = dst if dst < E_ALIGNED else dst + E_SHIFT
        o_ref[:, dst:dst + LANES] = wt_ref[src:src + LANES, :].T.astype(BF16)
    small = jnp.concatenate([wt_ref[E_ALIGNED:E_ALIGNED + E_SHIFT, :],
                             wt_ref[E_IN - 2 * GLA_RANK:E_IN, :],
                             jnp.zeros((SMALL_W - E_SHIFT - 2 * GLA_RANK, cols), F32)], axis=0)
    o_ref[:, E_ALIGNED + E_MAIN:E_OUT] = small.T.astype(BF16)


def _prep_even(w_in):
    d = w_in.shape[0]
    return pl.pallas_call(
        _prep_even_kernel,
        grid=(d // PREP_ROWS,),
        in_specs=[pl.BlockSpec((E_IN, PREP_ROWS), lambda i: (0, i))],
        out_specs=pl.BlockSpec((PREP_ROWS, E_OUT), lambda i: (i, 0)),
        out_shape=jax.ShapeDtypeStruct((d, E_OUT), BF16),
        compiler_params=_params("parallel"),
        name="prep_even",
    )(jnp.swapaxes(w_in, 0, 1))


def _prep_odd_kernel(w_ref, perm_ref, o_ref):
    def copy(c0, width):
        o_ref[:, c0:c0 + width] = w_ref[:, c0:c0 + width].astype(BF16)

    def permute(c0, width):
        for c in range(c0, c0 + width, 2 * LANES):
            o_ref[:, c:c + 2 * LANES] = jnp.dot(w_ref[:, c:c + 2 * LANES].astype(BF16), perm_ref[...],
                                                preferred_element_type=F32).astype(BF16)

    permute(0, ATT_KV_W)
    copy(ATT_KV_W, ATT_KV_W)
    permute(2 * ATT_KV_W, ATT_W)
    copy(2 * ATT_KV_W + ATT_W, ATT_W)


def _rope_perm(n):
    out = np.arange(n)
    head, rem = out // ATT_HEAD_DIM, out % ATT_HEAD_DIM
    half, axis, f = rem // (2 * ROPE_FREQS), (rem // ROPE_FREQS) % 2, rem % ROPE_FREQS
    return head * ATT_HEAD_DIM + axis * 2 * ROPE_FREQS + half * ROPE_FREQS + f


def _prep_odd(w_in):
    d = w_in.shape[0]
    src = _rope_perm(2 * LANES)
    perm = jnp.asarray(np.arange(2 * LANES)[:, None] == src[None, :], BF16)
    return pl.pallas_call(
        _prep_odd_kernel,
        grid=(d // PREP_ROWS,),
        in_specs=[pl.BlockSpec((PREP_ROWS, O_IN), lambda i: (i, 0)),
                  pl.BlockSpec((2 * LANES, 2 * LANES), lambda i: (0, 0))],
        out_specs=pl.BlockSpec((PREP_ROWS, O_IN), lambda i: (i, 0)),
        out_shape=jax.ShapeDtypeStruct((d, O_IN), BF16),
        compiler_params=_params("parallel"),
        name="prep_odd",
    )(w_in, perm)


EVEN_WIDTHS = ((SSD_WIDTH, BF16), (CONV_CH, BF16), (GLA_K, BF16), (GLA_K, BF16), (GLA_V, BF16), (GLA_V, BF16),
               (SMALL_W, F32))


def _expansion_matrices():
    rows = np.arange(SMALL_W)[:, None]
    heads = (np.arange(SSD_WIDTH) // SSD_HEAD_DIM)[None, :]
    ef = (rows == heads).astype(np.float32)
    eb = (rows - SSD_HEADS == heads).astype(np.float32)
    return jnp.asarray(ef, BF16), jnp.asarray(eb, BF16)


def _pad_lanes(v, width):
    return jnp.pad(v, ((0, 0), (0, width - v.shape[1])))


def _rope_tables(length):
    rows = length // GRID_W
    row = np.repeat(np.arange(rows, dtype=np.float64), GRID_W)
    col = np.tile(np.arange(GRID_W, dtype=np.float64), rows)
    inv = 1.0 / (ROPE_BASE ** (np.arange(ROPE_FREQS, dtype=np.float64) / ROPE_FREQS))
    ang_r = row[:, None] * inv
    ang_c = col[:, None] * inv
    cos = np.concatenate([np.cos(ang_r), np.cos(ang_c), np.cos(ang_r), np.cos(ang_c)], axis=1)
    sin = np.concatenate([-np.sin(ang_r), -np.sin(ang_c), np.sin(ang_r), np.sin(ang_c)], axis=1)
    return jnp.asarray(cos, F32), jnp.asarray(sin, F32)


def _rope_head_layout(v):
    return v[_rope_perm(ATT_HEAD_DIM)]


def _mod_rows(mod, rows, bsz, d):
    picked = jnp.broadcast_to(mod[rows], (bsz, 3 * d)) if isinstance(rows, int) else mod[rows]
    return [picked[:, None, j * d:(j + 1) * d] for j in range(3)]


PROJ_ROWS = 512
OUT_ROWS = 1024
ATT_Q_BLOCKS = 16
GLA_ROWS = 512
EVEN_TAIL_ROWS = 512
GLA_CUMSUM_ROWS = 256
SSD_FWD_ROWS = 4 * SSD_CHUNK
CVEC_ROWS = SUBLANES


def kernel(x, c, ctx, c_ctx, e_norm, e_mod_w, e_mod_b, e_w_in, e_conv_w, e_conv_b, e_dt_bias, e_a_log,
           e_d_skip, e_ssd_norm, e_gla_gate_w, e_gla_gate_b, e_gla_norm, e_w_out, o_norm, o_mod_w, o_mod_b,
           o_w_in, o_q_norm, o_k_norm, o_sink, o_w_out):
    bsz, length, d = x.shape
    n_ctx = ctx.shape[1]
    assert e_norm.shape[0] == 1 and o_norm.shape[0] == 1, "two-layer block only"
    assert length % SSD_CHUNK == 0 and n_ctx % SSD_CHUNK == 0 and length >= 3 * ATT_BLOCK
    assert bsz + 1 <= CVEC_ROWS

    cvecs = jnp.zeros((CVEC_ROWS, d), F32).at[:bsz].set(c).at[bsz].set(c_ctx)
    lat_rows = slice(0, bsz)

    mod = _adaln(cvecs, e_mod_w[0], e_mod_b[0])
    shift, scale, gate = _mod_rows(mod, lat_rows, bsz, d)
    c_shift, c_scale, c_gate = _mod_rows(mod, bsz, bsz, d)
    w_in = _prep_even(e_w_in[0])
    w_out = e_w_out[0].astype(BF16)
    ef, eb = _expansion_matrices()
    nh2 = 2 * SSD_HEADS
    bias_flat = e_dt_bias[0].reshape(1, nh2)
    alog_flat = e_a_log[0].reshape(1, nh2)
    bias_row, alog_row = _pad_lanes(bias_flat, SMALL_W), _pad_lanes(alog_flat, SMALL_W)
    bias_col, alog_col = bias_flat.reshape(nh2, 1), alog_flat.reshape(nh2, 1)
    dskip_row = jnp.repeat(e_d_skip[0], SSD_HEAD_DIM).reshape(1, SSD_WIDTH)
    conv_b = e_conv_b[0].reshape(1, CONV_CH)
    wg = jnp.zeros((SMALL_W, 2 * GLA_K), F32)
    wg = wg.at[nh2:nh2 + GLA_RANK, 0:GLA_K].set(e_gla_gate_w[0, 0])
    wg = wg.at[nh2 + GLA_RANK:nh2 + 2 * GLA_RANK, GLA_K:2 * GLA_K].set(e_gla_gate_w[0, 1])
    gb = e_gla_gate_b[0].reshape(1, 2 * GLA_K)

    def even_layer(stream, sc, sh, gt, ssd_init, gla_init):
        z, xbc, q, k, v, g, small = _proj_even(stream, e_norm[0], sc, sh, w_in, EVEN_WIDTHS, PROJ_ROWS)
        ya, cm, bt, wxb, p3, ssd_f = _sweep_call(
            _ssd_fwd_kernel, "ssd_fwd",
            _ssd_fwd(xbc, small, e_conv_w[0], conv_b, bias_row, bias_col, alog_row, alog_col, dskip_row,
                     ef, eb, ssd_init[0]))
        oa, qdb, kwb, decb, gla_f = _sweep_call(
            _gla_fwd_kernel, "gla_fwd", _gla_fwd(q, k, v, small, wg, gb, gla_init[0], GLA_ROWS))
        out, ssd_b, gla_b = _even_tail(stream, gt, w_out,
                                       (cm, bt, wxb, p3, ya, z, eb, e_ssd_norm[0], ssd_init[1]),
                                       (qdb, kwb, decb, v, oa, g, e_gla_norm[0], gla_init[1]))
        return out, (ssd_f, ssd_b), (gla_f, gla_b)

    ssd0 = jnp.zeros((bsz,) + SSD_STATE_SHAPE, F32)
    gla0 = jnp.zeros((bsz,) + GLA_STATE_SHAPE, F32)
    xc, ssd_fin, gla_fin = even_layer(ctx, c_scale, c_shift, c_gate, (ssd0, ssd0), (gla0, gla0))
    x, _, _ = even_layer(x, scale, shift, gate, ssd_fin, gla_fin)

    mod = _adaln(cvecs, o_mod_w[0], o_mod_b[0])
    shift, scale, gate = _mod_rows(mod, lat_rows, bsz, d)
    c_shift, c_scale, _ = _mod_rows(mod, bsz, bsz, d)
    w_in = _prep_odd(o_w_in[0])
    q_norm = _rope_head_layout(o_q_norm[0])
    k_norm = _rope_head_layout(o_k_norm[0])
    cos, sin = _rope_tables(length)
    no_rot = (jnp.ones((n_ctx, ATT_HEAD_DIM), F32), jnp.zeros((n_ctx, ATT_HEAD_DIM), F32))
    kc, vc = _proj_odd(xc, o_norm[0], c_scale, c_shift, w_in[:, :2 * ATT_KV_W], *no_rot,
                       q_norm, k_norm, False, PROJ_ROWS)
    k, v, q, g = _proj_odd(x, o_norm[0], scale, shift, w_in, cos, sin, q_norm, k_norm, True, PROJ_ROWS)
    o = _attention(o_sink[0].astype(F32), q, k, v, kc, vc, g)
    return _out_proj(x, gate, (o,), o_w_out[0].astype(BF16), OUT_ROWS, "out_odd")
```

```python
import functools

import jax
import jax.numpy as jnp
import numpy as np
from jax import lax
from jax.experimental import pallas as pl
from jax.experimental.pallas import tpu as pltpu

F32 = jnp.float32
BF16 = jnp.bfloat16

GRID_W = 64
SSD_HEADS = 16
SSD_HEAD_DIM = 64
SSD_WIDTH = SSD_HEADS * SSD_HEAD_DIM
SSD_GROUPS = 2
SSD_STATE = 128
SSD_CHUNK = 128
CONV_K = 5
CONV_CH = SSD_WIDTH + 2 * SSD_GROUPS * SSD_STATE
GLA_HEADS = 4
GLA_KEY_DIM = 128
GLA_VAL_DIM = 256
GLA_K = GLA_HEADS * GLA_KEY_DIM
GLA_V = GLA_HEADS * GLA_VAL_DIM
GLA_RANK = 16
GLA_GATE_NORMALIZER = 16.0
GLA_CHUNK = 64
ATT_HEADS = 16
ATT_KV_HEADS = 4
ATT_GROUP = ATT_HEADS // ATT_KV_HEADS
ATT_HEAD_DIM = 128
ATT_W = ATT_HEADS * ATT_HEAD_DIM
ATT_KV_W = ATT_KV_HEADS * ATT_HEAD_DIM
WINDOW = 128
ATT_BLOCK = 128
ROPE_BASE = 10000.0
ROPE_FREQS = ATT_HEAD_DIM // 4
NORM_EPS = 1e-6
LOG2E = 1.4426950408889634

LANES = 128
SUBLANES = 8
VMEM_LIMIT_BYTES = 56 * 1024 * 1024

SSD_GROUP_W = SSD_WIDTH // SSD_GROUPS
SSD_BC_W = SSD_GROUPS * SSD_STATE
SMALL_W = LANES
HALO = 2 * SUBLANES
HEADS_PER_DOT = 4
PROJ_ROW_PARTS = 4
PROJ_MIN_PART_ROWS = 128
CONV_ROW_STRIDE = 2 * SUBLANES + 1
CONV_OUT_ROWS = CONV_ROW_STRIDE * SUBLANES
assert SSD_CHUNK <= CONV_OUT_ROWS <= SSD_CHUNK + HALO - (CONV_K - 1) // 2


def _dot(a, b):
    return jnp.dot(a.astype(BF16), b.astype(BF16), preferred_element_type=F32)


def _dot_nt(a, b):
    return lax.dot_general(a.astype(BF16), b.astype(BF16), (((1,), (1,)), ((), ())),
                           preferred_element_type=F32)


def _dot_tn(a, b):
    return lax.dot_general(a.astype(BF16), b.astype(BF16), (((0,), (0,)), ((), ())),
                           preferred_element_type=F32)


def _split(v):
    hi = v.astype(BF16)
    lo = (v - hi.astype(F32)).astype(BF16)
    return hi, lo


def _dot_split_lhs(v, m):
    hi, lo = _split(v)
    return (jnp.dot(hi, m, preferred_element_type=F32) + jnp.dot(lo, m, preferred_element_type=F32))


def _dot_split_rhs(m, v):
    hi, lo = _split(v)
    return (jnp.dot(m, hi, preferred_element_type=F32) + jnp.dot(m, lo, preferred_element_type=F32))


def _dot3(a, b):
    ah, al = _split(a)
    bh, bl = _split(b)
    return (jnp.dot(ah, bh, preferred_element_type=F32) + jnp.dot(al, bh, preferred_element_type=F32)
            + jnp.dot(ah, bl, preferred_element_type=F32))


def _silu(v):
    h = 0.5 * v
    return h + h * jnp.tanh(h)


def _softplus(v):
    return jnp.maximum(v, 0.0) + jnp.log(1.0 + jnp.exp(-jnp.abs(v)))


def _log_sigmoid(v):
    return jnp.minimum(v, 0.0) - jnp.log(1.0 + jnp.exp(-jnp.abs(v)))


def _tri(n):
    row = lax.broadcasted_iota(jnp.int32, (n, n), 0)
    col = lax.broadcasted_iota(jnp.int32, (n, n), 1)
    return row >= col, col >= row


def _params(*sem):
    return pltpu.CompilerParams(dimension_semantics=sem, vmem_limit_bytes=VMEM_LIMIT_BYTES)


def _adaln_kernel(c_ref, w_ref, b_ref, o_ref):
    o_ref[...] = _dot3(_silu(c_ref[...]), w_ref[...]) + b_ref[...]


def _adaln(cvecs, w, b):
    rows, d = cvecs.shape
    n = w.shape[1]
    tn = 1024
    return pl.pallas_call(
        _adaln_kernel,
        grid=(n // tn,),
        in_specs=[pl.BlockSpec((rows, d), lambda j: (0, 0)),
                  pl.BlockSpec((d, tn), lambda j: (0, j)),
                  pl.BlockSpec((1, tn), lambda j: (0, j))],
        out_specs=pl.BlockSpec((rows, tn), lambda j: (0, j)),
        out_shape=jax.ShapeDtypeStruct((rows, n), F32),
        compiler_params=_params("parallel"),
        name="adaln",
    )(cvecs, w, b.reshape(1, n))


def _modulated_norm(x, g, sc, sh):
    r = lax.rsqrt(jnp.mean(x * x, axis=-1, keepdims=True) + NORM_EPS)
    return ((x * r) * g) * (1.0 + sc) + sh


def _store_cols(h, w_ref, off, ref, rs=slice(None), act=None):
    n = ref.shape[-1]
    for c0 in range(0, n, 512):
        c1 = min(n, c0 + 512)
        t = jnp.dot(h, w_ref[:, off + c0:off + c1], preferred_element_type=F32)
        ref[rs, c0:c1] = (t if act is None else act(t)).astype(ref.dtype)
    return off + n


def _row_parts(rows):
    step = min(rows, max(rows // PROJ_ROW_PARTS, PROJ_MIN_PART_ROWS))
    return [slice(r0, r0 + step) for r0 in range(0, rows, step)]


def _proj_even_kernel(x_ref, g_ref, sc_ref, sh_ref, w_ref, *out_refs):
    for rs in _row_parts(x_ref.shape[0]):
        h = _modulated_norm(x_ref[rs, :], g_ref[...], sc_ref[...], sh_ref[...]).astype(BF16)
        off = 0
        for ref in out_refs:
            off = _store_cols(h, w_ref, off, ref, rs)


def _head_norm(t, gain):
    r = lax.rsqrt(jnp.mean(t * t, axis=-1, keepdims=True) + NORM_EPS)
    return (t * r) * gain


def _rope(t, cos, sin_signed):
    return t * cos + pltpu.roll(t, ATT_HEAD_DIM // 2, 1) * sin_signed


def _proj_odd_kernel(x_ref, g_ref, sc_ref, sh_ref, w_ref, cos_ref, sin_ref, qn_ref, kn_ref,
                     k_ref, v_ref, *qg_refs):
    h = _modulated_norm(x_ref[...], g_ref[...], sc_ref[...], sh_ref[...]).astype(BF16)
    cos = cos_ref[...]
    sin = sin_ref[...]
    scale = ATT_HEAD_DIM ** -0.5 * LOG2E

    width = HEADS_PER_DOT * ATT_HEAD_DIM
    dh = ATT_HEAD_DIM
    pr = lax.broadcasted_iota(jnp.int32, (2 * dh, 2 * dh), 0) // dh
    pc = lax.broadcasted_iota(jnp.int32, (2 * dh, 2 * dh), 1) // dh
    head_ones = jnp.where(pr == pc, 1.0, 0.0).astype(BF16)

    def project(col):
        return jnp.dot(h, w_ref[:, col:col + width], preferred_element_type=F32)

    def finish_heads(t4, ref, j0, gain, out_scale):
        for j in range(0, HEADS_PER_DOT, 2):
            t2 = t4[:, j * dh:(j + 2) * dh]
            mean_sq = jnp.dot((t2 * t2).astype(BF16), head_ones, preferred_element_type=F32) * (1.0 / dh)
            t2 = t2 * lax.rsqrt(mean_sq + NORM_EPS)
            for jj in range(2):
                t = _rope(t2[:, jj * dh:(jj + 1) * dh] * gain, cos, sin)
                c0 = (j0 + j + jj) * dh
                ref[:, c0:c0 + dh] = (t if out_scale is None else t * out_scale).astype(ref.dtype)

    work = []
    if qg_refs:
        q_ref, gate_ref = qg_refs
        for j0 in range(0, ATT_HEADS, HEADS_PER_DOT):
            c0 = j0 * dh

            def finish_gate(t4, c0=c0):
                gate_ref[:, c0:c0 + width] = _silu(t4)

            work.append((2 * ATT_KV_W + ATT_W + c0, finish_gate))
            work.append((2 * ATT_KV_W + c0,
                         functools.partial(finish_heads, ref=q_ref, j0=j0, gain=qn_ref[...], out_scale=scale)))
    for j0 in range(0, ATT_KV_HEADS, HEADS_PER_DOT):
        work.append((j0 * dh, functools.partial(finish_heads, ref=k_ref, j0=j0, gain=kn_ref[...], out_scale=None)))
    for col, finish in work:
        finish(project(col))
    _store_cols(h, w_ref, ATT_KV_W, v_ref)


def _row_tile(length, want):
    return min(length, want)


def _proj_even(x, norm_g, scale, shift, w_bf16, widths, tm):
    bsz, length, d = x.shape
    tm = _row_tile(length, tm)
    n = w_bf16.shape[1]
    row = lambda b, i: (b, i, 0)
    mod = lambda b, i: (b, 0, 0)
    return pl.pallas_call(
        _proj_even_kernel,
        grid=(bsz, length // tm),
        in_specs=[pl.BlockSpec((None, tm, d), row),
                  pl.BlockSpec((1, d), lambda b, i: (0, 0)),
                  pl.BlockSpec((None, 1, d), mod),
                  pl.BlockSpec((None, 1, d), mod),
                  pl.BlockSpec((d, n), lambda b, i: (0, 0), pipeline_mode=pl.Buffered(1))],
        out_specs=[pl.BlockSpec((None, tm, wd), row) for wd, _ in widths],
        out_shape=[jax.ShapeDtypeStruct((bsz, length, wd), dt) for wd, dt in widths],
        compiler_params=_params("parallel", "parallel"),
        name="proj_even",
    )(x, norm_g.reshape(1, d), scale, shift, w_bf16)


def _proj_odd(x, norm_g, scale, shift, w_bf16, cos, sin, q_norm, k_norm, with_queries, tm):
    bsz, length, d = x.shape
    tm = _row_tile(length, tm)
    n = w_bf16.shape[1]
    row = lambda b, i: (b, i, 0)
    mod = lambda b, i: (b, 0, 0)
    const = lambda b, i: (0, 0)
    widths = [(ATT_KV_W, BF16), (ATT_KV_W, BF16)]
    if with_queries:
        widths += [(ATT_W, BF16), (ATT_W, F32)]
    return pl.pallas_call(
        _proj_odd_kernel,
        grid=(bsz, length // tm),
        in_specs=[pl.BlockSpec((None, tm, d), row),
                  pl.BlockSpec((1, d), const),
                  pl.BlockSpec((None, 1, d), mod),
                  pl.BlockSpec((None, 1, d), mod),
                  pl.BlockSpec((d, n), const, pipeline_mode=pl.Buffered(1)),
                  pl.BlockSpec((tm, ATT_HEAD_DIM), lambda b, i: (i, 0)),
                  pl.BlockSpec((tm, ATT_HEAD_DIM), lambda b, i: (i, 0)),
                  pl.BlockSpec((1, ATT_HEAD_DIM), const),
                  pl.BlockSpec((1, ATT_HEAD_DIM), const)],
        out_specs=[pl.BlockSpec((None, tm, wd), row) for wd, _ in widths],
        out_shape=[jax.ShapeDtypeStruct((bsz, length, wd), dt) for wd, dt in widths],
        compiler_params=_params("parallel", "parallel"),
        name="proj_odd_q" if with_queries else "proj_odd_kv",
    )(x, norm_g.reshape(1, d), scale, shift, w_bf16, cos, sin,
      q_norm.reshape(1, ATT_HEAD_DIM), k_norm.reshape(1, ATT_HEAD_DIM))


def _conv_silu(win_s, prev_ref, cur_ref, next_ref, has_prev, has_next, cw_ref, cb_ref, x_s):
    rows = cur_ref.shape[0]
    half = (CONV_K - 1) // 2
    for j in range(CONV_CH // LANES):
        ls = slice(j * LANES, (j + 1) * LANES)
        win_s[j, 0:HALO, :] = jnp.where(has_prev, prev_ref[:, ls].astype(F32), 0.0)
        win_s[j, HALO:HALO + rows, :] = cur_ref[:, ls].astype(F32)
        win_s[j, HALO + rows:2 * HALO + rows, :] = jnp.where(has_next, next_ref[:, ls].astype(F32), 0.0)
        taps = [cw_ref[k:k + 1, ls] for k in range(CONV_K)]
        bias = cb_ref[:, ls]
        win = win_s.at[j]
        out = x_s.at[j]
        for c in range(rows // SSD_CHUNK):
            for a in range(CONV_ROW_STRIDE):
                r0 = HALO + c * SSD_CHUNK - half + a
                acc = bias + taps[0] * win[pl.ds(r0, SUBLANES, stride=CONV_ROW_STRIDE), :]
                for k in range(1, CONV_K):
                    acc = acc + taps[k] * win[pl.ds(r0 + k, SUBLANES, stride=CONV_ROW_STRIDE), :]
                out[pl.ds(c * CONV_OUT_ROWS + a, SUBLANES, stride=CONV_ROW_STRIDE), :] = _silu(acc)


def _ssd_fwd_kernel(p_ref, c_ref, n_ref, sm_ref, cw_ref, cb_ref, bias_row_ref, bias_col_ref,
                    alog_row_ref, alog_col_ref, dskip_ref, ef_ref, eb_ref, sf0_ref,
                    ya_ref, cm_ref, bt_ref, wxb_ref, p3_ref, sff_ref,
                    sf_s, win_s, x_s):
    i = pl.program_id(1)
    nc = pl.num_programs(1)
    t = SSD_CHUNK
    nh = SSD_HEADS
    slabs_x = SSD_WIDTH // LANES
    slab_b = slabs_x
    slab_c = slabs_x + SSD_GROUPS

    @pl.when(i == 0)
    def _():
        sf_s[...] = sf0_ref[...]

    lower, upper = _tri(t)
    ltri = jnp.where(lower, 1.0, 0.0).astype(BF16)
    utri = jnp.where(upper, 1.0, 0.0).astype(BF16)
    a_row = -jnp.exp(alog_row_ref[...])
    a_col = -jnp.exp(alog_col_ref[...])
    ef = ef_ref[...]
    eb = eb_ref[...]
    neg_inf = float("-inf")

    _conv_silu(win_s, p_ref, c_ref, n_ref, i > 0, i < nc - 1, cw_ref, cb_ref, x_s)
    n_chunks = sm_ref.shape[0] // t
    lane = lax.broadcasted_iota(jnp.int32, (t, LANES), 1)
    left = lane < SSD_HEAD_DIM
    heads_per_group = nh // SSD_GROUPS
    pairs_per_group = heads_per_group // 2

    def chunk_decays(c):
        rs = slice(c * t, (c + 1) * t)
        sm = sm_ref[rs, :]
        dt = _softplus(sm + bias_row_ref[...])
        dta = dt * a_row
        cs = _dot_split_rhs(ltri, dta)
        rc = _dot_split_rhs(utri, dta)
        dt_t = _softplus(sm.T[0:2 * nh, :] + bias_col_ref[...])
        dta_t = dt_t * a_col
        log2_dt_t = jnp.log2(dt_t)
        f_q = cs * LOG2E
        g_q = rc * LOG2E
        f_k = _dot_split_lhs(dta_t, utri) * LOG2E - log2_dt_t
        g_k = _dot_split_lhs(dta_t, ltri) * LOG2E - log2_dt_t
        dec_f = _dot_split_lhs(jnp.exp2(f_q), ef)
        wgt_f = _dot_split_lhs(dt * jnp.exp(cs[t - 1:t, :] - cs), ef)
        wgt_b = _dot_split_lhs(dt * jnp.exp(rc[0:1, :] - rc), eb)
        p3_ref[rs, :] = jnp.exp2(g_q)
        for p in range(slabs_x):
            ls = slice(p * LANES, (p + 1) * LANES)
            wxb_ref[rs, ls] = (x_s[p, c * CONV_OUT_ROWS:c * CONV_OUT_ROWS + t, :] * wgt_b[:, ls]).astype(BF16)
        return f_q, g_q, f_k, g_k, dec_f, wgt_f

    decays = [chunk_decays(c) for c in range(n_chunks)]

    for c in range(n_chunks):
        f_q, g_q, f_k, g_k, dec_f, wgt_f = decays[c]
        rs = slice(c * t, (c + 1) * t)
        xr = slice(c * CONV_OUT_ROWS, c * CONV_OUT_ROWS + t)

        def decay_matrix(h, cbg):
            lf = jnp.exp2(jnp.where(lower, f_q[:, h:h + 1] - f_k[h:h + 1, :], neg_inf))
            ub = jnp.exp2(jnp.where(upper, g_q[:, nh + h:nh + h + 1] - g_k[nh + h:nh + h + 1, :], neg_inf))
            return cbg * (lf + ub)

        for g in range(SSD_GROUPS):
            cg = x_s[slab_c + g, xr, :]
            bg = x_s[slab_b + g, xr, :]
            gs = slice(g * SSD_GROUP_W, (g + 1) * SSD_GROUP_W)
            cm_ref[rs, g * SSD_STATE:(g + 1) * SSD_STATE] = cg.astype(BF16)
            cbg = _dot_nt(cg, bg)
            state = sf_s[g]
            y_off = _dot(cg, state) * dec_f[:, gs]
            wx = []
            for pair in range(pairs_per_group):
                h0 = g * heads_per_group + 2 * pair
                p = h0 // 2
                ls = slice(p * LANES, (p + 1) * LANES)
                xp = x_s[p, xr, :]
                xb = xp.astype(BF16)
                y = jnp.where(left, _dot(decay_matrix(h0, cbg), xb), _dot(decay_matrix(h0 + 1, cbg), xb))
                ya_ref[rs, ls] = y + y_off[:, pair * LANES:(pair + 1) * LANES] + dskip_ref[:, ls] * xp
                wx.append((xp * wgt_f[:, ls]).astype(BF16))
            bt = bg.T.astype(BF16)
            bt_ref[c, g * SSD_STATE:(g + 1) * SSD_STATE, :] = bt
            sf_s[g] = dec_f[t - 1:t, gs] * state + jnp.dot(bt, jnp.concatenate(wx, axis=1),
                                                           preferred_element_type=F32)

    @pl.when(i == nc - 1)
    def _():
        sff_ref[...] = sf_s[...]


def _ssd_bwd_kernel(cm_ref, bt_ref, wxb_ref, p3_ref, ya_ref, z_ref, eb_ref, sn_ref, sb0_ref,
                    ys_ref, sbf_ref, sb_s):
    i = pl.program_id(1)
    nc = pl.num_programs(1)

    @pl.when(i == 0)
    def _():
        sb_s[...] = sb0_ref[...]

    t = SSD_CHUNK
    n_sub = bt_ref.shape[0]
    dec_b = _dot_split_lhs(p3_ref[...], eb_ref[...])
    for g in range(SSD_GROUPS):
        gs = slice(g * SSD_GROUP_W, (g + 1) * SSD_GROUP_W)
        ss = slice(g * SSD_STATE, (g + 1) * SSD_STATE)
        chunks = [slice(s * t, (s + 1) * t) for s in range(n_sub)]
        incs = [jnp.dot(bt_ref[s, ss, :], wxb_ref[rs, gs], preferred_element_type=F32)
                for s, rs in enumerate(chunks)]
        states = [None] * n_sub
        state = sb_s[g]
        for s in reversed(range(n_sub)):
            states[s] = state
            state = dec_b[s * t:s * t + 1, gs] * state + incs[s]
        sb_s[g] = state
        for s, rs in enumerate(chunks):
            y_off = jnp.dot(cm_ref[rs, ss], states[s].astype(BF16), preferred_element_type=F32) * dec_b[rs, gs]
            y = (ya_ref[rs, gs] + y_off) * _silu(z_ref[rs, gs].astype(F32))
            ys_ref[rs, gs] = _head_norm(y, sn_ref[:, gs]).astype(ys_ref.dtype)

    @pl.when(i == nc - 1)
    def _():
        sbf_ref[...] = sb_s[...]


SSD_STATE_SHAPE = (SSD_GROUPS, SSD_STATE, SSD_GROUP_W)


def _ssd_fwd(xbc, small, conv_w, conv_b, bias_row, bias_col, alog_row, alog_col, dskip_row, ef, eb, sf0):
    bsz, length, _ = xbc.shape
    t = SSD_CHUNK
    rows = _row_tile(length, SSD_FWD_ROWS)
    nc = length // t
    per = rows // HALO
    last_halo = length // HALO - 1

    def cur(b, i): return (b, i, 0)
    def prev(b, i): return (b, jnp.maximum(i * per - 1, 0), 0)
    def nxt(b, i): return (b, jnp.minimum((i + 1) * per, last_halo), 0)
    const2 = lambda b, i: (0, 0)
    state = lambda b, i: (b, 0, 0, 0)

    halo_spec = lambda f: pl.BlockSpec((None, HALO, CONV_CH), f)
    chunk_spec = lambda w: pl.BlockSpec((None, rows, w), cur)
    state_spec = pl.BlockSpec((None,) + SSD_STATE_SHAPE, state)
    return dict(
        grid=(bsz, length // rows),
        in_specs=[halo_spec(prev), chunk_spec(CONV_CH), halo_spec(nxt), chunk_spec(SMALL_W),
                  pl.BlockSpec((CONV_K, CONV_CH), const2),
                  pl.BlockSpec((1, CONV_CH), const2),
                  pl.BlockSpec((1, SMALL_W), const2),
                  pl.BlockSpec((2 * SSD_HEADS, 1), const2),
                  pl.BlockSpec((1, SMALL_W), const2),
                  pl.BlockSpec((2 * SSD_HEADS, 1), const2),
                  pl.BlockSpec((1, SSD_WIDTH), const2),
                  pl.BlockSpec((SMALL_W, SSD_WIDTH), const2),
                  pl.BlockSpec((SMALL_W, SSD_WIDTH), const2),
                  state_spec],
        out_specs=[chunk_spec(SSD_WIDTH), chunk_spec(SSD_BC_W),
                   pl.BlockSpec((None, rows // t, SSD_BC_W, t), lambda b, i: (b, i, 0, 0)),
                   chunk_spec(SSD_WIDTH), chunk_spec(SMALL_W), state_spec],
        out_shape=[jax.ShapeDtypeStruct((bsz, length, SSD_WIDTH), F32),
                   jax.ShapeDtypeStruct((bsz, length, SSD_BC_W), BF16),
                   jax.ShapeDtypeStruct((bsz, nc, SSD_BC_W, t), BF16),
                   jax.ShapeDtypeStruct((bsz, length, SSD_WIDTH), BF16),
                   jax.ShapeDtypeStruct((bsz, length, SMALL_W), F32),
                   jax.ShapeDtypeStruct((bsz,) + SSD_STATE_SHAPE, F32)],
        scratch_shapes=[pltpu.VMEM(SSD_STATE_SHAPE, F32),
                        pltpu.VMEM((CONV_CH // LANES, rows + 2 * HALO, LANES), F32),
                        pltpu.VMEM((CONV_CH // LANES, (rows // t) * CONV_OUT_ROWS, LANES), F32)],
        args=(xbc, xbc, xbc, small, conv_w, conv_b, bias_row, bias_col, alog_row, alog_col, dskip_row, ef, eb, sf0))


def _ssd_bwd(cm, bt, wxb, p3, ya, z, eb, ssd_norm, sb0, rows):
    bsz, length, _ = ya.shape
    t = SSD_CHUNK
    n_steps = length // rows
    rev = lambda b, i: (b, n_steps - 1 - i, 0)
    const2 = lambda b, i: (0, 0)
    state = lambda b, i: (b, 0, 0, 0)
    chunk_spec = lambda w: pl.BlockSpec((None, rows, w), rev)
    state_spec = pl.BlockSpec((None,) + SSD_STATE_SHAPE, state)
    return dict(
        in_specs=[chunk_spec(SSD_BC_W),
                  pl.BlockSpec((None, rows // t, SSD_BC_W, t), lambda b, i: (b, n_steps - 1 - i, 0, 0)),
                  chunk_spec(SSD_WIDTH), chunk_spec(SMALL_W), chunk_spec(SSD_WIDTH), chunk_spec(SSD_WIDTH),
                  pl.BlockSpec((SMALL_W, SSD_WIDTH), const2),
                  pl.BlockSpec((1, SSD_WIDTH), const2),
                  state_spec],
        state_spec=state_spec,
        state_shape=jax.ShapeDtypeStruct((bsz,) + SSD_STATE_SHAPE, F32),
        state_scratch=pltpu.VMEM(SSD_STATE_SHAPE, F32),
        args=(cm, bt, wxb, p3, ya, z, eb, ssd_norm.reshape(1, SSD_WIDTH), sb0))


def _gla_fwd_kernel(q_ref, k_ref, v_ref, sm_ref, wg_ref, gb_ref, sf0_ref,
                    oa_ref, qdb_ref, kwb_ref, decb_ref, sff_ref, sf_s):
    i = pl.program_id(1)
    n_steps = pl.num_programs(1)
    t = GLA_CHUNK
    rows = q_ref.shape[0]
    n_sub = rows // t
    dk, dv = GLA_KEY_DIM, GLA_VAL_DIM
    qscale = dk ** -0.5
    inv_norm = 1.0 / GLA_GATE_NORMALIZER

    @pl.when(i == 0)
    def _():
        sf_s[...] = sf0_ref[...]

    lower, upper = _tri(t)
    grp = min(rows, GLA_CUMSUM_ROWS)
    row = lax.broadcasted_iota(jnp.int32, (grp, grp), 0)
    col = lax.broadcasted_iota(jnp.int32, (grp, grp), 1)
    diff = row - col
    pos = row % t
    bd_lower = jnp.where(diff >= 0, jnp.where(diff <= pos, 1.0, 0.0), 0.0).astype(BF16)
    bd_upper = jnp.where(diff <= 0, jnp.where(-diff <= t - 1 - pos, 1.0, 0.0), 0.0).astype(BF16)

    hi, lo = _split(sm_ref[...])
    wg_hi, wg_lo = _split(wg_ref[...])
    logits = (jnp.dot(hi, wg_hi, preferred_element_type=F32) + jnp.dot(lo, wg_hi, preferred_element_type=F32)
              + jnp.dot(hi, wg_lo, preferred_element_type=F32)) + gb_ref[...]
    lg = _log_sigmoid(logits) * inv_norm
    groups = [slice(r0, r0 + grp) for r0 in range(0, rows, grp)]
    cs = jnp.concatenate([_dot_split_rhs(bd_lower, lg[gr, 0:GLA_K]) for gr in groups], axis=0)
    rc = jnp.concatenate([_dot_split_rhs(bd_upper, lg[gr, GLA_K:2 * GLA_K]) for gr in groups], axis=0)

    def per_chunk_row(v, offset):
        return jnp.concatenate([jnp.broadcast_to(v[s * t + offset:s * t + offset + 1, :], (t, v.shape[1]))
                                for s in range(n_sub)], axis=0)

    for h in range(GLA_HEADS):
        ks = slice(h * dk, (h + 1) * dk)
        vs = slice(h * dv, (h + 1) * dv)
        c = cs[:, ks]
        r = rc[:, ks]
        qh = q_ref[:, ks].astype(F32) * qscale
        kh = k_ref[:, ks].astype(F32)
        qdf = (qh * jnp.exp(c)).astype(BF16)
        kif = (kh * jnp.exp(-c)).astype(BF16)
        qdb = (qh * jnp.exp(r)).astype(BF16)
        kib = (kh * jnp.exp(-r)).astype(BF16)
        kwf = (kh * jnp.exp(per_chunk_row(c, t - 1) - c)).astype(BF16)
        qdb_ref[:, ks] = qdb
        kwb_ref[:, ks] = (kh * jnp.exp(per_chunk_row(r, 0) - r)).astype(BF16)
        chunks = [slice(s * t, (s + 1) * t) for s in range(n_sub)]
        vhs = [v_ref[rs, vs].astype(BF16) for rs in chunks]
        atts = [(jnp.where(lower, _dot_nt(qdf[rs], kif[rs]), 0.0)
                 + jnp.where(upper, _dot_nt(qdb[rs], kib[rs]), 0.0)).astype(BF16) for rs in chunks]
        incs = [_dot_tn(vh, kwf[rs]) for vh, rs in zip(vhs, chunks)]
        states = [sf_s[h]]
        for s in range(n_sub):
            decb_ref[s, :, ks] = jnp.exp(r[s * t:s * t + 1, :])
            states.append(states[s] * jnp.exp(c[s * t + t - 1:s * t + t, :]) + incs[s])
        sf_s[h] = states[n_sub]
        for s, rs in enumerate(chunks):
            oa_ref[rs, vs] = (jnp.dot(atts[s], vhs[s], preferred_element_type=F32)
                              + _dot_nt(qdf[rs], states[s]))

    @pl.when(i == n_steps - 1)
    def _():
        sff_ref[...] = sf_s[...]


def _gla_bwd_kernel(qdb_ref, kwb_ref, decb_ref, v_ref, oa_ref, g_ref, gn_ref, sb0_ref,
                    os_ref, sbf_ref, sb_s):
    i = pl.program_id(1)
    n_steps = pl.num_programs(1)
    t = GLA_CHUNK
    n_sub = qdb_ref.shape[0] // t
    dk, dv = GLA_KEY_DIM, GLA_VAL_DIM

    @pl.when(i == 0)
    def _():
        sb_s[...] = sb0_ref[...]

    for h in range(GLA_HEADS):
        ks = slice(h * dk, (h + 1) * dk)
        vs = slice(h * dv, (h + 1) * dv)
        state = sb_s[h]
        for s in reversed(range(n_sub)):
            rs = slice(s * t, (s + 1) * t)
            o = oa_ref[rs, vs] + _dot_nt(qdb_ref[rs, ks], state)
            state = state * decb_ref[s, :, ks] + _dot_tn(v_ref[rs, vs], kwb_ref[rs, ks])
            os_ref[rs, vs] = (_head_norm(o, gn_ref[:, vs]) * _silu(g_ref[rs, vs].astype(F32))).astype(os_ref.dtype)
        sb_s[h] = state

    @pl.when(i == n_steps - 1)
    def _():
        sbf_ref[...] = sb_s[...]


GLA_STATE_SHAPE = (GLA_HEADS, GLA_VAL_DIM, GLA_KEY_DIM)


def _gla_fwd(q, k, v, small, wg, gb, sf0, rows):
    bsz, length, _ = q.shape
    rows = _row_tile(length, rows)
    n_steps = length // rows
    n_sub = rows // GLA_CHUNK
    fwd = lambda b, i: (b, i, 0)
    const2 = lambda b, i: (0, 0)
    state = lambda b, i: (b, 0, 0, 0)
    blk = lambda w: pl.BlockSpec((None, rows, w), fwd)
    state_spec = pl.BlockSpec((None,) + GLA_STATE_SHAPE, state)
    return dict(
        grid=(bsz, n_steps),
        in_specs=[blk(GLA_K), blk(GLA_K), blk(GLA_V), blk(SMALL_W),
                  pl.BlockSpec((SMALL_W, 2 * GLA_K), const2),
                  pl.BlockSpec((1, 2 * GLA_K), const2),
                  state_spec],
        out_specs=[blk(GLA_V), blk(GLA_K), blk(GLA_K),
                   pl.BlockSpec((None, n_sub, 1, GLA_K), lambda b, i: (b, i, 0, 0)),
                   state_spec],
        out_shape=[jax.ShapeDtypeStruct((bsz, length, GLA_V), F32),
                   jax.ShapeDtypeStruct((bsz, length, GLA_K), BF16),
                   jax.ShapeDtypeStruct((bsz, length, GLA_K), BF16),
                   jax.ShapeDtypeStruct((bsz, length // GLA_CHUNK, 1, GLA_K), F32),
                   jax.ShapeDtypeStruct((bsz,) + GLA_STATE_SHAPE, F32)],
        scratch_shapes=[pltpu.VMEM(GLA_STATE_SHAPE, F32)],
        args=(q, k, v, small, wg, gb, sf0))


def _sweep_call(body, name, call):
    return pl.pallas_call(
        body,
        grid=call["grid"],
        in_specs=call["in_specs"],
        out_specs=call["out_specs"],
        out_shape=call["out_shape"],
        scratch_shapes=call["scratch_shapes"],
        compiler_params=_params("arbitrary", "arbitrary"),
        name=name,
    )(*call["args"])


def _gla_bwd(qdb, kwb, decb, v, oa, g, gla_norm, sb0, rows):
    bsz, length, _ = oa.shape
    n_steps = length // rows
    n_sub = rows // GLA_CHUNK
    rev = lambda b, i: (b, n_steps - 1 - i, 0)
    const2 = lambda b, i: (0, 0)
    state = lambda b, i: (b, 0, 0, 0)
    blk = lambda w: pl.BlockSpec((None, rows, w), rev)
    state_spec = pl.BlockSpec((None,) + GLA_STATE_SHAPE, state)
    return dict(
        in_specs=[blk(GLA_K), blk(GLA_K),
                  pl.BlockSpec((None, n_sub, 1, GLA_K), lambda b, i: (b, n_steps - 1 - i, 0, 0)),
                  blk(GLA_V), blk(GLA_V), blk(GLA_V),
                  pl.BlockSpec((1, GLA_V), const2),
                  state_spec],
        state_spec=state_spec,
        state_shape=jax.ShapeDtypeStruct((bsz,) + GLA_STATE_SHAPE, F32),
        state_scratch=pltpu.VMEM(GLA_STATE_SHAPE, F32),
        args=(qdb, kwb, decb, v, oa, g, gla_norm.reshape(1, GLA_V), sb0))


SSD_BWD_INPUTS = 9
GLA_BWD_INPUTS = 8


def _even_tail_kernel(*refs):
    ssd_in = refs[:SSD_BWD_INPUTS]
    gla_in = refs[SSD_BWD_INPUTS:SSD_BWD_INPUTS + GLA_BWD_INPUTS]
    x_ref, gate_ref, w_ref, y_ref, ssd_fin_ref, gla_fin_ref, ssd_s, gla_s, mix_s = \
        refs[SSD_BWD_INPUTS + GLA_BWD_INPUTS:]
    _ssd_bwd_kernel(*ssd_in, mix_s.at[:, 0:SSD_WIDTH], ssd_fin_ref, ssd_s)
    _gla_bwd_kernel(*gla_in, mix_s.at[:, SSD_WIDTH:SSD_WIDTH + GLA_V], gla_fin_ref, gla_s)
    y_ref[...] = x_ref[...] + gate_ref[...] * jnp.dot(mix_s[...], w_ref[...], preferred_element_type=F32)


def _even_tail(x, gate, w_bf16, ssd_args, gla_args):
    bsz, length, d = x.shape
    rows = _row_tile(length, EVEN_TAIL_ROWS)
    n_steps = length // rows
    ssd = _ssd_bwd(*ssd_args, rows)
    gla = _gla_bwd(*gla_args, rows)
    assert len(ssd["in_specs"]) == SSD_BWD_INPUTS and len(gla["in_specs"]) == GLA_BWD_INPUTS
    rev = lambda b, i: (b, n_steps - 1 - i, 0)
    return pl.pallas_call(
        _even_tail_kernel,
        grid=(bsz, n_steps),
        in_specs=(ssd["in_specs"] + gla["in_specs"]
                  + [pl.BlockSpec((None, rows, d), rev),
                     pl.BlockSpec((None, 1, d), lambda b, i: (b, 0, 0)),
                     pl.BlockSpec(w_bf16.shape, lambda b, i: (0, 0), pipeline_mode=pl.Buffered(1))]),
        out_specs=[pl.BlockSpec((None, rows, d), rev), ssd["state_spec"], gla["state_spec"]],
        out_shape=[jax.ShapeDtypeStruct((bsz, length, d), F32), ssd["state_shape"], gla["state_shape"]],
        scratch_shapes=[ssd["state_scratch"], gla["state_scratch"],
                        pltpu.VMEM((rows, SSD_WIDTH + GLA_V), BF16)],
        compiler_params=_params("arbitrary", "arbitrary"),
        name="even_tail",
    )(*ssd["args"], *gla["args"], x, gate, w_bf16)


def _out_proj_kernel(x_ref, gate_ref, *refs):
    *in_refs, w_ref, y_ref = refs
    acc = None
    off = 0
    for ref in in_refs:
        n = ref.shape[-1]
        part = jnp.dot(ref[...], w_ref[off:off + n, :], preferred_element_type=F32)
        acc = part if acc is None else acc + part
        off += n
    y_ref[...] = x_ref[...] + gate_ref[...] * acc


def _out_proj(x, gate, mixed, w_bf16, tm, name):
    bsz, length, d = x.shape
    tm = _row_tile(length, tm)
    row = lambda b, i: (b, i, 0)
    return pl.pallas_call(
        _out_proj_kernel,
        grid=(bsz, length // tm),
        in_specs=([pl.BlockSpec((None, tm, d), row),
                   pl.BlockSpec((None, 1, d), lambda b, i: (b, 0, 0))]
                  + [pl.BlockSpec((None, tm, m.shape[-1]), row) for m in mixed]
                  + [pl.BlockSpec(w_bf16.shape, lambda b, i: (0, 0), pipeline_mode=pl.Buffered(1))]),
        out_specs=pl.BlockSpec((None, tm, d), row),
        out_shape=jax.ShapeDtypeStruct((bsz, length, d), F32),
        compiler_params=_params("parallel", "parallel"),
        name=name,
    )(x, gate, *mixed, w_bf16)


def _attn_kernel(sink_ref, q_ref, k_ref, v_ref, kc_ref, vc_ref, g_ref, o_ref):
    j = pl.program_id(1)
    i = pl.program_id(2)
    length = k_ref.shape[0]
    blk = ATT_BLOCK
    band = 3 * blk
    dh = ATT_HEAD_DIM
    n_q = q_ref.shape[0] // blk
    n_ctx = kc_ref.shape[0]
    row_minus_col = (lax.broadcasted_iota(jnp.int32, (blk, band), 0)
                     - lax.broadcasted_iota(jnp.int32, (blk, band), 1))
    sink = jnp.concatenate([jnp.full((blk, LANES), sink_ref[j * ATT_GROUP + g] * LOG2E, F32)
                            for g in range(ATT_GROUP)], axis=0)
    kc = kc_ref[...]
    vc_ext = jnp.concatenate([vc_ref[...], jnp.ones((n_ctx, dh), BF16)], axis=1)
    ones_band = jnp.ones((band, dh), BF16)

    def window_start(qb):
        blk_idx = i * n_q + qb
        return blk_idx, pl.multiple_of(jnp.clip((blk_idx - 1) * blk, 0, length - band), blk)

    def scores(qb):
        _, start = window_start(qb)
        rs = slice(qb * blk, (qb + 1) * blk)
        q = jnp.concatenate([q_ref[rs, g * dh:(g + 1) * dh] for g in range(ATT_GROUP)], axis=0)
        return _dot_nt(q, k_ref[pl.ds(start, band), :]), _dot_nt(q, kc)

    pending = scores(0)
    for qb in range(n_q):
        s_band, s_ctx = pending
        if qb + 1 < n_q:
            pending = scores(qb + 1)
        blk_idx, start = window_start(qb)
        inside = jnp.abs(row_minus_col + (blk_idx * blk - start)) <= WINDOW
        v_all = jnp.concatenate([jnp.concatenate([v_ref[pl.ds(start, band), :], ones_band], axis=1),
                                 vc_ext], axis=0)
        rs = slice(qb * blk, (qb + 1) * blk)
        cols = []
        for c in range(band // LANES):
            ls = slice(c * LANES, (c + 1) * LANES)
            cols.append(jnp.concatenate(
                [jnp.where(inside[:, ls], s_band[g * blk:(g + 1) * blk, ls], float("-inf"))
                 for g in range(ATT_GROUP)], axis=0))
        for c in range(n_ctx // LANES):
            cols.append(s_ctx[:, c * LANES:(c + 1) * LANES])
        m = jnp.maximum(sink, jnp.max(functools.reduce(jnp.maximum, cols), axis=-1, keepdims=True))
        p = jnp.concatenate([jnp.exp2(col - m) for col in cols], axis=1).astype(BF16)
        acc = jnp.dot(p, v_all, preferred_element_type=F32)
        out = acc[:, 0:dh] / (acc[:, dh:2 * dh] + jnp.exp2(sink - m))
        for g in range(ATT_GROUP):
            cs = slice(g * dh, (g + 1) * dh)
            o_ref[rs, cs] = (out[g * blk:(g + 1) * blk, :] * g_ref[rs, cs]).astype(o_ref.dtype)


def _attention(sink, q, k, v, kc, vc, gate):
    bsz, length, _ = q.shape
    n_ctx = kc.shape[1]
    rows = _row_tile(length, ATT_Q_BLOCKS * ATT_BLOCK)
    gw = ATT_GROUP * ATT_HEAD_DIM
    qmap = lambda b, j, i, s: (b, i, j)
    kvmap = lambda b, j, i, s: (b, 0, j)
    grid_spec = pltpu.PrefetchScalarGridSpec(
        num_scalar_prefetch=1,
        grid=(bsz, ATT_KV_HEADS, length // rows),
        in_specs=[pl.BlockSpec((None, rows, gw), qmap),
                  pl.BlockSpec((None, length, ATT_HEAD_DIM), kvmap),
                  pl.BlockSpec((None, length, ATT_HEAD_DIM), kvmap),
                  pl.BlockSpec((None, n_ctx, ATT_HEAD_DIM), kvmap),
                  pl.BlockSpec((None, n_ctx, ATT_HEAD_DIM), kvmap),
                  pl.BlockSpec((None, rows, gw), qmap)],
        out_specs=pl.BlockSpec((None, rows, gw), qmap),
    )
    return pl.pallas_call(
        _attn_kernel,
        grid_spec=grid_spec,
        out_shape=jax.ShapeDtypeStruct((bsz, length, ATT_W), BF16),
        compiler_params=_params("parallel", "parallel", "arbitrary"),
        name="attention",
    )(sink, q, k, v, kc, vc, gate)


E_IN_SIZES = (SSD_WIDTH, CONV_CH, 2 * SSD_HEADS, GLA_K, GLA_K, GLA_V, GLA_V, 2 * GLA_RANK)
E_IN = sum(E_IN_SIZES)
E_ALIGNED = SSD_WIDTH + CONV_CH
E_SHIFT = 2 * SSD_HEADS
E_MAIN = 2 * GLA_K + 2 * GLA_V
E_OUT = E_ALIGNED + E_MAIN + SMALL_W
PREP_ROWS = 256
O_IN = 2 * ATT_KV_W + 2 * ATT_W


def _prep_even_kernel(wt_ref, o_ref):
    cols = wt_ref.shape[1]
    for dst in range(0, E_ALIGNED + E_MAIN, LANES):
        src = dst if dst < E_ALIGNED else dst + E_SHIFT
        o_ref[:, dst:dst + LANES] = wt_ref[src:src + LANES, :].T.astype(BF16)
    small = jnp.concatenate([wt_ref[E_ALIGNED:E_ALIGNED + E_SHIFT, :],
                             wt_ref[E_IN - 2 * GLA_RANK:E_IN, :],
                             jnp.zeros((SMALL_W - E_SHIFT - 2 * GLA_RANK, cols), F32)], axis=0)
    o_ref[:, E_ALIGNED + E_MAIN:E_OUT] = small.T.astype(BF16)


def _prep_even(w_in):
    d = w_in.shape[0]
    return pl.pallas_call(
        _prep_even_kernel,
        grid=(d // PREP_ROWS,),
        in_specs=[pl.BlockSpec((E_IN, PREP_ROWS), lambda i: (0, i))],
        out_specs=pl.BlockSpec((PREP_ROWS, E_OUT), lambda i: (i, 0)),
        out_shape=jax.ShapeDtypeStruct((d, E_OUT), BF16),
        compiler_params=_params("parallel"),
        name="prep_even",
    )(jnp.swapaxes(w_in, 0, 1))


def _prep_odd_kernel(w_ref, perm_ref, o_ref):
    def copy(c0, width):
        o_ref[:, c0:c0 + width] = w_ref[:, c0:c0 + width].astype(BF16)

    def permute(c0, width):
        for c in range(c0, c0 + width, 2 * LANES):
            o_ref[:, c:c + 2 * LANES] = jnp.dot(w_ref[:, c:c + 2 * LANES].astype(BF16), perm_ref[...],
                                                preferred_element_type=F32).astype(BF16)

    permute(0, ATT_KV_W)
    copy(ATT_KV_W, ATT_KV_W)
    permute(2 * ATT_KV_W, ATT_W)
    copy(2 * ATT_KV_W + ATT_W, ATT_W)


def _rope_perm(n):
    out = np.arange(n)
    head, rem = out // ATT_HEAD_DIM, out % ATT_HEAD_DIM
    half, axis, f = rem // (2 * ROPE_FREQS), (rem // ROPE_FREQS) % 2, rem % ROPE_FREQS
    return head * ATT_HEAD_DIM + axis * 2 * ROPE_FREQS + half * ROPE_FREQS + f


def _prep_odd(w_in):
    d = w_in.shape[0]
    src = _rope_perm(2 * LANES)
    perm = jnp.asarray(np.arange(2 * LANES)[:, None] == src[None, :], BF16)
    return pl.pallas_call(
        _prep_odd_kernel,
        grid=(d // PREP_ROWS,),
        in_specs=[pl.BlockSpec((PREP_ROWS, O_IN), lambda i: (i, 0)),
                  pl.BlockSpec((2 * LANES, 2 * LANES), lambda i: (0, 0))],
        out_specs=pl.BlockSpec((PREP_ROWS, O_IN), lambda i: (i, 0)),
        out_shape=jax.ShapeDtypeStruct((d, O_IN), BF16),
        compiler_params=_params("parallel"),
        name="prep_odd",
    )(w_in, perm)


EVEN_WIDTHS = ((SSD_WIDTH, BF16), (CONV_CH, BF16), (GLA_K, BF16), (GLA_K, BF16), (GLA_V, BF16), (GLA_V, BF16),
               (SMALL_W, F32))


def _expansion_matrices():
    rows = np.arange(SMALL_W)[:, None]
    heads = (np.arange(SSD_WIDTH) // SSD_HEAD_DIM)[None, :]
    ef = (rows == heads).astype(np.float32)
    eb = (rows - SSD_HEADS == heads).astype(np.float32)
    return jnp.asarray(ef, BF16), jnp.asarray(eb, BF16)


def _pad_lanes(v, width):
    return jnp.pad(v, ((0, 0), (0, width - v.shape[1])))


def _rope_tables(length):
    rows = length // GRID_W
    row = np.repeat(np.arange(rows, dtype=np.float64), GRID_W)
    col = np.tile(np.arange(GRID_W, dtype=np.float64), rows)
    inv = 1.0 / (ROPE_BASE ** (np.arange(ROPE_FREQS, dtype=np.float64) / ROPE_FREQS))
    ang_r = row[:, None] * inv
    ang_c = col[:, None] * inv
    cos = np.concatenate([np.cos(ang_r), np.cos(ang_c), np.cos(ang_r), np.cos(ang_c)], axis=1)
    sin = np.concatenate([-np.sin(ang_r), -np.sin(ang_c), np.sin(ang_r), np.sin(ang_c)], axis=1)
    return jnp.asarray(cos, F32), jnp.asarray(sin, F32)


def _rope_head_layout(v):
    return v[_rope_perm(ATT_HEAD_DIM)]


def _mod_rows(mod, rows, bsz, d):
    picked = jnp.broadcast_to(mod[rows], (bsz, 3 * d)) if isinstance(rows, int) else mod[rows]
    return [picked[:, None, j * d:(j + 1) * d] for j in range(3)]


PROJ_ROWS = 512
OUT_ROWS = 1024
ATT_Q_BLOCKS = 32
GLA_ROWS = 512
EVEN_TAIL_ROWS = 512
GLA_CUMSUM_ROWS = 256
SSD_FWD_ROWS = 4 * SSD_CHUNK
CVEC_ROWS = SUBLANES


def kernel(x, c, ctx, c_ctx, e_norm, e_mod_w, e_mod_b, e_w_in, e_conv_w, e_conv_b, e_dt_bias, e_a_log,
           e_d_skip, e_ssd_norm, e_gla_gate_w, e_gla_gate_b, e_gla_norm, e_w_out, o_norm, o_mod_w, o_mod_b,
           o_w_in, o_q_norm, o_k_norm, o_sink, o_w_out):
    bsz, length, d = x.shape
    n_ctx = ctx.shape[1]
    assert e_norm.shape[0] == 1 and o_norm.shape[0] == 1, "two-layer block only"
    assert length % SSD_CHUNK == 0 and n_ctx % SSD_CHUNK == 0 and length >= 3 * ATT_BLOCK
    assert bsz + 1 <= CVEC_ROWS

    cvecs = jnp.zeros((CVEC_ROWS, d), F32).at[:bsz].set(c).at[bsz].set(c_ctx)
    lat_rows = slice(0, bsz)

    mod = _adaln(cvecs, e_mod_w[0], e_mod_b[0])
    shift, scale, gate = _mod_rows(mod, lat_rows, bsz, d)
    c_shift, c_scale, c_gate = _mod_rows(mod, bsz, bsz, d)
    w_in = _prep_even(e_w_in[0])
    w_out = e_w_out[0].astype(BF16)
    ef, eb = _expansion_matrices()
    nh2 = 2 * SSD_HEADS
    bias_flat = e_dt_bias[0].reshape(1, nh2)
    alog_flat = e_a_log[0].reshape(1, nh2)
    bias_row, alog_row = _pad_lanes(bias_flat, SMALL_W), _pad_lanes(alog_flat, SMALL_W)
    bias_col, alog_col = bias_flat.reshape(nh2, 1), alog_flat.reshape(nh2, 1)
    dskip_row = jnp.repeat(e_d_skip[0], SSD_HEAD_DIM).reshape(1, SSD_WIDTH)
    conv_b = e_conv_b[0].reshape(1, CONV_CH)
    wg = jnp.zeros((SMALL_W, 2 * GLA_K), F32)
    wg = wg.at[nh2:nh2 + GLA_RANK, 0:GLA_K].set(e_gla_gate_w[0, 0])
    wg = wg.at[nh2 + GLA_RANK:nh2 + 2 * GLA_RANK, GLA_K:2 * GLA_K].set(e_gla_gate_w[0, 1])
    gb = e_gla_gate_b[0].reshape(1, 2 * GLA_K)

    def even_layer(stream, sc, sh, gt, ssd_init, gla_init):
        z, xbc, q, k, v, g, small = _proj_even(stream, e_norm[0], sc, sh, w_in, EVEN_WIDTHS, PROJ_ROWS)
        ya, cm, bt, wxb, p3, ssd_f = _sweep_call(
            _ssd_fwd_kernel, "ssd_fwd",
            _ssd_fwd(xbc, small, e_conv_w[0], conv_b, bias_row, bias_col, alog_row, alog_col, dskip_row,
                     ef, eb, ssd_init[0]))
        oa, qdb, kwb, decb, gla_f = _sweep_call(
            _gla_fwd_kernel, "gla_fwd", _gla_fwd(q, k, v, small, wg, gb, gla_init[0], GLA_ROWS))
        out, ssd_b, gla_b = _even_tail(stream, gt, w_out,
                                       (cm, bt, wxb, p3, ya, z, eb, e_ssd_norm[0], ssd_init[1]),
                                       (qdb, kwb, decb, v, oa, g, e_gla_norm[0], gla_init[1]))
        return out, (ssd_f, ssd_b), (gla_f, gla_b)

    ssd0 = jnp.zeros((bsz,) + SSD_STATE_SHAPE, F32)
    gla0 = jnp.zeros((bsz,) + GLA_STATE_SHAPE, F32)
    xc, ssd_fin, gla_fin = even_layer(ctx, c_scale, c_shift, c_gate, (ssd0, ssd0), (gla0, gla0))
    x, _, _ = even_layer(x, scale, shift, gate, ssd_fin, gla_fin)

    mod = _adaln(cvecs, o_mod_w[0], o_mod_b[0])
    shift, scale, gate = _mod_rows(mod, lat_rows, bsz, d)
    c_shift, c_scale, _ = _mod_rows(mod, bsz, bsz, d)
    w_in = _prep_odd(o_w_in[0])
    q_norm = _rope_head_layout(o_q_norm[0])
    k_norm = _rope_head_layout(o_k_norm[0])
    cos, sin = _rope_tables(length)
    no_rot = (jnp.ones((n_ctx, ATT_HEAD_DIM), F32), jnp.zeros((n_ctx, ATT_HEAD_DIM), F32))
    kc, vc = _proj_odd(xc, o_norm[0], c_scale, c_shift, w_in[:, :2 * ATT_KV_W], *no_rot,
                       q_norm, k_norm, False, PROJ_ROWS)
    k, v, q, g = _proj_odd(x, o_norm[0], scale, shift, w_in, cos, sin, q_norm, k_norm, True, PROJ_ROWS)
    o = _attention(o_sink[0].astype(F32), q, k, v, kc, vc, g)
    return _out_proj(x, gate, (o,), o_w_out[0].astype(BF16), OUT_ROWS, "out_odd")
```

```python
import functools

import jax
import jax.numpy as jnp
import numpy as np
from jax import lax
from jax.experimental import pallas as pl
from jax.experimental.pallas import tpu as pltpu

F32 = jnp.float32
BF16 = jnp.bfloat16

GRID_W = 64
SSD_HEADS = 16
SSD_HEAD_DIM = 64
SSD_WIDTH = SSD_HEADS * SSD_HEAD_DIM
SSD_GROUPS = 2
SSD_STATE = 128
SSD_CHUNK = 128
CONV_K = 5
CONV_CH = SSD_WIDTH + 2 * SSD_GROUPS * SSD_STATE
GLA_HEADS = 4
GLA_KEY_DIM = 128
GLA_VAL_DIM = 256
GLA_K = GLA_HEADS * GLA_KEY_DIM
GLA_V = GLA_HEADS * GLA_VAL_DIM
GLA_RANK = 16
GLA_GATE_NORMALIZER = 16.0
GLA_CHUNK = 64
ATT_HEADS = 16
ATT_KV_HEADS = 4
ATT_GROUP = ATT_HEADS // ATT_KV_HEADS
ATT_HEAD_DIM = 128
ATT_W = ATT_HEADS * ATT_HEAD_DIM
ATT_KV_W = ATT_KV_HEADS * ATT_HEAD_DIM
WINDOW = 128
ATT_BLOCK = 128
ROPE_BASE = 10000.0
ROPE_FREQS = ATT_HEAD_DIM // 4
NORM_EPS = 1e-6
LOG2E = 1.4426950408889634

LANES = 128
SUBLANES = 8
VMEM_LIMIT_BYTES = 56 * 1024 * 1024

SSD_GROUP_W = SSD_WIDTH // SSD_GROUPS
SSD_BC_W = SSD_GROUPS * SSD_STATE
SMALL_W = LANES
HALO = 2 * SUBLANES
HEADS_PER_DOT = 4
PROJ_ROW_PARTS = 4
PROJ_MIN_PART_ROWS = 128
CONV_ROW_STRIDE = 2 * SUBLANES + 1
CONV_OUT_ROWS = CONV_ROW_STRIDE * SUBLANES
assert SSD_CHUNK <= CONV_OUT_ROWS <= SSD_CHUNK + HALO - (CONV_K - 1) // 2


def _dot(a, b):
    return jnp.dot(a.astype(BF16), b.astype(BF16), preferred_element_type=F32)


def _dot_nt(a, b):
    return lax.dot_general(a.astype(BF16), b.astype(BF16), (((1,), (1,)), ((), ())),
                           preferred_element_type=F32)


def _dot_tn(a, b):
    return lax.dot_general(a.astype(BF16), b.astype(BF16), (((0,), (0,)), ((), ())),
                           preferred_element_type=F32)


def _split(v):
    hi = v.astype(BF16)
    lo = (v - hi.astype(F32)).astype(BF16)
    return hi, lo


def _dot_split_lhs(v, m):
    hi, lo = _split(v)
    return (jnp.dot(hi, m, preferred_element_type=F32) + jnp.dot(lo, m, preferred_element_type=F32))


def _dot_split_rhs(m, v):
    hi, lo = _split(v)
    return (jnp.dot(m, hi, preferred_element_type=F32) + jnp.dot(m, lo, preferred_element_type=F32))


def _dot3(a, b):
    ah, al = _split(a)
    bh, bl = _split(b)
    return (jnp.dot(ah, bh, preferred_element_type=F32) + jnp.dot(al, bh, preferred_element_type=F32)
            + jnp.dot(ah, bl, preferred_element_type=F32))


def _silu(v):
    h = 0.5 * v
    return h + h * jnp.tanh(h)


def _softplus(v):
    return jnp.maximum(v, 0.0) + jnp.log(1.0 + jnp.exp(-jnp.abs(v)))


def _log_sigmoid(v):
    return jnp.minimum(v, 0.0) - jnp.log(1.0 + jnp.exp(-jnp.abs(v)))


def _tri(n):
    row = lax.broadcasted_iota(jnp.int32, (n, n), 0)
    col = lax.broadcasted_iota(jnp.int32, (n, n), 1)
    return row >= col, col >= row


def _params(*sem):
    return pltpu.CompilerParams(dimension_semantics=sem, vmem_limit_bytes=VMEM_LIMIT_BYTES)


def _adaln_kernel(c_ref, we_ref, be_ref, wo_ref, bo_ref, o_ref):
    j = pl.program_id(0)
    half = pl.num_programs(0) // 2
    act = _silu(c_ref[...])

    @pl.when(j < half)
    def _():
        o_ref[...] = _dot3(act, we_ref[...]) + be_ref[...]

    @pl.when(j >= half)
    def _():
        o_ref[...] = _dot3(act, wo_ref[...]) + bo_ref[...]


def _adaln(cvecs, w_even, b_even, w_odd, b_odd):
    rows, d = cvecs.shape
    n = w_even.shape[1]
    tn = 1024
    nt = n // tn
    even = lambda j: (0, jnp.minimum(j, nt - 1))
    odd = lambda j: (0, jnp.maximum(j - nt, 0))
    return pl.pallas_call(
        _adaln_kernel,
        grid=(2 * nt,),
        in_specs=[pl.BlockSpec((rows, d), lambda j: (0, 0)),
                  pl.BlockSpec((d, tn), even), pl.BlockSpec((1, tn), even),
                  pl.BlockSpec((d, tn), odd), pl.BlockSpec((1, tn), odd)],
        out_specs=pl.BlockSpec((rows, tn), lambda j: (0, j)),
        out_shape=jax.ShapeDtypeStruct((rows, 2 * n), F32),
        compiler_params=_params("arbitrary"),
        name="adaln",
    )(cvecs, w_even, b_even.reshape(1, n), w_odd, b_odd.reshape(1, n))


def _modulated_norm(x, g, sc, sh):
    r = lax.rsqrt(jnp.mean(x * x, axis=-1, keepdims=True) + NORM_EPS)
    return ((x * r) * g) * (1.0 + sc) + sh


def _store_cols(h, w_ref, off, ref, rs=slice(None), act=None):
    n = ref.shape[-1]
    for c0 in range(0, n, 512):
        c1 = min(n, c0 + 512)
        t = jnp.dot(h, w_ref[:, off + c0:off + c1], preferred_element_type=F32)
        ref[rs, c0:c1] = (t if act is None else act(t)).astype(ref.dtype)
    return off + n


def _row_parts(rows):
    step = min(rows, max(rows // PROJ_ROW_PARTS, PROJ_MIN_PART_ROWS))
    return [slice(r0, r0 + step) for r0 in range(0, rows, step)]


def _proj_even_kernel(x_ref, g_ref, sc_ref, sh_ref, w_ref, *out_refs):
    for rs in _row_parts(x_ref.shape[0]):
        h = _modulated_norm(x_ref[rs, :], g_ref[...], sc_ref[...], sh_ref[...]).astype(BF16)
        off = 0
        for ref in out_refs:
            off = _store_cols(h, w_ref, off, ref, rs)


def _head_norm(t, gain):
    r = lax.rsqrt(jnp.mean(t * t, axis=-1, keepdims=True) + NORM_EPS)
    return (t * r) * gain


def _rope(t, cos, sin_signed):
    return t * cos + pltpu.roll(t, ATT_HEAD_DIM // 2, 1) * sin_signed


def _proj_odd_kernel(x_ref, g_ref, sc_ref, sh_ref, w_ref, cos_ref, sin_ref, qn_ref, kn_ref,
                     k_ref, v_ref, *qg_refs):
    h = _modulated_norm(x_ref[...], g_ref[...], sc_ref[...], sh_ref[...]).astype(BF16)
    cos = cos_ref[...]
    sin = sin_ref[...]
    scale = ATT_HEAD_DIM ** -0.5 * LOG2E

    width = HEADS_PER_DOT * ATT_HEAD_DIM
    dh = ATT_HEAD_DIM
    pr = lax.broadcasted_iota(jnp.int32, (2 * dh, 2 * dh), 0) // dh
    pc = lax.broadcasted_iota(jnp.int32, (2 * dh, 2 * dh), 1) // dh
    head_ones = jnp.where(pr == pc, 1.0, 0.0).astype(BF16)

    def project(col):
        return jnp.dot(h, w_ref[:, col:col + width], preferred_element_type=F32)

    def finish_heads(t4, ref, j0, gain, out_scale):
        for j in range(0, HEADS_PER_DOT, 2):
            t2 = t4[:, j * dh:(j + 2) * dh]
            mean_sq = jnp.dot((t2 * t2).astype(BF16), head_ones, preferred_element_type=F32) * (1.0 / dh)
            t2 = t2 * lax.rsqrt(mean_sq + NORM_EPS)
            for jj in range(2):
                t = _rope(t2[:, jj * dh:(jj + 1) * dh] * gain, cos, sin)
                c0 = (j0 + j + jj) * dh
                ref[:, c0:c0 + dh] = (t if out_scale is None else t * out_scale).astype(ref.dtype)

    work = []
    if qg_refs:
        q_ref, gate_ref = qg_refs
        for j0 in range(0, ATT_HEADS, HEADS_PER_DOT):
            c0 = j0 * dh

            def finish_gate(t4, c0=c0):
                gate_ref[:, c0:c0 + width] = _silu(t4)

            work.append((2 * ATT_KV_W + ATT_W + c0, finish_gate))
            work.append((2 * ATT_KV_W + c0,
                         functools.partial(finish_heads, ref=q_ref, j0=j0, gain=qn_ref[...], out_scale=scale)))
    for j0 in range(0, ATT_KV_HEADS, HEADS_PER_DOT):
        work.append((j0 * dh, functools.partial(finish_heads, ref=k_ref, j0=j0, gain=kn_ref[...], out_scale=None)))
    for col, finish in work:
        finish(project(col))
    _store_cols(h, w_ref, ATT_KV_W, v_ref)


def _row_tile(length, want):
    return min(length, want)


def _proj_even(x, norm_g, scale, shift, w_bf16, widths, tm):
    bsz, length, d = x.shape
    tm = _row_tile(length, tm)
    n = w_bf16.shape[1]
    row = lambda b, i: (b, i, 0)
    mod = lambda b, i: (b, 0, 0)
    return pl.pallas_call(
        _proj_even_kernel,
        grid=(bsz, length // tm),
        in_specs=[pl.BlockSpec((None, tm, d), row),
                  pl.BlockSpec((1, d), lambda b, i: (0, 0)),
                  pl.BlockSpec((None, 1, d), mod),
                  pl.BlockSpec((None, 1, d), mod),
                  pl.BlockSpec((d, n), lambda b, i: (0, 0), pipeline_mode=pl.Buffered(1))],
        out_specs=[pl.BlockSpec((None, tm, wd), row) for wd, _ in widths],
        out_shape=[jax.ShapeDtypeStruct((bsz, length, wd), dt) for wd, dt in widths],
        compiler_params=_params("parallel", "parallel"),
        name="proj_even",
    )(x, norm_g.reshape(1, d), scale, shift, w_bf16)


def _proj_odd(x, norm_g, scale, shift, w_bf16, cos, sin, q_norm, k_norm, with_queries, tm):
    bsz, length, d = x.shape
    tm = _row_tile(length, tm)
    n = w_bf16.shape[1]
    row = lambda b, i: (b, i, 0)
    mod = lambda b, i: (b, 0, 0)
    const = lambda b, i: (0, 0)
    widths = [(ATT_KV_W, BF16), (ATT_KV_W, BF16)]
    if with_queries:
        widths += [(ATT_W, BF16), (ATT_W, F32)]
    return pl.pallas_call(
        _proj_odd_kernel,
        grid=(bsz, length // tm),
        in_specs=[pl.BlockSpec((None, tm, d), row),
                  pl.BlockSpec((1, d), const),
                  pl.BlockSpec((None, 1, d), mod),
                  pl.BlockSpec((None, 1, d), mod),
                  pl.BlockSpec((d, n), const, pipeline_mode=pl.Buffered(1)),
                  pl.BlockSpec((tm, ATT_HEAD_DIM), lambda b, i: (i, 0)),
                  pl.BlockSpec((tm, ATT_HEAD_DIM), lambda b, i: (i, 0)),
                  pl.BlockSpec((1, ATT_HEAD_DIM), const),
                  pl.BlockSpec((1, ATT_HEAD_DIM), const)],
        out_specs=[pl.BlockSpec((None, tm, wd), row) for wd, _ in widths],
        out_shape=[jax.ShapeDtypeStruct((bsz, length, wd), dt) for wd, dt in widths],
        compiler_params=_params("parallel", "parallel"),
        name="proj_odd_q" if with_queries else "proj_odd_kv",
    )(x, norm_g.reshape(1, d), scale, shift, w_bf16, cos, sin,
      q_norm.reshape(1, ATT_HEAD_DIM), k_norm.reshape(1, ATT_HEAD_DIM))


def _conv_silu(win_s, prev_ref, cur_ref, next_ref, has_prev, has_next, cw_ref, cb_ref, x_s):
    rows = cur_ref.shape[0]
    half = (CONV_K - 1) // 2
    for j in range(CONV_CH // LANES):
        ls = slice(j * LANES, (j + 1) * LANES)
        win_s[j, 0:HALO, :] = jnp.where(has_prev, prev_ref[:, ls].astype(F32), 0.0)
        win_s[j, HALO:HALO + rows, :] = cur_ref[:, ls].astype(F32)
        win_s[j, HALO + rows:2 * HALO + rows, :] = jnp.where(has_next, next_ref[:, ls].astype(F32), 0.0)
        taps = [cw_ref[k:k + 1, ls] for k in range(CONV_K)]
        bias = cb_ref[:, ls]
        win = win_s.at[j]
        out = x_s.at[j]
        for c in range(rows // SSD_CHUNK):
            for a in range(CONV_ROW_STRIDE):
                r0 = HALO + c * SSD_CHUNK - half + a
                acc = bias + taps[0] * win[pl.ds(r0, SUBLANES, stride=CONV_ROW_STRIDE), :]
                for k in range(1, CONV_K):
                    acc = acc + taps[k] * win[pl.ds(r0 + k, SUBLANES, stride=CONV_ROW_STRIDE), :]
                out[pl.ds(c * CONV_OUT_ROWS + a, SUBLANES, stride=CONV_ROW_STRIDE), :] = _silu(acc)


def _ssd_fwd_kernel(p_ref, c_ref, n_ref, sm_ref, cw_ref, cb_ref, bias_row_ref, bias_col_ref,
                    alog_row_ref, alog_col_ref, dskip_ref, ef_ref, eb_ref, sf0_ref,
                    ya_ref, cm_ref, bt_ref, wxb_ref, p3_ref, sff_ref,
                    sf_s, win_s, x_s):
    i = pl.program_id(1)
    nc = pl.num_programs(1)
    t = SSD_CHUNK
    nh = SSD_HEADS
    slabs_x = SSD_WIDTH // LANES
    slab_b = slabs_x
    slab_c = slabs_x + SSD_GROUPS

    @pl.when(i == 0)
    def _():
        sf_s[...] = sf0_ref[...]

    lower, upper = _tri(t)
    ltri = jnp.where(lower, 1.0, 0.0).astype(BF16)
    utri = jnp.where(upper, 1.0, 0.0).astype(BF16)
    a_row = -jnp.exp(alog_row_ref[...])
    a_col = -jnp.exp(alog_col_ref[...])
    ef = ef_ref[...]
    eb = eb_ref[...]
    neg_inf = float("-inf")

    _conv_silu(win_s, p_ref, c_ref, n_ref, i > 0, i < nc - 1, cw_ref, cb_ref, x_s)
    n_chunks = sm_ref.shape[0] // t
    lane = lax.broadcasted_iota(jnp.int32, (t, LANES), 1)
    left = lane < SSD_HEAD_DIM
    heads_per_group = nh // SSD_GROUPS
    pairs_per_group = heads_per_group // 2

    def chunk_decays(c):
        rs = slice(c * t, (c + 1) * t)
        sm = sm_ref[rs, :]
        dt = _softplus(sm + bias_row_ref[...])
        dta = dt * a_row
        cs = _dot_split_rhs(ltri, dta)
        rc = _dot_split_rhs(utri, dta)
        dt_t = _softplus(sm.T[0:2 * nh, :] + bias_col_ref[...])
        dta_t = dt_t * a_col
        log2_dt_t = jnp.log2(dt_t)
        f_q = cs * LOG2E
        g_q = rc * LOG2E
        f_k = _dot_split_lhs(dta_t, utri) * LOG2E - log2_dt_t
        g_k = _dot_split_lhs(dta_t, ltri) * LOG2E - log2_dt_t
        dec_f = _dot_split_lhs(jnp.exp2(f_q), ef)
        wgt_f = _dot_split_lhs(dt * jnp.exp(cs[t - 1:t, :] - cs), ef)
        wgt_b = _dot_split_lhs(dt * jnp.exp(rc[0:1, :] - rc), eb)
        p3_ref[rs, :] = jnp.exp2(g_q)
        for p in range(slabs_x):
            ls = slice(p * LANES, (p + 1) * LANES)
            wxb_ref[rs, ls] = (x_s[p, c * CONV_OUT_ROWS:c * CONV_OUT_ROWS + t, :] * wgt_b[:, ls]).astype(BF16)
        xr = slice(c * CONV_OUT_ROWS, c * CONV_OUT_ROWS + t)
        cbs = [_dot_nt(x_s[slab_c + g, xr, :], x_s[slab_b + g, xr, :]) for g in range(SSD_GROUPS)]
        return f_q, g_q, f_k, g_k, dec_f, wgt_f, cbs

    decays = [chunk_decays(c) for c in range(n_chunks)]

    for c in range(n_chunks):
        f_q, g_q, f_k, g_k, dec_f, wgt_f, cbs = decays[c]
        rs = slice(c * t, (c + 1) * t)
        xr = slice(c * CONV_OUT_ROWS, c * CONV_OUT_ROWS + t)

        def decay_matrix(h, cbg):
            lf = jnp.exp2(jnp.where(lower, f_q[:, h:h + 1] - f_k[h:h + 1, :], neg_inf))
            ub = jnp.exp2(jnp.where(upper, g_q[:, nh + h:nh + h + 1] - g_k[nh + h:nh + h + 1, :], neg_inf))
            return cbg * (lf + ub)

        for g in range(SSD_GROUPS):
            cg = x_s[slab_c + g, xr, :]
            bg = x_s[slab_b + g, xr, :]
            gs = slice(g * SSD_GROUP_W, (g + 1) * SSD_GROUP_W)
            cm_ref[rs, g * SSD_STATE:(g + 1) * SSD_STATE] = cg.astype(BF16)
            cbg = cbs[g]
            state = sf_s[g]
            y_off = _dot(cg, state) * dec_f[:, gs]
            wx = []
            for pair in range(pairs_per_group):
                h0 = g * heads_per_group + 2 * pair
                p = h0 // 2
                ls = slice(p * LANES, (p + 1) * LANES)
                xp = x_s[p, xr, :]
                xb = xp.astype(BF16)
                y = jnp.where(left, _dot(decay_matrix(h0, cbg), xb), _dot(decay_matrix(h0 + 1, cbg), xb))
                ya_ref[rs, ls] = y + y_off[:, pair * LANES:(pair + 1) * LANES] + dskip_ref[:, ls] * xp
                wx.append((xp * wgt_f[:, ls]).astype(BF16))
            bt = bg.T.astype(BF16)
            bt_ref[c, g * SSD_STATE:(g + 1) * SSD_STATE, :] = bt
            sf_s[g] = dec_f[t - 1:t, gs] * state + jnp.dot(bt, jnp.concatenate(wx, axis=1),
                                                           preferred_element_type=F32)

    @pl.when(i == nc - 1)
    def _():
        sff_ref[...] = sf_s[...]


def _ssd_bwd_kernel(cm_ref, bt_ref, wxb_ref, p3_ref, ya_ref, z_ref, eb_ref, sn_ref, sb0_ref,
                    ys_ref, sbf_ref, sb_s):
    i = pl.program_id(1)
    nc = pl.num_programs(1)

    @pl.when(i == 0)
    def _():
        sb_s[...] = sb0_ref[...]

    t = SSD_CHUNK
    n_sub = bt_ref.shape[0]
    dec_b = _dot_split_lhs(p3_ref[...], eb_ref[...])
    for g in range(SSD_GROUPS):
        gs = slice(g * SSD_GROUP_W, (g + 1) * SSD_GROUP_W)
        ss = slice(g * SSD_STATE, (g + 1) * SSD_STATE)
        chunks = [slice(s * t, (s + 1) * t) for s in range(n_sub)]
        incs = [jnp.dot(bt_ref[s, ss, :], wxb_ref[rs, gs], preferred_element_type=F32)
                for s, rs in enumerate(chunks)]
        states = [None] * n_sub
        state = sb_s[g]
        for s in reversed(range(n_sub)):
            states[s] = state
            state = dec_b[s * t:s * t + 1, gs] * state + incs[s]
        sb_s[g] = state
        for s, rs in enumerate(chunks):
            y_off = jnp.dot(cm_ref[rs, ss], states[s].astype(BF16), preferred_element_type=F32) * dec_b[rs, gs]
            y = (ya_ref[rs, gs] + y_off) * _silu(z_ref[rs, gs].astype(F32))
            ys_ref[rs, gs] = _head_norm(y, sn_ref[:, gs]).astype(ys_ref.dtype)

    @pl.when(i == nc - 1)
    def _():
        sbf_ref[...] = sb_s[...]


SSD_STATE_SHAPE = (SSD_GROUPS, SSD_STATE, SSD_GROUP_W)


def _ssd_fwd(xbc, small, conv_w, conv_b, bias_row, bias_col, alog_row, alog_col, dskip_row, ef, eb, sf0):
    bsz, length, _ = xbc.shape
    t = SSD_CHUNK
    rows = _row_tile(length, SSD_FWD_ROWS)
    nc = length // t
    per = rows // HALO
    last_halo = length // HALO - 1

    def cur(b, i): return (b, i, 0)
    def prev(b, i): return (b, jnp.maximum(i * per - 1, 0), 0)
    def nxt(b, i): return (b, jnp.minimum((i + 1) * per, last_halo), 0)
    const2 = lambda b, i: (0, 0)
    state = lambda b, i: (b, 0, 0, 0)

    halo_spec = lambda f: pl.BlockSpec((None, HALO, CONV_CH), f)
    chunk_spec = lambda w: pl.BlockSpec((None, rows, w), cur)
    state_spec = pl.BlockSpec((None,) + SSD_STATE_SHAPE, state)
    return dict(
        grid=(bsz, length // rows),
        in_specs=[halo_spec(prev), chunk_spec(CONV_CH), halo_spec(nxt), chunk_spec(SMALL_W),
                  pl.BlockSpec((CONV_K, CONV_CH), const2),
                  pl.BlockSpec((1, CONV_CH), const2),
                  pl.BlockSpec((1, SMALL_W), const2),
                  pl.BlockSpec((2 * SSD_HEADS, 1), const2),
                  pl.BlockSpec((1, SMALL_W), const2),
                  pl.BlockSpec((2 * SSD_HEADS, 1), const2),
                  pl.BlockSpec((1, SSD_WIDTH), const2),
                  pl.BlockSpec((SMALL_W, SSD_WIDTH), const2),
                  pl.BlockSpec((SMALL_W, SSD_WIDTH), const2),
                  state_spec],
        out_specs=[chunk_spec(SSD_WIDTH), chunk_spec(SSD_BC_W),
                   pl.BlockSpec((None, rows // t, SSD_BC_W, t), lambda b, i: (b, i, 0, 0)),
                   chunk_spec(SSD_WIDTH), chunk_spec(SMALL_W), state_spec],
        out_shape=[jax.ShapeDtypeStruct((bsz, length, SSD_WIDTH), F32),
                   jax.ShapeDtypeStruct((bsz, length, SSD_BC_W), BF16),
                   jax.ShapeDtypeStruct((bsz, nc, SSD_BC_W, t), BF16),
                   jax.ShapeDtypeStruct((bsz, length, SSD_WIDTH), BF16),
                   jax.ShapeDtypeStruct((bsz, length, SMALL_W), F32),
                   jax.ShapeDtypeStruct((bsz,) + SSD_STATE_SHAPE, F32)],
        scratch_shapes=[pltpu.VMEM(SSD_STATE_SHAPE, F32),
                        pltpu.VMEM((CONV_CH // LANES, rows + 2 * HALO, LANES), F32),
                        pltpu.VMEM((CONV_CH // LANES, (rows // t) * CONV_OUT_ROWS, LANES), F32)],
        args=(xbc, xbc, xbc, small, conv_w, conv_b, bias_row, bias_col, alog_row, alog_col, dskip_row, ef, eb, sf0))


def _ssd_bwd(cm, bt, wxb, p3, ya, z, eb, ssd_norm, sb0, rows):
    bsz, length, _ = ya.shape
    t = SSD_CHUNK
    n_steps = length // rows
    rev = lambda b, i: (b, n_steps - 1 - i, 0)
    const2 = lambda b, i: (0, 0)
    state = lambda b, i: (b, 0, 0, 0)
    chunk_spec = lambda w: pl.BlockSpec((None, rows, w), rev)
    state_spec = pl.BlockSpec((None,) + SSD_STATE_SHAPE, state)
    return dict(
        in_specs=[chunk_spec(SSD_BC_W),
                  pl.BlockSpec((None, rows // t, SSD_BC_W, t), lambda b, i: (b, n_steps - 1 - i, 0, 0)),
                  chunk_spec(SSD_WIDTH), chunk_spec(SMALL_W), chunk_spec(SSD_WIDTH), chunk_spec(SSD_WIDTH),
                  pl.BlockSpec((SMALL_W, SSD_WIDTH), const2),
                  pl.BlockSpec((1, SSD_WIDTH), const2),
                  state_spec],
        state_spec=state_spec,
        state_shape=jax.ShapeDtypeStruct((bsz,) + SSD_STATE_SHAPE, F32),
        state_scratch=pltpu.VMEM(SSD_STATE_SHAPE, F32),
        args=(cm, bt, wxb, p3, ya, z, eb, ssd_norm.reshape(1, SSD_WIDTH), sb0))


def _gla_fwd_kernel(q_ref, k_ref, v_ref, sm_ref, wg_ref, gb_ref, sf0_ref,
                    oa_ref, qdb_ref, kwb_ref, decb_ref, sff_ref, sf_s):
    i = pl.program_id(1)
    n_steps = pl.num_programs(1)
    t = GLA_CHUNK
    rows = q_ref.shape[0]
    n_sub = rows // t
    dk, dv = GLA_KEY_DIM, GLA_VAL_DIM
    qscale = dk ** -0.5
    inv_norm = 1.0 / GLA_GATE_NORMALIZER

    @pl.when(i == 0)
    def _():
        sf_s[...] = sf0_ref[...]

    lower, upper = _tri(t)
    grp = min(rows, GLA_CUMSUM_ROWS)
    row = lax.broadcasted_iota(jnp.int32, (grp, grp), 0)
    col = lax.broadcasted_iota(jnp.int32, (grp, grp), 1)
    diff = row - col
    pos = row % t
    bd_lower = jnp.where(diff >= 0, jnp.where(diff <= pos, 1.0, 0.0), 0.0).astype(BF16)
    bd_upper = jnp.where(diff <= 0, jnp.where(-diff <= t - 1 - pos, 1.0, 0.0), 0.0).astype(BF16)

    hi, lo = _split(sm_ref[...])
    wg_hi, wg_lo = _split(wg_ref[...])
    logits = (jnp.dot(hi, wg_hi, preferred_element_type=F32) + jnp.dot(lo, wg_hi, preferred_element_type=F32)
              + jnp.dot(hi, wg_lo, preferred_element_type=F32)) + gb_ref[...]
    lg = _log_sigmoid(logits) * inv_norm
    groups = [slice(r0, r0 + grp) for r0 in range(0, rows, grp)]
    cs = jnp.concatenate([_dot_split_rhs(bd_lower, lg[gr, 0:GLA_K]) for gr in groups], axis=0)
    rc = jnp.concatenate([_dot_split_rhs(bd_upper, lg[gr, GLA_K:2 * GLA_K]) for gr in groups], axis=0)

    def per_chunk_row(v, offset):
        return jnp.concatenate([jnp.broadcast_to(v[s * t + offset:s * t + offset + 1, :], (t, v.shape[1]))
                                for s in range(n_sub)], axis=0)

    for h in range(GLA_HEADS):
        ks = slice(h * dk, (h + 1) * dk)
        vs = slice(h * dv, (h + 1) * dv)
        c = cs[:, ks]
        r = rc[:, ks]
        qh = q_ref[:, ks].astype(F32) * qscale
        kh = k_ref[:, ks].astype(F32)
        qdf = (qh * jnp.exp(c)).astype(BF16)
        kif = (kh * jnp.exp(-c)).astype(BF16)
        qdb = (qh * jnp.exp(r)).astype(BF16)
        kib = (kh * jnp.exp(-r)).astype(BF16)
        kwf = (kh * jnp.exp(per_chunk_row(c, t - 1) - c)).astype(BF16)
        qdb_ref[:, ks] = qdb
        kwb_ref[:, ks] = (kh * jnp.exp(per_chunk_row(r, 0) - r)).astype(BF16)
        chunks = [slice(s * t, (s + 1) * t) for s in range(n_sub)]
        vhs = [v_ref[rs, vs].astype(BF16) for rs in chunks]
        atts = [(jnp.where(lower, _dot_nt(qdf[rs], kif[rs]), 0.0)
                 + jnp.where(upper, _dot_nt(qdb[rs], kib[rs]), 0.0)).astype(BF16) for rs in chunks]
        incs = [_dot_tn(vh, kwf[rs]) for vh, rs in zip(vhs, chunks)]
        states = [sf_s[h]]
        for s in range(n_sub):
            decb_ref[s, :, ks] = jnp.exp(r[s * t:s * t + 1, :])
            states.append(states[s] * jnp.exp(c[s * t + t - 1:s * t + t, :]) + incs[s])
        sf_s[h] = states[n_sub]
        for s, rs in enumerate(chunks):
            oa_ref[rs, vs] = (jnp.dot(atts[s], vhs[s], preferred_element_type=F32)
                              + _dot_nt(qdf[rs], states[s]))

    @pl.when(i == n_steps - 1)
    def _():
        sff_ref[...] = sf_s[...]


def _gla_bwd_kernel(qdb_ref, kwb_ref, decb_ref, v_ref, oa_ref, g_ref, gn_ref, sb0_ref,
                    os_ref, sbf_ref, sb_s):
    i = pl.program_id(1)
    n_steps = pl.num_programs(1)
    t = GLA_CHUNK
    n_sub = qdb_ref.shape[0] // t
    dk, dv = GLA_KEY_DIM, GLA_VAL_DIM

    @pl.when(i == 0)
    def _():
        sb_s[...] = sb0_ref[...]

    for h in range(GLA_HEADS):
        ks = slice(h * dk, (h + 1) * dk)
        vs = slice(h * dv, (h + 1) * dv)
        state = sb_s[h]
        for s in reversed(range(n_sub)):
            rs = slice(s * t, (s + 1) * t)
            o = oa_ref[rs, vs] + _dot_nt(qdb_ref[rs, ks], state)
            state = state * decb_ref[s, :, ks] + _dot_tn(v_ref[rs, vs], kwb_ref[rs, ks])
            os_ref[rs, vs] = (_head_norm(o, gn_ref[:, vs]) * _silu(g_ref[rs, vs].astype(F32))).astype(os_ref.dtype)
        sb_s[h] = state

    @pl.when(i == n_steps - 1)
    def _():
        sbf_ref[...] = sb_s[...]


GLA_STATE_SHAPE = (GLA_HEADS, GLA_VAL_DIM, GLA_KEY_DIM)


def _gla_fwd(q, k, v, small, wg, gb, sf0, rows):
    bsz, length, _ = q.shape
    rows = _row_tile(length, rows)
    n_steps = length // rows
    n_sub = rows // GLA_CHUNK
    fwd = lambda b, i: (b, i, 0)
    const2 = lambda b, i: (0, 0)
    state = lambda b, i: (b, 0, 0, 0)
    blk = lambda w: pl.BlockSpec((None, rows, w), fwd)
    state_spec = pl.BlockSpec((None,) + GLA_STATE_SHAPE, state)
    return dict(
        grid=(bsz, n_steps),
        in_specs=[blk(GLA_K), blk(GLA_K), blk(GLA_V), blk(SMALL_W),
                  pl.BlockSpec((SMALL_W, 2 * GLA_K), const2),
                  pl.BlockSpec((1, 2 * GLA_K), const2),
                  state_spec],
        out_specs=[blk(GLA_V), blk(GLA_K), blk(GLA_K),
                   pl.BlockSpec((None, n_sub, 1, GLA_K), lambda b, i: (b, i, 0, 0)),
                   state_spec],
        out_shape=[jax.ShapeDtypeStruct((bsz, length, GLA_V), F32),
                   jax.ShapeDtypeStruct((bsz, length, GLA_K), BF16),
                   jax.ShapeDtypeStruct((bsz, length, GLA_K), BF16),
                   jax.ShapeDtypeStruct((bsz, length // GLA_CHUNK, 1, GLA_K), F32),
                   jax.ShapeDtypeStruct((bsz,) + GLA_STATE_SHAPE, F32)],
        scratch_shapes=[pltpu.VMEM(GLA_STATE_SHAPE, F32)],
        args=(q, k, v, small, wg, gb, sf0))


def _sweep_call(body, name, call):
    return pl.pallas_call(
        body,
        grid=call["grid"],
        in_specs=call["in_specs"],
        out_specs=call["out_specs"],
        out_shape=call["out_shape"],
        scratch_shapes=call["scratch_shapes"],
        compiler_params=_params("arbitrary", "arbitrary"),
        name=name,
    )(*call["args"])


def _gla_bwd(qdb, kwb, decb, v, oa, g, gla_norm, sb0, rows):
    bsz, length, _ = oa.shape
    n_steps = length // rows
    n_sub = rows // GLA_CHUNK
    rev = lambda b, i: (b, n_steps - 1 - i, 0)
    const2 = lambda b, i: (0, 0)
    state = lambda b, i: (b, 0, 0, 0)
    blk = lambda w: pl.BlockSpec((None, rows, w), rev)
    state_spec = pl.BlockSpec((None,) + GLA_STATE_SHAPE, state)
    return dict(
        in_specs=[blk(GLA_K), blk(GLA_K),
                  pl.BlockSpec((None, n_sub, 1, GLA_K), lambda b, i: (b, n_steps - 1 - i, 0, 0)),
                  blk(GLA_V), blk(GLA_V), blk(GLA_V),
                  pl.BlockSpec((1, GLA_V), const2),
                  state_spec],
        state_spec=state_spec,
        state_shape=jax.ShapeDtypeStruct((bsz,) + GLA_STATE_SHAPE, F32),
        state_scratch=pltpu.VMEM(GLA_STATE_SHAPE, F32),
        args=(qdb, kwb, decb, v, oa, g, gla_norm.reshape(1, GLA_V), sb0))


SSD_BWD_INPUTS = 9
GLA_BWD_INPUTS = 8


def _even_tail_kernel(*refs):
    ssd_in = refs[:SSD_BWD_INPUTS]
    gla_in = refs[SSD_BWD_INPUTS:SSD_BWD_INPUTS + GLA_BWD_INPUTS]
    x_ref, gate_ref, w_ref, y_ref, ssd_fin_ref, gla_fin_ref, ssd_s, gla_s, mix_s = \
        refs[SSD_BWD_INPUTS + GLA_BWD_INPUTS:]
    _ssd_bwd_kernel(*ssd_in, mix_s.at[:, 0:SSD_WIDTH], ssd_fin_ref, ssd_s)
    _gla_bwd_kernel(*gla_in, mix_s.at[:, SSD_WIDTH:SSD_WIDTH + GLA_V], gla_fin_ref, gla_s)
    y_ref[...] = x_ref[...] + gate_ref[...] * jnp.dot(mix_s[...], w_ref[...], preferred_element_type=F32)


def _even_tail(x, gate, w_bf16, ssd_args, gla_args):
    bsz, length, d = x.shape
    rows = _row_tile(length, EVEN_TAIL_ROWS)
    n_steps = length // rows
    ssd = _ssd_bwd(*ssd_args, rows)
    gla = _gla_bwd(*gla_args, rows)
    assert len(ssd["in_specs"]) == SSD_BWD_INPUTS and len(gla["in_specs"]) == GLA_BWD_INPUTS
    rev = lambda b, i: (b, n_steps - 1 - i, 0)
    return pl.pallas_call(
        _even_tail_kernel,
        grid=(bsz, n_steps),
        in_specs=(ssd["in_specs"] + gla["in_specs"]
                  + [pl.BlockSpec((None, rows, d), rev),
                     pl.BlockSpec((None, 1, d), lambda b, i: (b, 0, 0)),
                     pl.BlockSpec(w_bf16.shape, lambda b, i: (0, 0), pipeline_mode=pl.Buffered(1))]),
        out_specs=[pl.BlockSpec((None, rows, d), rev), ssd["state_spec"], gla["state_spec"]],
        out_shape=[jax.ShapeDtypeStruct((bsz, length, d), F32), ssd["state_shape"], gla["state_shape"]],
        scratch_shapes=[ssd["state_scratch"], gla["state_scratch"],
                        pltpu.VMEM((rows, SSD_WIDTH + GLA_V), BF16)],
        compiler_params=_params("arbitrary", "arbitrary"),
        name="even_tail",
    )(*ssd["args"], *gla["args"], x, gate, w_bf16)


def _out_proj_kernel(x_ref, gate_ref, *refs):
    *in_refs, w_ref, y_ref = refs
    acc = None
    off = 0
    for ref in in_refs:
        n = ref.shape[-1]
        part = jnp.dot(ref[...], w_ref[off:off + n, :], preferred_element_type=F32)
        acc = part if acc is None else acc + part
        off += n
    y_ref[...] = x_ref[...] + gate_ref[...] * acc


def _out_proj(x, gate, mixed, w_bf16, tm, name):
    bsz, length, d = x.shape
    tm = _row_tile(length, tm)
    row = lambda b, i: (b, i, 0)
    return pl.pallas_call(
        _out_proj_kernel,
        grid=(bsz, length // tm),
        in_specs=([pl.BlockSpec((None, tm, d), row),
                   pl.BlockSpec((None, 1, d), lambda b, i: (b, 0, 0))]
                  + [pl.BlockSpec((None, tm, m.shape[-1]), row) for m in mixed]
                  + [pl.BlockSpec(w_bf16.shape, lambda b, i: (0, 0), pipeline_mode=pl.Buffered(1))]),
        out_specs=pl.BlockSpec((None, tm, d), row),
        out_shape=jax.ShapeDtypeStruct((bsz, length, d), F32),
        compiler_params=_params("parallel", "parallel"),
        name=name,
    )(x, gate, *mixed, w_bf16)


def _attn_kernel(sink_ref, q_ref, k_ref, v_ref, kc_ref, vc_ref, g_ref, o_ref):
    j = pl.program_id(1)
    i = pl.program_id(2)
    length = k_ref.shape[0]
    blk = ATT_BLOCK
    band = 3 * blk
    dh = ATT_HEAD_DIM
    n_q = q_ref.shape[0] // blk
    n_ctx = kc_ref.shape[0]
    row_minus_col = (lax.broadcasted_iota(jnp.int32, (blk, band), 0)
                     - lax.broadcasted_iota(jnp.int32, (blk, band), 1))
    sink = jnp.concatenate([jnp.full((blk, LANES), sink_ref[j * ATT_GROUP + g] * LOG2E, F32)
                            for g in range(ATT_GROUP)], axis=0)
    kc = kc_ref[...]
    vc_ext = jnp.concatenate([vc_ref[...], jnp.ones((n_ctx, dh), BF16)], axis=1)
    ones_band = jnp.ones((band, dh), BF16)

    def window_start(qb):
        blk_idx = i * n_q + qb
        return blk_idx, pl.multiple_of(jnp.clip((blk_idx - 1) * blk, 0, length - band), blk)

    def scores(qb):
        _, start = window_start(qb)
        rs = slice(qb * blk, (qb + 1) * blk)
        q = jnp.concatenate([q_ref[rs, g * dh:(g + 1) * dh] for g in range(ATT_GROUP)], axis=0)
        return _dot_nt(q, k_ref[pl.ds(start, band), :]), _dot_nt(q, kc)

    pending = scores(0)
    for qb in range(n_q):
        s_band, s_ctx = pending
        if qb + 1 < n_q:
            pending = scores(qb + 1)
        blk_idx, start = window_start(qb)
        inside = jnp.abs(row_minus_col + (blk_idx * blk - start)) <= WINDOW
        v_all = jnp.concatenate([jnp.concatenate([v_ref[pl.ds(start, band), :], ones_band], axis=1),
                                 vc_ext], axis=0)
        rs = slice(qb * blk, (qb + 1) * blk)
        cols = []
        for c in range(band // LANES):
            ls = slice(c * LANES, (c + 1) * LANES)
            cols.append(jnp.concatenate(
                [jnp.where(inside[:, ls], s_band[g * blk:(g + 1) * blk, ls], float("-inf"))
                 for g in range(ATT_GROUP)], axis=0))
        for c in range(n_ctx // LANES):
            cols.append(s_ctx[:, c * LANES:(c + 1) * LANES])
        m = jnp.maximum(sink, jnp.max(functools.reduce(jnp.maximum, cols), axis=-1, keepdims=True))
        p = jnp.concatenate([jnp.exp2(col - m) for col in cols], axis=1).astype(BF16)
        acc = jnp.dot(p, v_all, preferred_element_type=F32)
        out = acc[:, 0:dh] / (acc[:, dh:2 * dh] + jnp.exp2(sink - m))
        for g in range(ATT_GROUP):
            cs = slice(g * dh, (g + 1) * dh)
            o_ref[rs, cs] = (out[g * blk:(g + 1) * blk, :] * g_ref[rs, cs]).astype(o_ref.dtype)


def _attention(sink, q, k, v, kc, vc, gate):
    bsz, length, _ = q.shape
    n_ctx = kc.shape[1]
    rows = _row_tile(length, ATT_Q_BLOCKS * ATT_BLOCK)
    gw = ATT_GROUP * ATT_HEAD_DIM
    qmap = lambda b, j, i, s: (b, i, j)
    kvmap = lambda b, j, i, s: (b, 0, j)
    grid_spec = pltpu.PrefetchScalarGridSpec(
        num_scalar_prefetch=1,
        grid=(bsz, ATT_KV_HEADS, length // rows),
        in_specs=[pl.BlockSpec((None, rows, gw), qmap),
                  pl.BlockSpec((None, length, ATT_HEAD_DIM), kvmap),
                  pl.BlockSpec((None, length, ATT_HEAD_DIM), kvmap),
                  pl.BlockSpec((None, n_ctx, ATT_HEAD_DIM), kvmap),
                  pl.BlockSpec((None, n_ctx, ATT_HEAD_DIM), kvmap),
                  pl.BlockSpec((None, rows, gw), qmap)],
        out_specs=pl.BlockSpec((None, rows, gw), qmap),
    )
    return pl.pallas_call(
        _attn_kernel,
        grid_spec=grid_spec,
        out_shape=jax.ShapeDtypeStruct((bsz, length, ATT_W), BF16),
        compiler_params=_params("parallel", "parallel", "arbitrary"),
        name="attention",
    )(sink, q, k, v, kc, vc, gate)


E_IN_SIZES = (SSD_WIDTH, CONV_CH, 2 * SSD_HEADS, GLA_K, GLA_K, GLA_V, GLA_V, 2 * GLA_RANK)
E_IN = sum(E_IN_SIZES)
E_ALIGNED = SSD_WIDTH + CONV_CH
E_SHIFT = 2 * SSD_HEADS
E_MAIN = 2 * GLA_K + 2 * GLA_V
E_OUT = E_ALIGNED + E_MAIN + SMALL_W
PREP_ROWS = 256
O_IN = 2 * ATT_KV_W + 2 * ATT_W


def _prep_even_kernel(wt_ref, o_ref):
    cols = wt_ref.shape[1]
    for dst in range(0, E_ALIGNED + E_MAIN, LANES):
        src = dst if dst < E_ALIGNED else dst + E_SHIFT
        o_ref[:, dst:dst + LANES] = wt_ref[src:src + LANES, :].T.astype(BF16)
    small = jnp.concatenate([wt_ref[E_ALIGNED:E_ALIGNED + E_SHIFT, :],
                             wt_ref[E_IN - 2 * GLA_RANK:E_IN, :],
                             jnp.zeros((SMALL_W - E_SHIFT - 2 * GLA_RANK, cols), F32)], axis=0)
    o_ref[:, E_ALIGNED + E_MAIN:E_OUT] = small.T.astype(BF16)


def _prep_even(w_in):
    d = w_in.shape[0]
    return pl.pallas_call(
        _prep_even_kernel,
        grid=(d // PREP_ROWS,),
        in_specs=[pl.BlockSpec((E_IN, PREP_ROWS), lambda i: (0, i))],
        out_specs=pl.BlockSpec((PREP_ROWS, E_OUT), lambda i: (i, 0)),
        out_shape=jax.ShapeDtypeStruct((d, E_OUT), BF16),
        compiler_params=_params("parallel"),
        name="prep_even",
    )(jnp.swapaxes(w_in, 0, 1))


def _prep_odd_kernel(w_ref, perm_ref, o_ref):
    def copy(c0, width):
        o_ref[:, c0:c0 + width] = w_ref[:, c0:c0 + width].astype(BF16)

    def permute(c0, width):
        for c in range(c0, c0 + width, 2 * LANES):
            o_ref[:, c:c + 2 * LANES] = jnp.dot(w_ref[:, c:c + 2 * LANES].astype(BF16), perm_ref[...],
                                                preferred_element_type=F32).astype(BF16)

    permute(0, ATT_KV_W)
    copy(ATT_KV_W, ATT_KV_W)
    permute(2 * ATT_KV_W, ATT_W)
    copy(2 * ATT_KV_W + ATT_W, ATT_W)


def _rope_perm(n):
    out = np.arange(n)
    head, rem = out // ATT_HEAD_DIM, out % ATT_HEAD_DIM
    half, axis, f = rem // (2 * ROPE_FREQS), (rem // ROPE_FREQS) % 2, rem % ROPE_FREQS
    return head * ATT_HEAD_DIM + axis * 2 * ROPE_FREQS + half * ROPE_FREQS + f


def _prep_odd(w_in):
    d = w_in.shape[0]
    src = _rope_perm(2 * LANES)
    perm = jnp.asarray(np.arange(2 * LANES)[:, None] == src[None, :], BF16)
    return pl.pallas_call(
        _prep_odd_kernel,
        grid=(d // PREP_ROWS,),
        in_specs=[pl.BlockSpec((PREP_ROWS, O_IN), lambda i: (i, 0)),
                  pl.BlockSpec((2 * LANES, 2 * LANES), lambda i: (0, 0))],
        out_specs=pl.BlockSpec((PREP_ROWS, O_IN), lambda i: (i, 0)),
        out_shape=jax.ShapeDtypeStruct((d, O_IN), BF16),
        compiler_params=_params("parallel"),
        name="prep_odd",
    )(w_in, perm)


EVEN_WIDTHS = ((SSD_WIDTH, BF16), (CONV_CH, BF16), (GLA_K, BF16), (GLA_K, BF16), (GLA_V, BF16), (GLA_V, BF16),
               (SMALL_W, F32))


def _expansion_matrices():
    rows = np.arange(SMALL_W)[:, None]
    heads = (np.arange(SSD_WIDTH) // SSD_HEAD_DIM)[None, :]
    ef = (rows == heads).astype(np.float32)
    eb = (rows - SSD_HEADS == heads).astype(np.float32)
    return jnp.asarray(ef, BF16), jnp.asarray(eb, BF16)


def _pad_lanes(v, width):
    return jnp.pad(v, ((0, 0), (0, width - v.shape[1])))


def _rope_tables(length):
    rows = length // GRID_W
    row = np.repeat(np.arange(rows, dtype=np.float64), GRID_W)
    col = np.tile(np.arange(GRID_W, dtype=np.float64), rows)
    inv = 1.0 / (ROPE_BASE ** (np.arange(ROPE_FREQS, dtype=np.float64) / ROPE_FREQS))
    ang_r = row[:, None] * inv
    ang_c = col[:, None] * inv
    cos = np.concatenate([np.cos(ang_r), np.cos(ang_c), np.cos(ang_r), np.cos(ang_c)], axis=1)
    sin = np.concatenate([-np.sin(ang_r), -np.sin(ang_c), np.sin(ang_r), np.sin(ang_c)], axis=1)
    return jnp.asarray(cos, F32), jnp.asarray(sin, F32)


def _rope_head_layout(v):
    return v[_rope_perm(ATT_HEAD_DIM)]


def _mod_rows(mod, rows, bsz, d):
    picked = jnp.broadcast_to(mod[rows], (bsz, 3 * d)) if isinstance(rows, int) else mod[rows]
    return [picked[:, None, j * d:(j + 1) * d] for j in range(3)]


PROJ_ROWS = 512
OUT_ROWS = 1024
ATT_Q_BLOCKS = 32
GLA_ROWS = 512
EVEN_TAIL_ROWS = 512
GLA_CUMSUM_ROWS = 256
SSD_FWD_ROWS = 4 * SSD_CHUNK
CVEC_ROWS = SUBLANES


def kernel(x, c, ctx, c_ctx, e_norm, e_mod_w, e_mod_b, e_w_in, e_conv_w, e_conv_b, e_dt_bias, e_a_log,
           e_d_skip, e_ssd_norm, e_gla_gate_w, e_gla_gate_b, e_gla_norm, e_w_out, o_norm, o_mod_w, o_mod_b,
           o_w_in, o_q_norm, o_k_norm, o_sink, o_w_out):
    bsz, length, d = x.shape
    n_ctx = ctx.shape[1]
    assert e_norm.shape[0] == 1 and o_norm.shape[0] == 1, "two-layer block only"
    assert length % SSD_CHUNK == 0 and n_ctx % SSD_CHUNK == 0 and length >= 3 * ATT_BLOCK
    assert bsz + 1 <= CVEC_ROWS

    cvecs = jnp.zeros((CVEC_ROWS, d), F32).at[:bsz].set(c).at[bsz].set(c_ctx)
    lat_rows = slice(0, bsz)

    mod_both = _adaln(cvecs, e_mod_w[0], e_mod_b[0], o_mod_w[0], o_mod_b[0])
    mod = mod_both[:, :3 * d]
    shift, scale, gate = _mod_rows(mod, lat_rows, bsz, d)
    c_shift, c_scale, c_gate = _mod_rows(mod, bsz, bsz, d)
    w_in = _prep_even(e_w_in[0])
    w_out = e_w_out[0].astype(BF16)
    ef, eb = _expansion_matrices()
    nh2 = 2 * SSD_HEADS
    bias_flat = e_dt_bias[0].reshape(1, nh2)
    alog_flat = e_a_log[0].reshape(1, nh2)
    bias_row, alog_row = _pad_lanes(bias_flat, SMALL_W), _pad_lanes(alog_flat, SMALL_W)
    bias_col, alog_col = bias_flat.reshape(nh2, 1), alog_flat.reshape(nh2, 1)
    dskip_row = jnp.repeat(e_d_skip[0], SSD_HEAD_DIM).reshape(1, SSD_WIDTH)
    conv_b = e_conv_b[0].reshape(1, CONV_CH)
    wg = jnp.zeros((SMALL_W, 2 * GLA_K), F32)
    wg = wg.at[nh2:nh2 + GLA_RANK, 0:GLA_K].set(e_gla_gate_w[0, 0])
    wg = wg.at[nh2 + GLA_RANK:nh2 + 2 * GLA_RANK, GLA_K:2 * GLA_K].set(e_gla_gate_w[0, 1])
    gb = e_gla_gate_b[0].reshape(1, 2 * GLA_K)

    def even_layer(stream, sc, sh, gt, ssd_init, gla_init):
        z, xbc, q, k, v, g, small = _proj_even(stream, e_norm[0], sc, sh, w_in, EVEN_WIDTHS, PROJ_ROWS)
        ya, cm, bt, wxb, p3, ssd_f = _sweep_call(
            _ssd_fwd_kernel, "ssd_fwd",
            _ssd_fwd(xbc, small, e_conv_w[0], conv_b, bias_row, bias_col, alog_row, alog_col, dskip_row,
                     ef, eb, ssd_init[0]))
        oa, qdb, kwb, decb, gla_f = _sweep_call(
            _gla_fwd_kernel, "gla_fwd", _gla_fwd(q, k, v, small, wg, gb, gla_init[0], GLA_ROWS))
        out, ssd_b, gla_b = _even_tail(stream, gt, w_out,
                                       (cm, bt, wxb, p3, ya, z, eb, e_ssd_norm[0], ssd_init[1]),
                                       (qdb, kwb, decb, v, oa, g, e_gla_norm[0], gla_init[1]))
        return out, (ssd_f, ssd_b), (gla_f, gla_b)

    ssd0 = jnp.zeros((bsz,) + SSD_STATE_SHAPE, F32)
    gla0 = jnp.zeros((bsz,) + GLA_STATE_SHAPE, F32)
    xc, ssd_fin, gla_fin = even_layer(ctx, c_scale, c_shift, c_gate, (ssd0, ssd0), (gla0, gla0))
    x, _, _ = even_layer(x, scale, shift, gate, ssd_fin, gla_fin)

    mod = mod_both[:, 3 * d:]
    shift, scale, gate = _mod_rows(mod, lat_rows, bsz, d)
    c_shift, c_scale, _ = _mod_rows(mod, bsz, bsz, d)
    w_in = _prep_odd(o_w_in[0])
    q_norm = _rope_head_layout(o_q_norm[0])
    k_norm = _rope_head_layout(o_k_norm[0])
    cos, sin = _rope_tables(length)
    no_rot = (jnp.ones((n_ctx, ATT_HEAD_DIM), F32), jnp.zeros((n_ctx, ATT_HEAD_DIM), F32))
    kc, vc = _proj_odd(xc, o_norm[0], c_scale, c_shift, w_in[:, :2 * ATT_KV_W], *no_rot,
                       q_norm, k_norm, False, PROJ_ROWS)
    k, v, q, g = _proj_odd(x, o_norm[0], scale, shift, w_in, cos, sin, q_norm, k_norm, True, PROJ_ROWS)
    o = _attention(o_sink[0].astype(F32), q, k, v, kc, vc, g)
    return _out_proj(x, gate, (o,), o_w_out[0].astype(BF16), OUT_ROWS, "out_odd")
```

```python
import functools

import jax
import jax.numpy as jnp
import numpy as np
from jax import lax
from jax.experimental import pallas as pl
from jax.experimental.pallas import tpu as pltpu

F32 = jnp.float32
BF16 = jnp.bfloat16

GRID_W = 64
SSD_HEADS = 16
SSD_HEAD_DIM = 64
SSD_WIDTH = SSD_HEADS * SSD_HEAD_DIM
SSD_GROUPS = 2
SSD_STATE = 128
SSD_CHUNK = 128
CONV_K = 5
CONV_CH = SSD_WIDTH + 2 * SSD_GROUPS * SSD_STATE
GLA_HEADS = 4
GLA_KEY_DIM = 128
GLA_VAL_DIM = 256
GLA_K = GLA_HEADS * GLA_KEY_DIM
GLA_V = GLA_HEADS * GLA_VAL_DIM
GLA_RANK = 16
GLA_GATE_NORMALIZER = 16.0
GLA_CHUNK = 64
ATT_HEADS = 16
ATT_KV_HEADS = 4
ATT_GROUP = ATT_HEADS // ATT_KV_HEADS
ATT_HEAD_DIM = 128
ATT_W = ATT_HEADS * ATT_HEAD_DIM
ATT_KV_W = ATT_KV_HEADS * ATT_HEAD_DIM
WINDOW = 128
ATT_BLOCK = 128
ROPE_BASE = 10000.0
ROPE_FREQS = ATT_HEAD_DIM // 4
NORM_EPS = 1e-6
LOG2E = 1.4426950408889634

LANES = 128
SUBLANES = 8
VMEM_LIMIT_BYTES = 56 * 1024 * 1024

SSD_GROUP_W = SSD_WIDTH // SSD_GROUPS
SSD_BC_W = SSD_GROUPS * SSD_STATE
SMALL_W = LANES
HALO = 2 * SUBLANES
HEADS_PER_DOT = 4
PROJ_ROW_PARTS = 4
PROJ_MIN_PART_ROWS = 128
CONV_ROW_STRIDE = 2 * SUBLANES + 1
CONV_OUT_ROWS = CONV_ROW_STRIDE * SUBLANES
assert SSD_CHUNK <= CONV_OUT_ROWS <= SSD_CHUNK + HALO - (CONV_K - 1) // 2


def _dot(a, b):
    return jnp.dot(a.astype(BF16), b.astype(BF16), preferred_element_type=F32)


def _dot_nt(a, b):
    return lax.dot_general(a.astype(BF16), b.astype(BF16), (((1,), (1,)), ((), ())),
                           preferred_element_type=F32)


def _dot_tn(a, b):
    return lax.dot_general(a.astype(BF16), b.astype(BF16), (((0,), (0,)), ((), ())),
                           preferred_element_type=F32)


def _split(v):
    hi = v.astype(BF16)
    lo = (v - hi.astype(F32)).astype(BF16)
    return hi, lo


def _dot_split_lhs(v, m):
    hi, lo = _split(v)
    return (jnp.dot(hi, m, preferred_element_type=F32) + jnp.dot(lo, m, preferred_element_type=F32))


def _dot_split_rhs(m, v):
    hi, lo = _split(v)
    return (jnp.dot(m, hi, preferred_element_type=F32) + jnp.dot(m, lo, preferred_element_type=F32))


def _dot3(a, b):
    ah, al = _split(a)
    bh, bl = _split(b)
    return (jnp.dot(ah, bh, preferred_element_type=F32) + jnp.dot(al, bh, preferred_element_type=F32)
            + jnp.dot(ah, bl, preferred_element_type=F32))


def _silu(v):
    h = 0.5 * v
    return h + h * jnp.tanh(h)


def _softplus(v):
    return jnp.maximum(v, 0.0) + jnp.log(1.0 + jnp.exp(-jnp.abs(v)))


def _log_sigmoid(v):
    return jnp.minimum(v, 0.0) - jnp.log(1.0 + jnp.exp(-jnp.abs(v)))


def _tri(n):
    row = lax.broadcasted_iota(jnp.int32, (n, n), 0)
    col = lax.broadcasted_iota(jnp.int32, (n, n), 1)
    return row >= col, col >= row


def _params(*sem):
    return pltpu.CompilerParams(dimension_semantics=sem, vmem_limit_bytes=VMEM_LIMIT_BYTES)


def _adaln_kernel(c_ref, we_ref, be_ref, wo_ref, bo_ref, o_ref):
    j = pl.program_id(0)
    half = pl.num_programs(0) // 2
    act = _silu(c_ref[...])

    @pl.when(j < half)
    def _():
        o_ref[...] = _dot3(act, we_ref[...]) + be_ref[...]

    @pl.when(j >= half)
    def _():
        o_ref[...] = _dot3(act, wo_ref[...]) + bo_ref[...]


def _adaln(cvecs, w_even, b_even, w_odd, b_odd):
    rows, d = cvecs.shape
    n = w_even.shape[1]
    tn = 1024
    nt = n // tn
    even = lambda j: (0, jnp.minimum(j, nt - 1))
    odd = lambda j: (0, jnp.maximum(j - nt, 0))
    return pl.pallas_call(
        _adaln_kernel,
        grid=(2 * nt,),
        in_specs=[pl.BlockSpec((rows, d), lambda j: (0, 0)),
                  pl.BlockSpec((d, tn), even), pl.BlockSpec((1, tn), even),
                  pl.BlockSpec((d, tn), odd), pl.BlockSpec((1, tn), odd)],
        out_specs=pl.BlockSpec((rows, tn), lambda j: (0, j)),
        out_shape=jax.ShapeDtypeStruct((rows, 2 * n), F32),
        compiler_params=_params("arbitrary"),
        name="adaln",
    )(cvecs, w_even, b_even.reshape(1, n), w_odd, b_odd.reshape(1, n))


def _modulated_norm(x, g, sc, sh):
    r = lax.rsqrt(jnp.mean(x * x, axis=-1, keepdims=True) + NORM_EPS)
    return ((x * r) * g) * (1.0 + sc) + sh


def _store_cols(h, w_ref, off, ref, rs=slice(None), act=None):
    n = ref.shape[-1]
    for c0 in range(0, n, 512):
        c1 = min(n, c0 + 512)
        t = jnp.dot(h, w_ref[:, off + c0:off + c1], preferred_element_type=F32)
        ref[rs, c0:c1] = (t if act is None else act(t)).astype(ref.dtype)
    return off + n


def _row_parts(rows):
    step = min(rows, max(rows // PROJ_ROW_PARTS, PROJ_MIN_PART_ROWS))
    return [slice(r0, r0 + step) for r0 in range(0, rows, step)]


def _proj_even_kernel(x_ref, g_ref, sc_ref, sh_ref, w_ref, *out_refs):
    for rs in _row_parts(x_ref.shape[0]):
        h = _modulated_norm(x_ref[rs, :], g_ref[...], sc_ref[...], sh_ref[...]).astype(BF16)
        off = 0
        for ref in out_refs:
            off = _store_cols(h, w_ref, off, ref, rs)


def _head_norm(t, gain):
    r = lax.rsqrt(jnp.mean(t * t, axis=-1, keepdims=True) + NORM_EPS)
    return (t * r) * gain


def _rope(t, cos, sin_signed):
    return t * cos + pltpu.roll(t, ATT_HEAD_DIM // 2, 1) * sin_signed


def _proj_odd_kernel(x_ref, g_ref, sc_ref, sh_ref, w_ref, cos_ref, sin_ref, qn_ref, kn_ref,
                     k_ref, v_ref, *qg_refs):
    h = _modulated_norm(x_ref[...], g_ref[...], sc_ref[...], sh_ref[...]).astype(BF16)
    cos = cos_ref[...]
    sin = sin_ref[...]
    scale = ATT_HEAD_DIM ** -0.5 * LOG2E

    width = HEADS_PER_DOT * ATT_HEAD_DIM
    dh = ATT_HEAD_DIM
    pr = lax.broadcasted_iota(jnp.int32, (2 * dh, 2 * dh), 0) // dh
    pc = lax.broadcasted_iota(jnp.int32, (2 * dh, 2 * dh), 1) // dh
    head_ones = jnp.where(pr == pc, 1.0, 0.0).astype(BF16)

    def project(col):
        return jnp.dot(h, w_ref[:, col:col + width], preferred_element_type=F32)

    def finish_heads(t4, ref, j0, gain, out_scale):
        for j in range(0, HEADS_PER_DOT, 2):
            t2 = t4[:, j * dh:(j + 2) * dh]
            mean_sq = jnp.dot((t2 * t2).astype(BF16), head_ones, preferred_element_type=F32) * (1.0 / dh)
            t2 = t2 * lax.rsqrt(mean_sq + NORM_EPS)
            for jj in range(2):
                t = _rope(t2[:, jj * dh:(jj + 1) * dh] * gain, cos, sin)
                c0 = (j0 + j + jj) * dh
                ref[:, c0:c0 + dh] = (t if out_scale is None else t * out_scale).astype(ref.dtype)

    work = []
    if qg_refs:
        q_ref, gate_ref = qg_refs
        for j0 in range(0, ATT_HEADS, HEADS_PER_DOT):
            c0 = j0 * dh

            def finish_gate(t4, c0=c0):
                gate_ref[:, c0:c0 + width] = _silu(t4)

            work.append((2 * ATT_KV_W + ATT_W + c0, finish_gate))
            work.append((2 * ATT_KV_W + c0,
                         functools.partial(finish_heads, ref=q_ref, j0=j0, gain=qn_ref[...], out_scale=scale)))
    for j0 in range(0, ATT_KV_HEADS, HEADS_PER_DOT):
        work.append((j0 * dh, functools.partial(finish_heads, ref=k_ref, j0=j0, gain=kn_ref[...], out_scale=None)))
    for col, finish in work:
        finish(project(col))
    _store_cols(h, w_ref, ATT_KV_W, v_ref)


def _row_tile(length, want):
    return min(length, want)


def _proj_even(x, norm_g, scale, shift, w_bf16, widths, tm):
    bsz, length, d = x.shape
    tm = _row_tile(length, tm)
    n = w_bf16.shape[1]
    row = lambda b, i: (b, i, 0)
    mod = lambda b, i: (b, 0, 0)
    return pl.pallas_call(
        _proj_even_kernel,
        grid=(bsz, length // tm),
        in_specs=[pl.BlockSpec((None, tm, d), row),
                  pl.BlockSpec((1, d), lambda b, i: (0, 0)),
                  pl.BlockSpec((None, 1, d), mod),
                  pl.BlockSpec((None, 1, d), mod),
                  pl.BlockSpec((d, n), lambda b, i: (0, 0), pipeline_mode=pl.Buffered(1))],
        out_specs=[pl.BlockSpec((None, tm, wd), row) for wd, _ in widths],
        out_shape=[jax.ShapeDtypeStruct((bsz, length, wd), dt) for wd, dt in widths],
        compiler_params=_params("parallel", "parallel"),
        name="proj_even",
    )(x, norm_g.reshape(1, d), scale, shift, w_bf16)


def _proj_odd(x, norm_g, scale, shift, w_bf16, cos, sin, q_norm, k_norm, with_queries, tm):
    bsz, length, d = x.shape
    tm = _row_tile(length, tm)
    n = w_bf16.shape[1]
    row = lambda b, i: (b, i, 0)
    mod = lambda b, i: (b, 0, 0)
    const = lambda b, i: (0, 0)
    widths = [(ATT_KV_W, BF16), (ATT_KV_W, BF16)]
    if with_queries:
        widths += [(ATT_W, BF16), (ATT_W, F32)]
    return pl.pallas_call(
        _proj_odd_kernel,
        grid=(bsz, length // tm),
        in_specs=[pl.BlockSpec((None, tm, d), row),
                  pl.BlockSpec((1, d), const),
                  pl.BlockSpec((None, 1, d), mod),
                  pl.BlockSpec((None, 1, d), mod),
                  pl.BlockSpec((d, n), const, pipeline_mode=pl.Buffered(1)),
                  pl.BlockSpec((tm, ATT_HEAD_DIM), lambda b, i: (i, 0)),
                  pl.BlockSpec((tm, ATT_HEAD_DIM), lambda b, i: (i, 0)),
                  pl.BlockSpec((1, ATT_HEAD_DIM), const),
                  pl.BlockSpec((1, ATT_HEAD_DIM), const)],
        out_specs=[pl.BlockSpec((None, tm, wd), row) for wd, _ in widths],
        out_shape=[jax.ShapeDtypeStruct((bsz, length, wd), dt) for wd, dt in widths],
        compiler_params=_params("parallel", "parallel"),
        name="proj_odd_q" if with_queries else "proj_odd_kv",
    )(x, norm_g.reshape(1, d), scale, shift, w_bf16, cos, sin,
      q_norm.reshape(1, ATT_HEAD_DIM), k_norm.reshape(1, ATT_HEAD_DIM))


def _conv_silu(win_s, prev_ref, cur_ref, next_ref, has_prev, has_next, cw_ref, cb_ref, x_s):
    rows = cur_ref.shape[0]
    half = (CONV_K - 1) // 2
    for j in range(CONV_CH // LANES):
        ls = slice(j * LANES, (j + 1) * LANES)
        win_s[j, 0:HALO, :] = jnp.where(has_prev, prev_ref[:, ls].astype(F32), 0.0)
        win_s[j, HALO:HALO + rows, :] = cur_ref[:, ls].astype(F32)
        win_s[j, HALO + rows:2 * HALO + rows, :] = jnp.where(has_next, next_ref[:, ls].astype(F32), 0.0)
        taps = [cw_ref[k:k + 1, ls] for k in range(CONV_K)]
        bias = cb_ref[:, ls]
        win = win_s.at[j]
        out = x_s.at[j]
        for c in range(rows // SSD_CHUNK):
            for a in range(CONV_ROW_STRIDE):
                r0 = HALO + c * SSD_CHUNK - half + a
                acc = bias + taps[0] * win[pl.ds(r0, SUBLANES, stride=CONV_ROW_STRIDE), :]
                for k in range(1, CONV_K):
                    acc = acc + taps[k] * win[pl.ds(r0 + k, SUBLANES, stride=CONV_ROW_STRIDE), :]
                out[pl.ds(c * CONV_OUT_ROWS + a, SUBLANES, stride=CONV_ROW_STRIDE), :] = _silu(acc)


def _ssd_fwd_kernel(p_ref, c_ref, n_ref, sm_ref, cw_ref, cb_ref, bias_row_ref, bias_col_ref,
                    alog_row_ref, alog_col_ref, dskip_ref, ef_ref, eb_ref, sf0_ref,
                    ya_ref, cm_ref, bt_ref, wxb_ref, p3_ref, sff_ref,
                    sf_s, win_s, x_s):
    i = pl.program_id(1)
    nc = pl.num_programs(1)
    t = SSD_CHUNK
    nh = SSD_HEADS
    slabs_x = SSD_WIDTH // LANES
    slab_b = slabs_x
    slab_c = slabs_x + SSD_GROUPS

    @pl.when(i == 0)
    def _():
        sf_s[...] = sf0_ref[...]

    lower, upper = _tri(t)
    ltri = jnp.where(lower, 1.0, 0.0).astype(BF16)
    utri = jnp.where(upper, 1.0, 0.0).astype(BF16)
    a_row = -jnp.exp(alog_row_ref[...])
    a_col = -jnp.exp(alog_col_ref[...])
    ef = ef_ref[...]
    eb = eb_ref[...]
    neg_inf = float("-inf")

    _conv_silu(win_s, p_ref, c_ref, n_ref, i > 0, i < nc - 1, cw_ref, cb_ref, x_s)
    n_chunks = sm_ref.shape[0] // t
    lane = lax.broadcasted_iota(jnp.int32, (t, LANES), 1)
    left = lane < SSD_HEAD_DIM
    heads_per_group = nh // SSD_GROUPS
    pairs_per_group = heads_per_group // 2

    def chunk_decays(c):
        rs = slice(c * t, (c + 1) * t)
        sm = sm_ref[rs, :]
        dt = _softplus(sm + bias_row_ref[...])
        dta = dt * a_row
        cs = _dot_split_rhs(ltri, dta)
        rc = _dot_split_rhs(utri, dta)
        dt_t = _softplus(sm.T[0:2 * nh, :] + bias_col_ref[...])
        dta_t = dt_t * a_col
        log2_dt_t = jnp.log2(dt_t)
        f_q = cs * LOG2E
        g_q = rc * LOG2E
        f_k = _dot_split_lhs(dta_t, utri) * LOG2E - log2_dt_t
        g_k = _dot_split_lhs(dta_t, ltri) * LOG2E - log2_dt_t
        dec_f = _dot_split_lhs(jnp.exp2(f_q), ef)
        wgt_f = _dot_split_lhs(dt * jnp.exp(cs[t - 1:t, :] - cs), ef)
        wgt_b = _dot_split_lhs(dt * jnp.exp(rc[0:1, :] - rc), eb)
        p3_ref[rs, :] = jnp.exp2(g_q)
        for p in range(slabs_x):
            ls = slice(p * LANES, (p + 1) * LANES)
            wxb_ref[rs, ls] = (x_s[p, c * CONV_OUT_ROWS:c * CONV_OUT_ROWS + t, :] * wgt_b[:, ls]).astype(BF16)
        xr = slice(c * CONV_OUT_ROWS, c * CONV_OUT_ROWS + t)
        cbs = [_dot_nt(x_s[slab_c + g, xr, :], x_s[slab_b + g, xr, :]) for g in range(SSD_GROUPS)]
        bts = [x_s[slab_b + g, xr, :].T.astype(BF16) for g in range(SSD_GROUPS)]
        for g in range(SSD_GROUPS):
            bt_ref[c, g * SSD_STATE:(g + 1) * SSD_STATE, :] = bts[g]
        return f_q, g_q, f_k, g_k, dec_f, wgt_f, cbs, bts

    decays = [chunk_decays(c) for c in range(n_chunks)]

    for c in range(n_chunks):
        f_q, g_q, f_k, g_k, dec_f, wgt_f, cbs, bts = decays[c]
        rs = slice(c * t, (c + 1) * t)
        xr = slice(c * CONV_OUT_ROWS, c * CONV_OUT_ROWS + t)

        def decay_matrix(h, cbg):
            lf = jnp.exp2(jnp.where(lower, f_q[:, h:h + 1] - f_k[h:h + 1, :], neg_inf))
            ub = jnp.exp2(jnp.where(upper, g_q[:, nh + h:nh + h + 1] - g_k[nh + h:nh + h + 1, :], neg_inf))
            return cbg * (lf + ub)

        for g in range(SSD_GROUPS):
            cg = x_s[slab_c + g, xr, :]
            gs = slice(g * SSD_GROUP_W, (g + 1) * SSD_GROUP_W)
            cm_ref[rs, g * SSD_STATE:(g + 1) * SSD_STATE] = cg.astype(BF16)
            cbg = cbs[g]
            state = sf_s[g]
            y_off = _dot(cg, state) * dec_f[:, gs]
            wx = []
            for pair in range(pairs_per_group):
                h0 = g * heads_per_group + 2 * pair
                p = h0 // 2
                ls = slice(p * LANES, (p + 1) * LANES)
                xp = x_s[p, xr, :]
                xb = xp.astype(BF16)
                y = jnp.where(left, _dot(decay_matrix(h0, cbg), xb), _dot(decay_matrix(h0 + 1, cbg), xb))
                ya_ref[rs, ls] = y + y_off[:, pair * LANES:(pair + 1) * LANES] + dskip_ref[:, ls] * xp
                wx.append((xp * wgt_f[:, ls]).astype(BF16))
            sf_s[g] = dec_f[t - 1:t, gs] * state + jnp.dot(bts[g], jnp.concatenate(wx, axis=1),
                                                           preferred_element_type=F32)

    @pl.when(i == nc - 1)
    def _():
        sff_ref[...] = sf_s[...]


def _ssd_bwd_kernel(cm_ref, bt_ref, wxb_ref, p3_ref, ya_ref, z_ref, eb_ref, sn_ref, sb0_ref,
                    ys_ref, sbf_ref, sb_s):
    i = pl.program_id(1)
    nc = pl.num_programs(1)

    @pl.when(i == 0)
    def _():
        sb_s[...] = sb0_ref[...]

    t = SSD_CHUNK
    n_sub = bt_ref.shape[0]
    dec_b = _dot_split_lhs(p3_ref[...], eb_ref[...])
    for g in range(SSD_GROUPS):
        gs = slice(g * SSD_GROUP_W, (g + 1) * SSD_GROUP_W)
        ss = slice(g * SSD_STATE, (g + 1) * SSD_STATE)
        chunks = [slice(s * t, (s + 1) * t) for s in range(n_sub)]
        incs = [jnp.dot(bt_ref[s, ss, :], wxb_ref[rs, gs], preferred_element_type=F32)
                for s, rs in enumerate(chunks)]
        states = [None] * n_sub
        state = sb_s[g]
        for s in reversed(range(n_sub)):
            states[s] = state
            state = dec_b[s * t:s * t + 1, gs] * state + incs[s]
        sb_s[g] = state
        for s, rs in enumerate(chunks):
            y_off = jnp.dot(cm_ref[rs, ss], states[s].astype(BF16), preferred_element_type=F32) * dec_b[rs, gs]
            y = (ya_ref[rs, gs] + y_off) * _silu(z_ref[rs, gs].astype(F32))
            ys_ref[rs, gs] = _head_norm(y, sn_ref[:, gs]).astype(ys_ref.dtype)

    @pl.when(i == nc - 1)
    def _():
        sbf_ref[...] = sb_s[...]


SSD_STATE_SHAPE = (SSD_GROUPS, SSD_STATE, SSD_GROUP_W)


def _ssd_fwd(xbc, small, conv_w, conv_b, bias_row, bias_col, alog_row, alog_col, dskip_row, ef, eb, sf0):
    bsz, length, _ = xbc.shape
    t = SSD_CHUNK
    rows = _row_tile(length, SSD_FWD_ROWS)
    nc = length // t
    per = rows // HALO
    last_halo = length // HALO - 1

    def cur(b, i): return (b, i, 0)
    def prev(b, i): return (b, jnp.maximum(i * per - 1, 0), 0)
    def nxt(b, i): return (b, jnp.minimum((i + 1) * per, last_halo), 0)
    const2 = lambda b, i: (0, 0)
    state = lambda b, i: (b, 0, 0, 0)

    halo_spec = lambda f: pl.BlockSpec((None, HALO, CONV_CH), f)
    chunk_spec = lambda w: pl.BlockSpec((None, rows, w), cur)
    state_spec = pl.BlockSpec((None,) + SSD_STATE_SHAPE, state)
    return dict(
        grid=(bsz, length // rows),
        in_specs=[halo_spec(prev), chunk_spec(CONV_CH), halo_spec(nxt), chunk_spec(SMALL_W),
                  pl.BlockSpec((CONV_K, CONV_CH), const2),
                  pl.BlockSpec((1, CONV_CH), const2),
                  pl.BlockSpec((1, SMALL_W), const2),
                  pl.BlockSpec((2 * SSD_HEADS, 1), const2),
                  pl.BlockSpec((1, SMALL_W), const2),
                  pl.BlockSpec((2 * SSD_HEADS, 1), const2),
                  pl.BlockSpec((1, SSD_WIDTH), const2),
                  pl.BlockSpec((SMALL_W, SSD_WIDTH), const2),
                  pl.BlockSpec((SMALL_W, SSD_WIDTH), const2),
                  state_spec],
        out_specs=[chunk_spec(SSD_WIDTH), chunk_spec(SSD_BC_W),
                   pl.BlockSpec((None, rows // t, SSD_BC_W, t), lambda b, i: (b, i, 0, 0)),
                   chunk_spec(SSD_WIDTH), chunk_spec(SMALL_W), state_spec],
        out_shape=[jax.ShapeDtypeStruct((bsz, length, SSD_WIDTH), F32),
                   jax.ShapeDtypeStruct((bsz, length, SSD_BC_W), BF16),
                   jax.ShapeDtypeStruct((bsz, nc, SSD_BC_W, t), BF16),
                   jax.ShapeDtypeStruct((bsz, length, SSD_WIDTH), BF16),
                   jax.ShapeDtypeStruct((bsz, length, SMALL_W), F32),
                   jax.ShapeDtypeStruct((bsz,) + SSD_STATE_SHAPE, F32)],
        scratch_shapes=[pltpu.VMEM(SSD_STATE_SHAPE, F32),
                        pltpu.VMEM((CONV_CH // LANES, rows + 2 * HALO, LANES), F32),
                        pltpu.VMEM((CONV_CH // LANES, (rows // t) * CONV_OUT_ROWS, LANES), F32)],
        args=(xbc, xbc, xbc, small, conv_w, conv_b, bias_row, bias_col, alog_row, alog_col, dskip_row, ef, eb, sf0))


def _ssd_bwd(cm, bt, wxb, p3, ya, z, eb, ssd_norm, sb0, rows):
    bsz, length, _ = ya.shape
    t = SSD_CHUNK
    n_steps = length // rows
    rev = lambda b, i: (b, n_steps - 1 - i, 0)
    const2 = lambda b, i: (0, 0)
    state = lambda b, i: (b, 0, 0, 0)
    chunk_spec = lambda w: pl.BlockSpec((None, rows, w), rev)
    state_spec = pl.BlockSpec((None,) + SSD_STATE_SHAPE, state)
    return dict(
        in_specs=[chunk_spec(SSD_BC_W),
                  pl.BlockSpec((None, rows // t, SSD_BC_W, t), lambda b, i: (b, n_steps - 1 - i, 0, 0)),
                  chunk_spec(SSD_WIDTH), chunk_spec(SMALL_W), chunk_spec(SSD_WIDTH), chunk_spec(SSD_WIDTH),
                  pl.BlockSpec((SMALL_W, SSD_WIDTH), const2),
                  pl.BlockSpec((1, SSD_WIDTH), const2),
                  state_spec],
        state_spec=state_spec,
        state_shape=jax.ShapeDtypeStruct((bsz,) + SSD_STATE_SHAPE, F32),
        state_scratch=pltpu.VMEM(SSD_STATE_SHAPE, F32),
        args=(cm, bt, wxb, p3, ya, z, eb, ssd_norm.reshape(1, SSD_WIDTH), sb0))


def _gla_fwd_kernel(q_ref, k_ref, v_ref, sm_ref, wg_ref, gb_ref, sf0_ref,
                    oa_ref, qdb_ref, kwb_ref, decb_ref, sff_ref, sf_s):
    i = pl.program_id(1)
    n_steps = pl.num_programs(1)
    t = GLA_CHUNK
    rows = q_ref.shape[0]
    n_sub = rows // t
    dk, dv = GLA_KEY_DIM, GLA_VAL_DIM
    qscale = dk ** -0.5
    inv_norm = 1.0 / GLA_GATE_NORMALIZER

    @pl.when(i == 0)
    def _():
        sf_s[...] = sf0_ref[...]

    lower, upper = _tri(t)
    grp = min(rows, GLA_CUMSUM_ROWS)
    row = lax.broadcasted_iota(jnp.int32, (grp, grp), 0)
    col = lax.broadcasted_iota(jnp.int32, (grp, grp), 1)
    diff = row - col
    pos = row % t
    bd_lower = jnp.where(diff >= 0, jnp.where(diff <= pos, 1.0, 0.0), 0.0).astype(BF16)
    bd_upper = jnp.where(diff <= 0, jnp.where(-diff <= t - 1 - pos, 1.0, 0.0), 0.0).astype(BF16)

    hi, lo = _split(sm_ref[...])
    wg_hi, wg_lo = _split(wg_ref[...])
    logits = (jnp.dot(hi, wg_hi, preferred_element_type=F32) + jnp.dot(lo, wg_hi, preferred_element_type=F32)
              + jnp.dot(hi, wg_lo, preferred_element_type=F32)) + gb_ref[...]
    lg = _log_sigmoid(logits) * inv_norm
    groups = [slice(r0, r0 + grp) for r0 in range(0, rows, grp)]
    cs = jnp.concatenate([_dot_split_rhs(bd_lower, lg[gr, 0:GLA_K]) for gr in groups], axis=0)
    rc = jnp.concatenate([_dot_split_rhs(bd_upper, lg[gr, GLA_K:2 * GLA_K]) for gr in groups], axis=0)

    def per_chunk_row(v, offset):
        return jnp.concatenate([jnp.broadcast_to(v[s * t + offset:s * t + offset + 1, :], (t, v.shape[1]))
                                for s in range(n_sub)], axis=0)

    for h in range(GLA_HEADS):
        ks = slice(h * dk, (h + 1) * dk)
        vs = slice(h * dv, (h + 1) * dv)
        c = cs[:, ks]
        r = rc[:, ks]
        qh = q_ref[:, ks].astype(F32) * qscale
        kh = k_ref[:, ks].astype(F32)
        qdf = (qh * jnp.exp(c)).astype(BF16)
        kif = (kh * jnp.exp(-c)).astype(BF16)
        qdb = (qh * jnp.exp(r)).astype(BF16)
        kib = (kh * jnp.exp(-r)).astype(BF16)
        kwf = (kh * jnp.exp(per_chunk_row(c, t - 1) - c)).astype(BF16)
        qdb_ref[:, ks] = qdb
        kwb_ref[:, ks] = (kh * jnp.exp(per_chunk_row(r, 0) - r)).astype(BF16)
        chunks = [slice(s * t, (s + 1) * t) for s in range(n_sub)]
        vhs = [v_ref[rs, vs].astype(BF16) for rs in chunks]
        atts = [(jnp.where(lower, _dot_nt(qdf[rs], kif[rs]), 0.0)
                 + jnp.where(upper, _dot_nt(qdb[rs], kib[rs]), 0.0)).astype(BF16) for rs in chunks]
        incs = [_dot_tn(vh, kwf[rs]) for vh, rs in zip(vhs, chunks)]
        states = [sf_s[h]]
        for s in range(n_sub):
            decb_ref[s, :, ks] = jnp.exp(r[s * t:s * t + 1, :])
            states.append(states[s] * jnp.exp(c[s * t + t - 1:s * t + t, :]) + incs[s])
        sf_s[h] = states[n_sub]
        for s, rs in enumerate(chunks):
            oa_ref[rs, vs] = (jnp.dot(atts[s], vhs[s], preferred_element_type=F32)
                              + _dot_nt(qdf[rs], states[s]))

    @pl.when(i == n_steps - 1)
    def _():
        sff_ref[...] = sf_s[...]


def _gla_bwd_kernel(qdb_ref, kwb_ref, decb_ref, v_ref, oa_ref, g_ref, gn_ref, sb0_ref,
                    os_ref, sbf_ref, sb_s):
    i = pl.program_id(1)
    n_steps = pl.num_programs(1)
    t = GLA_CHUNK
    n_sub = qdb_ref.shape[0] // t
    dk, dv = GLA_KEY_DIM, GLA_VAL_DIM

    @pl.when(i == 0)
    def _():
        sb_s[...] = sb0_ref[...]

    for h in range(GLA_HEADS):
        ks = slice(h * dk, (h + 1) * dk)
        vs = slice(h * dv, (h + 1) * dv)
        state = sb_s[h]
        for s in reversed(range(n_sub)):
            rs = slice(s * t, (s + 1) * t)
            o = oa_ref[rs, vs] + _dot_nt(qdb_ref[rs, ks], state)
            state = state * decb_ref[s, :, ks] + _dot_tn(v_ref[rs, vs], kwb_ref[rs, ks])
            os_ref[rs, vs] = (_head_norm(o, gn_ref[:, vs]) * _silu(g_ref[rs, vs].astype(F32))).astype(os_ref.dtype)
        sb_s[h] = state

    @pl.when(i == n_steps - 1)
    def _():
        sbf_ref[...] = sb_s[...]


GLA_STATE_SHAPE = (GLA_HEADS, GLA_VAL_DIM, GLA_KEY_DIM)


def _gla_fwd(q, k, v, small, wg, gb, sf0, rows):
    bsz, length, _ = q.shape
    rows = _row_tile(length, rows)
    n_steps = length // rows
    n_sub = rows // GLA_CHUNK
    fwd = lambda b, i: (b, i, 0)
    const2 = lambda b, i: (0, 0)
    state = lambda b, i: (b, 0, 0, 0)
    blk = lambda w: pl.BlockSpec((None, rows, w), fwd)
    state_spec = pl.BlockSpec((None,) + GLA_STATE_SHAPE, state)
    return dict(
        grid=(bsz, n_steps),
        in_specs=[blk(GLA_K), blk(GLA_K), blk(GLA_V), blk(SMALL_W),
                  pl.BlockSpec((SMALL_W, 2 * GLA_K), const2),
                  pl.BlockSpec((1, 2 * GLA_K), const2),
                  state_spec],
        out_specs=[blk(GLA_V), blk(GLA_K), blk(GLA_K),
                   pl.BlockSpec((None, n_sub, 1, GLA_K), lambda b, i: (b, i, 0, 0)),
                   state_spec],
        out_shape=[jax.ShapeDtypeStruct((bsz, length, GLA_V), F32),
                   jax.ShapeDtypeStruct((bsz, length, GLA_K), BF16),
                   jax.ShapeDtypeStruct((bsz, length, GLA_K), BF16),
                   jax.ShapeDtypeStruct((bsz, length // GLA_CHUNK, 1, GLA_K), F32),
                   jax.ShapeDtypeStruct((bsz,) + GLA_STATE_SHAPE, F32)],
        scratch_shapes=[pltpu.VMEM(GLA_STATE_SHAPE, F32)],
        args=(q, k, v, small, wg, gb, sf0))


def _sweep_call(body, name, call):
    return pl.pallas_call(
        body,
        grid=call["grid"],
        in_specs=call["in_specs"],
        out_specs=call["out_specs"],
        out_shape=call["out_shape"],
        scratch_shapes=call["scratch_shapes"],
        compiler_params=_params("arbitrary", "arbitrary"),
        name=name,
    )(*call["args"])


def _gla_bwd(qdb, kwb, decb, v, oa, g, gla_norm, sb0, rows):
    bsz, length, _ = oa.shape
    n_steps = length // rows
    n_sub = rows // GLA_CHUNK
    rev = lambda b, i: (b, n_steps - 1 - i, 0)
    const2 = lambda b, i: (0, 0)
    state = lambda b, i: (b, 0, 0, 0)
    blk = lambda w: pl.BlockSpec((None, rows, w), rev)
    state_spec = pl.BlockSpec((None,) + GLA_STATE_SHAPE, state)
    return dict(
        in_specs=[blk(GLA_K), blk(GLA_K),
                  pl.BlockSpec((None, n_sub, 1, GLA_K), lambda b, i: (b, n_steps - 1 - i, 0, 0)),
                  blk(GLA_V), blk(GLA_V), blk(GLA_V),
                  pl.BlockSpec((1, GLA_V), const2),
                  state_spec],
        state_spec=state_spec,
        state_shape=jax.ShapeDtypeStruct((bsz,) + GLA_STATE_SHAPE, F32),
        state_scratch=pltpu.VMEM(GLA_STATE_SHAPE, F32),
        args=(qdb, kwb, decb, v, oa, g, gla_norm.reshape(1, GLA_V), sb0))


SSD_BWD_INPUTS = 9
GLA_BWD_INPUTS = 8


def _even_tail_kernel(*refs):
    ssd_in = refs[:SSD_BWD_INPUTS]
    gla_in = refs[SSD_BWD_INPUTS:SSD_BWD_INPUTS + GLA_BWD_INPUTS]
    x_ref, gate_ref, w_ref, y_ref, ssd_fin_ref, gla_fin_ref, ssd_s, gla_s, mix_s = \
        refs[SSD_BWD_INPUTS + GLA_BWD_INPUTS:]
    _ssd_bwd_kernel(*ssd_in, mix_s.at[:, 0:SSD_WIDTH], ssd_fin_ref, ssd_s)
    _gla_bwd_kernel(*gla_in, mix_s.at[:, SSD_WIDTH:SSD_WIDTH + GLA_V], gla_fin_ref, gla_s)
    y_ref[...] = x_ref[...] + gate_ref[...] * jnp.dot(mix_s[...], w_ref[...], preferred_element_type=F32)


def _even_tail(x, gate, w_bf16, ssd_args, gla_args):
    bsz, length, d = x.shape
    rows = _row_tile(length, EVEN_TAIL_ROWS)
    n_steps = length // rows
    ssd = _ssd_bwd(*ssd_args, rows)
    gla = _gla_bwd(*gla_args, rows)
    assert len(ssd["in_specs"]) == SSD_BWD_INPUTS and len(gla["in_specs"]) == GLA_BWD_INPUTS
    rev = lambda b, i: (b, n_steps - 1 - i, 0)
    return pl.pallas_call(
        _even_tail_kernel,
        grid=(bsz, n_steps),
        in_specs=(ssd["in_specs"] + gla["in_specs"]
                  + [pl.BlockSpec((None, rows, d), rev),
                     pl.BlockSpec((None, 1, d), lambda b, i: (b, 0, 0)),
                     pl.BlockSpec(w_bf16.shape, lambda b, i: (0, 0), pipeline_mode=pl.Buffered(1))]),
        out_specs=[pl.BlockSpec((None, rows, d), rev), ssd["state_spec"], gla["state_spec"]],
        out_shape=[jax.ShapeDtypeStruct((bsz, length, d), F32), ssd["state_shape"], gla["state_shape"]],
        scratch_shapes=[ssd["state_scratch"], gla["state_scratch"],
                        pltpu.VMEM((rows, SSD_WIDTH + GLA_V), BF16)],
        compiler_params=_params("arbitrary", "arbitrary"),
        name="even_tail",
    )(*ssd["args"], *gla["args"], x, gate, w_bf16)


def _out_proj_kernel(x_ref, gate_ref, *refs):
    *in_refs, w_ref, y_ref = refs
    acc = None
    off = 0
    for ref in in_refs:
        n = ref.shape[-1]
        part = jnp.dot(ref[...], w_ref[off:off + n, :], preferred_element_type=F32)
        acc = part if acc is None else acc + part
        off += n
    y_ref[...] = x_ref[...] + gate_ref[...] * acc


def _out_proj(x, gate, mixed, w_bf16, tm, name):
    bsz, length, d = x.shape
    tm = _row_tile(length, tm)
    row = lambda b, i: (b, i, 0)
    return pl.pallas_call(
        _out_proj_kernel,
        grid=(bsz, length // tm),
        in_specs=([pl.BlockSpec((None, tm, d), row),
                   pl.BlockSpec((None, 1, d), lambda b, i: (b, 0, 0))]
                  + [pl.BlockSpec((None, tm, m.shape[-1]), row) for m in mixed]
                  + [pl.BlockSpec(w_bf16.shape, lambda b, i: (0, 0), pipeline_mode=pl.Buffered(1))]),
        out_specs=pl.BlockSpec((None, tm, d), row),
        out_shape=jax.ShapeDtypeStruct((bsz, length, d), F32),
        compiler_params=_params("parallel", "parallel"),
        name=name,
    )(x, gate, *mixed, w_bf16)


def _attn_kernel(sink_ref, q_ref, k_ref, v_ref, kc_ref, vc_ref, g_ref, o_ref):
    j = pl.program_id(1)
    i = pl.program_id(2)
    length = k_ref.shape[0]
    blk = ATT_BLOCK
    band = 3 * blk
    dh = ATT_HEAD_DIM
    n_q = q_ref.shape[0] // blk
    n_ctx = kc_ref.shape[0]
    row_minus_col = (lax.broadcasted_iota(jnp.int32, (blk, band), 0)
                     - lax.broadcasted_iota(jnp.int32, (blk, band), 1))
    sink = jnp.concatenate([jnp.full((blk, LANES), sink_ref[j * ATT_GROUP + g] * LOG2E, F32)
                            for g in range(ATT_GROUP)], axis=0)
    kc = kc_ref[...]
    vc_ext = jnp.concatenate([vc_ref[...], jnp.ones((n_ctx, dh), BF16)], axis=1)
    ones_band = jnp.ones((band, dh), BF16)

    def window_start(qb):
        blk_idx = i * n_q + qb
        return blk_idx, pl.multiple_of(jnp.clip((blk_idx - 1) * blk, 0, length - band), blk)

    def scores(qb):
        _, start = window_start(qb)
        rs = slice(qb * blk, (qb + 1) * blk)
        q = jnp.concatenate([q_ref[rs, g * dh:(g + 1) * dh] for g in range(ATT_GROUP)], axis=0)
        return _dot_nt(q, k_ref[pl.ds(start, band), :]), _dot_nt(q, kc)

    pending = scores(0)
    for qb in range(n_q):
        s_band, s_ctx = pending
        if qb + 1 < n_q:
            pending = scores(qb + 1)
        blk_idx, start = window_start(qb)
        inside = jnp.abs(row_minus_col + (blk_idx * blk - start)) <= WINDOW
        v_all = jnp.concatenate([jnp.concatenate([v_ref[pl.ds(start, band), :], ones_band], axis=1),
                                 vc_ext], axis=0)
        rs = slice(qb * blk, (qb + 1) * blk)
        cols = []
        for c in range(band // LANES):
            ls = slice(c * LANES, (c + 1) * LANES)
            cols.append(jnp.concatenate(
                [jnp.where(inside[:, ls], s_band[g * blk:(g + 1) * blk, ls], float("-inf"))
                 for g in range(ATT_GROUP)], axis=0))
        for c in range(n_ctx // LANES):
            cols.append(s_ctx[:, c * LANES:(c + 1) * LANES])
        m = jnp.maximum(sink, jnp.max(functools.reduce(jnp.maximum, cols), axis=-1, keepdims=True))
        p = jnp.concatenate([jnp.exp2(col - m) for col in cols], axis=1).astype(BF16)
        acc = jnp.dot(p, v_all, preferred_element_type=F32)
        out = acc[:, 0:dh] / (acc[:, dh:2 * dh] + jnp.exp2(sink - m))
        for g in range(ATT_GROUP):
            cs = slice(g * dh, (g + 1) * dh)
            o_ref[rs, cs] = (out[g * blk:(g + 1) * blk, :] * g_ref[rs, cs]).astype(o_ref.dtype)


def _attention(sink, q, k, v, kc, vc, gate):
    bsz, length, _ = q.shape
    n_ctx = kc.shape[1]
    rows = _row_tile(length, ATT_Q_BLOCKS * ATT_BLOCK)
    gw = ATT_GROUP * ATT_HEAD_DIM
    qmap = lambda b, j, i, s: (b, i, j)
    kvmap = lambda b, j, i, s: (b, 0, j)
    grid_spec = pltpu.PrefetchScalarGridSpec(
        num_scalar_prefetch=1,
        grid=(bsz, ATT_KV_HEADS, length // rows),
        in_specs=[pl.BlockSpec((None, rows, gw), qmap),
                  pl.BlockSpec((None, length, ATT_HEAD_DIM), kvmap),
                  pl.BlockSpec((None, length, ATT_HEAD_DIM), kvmap),
                  pl.BlockSpec((None, n_ctx, ATT_HEAD_DIM), kvmap),
                  pl.BlockSpec((None, n_ctx, ATT_HEAD_DIM), kvmap),
                  pl.BlockSpec((None, rows, gw), qmap)],
        out_specs=pl.BlockSpec((None, rows, gw), qmap),
    )
    return pl.pallas_call(
        _attn_kernel,
        grid_spec=grid_spec,
        out_shape=jax.ShapeDtypeStruct((bsz, length, ATT_W), BF16),
        compiler_params=_params("parallel", "parallel", "arbitrary"),
        name="attention",
    )(sink, q, k, v, kc, vc, gate)


E_IN_SIZES = (SSD_WIDTH, CONV_CH, 2 * SSD_HEADS, GLA_K, GLA_K, GLA_V, GLA_V, 2 * GLA_RANK)
E_IN = sum(E_IN_SIZES)
E_ALIGNED = SSD_WIDTH + CONV_CH
E_SHIFT = 2 * SSD_HEADS
E_MAIN = 2 * GLA_K + 2 * GLA_V
E_OUT = E_ALIGNED + E_MAIN + SMALL_W
PREP_ROWS = 256
O_IN = 2 * ATT_KV_W + 2 * ATT_W


def _prep_even_kernel(wt_ref, o_ref):
    cols = wt_ref.shape[1]
    for dst in range(0, E_ALIGNED + E_MAIN, LANES):
        src = dst if dst < E_ALIGNED else dst + E_SHIFT
        o_ref[:, dst:dst + LANES] = wt_ref[src:src + LANES, :].T.astype(BF16)
    small = jnp.concatenate([wt_ref[E_ALIGNED:E_ALIGNED + E_SHIFT, :],
                             wt_ref[E_IN - 2 * GLA_RANK:E_IN, :],
                             jnp.zeros((SMALL_W - E_SHIFT - 2 * GLA_RANK, cols), F32)], axis=0)
    o_ref[:, E_ALIGNED + E_MAIN:E_OUT] = small.T.astype(BF16)


def _prep_even(w_in):
    d = w_in.shape[0]
    return pl.pallas_call(
        _prep_even_kernel,
        grid=(d // PREP_ROWS,),
        in_specs=[pl.BlockSpec((E_IN, PREP_ROWS), lambda i: (0, i))],
        out_specs=pl.BlockSpec((PREP_ROWS, E_OUT), lambda i: (i, 0)),
        out_shape=jax.ShapeDtypeStruct((d, E_OUT), BF16),
        compiler_params=_params("parallel"),
        name="prep_even",
    )(jnp.swapaxes(w_in, 0, 1))


def _prep_odd_kernel(w_ref, perm_ref, o_ref):
    def copy(c0, width):
        o_ref[:, c0:c0 + width] = w_ref[:, c0:c0 + width].astype(BF16)

    def permute(c0, width):
        for c in range(c0, c0 + width, 2 * LANES):
            o_ref[:, c:c + 2 * LANES] = jnp.dot(w_ref[:, c:c + 2 * LANES].astype(BF16), perm_ref[...],
                                                preferred_element_type=F32).astype(BF16)

    permute(0, ATT_KV_W)
    copy(ATT_KV_W, ATT_KV_W)
    permute(2 * ATT_KV_W, ATT_W)
    copy(2 * ATT_KV_W + ATT_W, ATT_W)


def _rope_perm(n):
    out = np.arange(n)
    head, rem = out // ATT_HEAD_DIM, out % ATT_HEAD_DIM
    half, axis, f = rem // (2 * ROPE_FREQS), (rem // ROPE_FREQS) % 2, rem % ROPE_FREQS
    return head * ATT_HEAD_DIM + axis * 2 * ROPE_FREQS + half * ROPE_FREQS + f


def _prep_odd(w_in):
    d = w_in.shape[0]
    src = _rope_perm(2 * LANES)
    perm = jnp.asarray(np.arange(2 * LANES)[:, None] == src[None, :], BF16)
    return pl.pallas_call(
        _prep_odd_kernel,
        grid=(d // PREP_ROWS,),
        in_specs=[pl.BlockSpec((PREP_ROWS, O_IN), lambda i: (i, 0)),
                  pl.BlockSpec((2 * LANES, 2 * LANES), lambda i: (0, 0))],
        out_specs=pl.BlockSpec((PREP_ROWS, O_IN), lambda i: (i, 0)),
        out_shape=jax.ShapeDtypeStruct((d, O_IN), BF16),
        compiler_params=_params("parallel"),
        name="prep_odd",
    )(w_in, perm)


EVEN_WIDTHS = ((SSD_WIDTH, BF16), (CONV_CH, BF16), (GLA_K, BF16), (GLA_K, BF16), (GLA_V, BF16), (GLA_V, BF16),
               (SMALL_W, F32))


def _expansion_matrices():
    rows = np.arange(SMALL_W)[:, None]
    heads = (np.arange(SSD_WIDTH) // SSD_HEAD_DIM)[None, :]
    ef = (rows == heads).astype(np.float32)
    eb = (rows - SSD_HEADS == heads).astype(np.float32)
    return jnp.asarray(ef, BF16), jnp.asarray(eb, BF16)


def _pad_lanes(v, width):
    return jnp.pad(v, ((0, 0), (0, width - v.shape[1])))


def _rope_tables(length):
    rows = length // GRID_W
    row = np.repeat(np.arange(rows, dtype=np.float64), GRID_W)
    col = np.tile(np.arange(GRID_W, dtype=np.float64), rows)
    inv = 1.0 / (ROPE_BASE ** (np.arange(ROPE_FREQS, dtype=np.float64) / ROPE_FREQS))
    ang_r = row[:, None] * inv
    ang_c = col[:, None] * inv
    cos = np.concatenate([np.cos(ang_r), np.cos(ang_c), np.cos(ang_r), np.cos(ang_c)], axis=1)
    sin = np.concatenate([-np.sin(ang_r), -np.sin(ang_c), np.sin(ang_r), np.sin(ang_c)], axis=1)
    return jnp.asarray(cos, F32), jnp.asarray(sin, F32)


def _rope_head_layout(v):
    return v[_rope_perm(ATT_HEAD_DIM)]


def _mod_rows(mod, rows, bsz, d):
    picked = jnp.broadcast_to(mod[rows], (bsz, 3 * d)) if isinstance(rows, int) else mod[rows]
    return [picked[:, None, j * d:(j + 1) * d] for j in range(3)]


PROJ_ROWS = 512
OUT_ROWS = 1024
ATT_Q_BLOCKS = 32
GLA_ROWS = 512
EVEN_TAIL_ROWS = 512
GLA_CUMSUM_ROWS = 256
SSD_FWD_ROWS = 4 * SSD_CHUNK
CVEC_ROWS = SUBLANES


def kernel(x, c, ctx, c_ctx, e_norm, e_mod_w, e_mod_b, e_w_in, e_conv_w, e_conv_b, e_dt_bias, e_a_log,
           e_d_skip, e_ssd_norm, e_gla_gate_w, e_gla_gate_b, e_gla_norm, e_w_out, o_norm, o_mod_w, o_mod_b,
           o_w_in, o_q_norm, o_k_norm, o_sink, o_w_out):
    bsz, length, d = x.shape
    n_ctx = ctx.shape[1]
    assert e_norm.shape[0] == 1 and o_norm.shape[0] == 1, "two-layer block only"
    assert length % SSD_CHUNK == 0 and n_ctx % SSD_CHUNK == 0 and length >= 3 * ATT_BLOCK
    assert bsz + 1 <= CVEC_ROWS

    cvecs = jnp.zeros((CVEC_ROWS, d), F32).at[:bsz].set(c).at[bsz].set(c_ctx)
    lat_rows = slice(0, bsz)

    mod_both = _adaln(cvecs, e_mod_w[0], e_mod_b[0], o_mod_w[0], o_mod_b[0])
    mod = mod_both[:, :3 * d]
    shift, scale, gate = _mod_rows(mod, lat_rows, bsz, d)
    c_shift, c_scale, c_gate = _mod_rows(mod, bsz, bsz, d)
    w_in = _prep_even(e_w_in[0])
    w_out = e_w_out[0].astype(BF16)
    ef, eb = _expansion_matrices()
    nh2 = 2 * SSD_HEADS
    bias_flat = e_dt_bias[0].reshape(1, nh2)
    alog_flat = e_a_log[0].reshape(1, nh2)
    bias_row, alog_row = _pad_lanes(bias_flat, SMALL_W), _pad_lanes(alog_flat, SMALL_W)
    bias_col, alog_col = bias_flat.reshape(nh2, 1), alog_flat.reshape(nh2, 1)
    dskip_row = jnp.repeat(e_d_skip[0], SSD_HEAD_DIM).reshape(1, SSD_WIDTH)
    conv_b = e_conv_b[0].reshape(1, CONV_CH)
    wg = jnp.zeros((SMALL_W, 2 * GLA_K), F32)
    wg = wg.at[nh2:nh2 + GLA_RANK, 0:GLA_K].set(e_gla_gate_w[0, 0])
    wg = wg.at[nh2 + GLA_RANK:nh2 + 2 * GLA_RANK, GLA_K:2 * GLA_K].set(e_gla_gate_w[0, 1])
    gb = e_gla_gate_b[0].reshape(1, 2 * GLA_K)

    def even_layer(stream, sc, sh, gt, ssd_init, gla_init):
        z, xbc, q, k, v, g, small = _proj_even(stream, e_norm[0], sc, sh, w_in, EVEN_WIDTHS, PROJ_ROWS)
        ya, cm, bt, wxb, p3, ssd_f = _sweep_call(
            _ssd_fwd_kernel, "ssd_fwd",
            _ssd_fwd(xbc, small, e_conv_w[0], conv_b, bias_row, bias_col, alog_row, alog_col, dskip_row,
                     ef, eb, ssd_init[0]))
        oa, qdb, kwb, decb, gla_f = _sweep_call(
            _gla_fwd_kernel, "gla_fwd", _gla_fwd(q, k, v, small, wg, gb, gla_init[0], GLA_ROWS))
        out, ssd_b, gla_b = _even_tail(stream, gt, w_out,
                                       (cm, bt, wxb, p3, ya, z, eb, e_ssd_norm[0], ssd_init[1]),
                                       (qdb, kwb, decb, v, oa, g, e_gla_norm[0], gla_init[1]))
        return out, (ssd_f, ssd_b), (gla_f, gla_b)

    ssd0 = jnp.zeros((bsz,) + SSD_STATE_SHAPE, F32)
    gla0 = jnp.zeros((bsz,) + GLA_STATE_SHAPE, F32)
    xc, ssd_fin, gla_fin = even_layer(ctx, c_scale, c_shift, c_gate, (ssd0, ssd0), (gla0, gla0))
    x, _, _ = even_layer(x, scale, shift, gate, ssd_fin, gla_fin)

    mod = mod_both[:, 3 * d:]
    shift, scale, gate = _mod_rows(mod, lat_rows, bsz, d)
    c_shift, c_scale, _ = _mod_rows(mod, bsz, bsz, d)
    w_in = _prep_odd(o_w_in[0])
    q_norm = _rope_head_layout(o_q_norm[0])
    k_norm = _rope_head_layout(o_k_norm[0])
    cos, sin = _rope_tables(length)
    no_rot = (jnp.ones((n_ctx, ATT_HEAD_DIM), F32), jnp.zeros((n_ctx, ATT_HEAD_DIM), F32))
    kc, vc = _proj_odd(xc, o_norm[0], c_scale, c_shift, w_in[:, :2 * ATT_KV_W], *no_rot,
                       q_norm, k_norm, False, PROJ_ROWS)
    k, v, q, g = _proj_odd(x, o_norm[0], scale, shift, w_in, cos, sin, q_norm, k_norm, True, PROJ_ROWS)
    o = _attention(o_sink[0].astype(F32), q, k, v, kc, vc, g)
    return _out_proj(x, gate, (o,), o_w_out[0].astype(BF16), OUT_ROWS, "out_odd")
```
